```python
import math
import jax, jax.numpy as jnp
from jax import lax
import numpy as np

D_MODEL = 2048
BATCH = 8
SEQ = 4096
DEPTH = 4

D_MIX = D_MODEL
D_CONV = D_MIX // 4
CONV_GROUPS = 4
CONV_WIDTH = 3
HEAD_DIM = 128
D_ATT = D_MIX // 2
N_ATT_HEADS = D_ATT // HEAD_DIM
D_LRU = D_MIX - D_CONV - D_ATT
LRU_BLOCKS = 4
LRU_BLOCK = D_LRU // LRU_BLOCKS
LRU_CONV_WIDTH = 4
LRU_C = 8.0
D_IN = 3 * D_CONV + 3 * D_ATT + N_ATT_HEADS + 2 * D_LRU
D_FF = 256 * int(math.ceil(8 * D_MODEL / 3 / 256))
Q_BLOCK = 128
EPS = 1e-6

kernel_name = "hymba_parallel_conv_fox_rglru_macaron"


def rmsnorm(x, g):
    xf = x.astype(jnp.float32)
    y = xf * lax.rsqrt(jnp.mean(xf * xf, axis=-1, keepdims=True) + EPS)
    return (y * g.astype(jnp.float32)).astype(x.dtype)


def swiglu(h, w_in, w_out):
    g, u = jnp.split(h @ w_in, 2, axis=-1)
    return (jax.nn.silu(g) * u) @ w_out


def causal_depthwise_conv(x, w):
    k_width, ch = w.shape
    return lax.conv_general_dilated(
        x, w[:, None, :].astype(x.dtype), window_strides=(1,),
        padding=[(k_width - 1, 0)], dimension_numbers=("NWC", "WIO", "NWC"),
        feature_group_count=ch)


def forgetting_attention(q, k, v, f_logit, b_f):
    bsz, seq, _ = q.shape
    def heads(t):
        return t.reshape(bsz, seq, N_ATT_HEADS, HEAD_DIM).transpose(0, 2, 1, 3)
    q, k, v = heads(q), heads(k), heads(v)
    log_f = jax.nn.log_sigmoid(f_logit.astype(jnp.float32) + b_f.astype(jnp.float32))
    cum = jnp.cumsum(log_f, axis=1).transpose(0, 2, 1)
    scale = HEAD_DIM ** -0.5
    k_pos = jnp.arange(seq)

    def block(i):
        start = i * Q_BLOCK
        qb = lax.dynamic_slice_in_dim(q, start, Q_BLOCK, axis=2)
        cb = lax.dynamic_slice_in_dim(cum, start, Q_BLOCK, axis=2)
        s = (jnp.einsum("bhqd,bhkd->bhqk", qb, k).astype(jnp.float32) * scale
             + cb[..., None] - cum[:, :, None, :])
        q_pos = start + jnp.arange(Q_BLOCK)
        s = jnp.where(k_pos[None, :] <= q_pos[:, None], s, -jnp.inf)
        p = jax.nn.softmax(s, axis=-1)
        return jnp.einsum("bhqk,bhkd->bhqd", p.astype(v.dtype), v)

    o = lax.map(block, jnp.arange(seq // Q_BLOCK))
    return o.transpose(1, 0, 3, 2, 4).reshape(bsz, seq, D_ATT)


def rg_lru(x, w_a, b_a, w_x, b_x, lam):
    bsz, seq, _ = x.shape
    xb = x.reshape(bsz, seq, LRU_BLOCKS, LRU_BLOCK)
    r = jax.nn.sigmoid(jnp.einsum("btgi,gij->btgj", xb, w_a).reshape(bsz, seq, D_LRU) + b_a)
    i = jax.nn.sigmoid(jnp.einsum("btgi,gij->btgj", xb, w_x).reshape(bsz, seq, D_LRU) + b_x)
    log_a = -LRU_C * r.astype(jnp.float32) * jax.nn.softplus(-lam.astype(jnp.float32))
    a = jnp.exp(log_a)
    u = jnp.sqrt(-jnp.expm1(2.0 * log_a)) * (i * x).astype(jnp.float32)

    def combine(left, right):
        a1, b1 = left
        a2, b2 = right
        return a1 * a2, a2 * b1 + b2

    _, h = lax.associative_scan(combine, (a, u), axis=1)
    return h.astype(x.dtype)


def parallel_mixer(h, w_in, conv_w, fgate_b, lru_conv_w, lru_conv_b, lru_w_a, lru_b_a,
                   lru_w_x, lru_b_x, lru_lambda, out_norm, w_out):
    sizes = [D_CONV] * 3 + [D_ATT] * 3 + [N_ATT_HEADS] + [D_LRU] * 2
    splits = [int(s) for s in np.cumsum(sizes)[:-1]]
    c_b, c_c, c_v, q, k, v, f_logit, lru_gate, lru_x = jnp.split(h @ w_in, splits, axis=-1)
    y_conv = c_b * causal_depthwise_conv(c_c * c_v, conv_w)
    y_att = forgetting_attention(q, k, v, f_logit, fgate_b)
    xr = causal_depthwise_conv(lru_x, lru_conv_w) + lru_conv_b
    y_lru = jax.nn.gelu(lru_gate) * rg_lru(xr, lru_w_a, lru_b_a, lru_w_x, lru_b_x, lru_lambda)
    g_c, g_a, g_l = jnp.split(out_norm, [D_CONV, D_CONV + D_ATT])
    y = jnp.concatenate([rmsnorm(y_conv, g_c), rmsnorm(y_att, g_a), rmsnorm(y_lru, g_l)], axis=-1)
    return y @ w_out


def _fwd_setup_inputs(seed: int = 0) -> dict:
    key = jax.random.key(seed)
    ks = iter(jax.random.split(key, 32))
    f32 = jnp.float32
    res_scale = (2.0 * DEPTH) ** -0.5

    def normal(shape, std):
        return jax.random.normal(next(ks), shape, f32) * std

    def gain(shape):
        return 1.0 + normal(shape, 0.02)

    x = jax.random.normal(next(ks), (BATCH, SEQ, D_MODEL), f32)
    a_c = jax.random.uniform(next(ks), (DEPTH, D_LRU), f32, 0.9, 0.999)
    s = a_c ** (1.0 / LRU_C)
    lru_lambda = jnp.log(s) - jnp.log1p(-s)
    return {
        "x": x,
        "norm_ffn1": gain((DEPTH, D_MODEL)),
        "ffn1_w_in": normal((DEPTH, D_MODEL, 2 * D_FF), D_MODEL ** -0.5),
        "ffn1_w_out": normal((DEPTH, D_FF, D_MODEL), D_FF ** -0.5 * res_scale),
        "norm_mix": gain((DEPTH, D_MODEL)),
        "mix_w_in": normal((DEPTH, D_MODEL, D_IN), D_MODEL ** -0.5),
        "conv_w": normal((DEPTH, CONV_WIDTH, D_CONV), CONV_WIDTH ** -0.5),
        "fgate_b": jax.random.uniform(next(ks), (DEPTH, N_ATT_HEADS), f32, 1.0, 4.0),
        "lru_conv_w": normal((DEPTH, LRU_CONV_WIDTH, D_LRU), LRU_CONV_WIDTH ** -0.5),
        "lru_conv_b": normal((DEPTH, D_LRU), 0.02),
        "lru_w_a": normal((DEPTH, LRU_BLOCKS, LRU_BLOCK, LRU_BLOCK), LRU_BLOCK ** -0.5),
        "lru_b_a": normal((DEPTH, D_LRU), 0.02),
        "lru_w_x": normal((DEPTH, LRU_BLOCKS, LRU_BLOCK, LRU_BLOCK), LRU_BLOCK ** -0.5),
        "lru_b_x": normal((DEPTH, D_LRU), 0.02),
        "lru_lambda": lru_lambda,
        "mix_out_norm": gain((DEPTH, D_MIX)),
        "mix_w_out": normal((DEPTH, D_MIX, D_MODEL), D_MIX ** -0.5 * res_scale),
        "norm_ffn2": gain((DEPTH, D_MODEL)),
        "ffn2_w_in": normal((DEPTH, D_MODEL, 2 * D_FF), D_MODEL ** -0.5),
        "ffn2_w_out": normal((DEPTH, D_FF, D_MODEL), D_FF ** -0.5 * res_scale),
        "final_norm": gain((D_MODEL,)),
    }


def _fwd_reference(x, norm_ffn1, ffn1_w_in, ffn1_w_out, norm_mix, mix_w_in, conv_w, fgate_b,
              lru_conv_w, lru_conv_b, lru_w_a, lru_b_a, lru_w_x, lru_b_x, lru_lambda,
              mix_out_norm, mix_w_out, norm_ffn2, ffn2_w_in, ffn2_w_out, final_norm):
    for l in range(DEPTH):
        x = x + 0.5 * swiglu(rmsnorm(x, norm_ffn1[l]), ffn1_w_in[l], ffn1_w_out[l])
        x = x + parallel_mixer(rmsnorm(x, norm_mix[l]), mix_w_in[l], conv_w[l], fgate_b[l],
                               lru_conv_w[l], lru_conv_b[l], lru_w_a[l], lru_b_a[l],
                               lru_w_x[l], lru_b_x[l], lru_lambda[l], mix_out_norm[l],
                               mix_w_out[l])
        x = x + 0.5 * swiglu(rmsnorm(x, norm_ffn2[l]), ffn2_w_in[l], ffn2_w_out[l])
    return rmsnorm(x, final_norm)


import jax as _jax
import jax.numpy as _jnp

TWIN_FORMAT = 'train_step'
FWD_PARAMS = ['x', 'norm_ffn1', 'ffn1_w_in', 'ffn1_w_out', 'norm_mix', 'mix_w_in', 'conv_w', 'fgate_b', 'lru_conv_w', 'lru_conv_b', 'lru_w_a', 'lru_b_a', 'lru_w_x', 'lru_b_x', 'lru_lambda', 'mix_out_norm', 'mix_w_out', 'norm_ffn2', 'ffn2_w_in', 'ffn2_w_out', 'final_norm']
TWIN_WEIGHTS = ['norm_ffn1', 'ffn1_w_in', 'ffn1_w_out', 'norm_mix', 'mix_w_in', 'conv_w', 'fgate_b', 'lru_conv_w', 'lru_conv_b', 'lru_w_a', 'lru_b_a', 'lru_w_x', 'lru_b_x', 'lru_lambda', 'mix_out_norm', 'mix_w_out', 'norm_ffn2', 'ffn2_w_in', 'ffn2_w_out', 'final_norm']
TWIN_DIFF_INPUT = 'x'
TWIN_INPUTS = ['x', 'norm_ffn1', 'ffn1_w_in', 'ffn1_w_out', 'norm_mix', 'mix_w_in', 'conv_w', 'fgate_b', 'lru_conv_w', 'lru_conv_b', 'lru_w_a', 'lru_b_a', 'lru_w_x', 'lru_b_x', 'lru_lambda', 'mix_out_norm', 'mix_w_out', 'norm_ffn2', 'ffn2_w_in', 'ffn2_w_out', 'final_norm', 'loss_target', 'm_norm_ffn1', 'm_ffn1_w_in', 'm_ffn1_w_out', 'm_norm_mix', 'm_mix_w_in', 'm_conv_w', 'm_fgate_b', 'm_lru_conv_w', 'm_lru_conv_b', 'm_lru_w_a', 'm_lru_b_a', 'm_lru_w_x', 'm_lru_b_x', 'm_lru_lambda', 'm_mix_out_norm', 'm_mix_w_out', 'm_norm_ffn2', 'm_ffn2_w_in', 'm_ffn2_w_out', 'm_final_norm', 'v_norm_ffn1', 'v_ffn1_w_in', 'v_ffn1_w_out', 'v_norm_mix', 'v_mix_w_in', 'v_conv_w', 'v_fgate_b', 'v_lru_conv_w', 'v_lru_conv_b', 'v_lru_w_a', 'v_lru_b_a', 'v_lru_w_x', 'v_lru_b_x', 'v_lru_lambda', 'v_mix_out_norm', 'v_mix_w_out', 'v_norm_ffn2', 'v_ffn2_w_in', 'v_ffn2_w_out', 'v_final_norm']
TWIN_OUTPUTS = ['loss', 'grad_x', 'grad_norm_ffn1', 'grad_ffn1_w_in', 'grad_ffn1_w_out', 'grad_norm_mix', 'grad_mix_w_in', 'grad_conv_w', 'grad_fgate_b', 'grad_lru_conv_w', 'grad_lru_conv_b', 'grad_lru_w_a', 'grad_lru_b_a', 'grad_lru_w_x', 'grad_lru_b_x', 'grad_lru_lambda', 'grad_mix_out_norm', 'grad_mix_w_out', 'grad_norm_ffn2', 'grad_ffn2_w_in', 'grad_ffn2_w_out', 'grad_final_norm', 'delta_norm_ffn1', 'delta_ffn1_w_in', 'delta_ffn1_w_out', 'delta_norm_mix', 'delta_mix_w_in', 'delta_conv_w', 'delta_fgate_b', 'delta_lru_conv_w', 'delta_lru_conv_b', 'delta_lru_w_a', 'delta_lru_b_a', 'delta_lru_w_x', 'delta_lru_b_x', 'delta_lru_lambda', 'delta_mix_out_norm', 'delta_mix_w_out', 'delta_norm_ffn2', 'delta_ffn2_w_in', 'delta_ffn2_w_out', 'delta_final_norm', 'new_m_norm_ffn1', 'new_m_ffn1_w_in', 'new_m_ffn1_w_out', 'new_m_norm_mix', 'new_m_mix_w_in', 'new_m_conv_w', 'new_m_fgate_b', 'new_m_lru_conv_w', 'new_m_lru_conv_b', 'new_m_lru_w_a', 'new_m_lru_b_a', 'new_m_lru_w_x', 'new_m_lru_b_x', 'new_m_lru_lambda', 'new_m_mix_out_norm', 'new_m_mix_w_out', 'new_m_norm_ffn2', 'new_m_ffn2_w_in', 'new_m_ffn2_w_out', 'new_m_final_norm', 'new_v_norm_ffn1', 'new_v_ffn1_w_in', 'new_v_ffn1_w_out', 'new_v_norm_mix', 'new_v_mix_w_in', 'new_v_conv_w', 'new_v_fgate_b', 'new_v_lru_conv_w', 'new_v_lru_conv_b', 'new_v_lru_w_a', 'new_v_lru_b_a', 'new_v_lru_w_x', 'new_v_lru_b_x', 'new_v_lru_lambda', 'new_v_mix_out_norm', 'new_v_mix_w_out', 'new_v_norm_ffn2', 'new_v_ffn2_w_in', 'new_v_ffn2_w_out', 'new_v_final_norm']
TWIN_LEAF_KINDS = {'loss': 'loss', 'grad_x': 'grad_x', 'grad_norm_ffn1': 'grad_w', 'grad_ffn1_w_in': 'grad_w', 'grad_ffn1_w_out': 'grad_w', 'grad_norm_mix': 'grad_w', 'grad_mix_w_in': 'grad_w', 'grad_conv_w': 'grad_w', 'grad_fgate_b': 'grad_w', 'grad_lru_conv_w': 'grad_w', 'grad_lru_conv_b': 'grad_w', 'grad_lru_w_a': 'grad_w', 'grad_lru_b_a': 'grad_w', 'grad_lru_w_x': 'grad_w', 'grad_lru_b_x': 'grad_w', 'grad_lru_lambda': 'grad_w', 'grad_mix_out_norm': 'grad_w', 'grad_mix_w_out': 'grad_w', 'grad_norm_ffn2': 'grad_w', 'grad_ffn2_w_in': 'grad_w', 'grad_ffn2_w_out': 'grad_w', 'grad_final_norm': 'grad_w', 'delta_norm_ffn1': 'delta_w', 'delta_ffn1_w_in': 'delta_w', 'delta_ffn1_w_out': 'delta_w', 'delta_norm_mix': 'delta_w', 'delta_mix_w_in': 'delta_w', 'delta_conv_w': 'delta_w', 'delta_fgate_b': 'delta_w', 'delta_lru_conv_w': 'delta_w', 'delta_lru_conv_b': 'delta_w', 'delta_lru_w_a': 'delta_w', 'delta_lru_b_a': 'delta_w', 'delta_lru_w_x': 'delta_w', 'delta_lru_b_x': 'delta_w', 'delta_lru_lambda': 'delta_w', 'delta_mix_out_norm': 'delta_w', 'delta_mix_w_out': 'delta_w', 'delta_norm_ffn2': 'delta_w', 'delta_ffn2_w_in': 'delta_w', 'delta_ffn2_w_out': 'delta_w', 'delta_final_norm': 'delta_w', 'new_m_norm_ffn1': 'new_m', 'new_m_ffn1_w_in': 'new_m', 'new_m_ffn1_w_out': 'new_m', 'new_m_norm_mix': 'new_m', 'new_m_mix_w_in': 'new_m', 'new_m_conv_w': 'new_m', 'new_m_fgate_b': 'new_m', 'new_m_lru_conv_w': 'new_m', 'new_m_lru_conv_b': 'new_m', 'new_m_lru_w_a': 'new_m', 'new_m_lru_b_a': 'new_m', 'new_m_lru_w_x': 'new_m', 'new_m_lru_b_x': 'new_m', 'new_m_lru_lambda': 'new_m', 'new_m_mix_out_norm': 'new_m', 'new_m_mix_w_out': 'new_m', 'new_m_norm_ffn2': 'new_m', 'new_m_ffn2_w_in': 'new_m', 'new_m_ffn2_w_out': 'new_m', 'new_m_final_norm': 'new_m', 'new_v_norm_ffn1': 'new_v', 'new_v_ffn1_w_in': 'new_v', 'new_v_ffn1_w_out': 'new_v', 'new_v_norm_mix': 'new_v', 'new_v_mix_w_in': 'new_v', 'new_v_conv_w': 'new_v', 'new_v_fgate_b': 'new_v', 'new_v_lru_conv_w': 'new_v', 'new_v_lru_conv_b': 'new_v', 'new_v_lru_w_a': 'new_v', 'new_v_lru_b_a': 'new_v', 'new_v_lru_w_x': 'new_v', 'new_v_lru_b_x': 'new_v', 'new_v_lru_lambda': 'new_v', 'new_v_mix_out_norm': 'new_v', 'new_v_mix_w_out': 'new_v', 'new_v_norm_ffn2': 'new_v', 'new_v_ffn2_w_in': 'new_v', 'new_v_ffn2_w_out': 'new_v', 'new_v_final_norm': 'new_v'}


def _forward(args):
    return _fwd_reference(*[args[k] for k in FWD_PARAMS])


def _output_shape():
    def fwd():
        inp = _fwd_setup_inputs(0)
        return _fwd_reference(*[inp[k] for k in FWD_PARAMS])
    out = _jax.eval_shape(fwd)
    return out.shape, out.dtype

N_MICROBATCH = 1
ADAM_LR = 0.001
ADAM_B1 = 0.9
ADAM_B2 = 0.999
ADAM_EPS = 1e-08
ADAM_WD = 0.01
ADAM_STEP = 10
PER_EXAMPLE_BATCH_AXIS = {'x': 0, 'loss_target': 0}
SHARED_INPUTS = []
_WEIGHT_DTYPES = {'norm_ffn1': _jnp.float32, 'ffn1_w_in': _jnp.float32, 'ffn1_w_out': _jnp.float32, 'norm_mix': _jnp.float32, 'mix_w_in': _jnp.float32, 'conv_w': _jnp.float32, 'fgate_b': _jnp.float32, 'lru_conv_w': _jnp.float32, 'lru_conv_b': _jnp.float32, 'lru_w_a': _jnp.float32, 'lru_b_a': _jnp.float32, 'lru_w_x': _jnp.float32, 'lru_b_x': _jnp.float32, 'lru_lambda': _jnp.float32, 'mix_out_norm': _jnp.float32, 'mix_w_out': _jnp.float32, 'norm_ffn2': _jnp.float32, 'ffn2_w_in': _jnp.float32, 'ffn2_w_out': _jnp.float32, 'final_norm': _jnp.float32}
MOMENT_SCALE = {'norm_ffn1': 1.434589e-02, 'ffn1_w_in': 6.128378e-03, 'ffn1_w_out': 2.829652e-02, 'norm_mix': 4.744357e-02, 'mix_w_in': 2.830434e-02, 'conv_w': 3.075757e-02, 'fgate_b': 1.912410e-01, 'lru_conv_w': 3.355757e-02, 'lru_conv_b': 3.509435e-01, 'lru_w_a': 9.137879e-03, 'lru_b_a': 7.529212e-03, 'lru_w_x': 1.637001e-02, 'lru_b_x': 1.224642e-02, 'lru_lambda': 1.563363e-02, 'mix_out_norm': 3.222851e-02, 'mix_w_out': 9.035761e-02, 'norm_ffn2': 1.276218e-02, 'ffn2_w_in': 5.425657e-03, 'ffn2_w_out': 2.505026e-02, 'final_norm': 1.600536e+01}


def _to_microbatches(a, axis):
    t = _jnp.moveaxis(a, axis, 0)
    t = t.reshape((N_MICROBATCH, t.shape[0] // N_MICROBATCH) + t.shape[1:])
    return _jnp.moveaxis(t, 1, axis + 1)


def setup_inputs(seed: int = 0) -> dict:
    inp = _fwd_setup_inputs(seed)
    key = _jax.random.fold_in(_jax.random.key(seed), 7919)
    shape, _ = _output_shape()
    out = dict(inp)
    out["loss_target"] = _jax.random.normal(_jax.random.fold_in(key, 0), shape, _jnp.float32)
    for i, name in enumerate(TWIN_WEIGHTS):
        w = inp[name].astype(_jnp.float32)
        if MOMENT_SCALE is None:
            s = _jnp.sqrt(_jnp.mean(_jnp.square(w)) + 1e-30)
        else:
            s = MOMENT_SCALE[name]
        km, kv = _jax.random.split(_jax.random.fold_in(key, i + 1))
        out[name] = w
        out["m_" + name] = s * _jax.random.normal(km, w.shape, _jnp.float32)
        out["v_" + name] = (s * s) * _jax.random.uniform(kv, w.shape, _jnp.float32, 0.5, 1.5)
    if N_MICROBATCH > 1:
        for name, axis in PER_EXAMPLE_BATCH_AXIS.items():
            out[name] = _to_microbatches(out[name], axis)
    return {'x': out['x'], 'norm_ffn1': out['norm_ffn1'], 'ffn1_w_in': out['ffn1_w_in'], 'ffn1_w_out': out['ffn1_w_out'], 'norm_mix': out['norm_mix'], 'mix_w_in': out['mix_w_in'], 'conv_w': out['conv_w'], 'fgate_b': out['fgate_b'], 'lru_conv_w': out['lru_conv_w'], 'lru_conv_b': out['lru_conv_b'], 'lru_w_a': out['lru_w_a'], 'lru_b_a': out['lru_b_a'], 'lru_w_x': out['lru_w_x'], 'lru_b_x': out['lru_b_x'], 'lru_lambda': out['lru_lambda'], 'mix_out_norm': out['mix_out_norm'], 'mix_w_out': out['mix_w_out'], 'norm_ffn2': out['norm_ffn2'], 'ffn2_w_in': out['ffn2_w_in'], 'ffn2_w_out': out['ffn2_w_out'], 'final_norm': out['final_norm'], 'loss_target': out['loss_target'], 'm_norm_ffn1': out['m_norm_ffn1'], 'm_ffn1_w_in': out['m_ffn1_w_in'], 'm_ffn1_w_out': out['m_ffn1_w_out'], 'm_norm_mix': out['m_norm_mix'], 'm_mix_w_in': out['m_mix_w_in'], 'm_conv_w': out['m_conv_w'], 'm_fgate_b': out['m_fgate_b'], 'm_lru_conv_w': out['m_lru_conv_w'], 'm_lru_conv_b': out['m_lru_conv_b'], 'm_lru_w_a': out['m_lru_w_a'], 'm_lru_b_a': out['m_lru_b_a'], 'm_lru_w_x': out['m_lru_w_x'], 'm_lru_b_x': out['m_lru_b_x'], 'm_lru_lambda': out['m_lru_lambda'], 'm_mix_out_norm': out['m_mix_out_norm'], 'm_mix_w_out': out['m_mix_w_out'], 'm_norm_ffn2': out['m_norm_ffn2'], 'm_ffn2_w_in': out['m_ffn2_w_in'], 'm_ffn2_w_out': out['m_ffn2_w_out'], 'm_final_norm': out['m_final_norm'], 'v_norm_ffn1': out['v_norm_ffn1'], 'v_ffn1_w_in': out['v_ffn1_w_in'], 'v_ffn1_w_out': out['v_ffn1_w_out'], 'v_norm_mix': out['v_norm_mix'], 'v_mix_w_in': out['v_mix_w_in'], 'v_conv_w': out['v_conv_w'], 'v_fgate_b': out['v_fgate_b'], 'v_lru_conv_w': out['v_lru_conv_w'], 'v_lru_conv_b': out['v_lru_conv_b'], 'v_lru_w_a': out['v_lru_w_a'], 'v_lru_b_a': out['v_lru_b_a'], 'v_lru_w_x': out['v_lru_w_x'], 'v_lru_b_x': out['v_lru_b_x'], 'v_lru_lambda': out['v_lru_lambda'], 'v_mix_out_norm': out['v_mix_out_norm'], 'v_mix_w_out': out['v_mix_w_out'], 'v_norm_ffn2': out['v_norm_ffn2'], 'v_ffn2_w_in': out['v_ffn2_w_in'], 'v_ffn2_w_out': out['v_ffn2_w_out'], 'v_final_norm': out['v_final_norm']}


def _loss(weights, diff, rest, loss_target):
    with _jax.named_scope("forward"):
        args = {**rest, TWIN_DIFF_INPUT: diff, **{k: w.astype(_WEIGHT_DTYPES[k]) for k, w in weights.items()}}
        y = _forward(args)
    with _jax.named_scope("loss_head"):
        err = _jnp.square(y.astype(_jnp.float32) - loss_target)
        return 0.5 * _jnp.sum(_jnp.mean(err, axis=-1)) if err.ndim else 0.5 * err


def _adamw(w, g, m, v):
    m = ADAM_B1 * m + (1.0 - ADAM_B1) * g
    v = ADAM_B2 * v + (1.0 - ADAM_B2) * _jnp.square(g)
    m_hat = m / (1.0 - ADAM_B1 ** ADAM_STEP)
    v_hat = v / (1.0 - ADAM_B2 ** ADAM_STEP)
    delta = -ADAM_LR * (m_hat / (_jnp.sqrt(v_hat) + ADAM_EPS) + ADAM_WD * w)
    return delta, m, v


def reference(x, norm_ffn1, ffn1_w_in, ffn1_w_out, norm_mix, mix_w_in, conv_w, fgate_b, lru_conv_w, lru_conv_b, lru_w_a, lru_b_a, lru_w_x, lru_b_x, lru_lambda, mix_out_norm, mix_w_out, norm_ffn2, ffn2_w_in, ffn2_w_out, final_norm, loss_target, m_norm_ffn1, m_ffn1_w_in, m_ffn1_w_out, m_norm_mix, m_mix_w_in, m_conv_w, m_fgate_b, m_lru_conv_w, m_lru_conv_b, m_lru_w_a, m_lru_b_a, m_lru_w_x, m_lru_b_x, m_lru_lambda, m_mix_out_norm, m_mix_w_out, m_norm_ffn2, m_ffn2_w_in, m_ffn2_w_out, m_final_norm, v_norm_ffn1, v_ffn1_w_in, v_ffn1_w_out, v_norm_mix, v_mix_w_in, v_conv_w, v_fgate_b, v_lru_conv_w, v_lru_conv_b, v_lru_w_a, v_lru_b_a, v_lru_w_x, v_lru_b_x, v_lru_lambda, v_mix_out_norm, v_mix_w_out, v_norm_ffn2, v_ffn2_w_in, v_ffn2_w_out, v_final_norm):
    given = dict(x=x, norm_ffn1=norm_ffn1, ffn1_w_in=ffn1_w_in, ffn1_w_out=ffn1_w_out, norm_mix=norm_mix, mix_w_in=mix_w_in, conv_w=conv_w, fgate_b=fgate_b, lru_conv_w=lru_conv_w, lru_conv_b=lru_conv_b, lru_w_a=lru_w_a, lru_b_a=lru_b_a, lru_w_x=lru_w_x, lru_b_x=lru_b_x, lru_lambda=lru_lambda, mix_out_norm=mix_out_norm, mix_w_out=mix_w_out, norm_ffn2=norm_ffn2, ffn2_w_in=ffn2_w_in, ffn2_w_out=ffn2_w_out, final_norm=final_norm, loss_target=loss_target, m_norm_ffn1=m_norm_ffn1, m_ffn1_w_in=m_ffn1_w_in, m_ffn1_w_out=m_ffn1_w_out, m_norm_mix=m_norm_mix, m_mix_w_in=m_mix_w_in, m_conv_w=m_conv_w, m_fgate_b=m_fgate_b, m_lru_conv_w=m_lru_conv_w, m_lru_conv_b=m_lru_conv_b, m_lru_w_a=m_lru_w_a, m_lru_b_a=m_lru_b_a, m_lru_w_x=m_lru_w_x, m_lru_b_x=m_lru_b_x, m_lru_lambda=m_lru_lambda, m_mix_out_norm=m_mix_out_norm, m_mix_w_out=m_mix_w_out, m_norm_ffn2=m_norm_ffn2, m_ffn2_w_in=m_ffn2_w_in, m_ffn2_w_out=m_ffn2_w_out, m_final_norm=m_final_norm, v_norm_ffn1=v_norm_ffn1, v_ffn1_w_in=v_ffn1_w_in, v_ffn1_w_out=v_ffn1_w_out, v_norm_mix=v_norm_mix, v_mix_w_in=v_mix_w_in, v_conv_w=v_conv_w, v_fgate_b=v_fgate_b, v_lru_conv_w=v_lru_conv_w, v_lru_conv_b=v_lru_conv_b, v_lru_w_a=v_lru_w_a, v_lru_b_a=v_lru_b_a, v_lru_w_x=v_lru_w_x, v_lru_b_x=v_lru_b_x, v_lru_lambda=v_lru_lambda, v_mix_out_norm=v_mix_out_norm, v_mix_w_out=v_mix_w_out, v_norm_ffn2=v_norm_ffn2, v_ffn2_w_in=v_ffn2_w_in, v_ffn2_w_out=v_ffn2_w_out, v_final_norm=v_final_norm)
    weights = {n: given[n] for n in TWIN_WEIGHTS}
    shared = {n: given[n] for n in SHARED_INPUTS}
    per_example = {n: given[n] for n in ['x']}
    grad_fn = _jax.value_and_grad(_loss, argnums=(0, 1))

    def one_microbatch(ex, loss_target):
        ex = dict(ex)
        diff = ex.pop(TWIN_DIFF_INPUT)
        return grad_fn(weights, diff, {**shared, **ex}, loss_target)

    if N_MICROBATCH == 1:
        loss, (grad_w, grad_x) = one_microbatch(per_example, given["loss_target"])
    else:
        def body(carry, xs):
            loss_sum, grad_sum = carry
            l_k, (gw_k, gx_k) = one_microbatch(xs[0], xs[1])
            with _jax.named_scope("update"):
                return (loss_sum + l_k, _jax.tree.map(_jnp.add, grad_sum, gw_k)), gx_k

        init = (_jnp.zeros((), _jnp.float32), _jax.tree.map(_jnp.zeros_like, weights))
        (loss, grad_w), grad_x = _jax.lax.scan(body, init, (per_example, given["loss_target"]))
    with _jax.named_scope("update"):
        delta_w, new_m, new_v = {}, {}, {}
        for n in TWIN_WEIGHTS:
            delta_w[n], new_m[n], new_v[n] = _adamw(weights[n], grad_w[n], given["m_" + n], given["v_" + n])
    return (loss, grad_x, *[grad_w[n] for n in TWIN_WEIGHTS], *[delta_w[n] for n in TWIN_WEIGHTS],
            *[new_m[n] for n in TWIN_WEIGHTS], *[new_v[n] for n in TWIN_WEIGHTS])
```

```python
import math

import jax
import jax.numpy as jnp
from jax import lax
from jax.experimental import pallas as pl
from jax.experimental.pallas import tpu as pltpu

F32 = jnp.float32
BF = jnp.bfloat16
EPS = 1e-6
LANES = 128
VMEM_LIMIT_V7X = 56 * 1024 * 1024
MESH = pl.DeviceIdType.MESH
N_DEV = 8
LRU_C = 8.0
ADAM_LR, ADAM_B1, ADAM_B2, ADAM_EPS, ADAM_WD, ADAM_STEP = 0.001, 0.9, 0.999, 1e-08, 0.01, 10
GELU_C = math.sqrt(2.0 / math.pi)
GELU_K = 0.044715


def _params(*sem):
    return pltpu.CompilerParams(dimension_semantics=sem, vmem_limit_bytes=VMEM_LIMIT_V7X)


def _any():
    return pl.BlockSpec(memory_space=pl.ANY)


def _tile(n, target):
    if n <= target:
        return n
    t = target - target % 16
    while t >= 16:
        if n % t == 0:
            return t
        t -= 16
    return n


def _mm(name, grid, a, b, o, dims, scale=1.0, resid=None):
    nk = grid[-1]
    acc_shape = tuple(d for d in o[1] if d is not None)
    has_resid = resid is not None

    def body(*refs):
        a_ref, b_ref = refs[0], refs[1]
        r_ref = refs[2] if has_resid else None
        o_ref = refs[3] if has_resid else refs[2]

        def finish(acc):
            r = acc * scale if scale != 1.0 else acc
            if has_resid:
                r = r + r_ref[...]
            o_ref[...] = r.astype(o_ref.dtype)

        part = lax.dot_general(a_ref[...].astype(BF), b_ref[...].astype(BF), (dims, ((), ())),
                               preferred_element_type=F32)
        if nk == 1:
            finish(part)
        else:
            acc_ref = refs[-1]
            k = pl.program_id(len(grid) - 1)

            @pl.when(k == 0)
            def _():
                acc_ref[...] = part

            @pl.when(k > 0)
            def _():
                acc_ref[...] += part

            @pl.when(k == nk - 1)
            def _():
                finish(acc_ref[...])

    ins = [a, b] + ([resid] if has_resid else [])
    return pl.pallas_call(
        body, name=name, grid=grid,
        in_specs=[pl.BlockSpec(blk, idx) for (_, blk, idx) in ins],
        out_specs=pl.BlockSpec(o[1], o[2]),
        out_shape=o[0],
        scratch_shapes=[pltpu.VMEM(acc_shape, F32)] if nk > 1 else [],
        compiler_params=_params(*(["parallel"] * (len(grid) - 1) + ["arbitrary"])),
    )(*[x[0] for x in ins])


NN = ((1,), (0,))
NT = ((1,), (1,))
TN = ((0,), (0,))


def _rmsnorm(x, gain):
    T, D = x.shape
    tr = _tile(T, 256)

    def body(x_ref, g_ref, o_ref):
        xv = x_ref[...]
        r = lax.rsqrt(jnp.mean(xv * xv, axis=-1, keepdims=True) + EPS)
        o_ref[...] = (xv * r * g_ref[...]).astype(BF)

    return pl.pallas_call(
        body, name="rmsnorm_fwd", grid=(T // tr,),
        in_specs=[pl.BlockSpec((tr, D), lambda i: (i, 0)), pl.BlockSpec((1, D), lambda i: (0, 0))],
        out_specs=pl.BlockSpec((tr, D), lambda i: (i, 0)),
        out_shape=jax.ShapeDtypeStruct((T, D), BF),
        compiler_params=_params("parallel"),
    )(x, gain)


def _rmsnorm_bwd(dh, x, gain, dres):
    T, D = x.shape
    tr = _tile(T, 256)

    def body(dh_ref, x_ref, g_ref, dres_ref, dx_ref, dg_ref):
        i = pl.program_id(0)
        xv = x_ref[...]
        r = lax.rsqrt(jnp.mean(xv * xv, axis=-1, keepdims=True) + EPS)
        xh = xv * r
        dy = dh_ref[...].astype(F32)
        dgp = jnp.sum(dy * xh, axis=0, keepdims=True)

        @pl.when(i == 0)
        def _():
            dg_ref[...] = dgp

        @pl.when(i > 0)
        def _():
            dg_ref[...] += dgp

        dxh = dy * g_ref[...]
        dx_ref[...] = dres_ref[...] + r * (dxh - xh * jnp.mean(dxh * xh, axis=-1, keepdims=True))

    return pl.pallas_call(
        body, name="rmsnorm_bwd", grid=(T // tr,),
        in_specs=[pl.BlockSpec((tr, D), lambda i: (i, 0)), pl.BlockSpec((tr, D), lambda i: (i, 0)),
                  pl.BlockSpec((1, D), lambda i: (0, 0)), pl.BlockSpec((tr, D), lambda i: (i, 0))],
        out_specs=[pl.BlockSpec((tr, D), lambda i: (i, 0)), pl.BlockSpec((1, D), lambda i: (0, 0))],
        out_shape=[jax.ShapeDtypeStruct((T, D), F32), jax.ShapeDtypeStruct((1, D), F32)],
        compiler_params=_params("arbitrary"),
    )(dh, x, gain, dres)


def _loss_head(x, gain, target):
    T, D = x.shape
    tr = _tile(T, 256)

    def body(x_ref, g_ref, t_ref, loss_ref, dx_ref, dg_ref):
        i = pl.program_id(0)
        xv = x_ref[...]
        g = g_ref[...]
        r = lax.rsqrt(jnp.mean(xv * xv, axis=-1, keepdims=True) + EPS)
        xh = xv * r
        err = xh * g - t_ref[...]
        lp = 0.5 * jnp.sum(jnp.mean(err * err, axis=-1, keepdims=True), axis=0, keepdims=True)
        dy = err * (1.0 / D)
        dgp = jnp.sum(dy * xh, axis=0, keepdims=True)

        @pl.when(i == 0)
        def _():
            loss_ref[...] = jnp.broadcast_to(lp, loss_ref.shape)
            dg_ref[...] = dgp

        @pl.when(i > 0)
        def _():
            loss_ref[...] += jnp.broadcast_to(lp, loss_ref.shape)
            dg_ref[...] += dgp

        dxh = dy * g
        dx_ref[...] = r * (dxh - xh * jnp.mean(dxh * xh, axis=-1, keepdims=True))

    return pl.pallas_call(
        body, name="loss_head", grid=(T // tr,),
        in_specs=[pl.BlockSpec((tr, D), lambda i: (i, 0)), pl.BlockSpec((1, D), lambda i: (0, 0)),
                  pl.BlockSpec((tr, D), lambda i: (i, 0))],
        out_specs=[pl.BlockSpec((1, LANES), lambda i: (0, 0)), pl.BlockSpec((tr, D), lambda i: (i, 0)),
                   pl.BlockSpec((1, D), lambda i: (0, 0))],
        out_shape=[jax.ShapeDtypeStruct((1, LANES), F32), jax.ShapeDtypeStruct((T, D), F32),
                   jax.ShapeDtypeStruct((1, D), F32)],
        compiler_params=_params("arbitrary"),
    )(x, gain, target)


def _group_slices(D):
    dc, da = D // 4, D // 2
    return [(0, dc), (dc, dc + da), (dc + da, D)]


def _groupnorm(yc, ya, yl, gain):
    T = yc.shape[0]
    D = yc.shape[1] + ya.shape[1] + yl.shape[1]
    tr = _tile(T, 256)
    sl = _group_slices(D)

    def body(yc_ref, ya_ref, yl_ref, g_ref, o_ref):
        for y_ref, (lo, hi) in zip((yc_ref, ya_ref, yl_ref), sl):
            y = y_ref[...]
            r = lax.rsqrt(jnp.mean(y * y, axis=-1, keepdims=True) + EPS)
            o_ref[:, lo:hi] = (y * r * g_ref[:, lo:hi]).astype(BF)

    return pl.pallas_call(
        body, name="groupnorm_fwd", grid=(T // tr,),
        in_specs=[pl.BlockSpec((tr, y.shape[1]), lambda i: (i, 0)) for y in (yc, ya, yl)]
        + [pl.BlockSpec((1, D), lambda i: (0, 0))],
        out_specs=pl.BlockSpec((tr, D), lambda i: (i, 0)),
        out_shape=jax.ShapeDtypeStruct((T, D), BF),
        compiler_params=_params("parallel"),
    )(yc, ya, yl, gain)


def _groupnorm_bwd(dyn, yc, ya, yl, gain):
    T, D = dyn.shape
    tr = _tile(T, 256)
    sl = _group_slices(D)

    def body(dyn_ref, yc_ref, ya_ref, yl_ref, g_ref, dc_ref, da_ref, dl_ref, dg_ref):
        i = pl.program_id(0)
        for y_ref, d_ref, (lo, hi) in zip((yc_ref, ya_ref, yl_ref), (dc_ref, da_ref, dl_ref), sl):
            y = y_ref[...]
            r = lax.rsqrt(jnp.mean(y * y, axis=-1, keepdims=True) + EPS)
            yh = y * r
            dy = dyn_ref[:, lo:hi]
            dgp = jnp.sum(dy * yh, axis=0, keepdims=True)

            @pl.when(i == 0)
            def _():
                dg_ref[:, lo:hi] = dgp

            @pl.when(i > 0)
            def _():
                dg_ref[:, lo:hi] += dgp

            dyh = dy * g_ref[:, lo:hi]
            d_ref[...] = r * (dyh - yh * jnp.mean(dyh * yh, axis=-1, keepdims=True))

    return pl.pallas_call(
        body, name="groupnorm_bwd", grid=(T // tr,),
        in_specs=[pl.BlockSpec((tr, D), lambda i: (i, 0))]
        + [pl.BlockSpec((tr, y.shape[1]), lambda i: (i, 0)) for y in (yc, ya, yl)]
        + [pl.BlockSpec((1, D), lambda i: (0, 0))],
        out_specs=[pl.BlockSpec((tr, y.shape[1]), lambda i: (i, 0)) for y in (yc, ya, yl)]
        + [pl.BlockSpec((1, D), lambda i: (0, 0))],
        out_shape=[jax.ShapeDtypeStruct(y.shape, F32) for y in (yc, ya, yl)] + [jax.ShapeDtypeStruct((1, D), F32)],
        compiler_params=_params("arbitrary"),
    )(dyn, yc, ya, yl, gain)


def _ffn_in(h, w3):
    T, D = h.shape
    tn = w3.shape[2]
    F = 4 * tn
    tm = _tile(T, 512)

    def body(h_ref, wg_ref, wu_ref, g_ref, u_ref, a_ref):
        hv = h_ref[...]
        g = jnp.dot(hv, wg_ref[...], preferred_element_type=F32)
        u = jnp.dot(hv, wu_ref[...], preferred_element_type=F32)
        g_ref[...] = g.astype(BF)
        u_ref[...] = u.astype(BF)
        a_ref[...] = (g * jax.nn.sigmoid(g) * u).astype(BF)

    out = jax.ShapeDtypeStruct((T, F), BF)
    return pl.pallas_call(
        body, name="ffn_in_swiglu", grid=(4, T // tm),
        in_specs=[pl.BlockSpec((tm, D), lambda j, i: (i, 0)),
                  pl.BlockSpec((None, D, tn), lambda j, i: (j, 0, 0)),
                  pl.BlockSpec((None, D, tn), lambda j, i: (j + 4, 0, 0))],
        out_specs=[pl.BlockSpec((tm, tn), lambda j, i: (i, j))] * 3,
        out_shape=[out, out, out],
        compiler_params=_params("parallel", "parallel"),
    )(h, w3, w3)


def _ffn_bwd_in(dx, wout, g, u):
    T, D = dx.shape
    F = wout.shape[0]
    tn = F // 4
    tm = _tile(T, 512)

    def body(dx_ref, w_ref, g_ref, u_ref, o_ref):
        da = 0.5 * lax.dot_general(dx_ref[...].astype(BF), w_ref[...], (NT, ((), ())), preferred_element_type=F32)
        gv = g_ref[...].astype(F32)
        s = jax.nn.sigmoid(gv)
        o_ref[0] = (da * u_ref[...].astype(F32) * (s * (1.0 + gv * (1.0 - s)))).astype(BF)
        o_ref[1] = (da * gv * s).astype(BF)

    return pl.pallas_call(
        body, name="ffn_bwd_swiglu", grid=(4, T // tm),
        in_specs=[pl.BlockSpec((tm, D), lambda j, i: (i, 0)), pl.BlockSpec((tn, D), lambda j, i: (j, 0)),
                  pl.BlockSpec((tm, tn), lambda j, i: (i, j)), pl.BlockSpec((tm, tn), lambda j, i: (i, j))],
        out_specs=pl.BlockSpec((2, tm, tn), lambda j, i: (0, i, j)),
        out_shape=jax.ShapeDtypeStruct((2, T, F), BF),
        compiler_params=_params("parallel", "parallel"),
    )(dx, wout, g, u)


def _ffn_forward(x, gain, w3, wout):
    T, D = x.shape
    F = wout.shape[0]
    h = _rmsnorm(x, gain)
    g, u, a = _ffn_in(h, w3)
    tm, tk = _tile(T, 512), F // 4
    xn = _mm("ffn_out", (T // tm, 1, F // tk),
             (a, (tm, tk), lambda i, j, k: (i, k)), (wout, (tk, D), lambda i, j, k: (k, 0)),
             (jax.ShapeDtypeStruct((T, D), F32), (tm, D), lambda i, j, k: (i, 0)), NN, scale=0.5,
             resid=(x, (tm, D), lambda i, j, k: (i, 0)))
    return xn, (x, h, g, u, a)


def _ffn_backward(dx, saved, gain, w3, wout):
    x, h, g, u, a = saved
    T, D = x.shape
    F = wout.shape[0]
    tn3 = F // 4
    dgu = _ffn_bwd_in(dx, wout, g, u)
    tk = _tile(T, 512)
    dwout = _mm("ffn_dwout", (F // tn3, 1, T // tk),
                (a, (tk, tn3), lambda i, j, k: (k, i)), (dx, (tk, D), lambda i, j, k: (k, 0)),
                (jax.ShapeDtypeStruct((F, D), BF), (tn3, D), lambda i, j, k: (i, 0)), TN, scale=0.5)
    tmd = _tile(D, 1024)
    dw3 = _mm("ffn_dwin", (8, D // tmd, T // tk),
              (h, (tk, tmd), lambda s, i, k: (k, i)), (dgu, (None, tk, tn3), lambda s, i, k: (s // 4, k, s % 4)),
              (jax.ShapeDtypeStruct((8, D, tn3), BF), (None, tmd, tn3), lambda s, i, k: (s, i, 0)), TN)
    tm = _tile(T, 512)
    dh = _mm("ffn_dh", (T // tm, 1, 8),
             (dgu, (None, tm, tn3), lambda i, j, k: (k // 4, i, k % 4)), (w3, (None, D, tn3), lambda i, j, k: (k, 0, 0)),
             (jax.ShapeDtypeStruct((T, D), F32), (tm, D), lambda i, j, k: (i, 0)), NT)
    dxn, dgain = _rmsnorm_bwd(dh, x, gain, dx)
    return dxn, dgain, dw3, dwout.reshape(N_DEV, F // N_DEV, D)


def _rows(shape):
    return lax.broadcasted_iota(jnp.int32, shape, 0)


def _down(x, s, fill, rows):
    return jnp.where(rows >= s, pltpu.roll(x, s, 0), fill)


def _up(x, s, fill, rows):
    T = x.shape[0]
    return jnp.where(rows < T - s, pltpu.roll(x, T - s, 0), fill)


def _scan_linear(a, b, rows, shift):
    T = a.shape[0]
    s = 1
    while s < T:
        b = a * shift(b, s, 0.0, rows) + b
        if 2 * s < T:
            a = a * shift(a, s, 1.0, rows)
        s *= 2
    return b


def _cumsum(c, rows, shift):
    T = c.shape[0]
    s = 1
    while s < T:
        c = c + shift(c, s, 0.0, rows)
        s *= 2
    return c


def _log1p_small(e):
    return jnp.where(e < 0.01, e * (1.0 - e * (0.5 - e * (1.0 / 3.0))), jnp.log(1.0 + e))


def _softplus(x):
    return jnp.maximum(x, 0.0) + _log1p_small(jnp.exp(-jnp.abs(x)))


def _one_minus_exp_neg(z):
    return jnp.where(z < 0.1, z * (1.0 - z * (0.5 - z * (1.0 / 6.0 - z * (1.0 / 24.0)))), 1.0 - jnp.exp(-z))


def _fgate_cum(f, b):
    T = f.shape[0]

    def body(f_ref, b_ref, o_ref):
        z = f_ref[...] + b_ref[...]
        o_ref[...] = _cumsum(-_softplus(-z), _rows(z.shape), _down)

    return pl.pallas_call(
        body, name="fgate_cumsum",
        out_shape=jax.ShapeDtypeStruct((T, LANES), F32),
        compiler_params=pltpu.CompilerParams(vmem_limit_bytes=VMEM_LIMIT_V7X),
    )(f, b)


def _fgate_cum_bwd(drow, dcol, f, b):
    T = f.shape[0]

    def body(dr_ref, dc_ref, f_ref, b_ref, df_ref, db_ref):
        z = f_ref[...] + b_ref[...]
        dlogf = _cumsum(dr_ref[...] - dc_ref[...], _rows(z.shape), _up)
        dz = dlogf * jax.nn.sigmoid(-z)
        df_ref[...] = dz.astype(BF)
        db_ref[...] = jnp.sum(dz, axis=0, keepdims=True)

    return pl.pallas_call(
        body, name="fgate_cumsum_bwd",
        out_shape=[jax.ShapeDtypeStruct((T, LANES), BF), jax.ShapeDtypeStruct((1, LANES), F32)],
        compiler_params=pltpu.CompilerParams(vmem_limit_bytes=VMEM_LIMIT_V7X),
    )(drow, dcol, f, b)


def _col(blk0):
    return lambda g: (0, blk0 + g)


def _conv_fwd(p, w, nb):
    T = p.shape[0]

    def body(b_ref, c_ref, v_ref, w_ref, o_ref):
        z = c_ref[...].astype(F32) * v_ref[...].astype(F32)
        rows = _rows(z.shape)
        conv = w_ref[2:3, :] * z + w_ref[1:2, :] * _down(z, 1, 0.0, rows) + w_ref[0:1, :] * _down(z, 2, 0.0, rows)
        o_ref[...] = b_ref[...].astype(F32) * conv

    return pl.pallas_call(
        body, name="conv_fwd", grid=(nb,),
        in_specs=[pl.BlockSpec((T, LANES), _col(0)), pl.BlockSpec((T, LANES), _col(nb)),
                  pl.BlockSpec((T, LANES), _col(2 * nb)), pl.BlockSpec((3, LANES), lambda g: (0, g))],
        out_specs=pl.BlockSpec((T, LANES), lambda g: (0, g)),
        out_shape=jax.ShapeDtypeStruct((T, nb * LANES), F32),
        compiler_params=_params("parallel"),
    )(p, p, p, w)


def _conv_bwd(dy, p, w, nb):
    T = p.shape[0]

    def body(dy_ref, b_ref, c_ref, v_ref, w_ref, db_ref, dc_ref, dv_ref, dw_ref):
        cv, vv = c_ref[...].astype(F32), v_ref[...].astype(F32)
        z = cv * vv
        rows = _rows(z.shape)
        z1, z2 = _down(z, 1, 0.0, rows), _down(z, 2, 0.0, rows)
        dyv = dy_ref[...]
        db_ref[...] = (dyv * (w_ref[2:3, :] * z + w_ref[1:2, :] * z1 + w_ref[0:1, :] * z2)).astype(BF)
        dconv = dyv * b_ref[...].astype(F32)
        dz = (w_ref[2:3, :] * dconv + w_ref[1:2, :] * _up(dconv, 1, 0.0, rows)
              + w_ref[0:1, :] * _up(dconv, 2, 0.0, rows))
        dc_ref[...] = (dz * vv).astype(BF)
        dv_ref[...] = (dz * cv).astype(BF)
        dw_ref[0:1, :] = jnp.sum(dconv * z2, axis=0, keepdims=True)
        dw_ref[1:2, :] = jnp.sum(dconv * z1, axis=0, keepdims=True)
        dw_ref[2:3, :] = jnp.sum(dconv * z, axis=0, keepdims=True)

    return pl.pallas_call(
        body, name="conv_bwd", grid=(nb,),
        in_specs=[pl.BlockSpec((T, LANES), lambda g: (0, g)), pl.BlockSpec((T, LANES), _col(0)),
                  pl.BlockSpec((T, LANES), _col(nb)), pl.BlockSpec((T, LANES), _col(2 * nb)),
                  pl.BlockSpec((3, LANES), lambda g: (0, g))],
        out_specs=[pl.BlockSpec((T, LANES), lambda g: (0, g))] * 3 + [pl.BlockSpec((3, LANES), lambda g: (0, g))],
        out_shape=[jax.ShapeDtypeStruct((T, nb * LANES), BF)] * 3 + [jax.ShapeDtypeStruct((3, nb * LANES), F32)],
        compiler_params=_params("parallel"),
    )(dy, p, p, p, w)


def _gelu(x):
    t = jnp.tanh(GELU_C * (x + GELU_K * x * x * x))
    return 0.5 * x * (1.0 + t), t


def _lru_common(x, cw_ref, cb_ref, wa_ref, ba_ref, wx_ref, bx_ref, lam_ref, rows):
    xr = (cb_ref[...] + cw_ref[3:4, :] * x + cw_ref[2:3, :] * _down(x, 1, 0.0, rows)
          + cw_ref[1:2, :] * _down(x, 2, 0.0, rows) + cw_ref[0:1, :] * _down(x, 3, 0.0, rows))
    xrb = xr.astype(BF)
    r = jax.nn.sigmoid(jnp.dot(xrb, wa_ref[...].astype(BF), preferred_element_type=F32) + ba_ref[...])
    i = jax.nn.sigmoid(jnp.dot(xrb, wx_ref[...].astype(BF), preferred_element_type=F32) + bx_ref[...])
    sp = _softplus(-lam_ref[...])
    log_a = -LRU_C * r * sp
    a = jnp.exp(log_a)
    m = jnp.sqrt(_one_minus_exp_neg(-2.0 * log_a))
    return xr, xrb, r, i, sp, a, m


def _lru_specs(T, nb, gate_blk0, x_blk0):
    vec = pl.BlockSpec((1, LANES), lambda g: (0, g))
    mat = pl.BlockSpec((None, LANES, LANES), lambda g: (g, 0, 0))
    return [pl.BlockSpec((T, LANES), _col(gate_blk0)), pl.BlockSpec((T, LANES), _col(x_blk0)),
            pl.BlockSpec((4, LANES), lambda g: (0, g)), vec, mat, vec, mat, vec, vec]


def _lru_fwd(p, cw, cb, wa, ba, wx, bx, lam, nb, gate_blk0):
    T = p.shape[0]

    def body(gate_ref, x_ref, cw_ref, cb_ref, wa_ref, ba_ref, wx_ref, bx_ref, lam_ref, y_ref, h_ref):
        x = x_ref[...].astype(F32)
        rows = _rows(x.shape)
        xr, _, _, i, _, a, m = _lru_common(x, cw_ref, cb_ref, wa_ref, ba_ref, wx_ref, bx_ref, lam_ref, rows)
        h = _scan_linear(a, m * (i * xr), rows, _down)
        h_ref[...] = h
        y_ref[...] = _gelu(gate_ref[...].astype(F32))[0] * h

    out = jax.ShapeDtypeStruct((T, nb * LANES), F32)
    return pl.pallas_call(
        body, name="lru_fwd", grid=(nb,),
        in_specs=_lru_specs(T, nb, gate_blk0, gate_blk0 + nb),
        out_specs=[pl.BlockSpec((T, LANES), lambda g: (0, g))] * 2,
        out_shape=[out, out],
        compiler_params=_params("parallel"),
    )(p, p, cw, cb, wa, ba, wx, bx, lam)


def _lru_bwd(dy, hs, p, cw, cb, wa, ba, wx, bx, lam, nb, gate_blk0):
    T = p.shape[0]

    def body(dy_ref, hs_ref, gate_ref, x_ref, cw_ref, cb_ref, wa_ref, ba_ref, wx_ref, bx_ref, lam_ref,
             dgate_ref, dx_ref, dcw_ref, dcb_ref, dwa_ref, dba_ref, dwx_ref, dbx_ref, dlam_ref):
        x = x_ref[...].astype(F32)
        rows = _rows(x.shape)
        xr, xrb, r, i, sp, a, m = _lru_common(x, cw_ref, cb_ref, wa_ref, ba_ref, wx_ref, bx_ref, lam_ref, rows)
        gate = gate_ref[...].astype(F32)
        gl, t = _gelu(gate)
        h = hs_ref[...]
        dyv = dy_ref[...]
        dgelu = 0.5 * (1.0 + t) + 0.5 * gate * (1.0 - t * t) * GELU_C * (1.0 + 3.0 * GELU_K * gate * gate)
        dgate_ref[...] = (dyv * h * dgelu).astype(BF)
        lam_adj = _scan_linear(_up(a, 1, 0.0, rows), dyv * gl, rows, _up)
        da = lam_adj * _down(h, 1, 0.0, rows)
        ix = i * xr
        dix = lam_adj * m
        dm = lam_adj * ix
        dlog_a = da * a - dm * (a * a) / jnp.maximum(m, 1e-30)
        dr = dlog_a * (-LRU_C * sp)
        dsp = jnp.sum(dlog_a * (-LRU_C * r), axis=0, keepdims=True)
        dlam_ref[...] = -dsp * jax.nn.sigmoid(-lam_ref[...])
        dpa = dr * r * (1.0 - r)
        dpx = dix * xr * i * (1.0 - i)
        dpab, dpxb = dpa.astype(BF), dpx.astype(BF)
        dxr = (dix * i
               + lax.dot_general(dpab, wa_ref[...].astype(BF), (NT, ((), ())), preferred_element_type=F32)
               + lax.dot_general(dpxb, wx_ref[...].astype(BF), (NT, ((), ())), preferred_element_type=F32))
        dwa_ref[...] = lax.dot_general(xrb, dpab, (TN, ((), ())), preferred_element_type=F32)
        dwx_ref[...] = lax.dot_general(xrb, dpxb, (TN, ((), ())), preferred_element_type=F32)
        dba_ref[...] = jnp.sum(dpa, axis=0, keepdims=True)
        dbx_ref[...] = jnp.sum(dpx, axis=0, keepdims=True)
        dcb_ref[...] = jnp.sum(dxr, axis=0, keepdims=True)
        dx_ref[...] = (cw_ref[3:4, :] * dxr + cw_ref[2:3, :] * _up(dxr, 1, 0.0, rows)
                       + cw_ref[1:2, :] * _up(dxr, 2, 0.0, rows) + cw_ref[0:1, :] * _up(dxr, 3, 0.0, rows)).astype(BF)
        for k in range(4):
            xs = x if k == 3 else _down(x, 3 - k, 0.0, rows)
            dcw_ref[k:k + 1, :] = jnp.sum(dxr * xs, axis=0, keepdims=True)

    C = nb * LANES
    seq = jax.ShapeDtypeStruct((T, C), BF)
    vec = jax.ShapeDtypeStruct((1, C), F32)
    mat = jax.ShapeDtypeStruct((nb, LANES, LANES), F32)
    vspec = pl.BlockSpec((1, LANES), lambda g: (0, g))
    mspec = pl.BlockSpec((None, LANES, LANES), lambda g: (g, 0, 0))
    sspec = pl.BlockSpec((T, LANES), lambda g: (0, g))
    return pl.pallas_call(
        body, name="lru_bwd", grid=(nb,),
        in_specs=[sspec, sspec] + _lru_specs(T, nb, gate_blk0, gate_blk0 + nb),
        out_specs=[sspec, sspec, pl.BlockSpec((4, LANES), lambda g: (0, g)), vspec, mspec, vspec, mspec, vspec, vspec],
        out_shape=[seq, seq, jax.ShapeDtypeStruct((4, C), F32), vec, mat, vec, mat, vec, vec],
        compiler_params=_params("parallel"),
    )(dy, hs, p, p, cw, cb, wa, ba, wx, bx, lam)


def _attn_fwd(p, cq, ck, nh, q_blk0):
    T = p.shape[0]
    tq = _tile(T, 512)
    nq = T // tq
    scale = LANES ** -0.5

    def body(q_ref, k_ref, v_ref, cq_ref, ck_ref, o_ref, lse_ref, m_ref, l_ref, acc_ref):
        i, j = pl.program_id(1), pl.program_id(2)

        @pl.when(j == 0)
        def _():
            m_ref[...] = jnp.full(m_ref.shape, -jnp.inf, F32)
            l_ref[...] = jnp.zeros(l_ref.shape, F32)
            acc_ref[...] = jnp.zeros(acc_ref.shape, F32)

        @pl.when(j <= i)
        def _():
            s = lax.dot_general(q_ref[...], k_ref[...], (NT, ((), ())), preferred_element_type=F32) * scale
            s = s + cq_ref[...] - ck_ref[...]
            qpos = i * tq + lax.broadcasted_iota(jnp.int32, s.shape, 0)
            kpos = j * tq + lax.broadcasted_iota(jnp.int32, s.shape, 1)
            s = jnp.where(kpos <= qpos, s, -jnp.inf)
            m_new = jnp.maximum(m_ref[...], jnp.max(s, axis=-1, keepdims=True))
            alpha = jnp.exp(m_ref[...] - m_new)
            pr = jnp.exp(s - m_new)
            l_ref[...] = alpha * l_ref[...] + jnp.sum(pr, axis=-1, keepdims=True)
            acc_ref[...] = alpha * acc_ref[...] + jnp.dot(pr.astype(BF), v_ref[...], preferred_element_type=F32)
            m_ref[...] = m_new

        @pl.when(j == nq - 1)
        def _():
            o_ref[...] = acc_ref[...] / l_ref[...]
            lse_ref[...] = m_ref[...] + jnp.log(l_ref[...])

    def kv(off):
        return pl.BlockSpec((tq, LANES), lambda h, i, j: (jnp.minimum(j, i), q_blk0 + off * nh + h))

    return pl.pallas_call(
        body, name="attn_fwd", grid=(nh, nq, nq),
        in_specs=[pl.BlockSpec((tq, LANES), lambda h, i, j: (i, q_blk0 + h)), kv(1), kv(2),
                  pl.BlockSpec((None, tq, 1), lambda h, i, j: (h, i, 0)),
                  pl.BlockSpec((None, 1, tq), lambda h, i, j: (h, 0, jnp.minimum(j, i)))],
        out_specs=[pl.BlockSpec((tq, LANES), lambda h, i, j: (i, h)),
                   pl.BlockSpec((None, tq, 1), lambda h, i, j: (h, i, 0))],
        out_shape=[jax.ShapeDtypeStruct((T, nh * LANES), F32), jax.ShapeDtypeStruct((nh, T, 1), F32)],
        scratch_shapes=[pltpu.VMEM((tq, 1), F32), pltpu.VMEM((tq, 1), F32), pltpu.VMEM((tq, LANES), F32)],
        compiler_params=_params("parallel", "parallel", "arbitrary"),
    )(p, p, p, cq, ck)


def _attn_bwd_q(p, cq, ck, lse, do, o, nh, q_blk0):
    T = p.shape[0]
    tq = _tile(T, 512)
    nq = T // tq
    scale = LANES ** -0.5

    def body(q_ref, k_ref, v_ref, cq_ref, ck_ref, lse_ref, do_ref, o_ref, dq_ref, dl_ref, dr_ref, acc_ref):
        i, j = pl.program_id(1), pl.program_id(2)

        @pl.when(j == 0)
        def _():
            dl_ref[...] = jnp.sum(do_ref[...] * o_ref[...], axis=-1, keepdims=True)
            dr_ref[...] = jnp.zeros(dr_ref.shape, F32)
            acc_ref[...] = jnp.zeros(acc_ref.shape, F32)

        @pl.when(j <= i)
        def _():
            s = lax.dot_general(q_ref[...], k_ref[...], (NT, ((), ())), preferred_element_type=F32) * scale
            s = s + cq_ref[...] - ck_ref[...]
            qpos = i * tq + lax.broadcasted_iota(jnp.int32, s.shape, 0)
            kpos = j * tq + lax.broadcasted_iota(jnp.int32, s.shape, 1)
            pr = jnp.where(kpos <= qpos, jnp.exp(s - lse_ref[...]), 0.0)
            dp = lax.dot_general(do_ref[...].astype(BF), v_ref[...], (NT, ((), ())), preferred_element_type=F32)
            ds = pr * (dp - dl_ref[...])
            dr_ref[...] += jnp.sum(ds, axis=-1, keepdims=True)
            acc_ref[...] += jnp.dot(ds.astype(BF), k_ref[...], preferred_element_type=F32)

        @pl.when(j == nq - 1)
        def _():
            dq_ref[...] = (acc_ref[...] * scale).astype(BF)

    def kv(off):
        return pl.BlockSpec((tq, LANES), lambda h, i, j: (jnp.minimum(j, i), q_blk0 + off * nh + h))

    col = pl.BlockSpec((None, tq, 1), lambda h, i, j: (h, i, 0))
    head = pl.BlockSpec((tq, LANES), lambda h, i, j: (i, h))
    return pl.pallas_call(
        body, name="attn_bwd_q", grid=(nh, nq, nq),
        in_specs=[pl.BlockSpec((tq, LANES), lambda h, i, j: (i, q_blk0 + h)), kv(1), kv(2), col,
                  pl.BlockSpec((None, 1, tq), lambda h, i, j: (h, 0, jnp.minimum(j, i))), col, head, head],
        out_specs=[head, col, col],
        out_shape=[jax.ShapeDtypeStruct((T, nh * LANES), BF), jax.ShapeDtypeStruct((nh, T, 1), F32),
                   jax.ShapeDtypeStruct((nh, T, 1), F32)],
        scratch_shapes=[pltpu.VMEM((tq, LANES), F32)],
        compiler_params=_params("parallel", "parallel", "arbitrary"),
    )(p, p, p, cq, ck, lse, do, o)


def _attn_bwd_kv(p, cq_row, ck_col, lse_row, delta_row, do, nh, q_blk0):
    T = p.shape[0]
    tk = _tile(T, 512)
    nk = T // tk
    scale = LANES ** -0.5

    def body(q_ref, k_ref, v_ref, cq_ref, ck_ref, lse_ref, dl_ref, do_ref, dk_ref, dv_ref, dc_ref,
             dk_acc, dv_acc, dc_acc):
        j, i = pl.program_id(1), pl.program_id(2)

        @pl.when(i == 0)
        def _():
            dk_acc[...] = jnp.zeros(dk_acc.shape, F32)
            dv_acc[...] = jnp.zeros(dv_acc.shape, F32)
            dc_acc[...] = jnp.zeros(dc_acc.shape, F32)

        @pl.when(i >= j)
        def _():
            st = lax.dot_general(k_ref[...], q_ref[...], (NT, ((), ())), preferred_element_type=F32) * scale
            st = st + cq_ref[...] - ck_ref[...]
            kpos = j * tk + lax.broadcasted_iota(jnp.int32, st.shape, 0)
            qpos = i * tk + lax.broadcasted_iota(jnp.int32, st.shape, 1)
            pt = jnp.where(kpos <= qpos, jnp.exp(st - lse_ref[...]), 0.0)
            dob = do_ref[...].astype(BF)
            dv_acc[...] += jnp.dot(pt.astype(BF), dob, preferred_element_type=F32)
            dpt = lax.dot_general(v_ref[...], dob, (NT, ((), ())), preferred_element_type=F32)
            dst = pt * (dpt - dl_ref[...])
            dk_acc[...] += jnp.dot(dst.astype(BF), q_ref[...], preferred_element_type=F32)
            dc_acc[...] += jnp.sum(dst, axis=-1, keepdims=True)

        @pl.when(i == nk - 1)
        def _():
            dk_ref[...] = (dk_acc[...] * scale).astype(BF)
            dv_ref[...] = dv_acc[...].astype(BF)
            dc_ref[...] = dc_acc[...]

    def qside(blk):
        return pl.BlockSpec((tk, LANES), lambda h, j, i: (jnp.maximum(i, j), blk + h))

    def kside(off):
        return pl.BlockSpec((tk, LANES), lambda h, j, i: (j, q_blk0 + off * nh + h))

    row = pl.BlockSpec((None, 1, tk), lambda h, j, i: (h, 0, jnp.maximum(i, j)))
    col = pl.BlockSpec((None, tk, 1), lambda h, j, i: (h, j, 0))
    head = pl.BlockSpec((tk, LANES), lambda h, j, i: (j, h))
    return pl.pallas_call(
        body, name="attn_bwd_kv", grid=(nh, nk, nk),
        in_specs=[qside(q_blk0), kside(1), kside(2), row, col, row, row, qside(0)],
        out_specs=[head, head, col],
        out_shape=[jax.ShapeDtypeStruct((T, nh * LANES), BF), jax.ShapeDtypeStruct((T, nh * LANES), BF),
                   jax.ShapeDtypeStruct((nh, T, 1), F32)],
        scratch_shapes=[pltpu.VMEM((tk, LANES), F32), pltpu.VMEM((tk, LANES), F32), pltpu.VMEM((tk, 1), F32)],
        compiler_params=_params("parallel", "parallel", "arbitrary"),
    )(p, p, p, cq_row, ck_col, lse_row, delta_row, do)


def _mixer_dims(D):
    dc, da, dl = D // 4, D // 2, D // 4
    nh = da // LANES
    n_main = 3 * dc + 3 * da + 2 * dl
    return dc, da, dl, nh, n_main


def _pad_mix_w_in(wfull):
    D = wfull.shape[0]
    dc, da, dl, nh, n_main = _mixer_dims(D)
    a = 3 * dc + 3 * da
    return jnp.concatenate([wfull[:, :a], wfull[:, a + nh:], wfull[:, a:a + nh],
                            jnp.zeros((D, LANES - nh), wfull.dtype)], axis=1)


def _unpad_mix_w_in(wp):
    D = wp.shape[0]
    dc, da, dl, nh, n_main = _mixer_dims(D)
    a = 3 * dc + 3 * da
    return jnp.concatenate([wp[:, :a], wp[:, n_main:n_main + nh], wp[:, a:n_main]], axis=1)


def _head_cols(c, nh):
    t = jnp.transpose(c[:, :nh])
    return t[:, :, None], t[:, None, :]


def _mixer_forward(x, gain, wp, wo, sp):
    T, D = x.shape
    dc, da, dl, nh, n_main = _mixer_dims(D)
    nbc, nbl = dc // LANES, dl // LANES
    h = _rmsnorm(x, gain)
    tm = _tile(T, 512)
    tn = n_main // 4
    p = _mm("mix_in", (4, T // tm, 1),
            (h, (tm, D), lambda j, i, k: (i, 0)), (wp, (D, tn), lambda j, i, k: (0, j)),
            (jax.ShapeDtypeStruct((T, n_main), BF), (tm, tn), lambda j, i, k: (i, j)), NN)
    f = _mm("mix_in_fgate", (T // tm, 1, 1),
            (h, (tm, D), lambda i, j, k: (i, 0)), (wp, (D, LANES), lambda i, j, k: (0, n_main // LANES)),
            (jax.ShapeDtypeStruct((T, LANES), F32), (tm, LANES), lambda i, j, k: (i, 0)), NN)
    cum = _fgate_cum(f, sp["fgate_b"])
    cq, ck = _head_cols(cum, nh)
    yc = _conv_fwd(p, sp["conv_w"], nbc)
    q_blk0 = 3 * nbc
    ya, lse = _attn_fwd(p, cq, ck, nh, q_blk0)
    gate_blk0 = q_blk0 + 3 * nh
    yl, hs = _lru_fwd(p, sp["lru_conv_w"], sp["lru_conv_b"], sp["lru_w_a"], sp["lru_b_a"], sp["lru_w_x"],
                      sp["lru_b_x"], sp["lru_lambda"], nbl, gate_blk0)
    yn = _groupnorm(yc, ya, yl, sp["mix_out_norm"])
    xn = _mm("mix_out", (T // tm, 1, 1),
             (yn, (tm, D), lambda i, j, k: (i, 0)), (wo, (D, D), lambda i, j, k: (0, 0)),
             (jax.ShapeDtypeStruct((T, D), F32), (tm, D), lambda i, j, k: (i, 0)), NN,
             resid=(x, (tm, D), lambda i, j, k: (i, 0)))
    return xn, (x, h, p, f, cq, ck, yc, ya, lse, yl, hs, yn)


def _mixer_backward(dx, saved, gain, wp, wo, sp):
    x, h, p, f, cq, ck, yc, ya, lse, yl, hs, yn = saved
    T, D = x.shape
    dc, da, dl, nh, n_main = _mixer_dims(D)
    nbc, nbl = dc // LANES, dl // LANES
    q_blk0 = 3 * nbc
    gate_blk0 = q_blk0 + 3 * nh
    tm, tk = _tile(T, 512), _tile(T, 512)
    tmd = _tile(D, 1024)
    dyn = _mm("mix_out_dy", (T // tm, 1, 1),
              (dx, (tm, D), lambda i, j, k: (i, 0)), (wo, (D, D), lambda i, j, k: (0, 0)),
              (jax.ShapeDtypeStruct((T, D), F32), (tm, D), lambda i, j, k: (i, 0)), NT)
    dwo = _mm("mix_out_dw", (D // tmd, 1, T // tk),
              (yn, (tk, tmd), lambda i, j, k: (k, i)), (dx, (tk, D), lambda i, j, k: (k, 0)),
              (jax.ShapeDtypeStruct((D, D), BF), (tmd, D), lambda i, j, k: (i, 0)), TN)
    dyc, dya, dyl, dgn = _groupnorm_bwd(dyn, yc, ya, yl, sp["mix_out_norm"])
    dcb, dcc, dcv, dconv_w = _conv_bwd(dyc, p, sp["conv_w"], nbc)
    dq, delta, drow = _attn_bwd_q(p, cq, ck, lse, dya, ya, nh, q_blk0)
    as_row = lambda t: t.reshape(nh, 1, T)
    dk, dv, dcol = _attn_bwd_kv(p, ck, cq, as_row(lse), as_row(delta), dya, nh, q_blk0)
    lanes = lambda t: jnp.pad(jnp.transpose(t.reshape(nh, T)), ((0, 0), (0, LANES - nh)))
    df, dfb = _fgate_cum_bwd(lanes(drow), lanes(dcol), f, sp["fgate_b"])
    dgate, dlx, dcw, dcbias, dwa, dba, dwx, dbx, dlam = _lru_bwd(
        dyl, hs, p, sp["lru_conv_w"], sp["lru_conv_b"], sp["lru_w_a"], sp["lru_b_a"], sp["lru_w_x"],
        sp["lru_b_x"], sp["lru_lambda"], nbl, gate_blk0)
    dp = jnp.concatenate([dcb, dcc, dcv, dq, dk, dv, dgate, dlx, df], axis=1)
    n_pad = n_main + LANES
    tn = _tile(n_pad, 1152) if n_pad % 1152 == 0 else LANES
    dwp = _mm("mix_in_dw", (n_pad // tn, D // tmd, T // tk),
              (h, (tk, tmd), lambda j, i, k: (k, i)), (dp, (tk, tn), lambda j, i, k: (k, j)),
              (jax.ShapeDtypeStruct((D, n_pad), BF), (tmd, tn), lambda j, i, k: (i, j)), TN)
    dh = _mm("mix_in_dh", (T // tm, 1, n_pad // tn),
             (dp, (tm, tn), lambda i, j, k: (i, k)), (wp, (D, tn), lambda i, j, k: (0, k)),
             (jax.ShapeDtypeStruct((T, D), F32), (tm, D), lambda i, j, k: (i, 0)), NT)
    dxn, dgain = _rmsnorm_bwd(dh, x, gain, dx)
    small = {"norm_mix": dgain, "mix_out_norm": dgn, "conv_w": dconv_w, "fgate_b": dfb[:, :nh],
             "lru_conv_w": dcw, "lru_conv_b": dcbias, "lru_w_a": dwa, "lru_b_a": dba, "lru_w_x": dwx,
             "lru_b_x": dbx, "lru_lambda": dlam}
    return dxn, dwp, dwo, small


def _position():
    x, y, c = lax.axis_index("x"), lax.axis_index("y"), lax.axis_index("c")
    return x, y, c, [(1 - x, y), (x, 1 - y), (1 - x, 1 - y)]


def _all_gather(name, xs):
    n = len(xs)

    def body(*refs):
        x_refs, o_refs = refs[:n], refs[n:2 * n]
        send_sems, recv_sems, local_sems = refs[2 * n:]
        x, y, c, chips = _position()
        me, sibling = (x, y, c), (x, y, 1 - c)

        def copy(t, k, block, to, src=None):
            dst = o_refs[t].at[4 * block[0] + 2 * block[1] + block[2]]
            return pltpu.make_async_remote_copy(
                src_ref=dst if src is None else src, dst_ref=dst, send_sem=send_sems.at[t, k],
                recv_sem=recv_sems.at[t, k], device_id=to, device_id_type=MESH)

        mine = [pltpu.make_async_copy(x_refs[t], o_refs[t].at[4 * x + 2 * y + c], local_sems.at[t]) for t in range(n)]
        for cp in mine:
            cp.start()
        first = []
        for t in range(n):
            first.append(copy(t, 0, me, sibling, src=x_refs[t]))
            first += [copy(t, 1 + j, me, (*chip, c), src=x_refs[t]) for j, chip in enumerate(chips)]
        for cp in first:
            cp.start()
        passed = []
        for j, chip in enumerate(chips):
            for t in range(n):
                copy(t, 1 + j, (*chip, c), me).wait_recv()
                passed.append(copy(t, 4 + j, (*chip, c), sibling))
                passed[-1].start()
        for t in range(n):
            copy(t, 0, sibling, me).wait_recv()
            for j, chip in enumerate(chips):
                copy(t, 4 + j, (*chip, 1 - c), me).wait_recv()
        for cp in first + passed:
            cp.wait_send()
        for cp in mine:
            cp.wait()

    return pl.pallas_call(
        body, name=name,
        in_specs=[_any()] * n, out_specs=[_any()] * n,
        out_shape=[jax.ShapeDtypeStruct((N_DEV,) + v.shape, v.dtype) for v in xs],
        scratch_shapes=[pltpu.SemaphoreType.DMA((n, 7)), pltpu.SemaphoreType.DMA((n, 7)),
                        pltpu.SemaphoreType.DMA((n,))],
    )(*xs)


def _exchange_sibling(name, gs):
    n = len(gs)
    g4 = [g.reshape((4, 2) + g.shape[1:]) for g in gs]

    def body(*refs):
        g_refs, o_refs = refs[:n], refs[n:2 * n]
        send_sems, recv_sems = refs[2 * n:]
        x, y, c, _ = _position()
        copies = [pltpu.make_async_remote_copy(
            src_ref=g_refs[t].at[:, pl.ds(1 - c, 1)], dst_ref=o_refs[t], send_sem=send_sems.at[t],
            recv_sem=recv_sems.at[t], device_id=(x, y, 1 - c), device_id_type=MESH) for t in range(n)]
        for cp in copies:
            cp.start()
        for cp in copies:
            cp.wait()

    return pl.pallas_call(
        body, name=name,
        in_specs=[_any()] * n, out_specs=[_any()] * n,
        out_shape=[jax.ShapeDtypeStruct((4, 1) + g.shape[2:], g.dtype) for g in g4],
        scratch_shapes=[pltpu.SemaphoreType.DMA((n,)), pltpu.SemaphoreType.DMA((n,))],
    )(*g4), g4


def _add_sibling(g4, recv, c_idx):
    _, _, R, C = g4.shape
    tr = _tile(R, 256)

    def body(c_ref, g_ref, r_ref, o_ref):
        o_ref[...] = (g_ref[...].astype(F32) + r_ref[...].astype(F32)).astype(o_ref.dtype)

    return pl.pallas_call(
        body, name="rs_add_sibling",
        grid_spec=pltpu.PrefetchScalarGridSpec(
            num_scalar_prefetch=1, grid=(4, R // tr),
            in_specs=[pl.BlockSpec((None, None, tr, C), lambda q, i, c_ref: (q, c_ref[0], i, 0)),
                      pl.BlockSpec((None, None, tr, C), lambda q, i, c_ref: (q, 0, i, 0))],
            out_specs=pl.BlockSpec((None, tr, C), lambda q, i, c_ref: (q, i, 0))),
        out_shape=jax.ShapeDtypeStruct((4, R, C), g4.dtype),
        compiler_params=_params("parallel", "parallel"),
    )(c_idx, g4, recv)


def _exchange_chips(name, ps):
    n = len(ps)

    def body(*refs):
        p_refs, o_refs = refs[:n], refs[n:2 * n]
        send_sems, recv_sems, local_sems = refs[2 * n:]
        x, y, c, chips = _position()
        mine = [pltpu.make_async_copy(p_refs[t].at[2 * x + y], o_refs[t].at[3], local_sems.at[t]) for t in range(n)]
        for cp in mine:
            cp.start()
        copies = [pltpu.make_async_remote_copy(
            src_ref=p_refs[t].at[2 * chip[0] + chip[1]], dst_ref=o_refs[t].at[k], send_sem=send_sems.at[t, k],
            recv_sem=recv_sems.at[t, k], device_id=(*chip, c), device_id_type=MESH)
            for t in range(n) for k, chip in enumerate(chips)]
        for cp in copies:
            cp.start()
        for cp in copies:
            cp.wait()
        for cp in mine:
            cp.wait()

    return pl.pallas_call(
        body, name=name,
        in_specs=[_any()] * n, out_specs=[_any()] * n,
        out_shape=[jax.ShapeDtypeStruct(v.shape, v.dtype) for v in ps],
        scratch_shapes=[pltpu.SemaphoreType.DMA((n, 3)), pltpu.SemaphoreType.DMA((n, 3)),
                        pltpu.SemaphoreType.DMA((n,))],
    )(*ps)


def _reduce_scatter(tag, gs, c_idx):
    recv, g4 = _exchange_sibling("rs_sibling_" + tag, gs)
    ps = [_add_sibling(g, r, c_idx) for g, r in zip(g4, recv)]
    return _exchange_chips("rs_chips_" + tag, ps)


def _sum_devices(parts):
    _, R, C = parts.shape
    tr = _tile(R, 512)

    def body(p_ref, o_ref):
        acc = p_ref[0]
        for d in range(1, N_DEV):
            acc = acc + p_ref[d]
        o_ref[...] = acc

    return pl.pallas_call(
        body, name="sum_devices", grid=(R // tr,),
        in_specs=[pl.BlockSpec((N_DEV, tr, C), lambda i: (0, i, 0))],
        out_specs=pl.BlockSpec((tr, C), lambda i: (i, 0)),
        out_shape=jax.ShapeDtypeStruct((R, C), F32),
        compiler_params=_params("parallel"),
    )(parts)


def _adam_math(w, g, m, v):
    m = ADAM_B1 * m + (1.0 - ADAM_B1) * g
    v = ADAM_B2 * v + (1.0 - ADAM_B2) * (g * g)
    m_hat = m / (1.0 - ADAM_B1 ** ADAM_STEP)
    v_hat = v / (1.0 - ADAM_B2 ** ADAM_STEP)
    return -ADAM_LR * (m_hat / (jnp.sqrt(v_hat) + ADAM_EPS) + ADAM_WD * w), m, v


def _adam_layer(name, layer, w, m, v, parts, prev):
    L, R, C = w.shape
    tr = _tile(R, 128)
    slab = pl.BlockSpec((None, tr, C), lambda i: (layer, i, 0))

    def body(w_ref, m_ref, v_ref, p_ref, *rest):
        g_ref, d_ref, nm_ref, nv_ref = rest[-4:]
        g = p_ref[0].astype(F32)
        for k in range(1, 4):
            g = g + p_ref[k].astype(F32)
        d, nm, nv = _adam_math(w_ref[...], g, m_ref[...], v_ref[...])
        g_ref[...] = g
        d_ref[...] = d
        nm_ref[...] = nm
        nv_ref[...] = nv

    n_prev = 0 if prev is None else 4
    out = jax.ShapeDtypeStruct((L, R, C), F32)
    return pl.pallas_call(
        body, name=name, grid=(R // tr,),
        in_specs=[slab, slab, slab, pl.BlockSpec((4, tr, C), lambda i: (0, i, 0))] + [_any()] * n_prev,
        out_specs=[slab] * 4,
        out_shape=[out] * 4,
        input_output_aliases={4 + k: k for k in range(n_prev)},
        compiler_params=_params("parallel"),
    )(w, m, v, parts, *(prev or ()))


def _adam_small(w, g, m, v):
    R, C = w.shape
    tr = _tile(R, 512)
    spec = pl.BlockSpec((tr, C), lambda i: (i, 0))

    def body(w_ref, g_ref, m_ref, v_ref, d_ref, nm_ref, nv_ref):
        d, nm, nv = _adam_math(w_ref[...], g_ref[...], m_ref[...], v_ref[...])
        d_ref[...] = d
        nm_ref[...] = nm
        nv_ref[...] = nv

    out = jax.ShapeDtypeStruct((R, C), F32)
    return pl.pallas_call(
        body, name="adam_small", grid=(R // tr,),
        in_specs=[spec] * 4, out_specs=[spec] * 3, out_shape=[out] * 3,
        compiler_params=_params("parallel"),
    )(w, g, m, v)


def _pack(arrays, row_multiple=512):
    flat = jnp.concatenate([a.reshape(-1).astype(F32) for a in arrays])
    per = row_multiple * LANES
    total = -(-flat.shape[0] // per) * per
    return jnp.pad(flat, (0, total - flat.shape[0])).reshape(total // LANES, LANES)


def _unpack(packed, shapes):
    flat = packed.reshape(-1)
    out, off = [], 0
    for s in shapes:
        n = math.prod(s)
        out.append(flat[off:off + n].reshape(s))
        off += n
    return out


BIG = ["ffn1_w_in", "ffn1_w_out", "mix_w_in", "mix_w_out", "ffn2_w_in", "ffn2_w_out"]
SHARDED_SMALL = ["conv_w", "lru_conv_w"]
WEIGHTS = ["norm_ffn1", "ffn1_w_in", "ffn1_w_out", "norm_mix", "mix_w_in", "conv_w", "fgate_b", "lru_conv_w",
           "lru_conv_b", "lru_w_a", "lru_b_a", "lru_w_x", "lru_b_x", "lru_lambda", "mix_out_norm", "mix_w_out",
           "norm_ffn2", "ffn2_w_in", "ffn2_w_out", "final_norm"]
REPLICATED = [n for n in WEIGHTS if n not in BIG and n not in SHARDED_SMALL]


def _step(x, target, w, m, v):
    L = w["norm_ffn1"].shape[0]
    T, D = x.shape[1], x.shape[2]
    dc, da, dl, nh, n_main = _mixer_dims(D)
    xi, yi, ci = lax.axis_index("x"), lax.axis_index("y"), lax.axis_index("c")
    me = 4 * xi + 2 * yi + ci
    c_idx = ci.astype(jnp.int32).reshape(1)

    taps = jnp.concatenate([w["conv_w"].reshape(-1), w["lru_conv_w"].reshape(-1)])
    taps = jnp.pad(taps, (0, (-taps.shape[0]) % (8 * LANES))).reshape(-1, LANES)
    gathered = []
    for l in range(L):
        shards = [w[n][l].astype(BF) for n in BIG]
        if l == 0:
            shards.append(taps)
        gathered.append(_all_gather("gather_weights_%d" % l, shards))
    taps_all = gathered[0][len(BIG)].reshape(N_DEV, -1)
    n_cw = math.prod(w["conv_w"].shape)
    ch = w["conv_w"].shape[-1]
    conv_w_full = jnp.moveaxis(taps_all[:, :n_cw].reshape((N_DEV,) + w["conv_w"].shape), 0, -2).reshape(L, 3, N_DEV * ch)
    n_lw = math.prod(w["lru_conv_w"].shape)
    lru_conv_w_full = jnp.moveaxis(taps_all[:, n_cw:n_cw + n_lw].reshape((N_DEV,) + w["lru_conv_w"].shape), 0, -2
                                   ).reshape(L, 4, N_DEV * ch)

    def layer_weights(l):
        g = dict(zip(BIG, gathered[l]))
        F = g["ffn1_w_out"].shape[1] * N_DEV
        wmix = jnp.transpose(g["mix_w_in"], (1, 0, 2)).reshape(D, -1)
        return {"ffn1_w_in": g["ffn1_w_in"], "ffn1_w_out": g["ffn1_w_out"].reshape(F, D),
                "wp": _pad_mix_w_in(wmix), "wo": g["mix_w_out"].reshape(D, D),
                "ffn2_w_in": g["ffn2_w_in"], "ffn2_w_out": g["ffn2_w_out"].reshape(F, D)}

    def small_params(l):
        return {"fgate_b": jnp.pad(w["fgate_b"][l], (0, LANES - nh)).reshape(1, LANES),
                "conv_w": conv_w_full[l], "lru_conv_w": lru_conv_w_full[l],
                "lru_conv_b": w["lru_conv_b"][l].reshape(1, dl), "lru_w_a": w["lru_w_a"][l],
                "lru_b_a": w["lru_b_a"][l].reshape(1, dl), "lru_w_x": w["lru_w_x"][l],
                "lru_b_x": w["lru_b_x"][l].reshape(1, dl), "lru_lambda": w["lru_lambda"][l].reshape(1, dl),
                "mix_out_norm": w["mix_out_norm"][l].reshape(1, D)}

    gain = lambda n, l: w[n][l].reshape(1, D)

    xc = x[0]
    saved, lw, sps = [], [], []
    for l in range(L):
        lw.append(layer_weights(l))
        sps.append(small_params(l))
        xc, s1 = _ffn_forward(xc, gain("norm_ffn1", l), lw[l]["ffn1_w_in"], lw[l]["ffn1_w_out"])
        xc, s2 = _mixer_forward(xc, gain("norm_mix", l), lw[l]["wp"], lw[l]["wo"], sps[l])
        xc, s3 = _ffn_forward(xc, gain("norm_ffn2", l), lw[l]["ffn2_w_in"], lw[l]["ffn2_w_out"])
        saved.append((s1, s2, s3))
    loss_part, dx, d_final = _loss_head(xc, w["final_norm"].reshape(1, D), target[0])

    small_grads = [None] * L
    parts = [None] * L
    for l in reversed(range(L)):
        s1, s2, s3 = saved[l]
        dx, dg3, dw3_2, dwout_2 = _ffn_backward(dx, s3, gain("norm_ffn2", l), lw[l]["ffn2_w_in"], lw[l]["ffn2_w_out"])
        dx, dwp, dwo, sg = _mixer_backward(dx, s2, gain("norm_mix", l), lw[l]["wp"], lw[l]["wo"], sps[l])
        dx, dg1, dw3_1, dwout_1 = _ffn_backward(dx, s1, gain("norm_ffn1", l), lw[l]["ffn1_w_in"], lw[l]["ffn1_w_out"])
        sg["norm_ffn2"], sg["norm_ffn1"] = dg3, dg1
        small_grads[l] = sg
        dmix = jnp.transpose(_unpad_mix_w_in(dwp).reshape(D, N_DEV, -1), (1, 0, 2))
        grads = [dw3_1, dwout_1, dmix, dwo.reshape(N_DEV, D // N_DEV, D), dw3_2, dwout_2]
        parts[l] = dict(zip(BIG, _reduce_scatter(str(l), grads, c_idx)))

    out_g, out_d, out_m, out_v = {}, {}, {}, {}
    for n in BIG:
        prev = None
        for l in range(L):
            prev = _adam_layer("adam_%s_%d" % (n, l), l, w[n], m[n], v[n], parts[l][n], prev)
        out_g[n], out_d[n], out_m[n], out_v[n] = prev

    small_names = REPLICATED + SHARDED_SMALL
    partial = []
    for n in small_names:
        if n == "final_norm":
            partial.append(d_final)
        else:
            partial.append(jnp.stack([small_grads[l][n].reshape(w[n].shape[1:]) if n not in SHARDED_SMALL
                                      else small_grads[l][n] for l in range(L)]))
    partial.append(loss_part[0, :1])
    packed = _pack(partial)
    summed = _sum_devices(_all_gather("gather_small_grads", [packed])[0])
    full_shapes = [w[n].shape for n in REPLICATED] + [(L, 3, N_DEV * ch), (L, 4, N_DEV * ch), (1,)]
    full = _unpack(summed, full_shapes)
    loss = full[-1][0]
    g_small = dict(zip(small_names, full[:-1]))
    for n in SHARDED_SMALL:
        g_small[n] = lax.dynamic_slice_in_dim(g_small[n], me * ch, ch, axis=2)
    shapes = [w[n].shape for n in small_names]
    d_s, m_s, v_s = _adam_small(_pack([w[n] for n in small_names]), _pack([g_small[n] for n in small_names]),
                                _pack([m[n] for n in small_names]), _pack([v[n] for n in small_names]))
    for n, d_, m_, v_ in zip(small_names, _unpack(d_s, shapes), _unpack(m_s, shapes), _unpack(v_s, shapes)):
        out_g[n], out_d[n], out_m[n], out_v[n] = g_small[n], d_, m_, v_

    return (loss, dx[None], *[out_g[n] for n in WEIGHTS], *[out_d[n] for n in WEIGHTS],
            *[out_m[n] for n in WEIGHTS], *[out_v[n] for n in WEIGHTS])


def kernel(x, norm_ffn1, ffn1_w_in, ffn1_w_out, norm_mix, mix_w_in, conv_w, fgate_b, lru_conv_w, lru_conv_b, lru_w_a, lru_b_a, lru_w_x, lru_b_x, lru_lambda, mix_out_norm, mix_w_out, norm_ffn2, ffn2_w_in, ffn2_w_out, final_norm, loss_target, m_norm_ffn1, m_ffn1_w_in, m_ffn1_w_out, m_norm_mix, m_mix_w_in, m_conv_w, m_fgate_b, m_lru_conv_w, m_lru_conv_b, m_lru_w_a, m_lru_b_a, m_lru_w_x, m_lru_b_x, m_lru_lambda, m_mix_out_norm, m_mix_w_out, m_norm_ffn2, m_ffn2_w_in, m_ffn2_w_out, m_final_norm, v_norm_ffn1, v_ffn1_w_in, v_ffn1_w_out, v_norm_mix, v_mix_w_in, v_conv_w, v_fgate_b, v_lru_conv_w, v_lru_conv_b, v_lru_w_a, v_lru_b_a, v_lru_w_x, v_lru_b_x, v_lru_lambda, v_mix_out_norm, v_mix_w_out, v_norm_ffn2, v_ffn2_w_in, v_ffn2_w_out, v_final_norm):
    w = dict(norm_ffn1=norm_ffn1, ffn1_w_in=ffn1_w_in, ffn1_w_out=ffn1_w_out, norm_mix=norm_mix, mix_w_in=mix_w_in,
             conv_w=conv_w, fgate_b=fgate_b, lru_conv_w=lru_conv_w, lru_conv_b=lru_conv_b, lru_w_a=lru_w_a,
             lru_b_a=lru_b_a, lru_w_x=lru_w_x, lru_b_x=lru_b_x, lru_lambda=lru_lambda, mix_out_norm=mix_out_norm,
             mix_w_out=mix_w_out, norm_ffn2=norm_ffn2, ffn2_w_in=ffn2_w_in, ffn2_w_out=ffn2_w_out,
             final_norm=final_norm)
    m = dict(norm_ffn1=m_norm_ffn1, ffn1_w_in=m_ffn1_w_in, ffn1_w_out=m_ffn1_w_out, norm_mix=m_norm_mix,
             mix_w_in=m_mix_w_in, conv_w=m_conv_w, fgate_b=m_fgate_b, lru_conv_w=m_lru_conv_w,
             lru_conv_b=m_lru_conv_b, lru_w_a=m_lru_w_a, lru_b_a=m_lru_b_a, lru_w_x=m_lru_w_x, lru_b_x=m_lru_b_x,
             lru_lambda=m_lru_lambda, mix_out_norm=m_mix_out_norm, mix_w_out=m_mix_w_out, norm_ffn2=m_norm_ffn2,
             ffn2_w_in=m_ffn2_w_in, ffn2_w_out=m_ffn2_w_out, final_norm=m_final_norm)
    v = dict(norm_ffn1=v_norm_ffn1, ffn1_w_in=v_ffn1_w_in, ffn1_w_out=v_ffn1_w_out, norm_mix=v_norm_mix,
             mix_w_in=v_mix_w_in, conv_w=v_conv_w, fgate_b=v_fgate_b, lru_conv_w=v_lru_conv_w,
             lru_conv_b=v_lru_conv_b, lru_w_a=v_lru_w_a, lru_b_a=v_lru_b_a, lru_w_x=v_lru_w_x, lru_b_x=v_lru_b_x,
             lru_lambda=v_lru_lambda, mix_out_norm=v_mix_out_norm, mix_w_out=v_mix_w_out, norm_ffn2=v_norm_ffn2,
             ffn2_w_in=v_ffn2_w_in, ffn2_w_out=v_ffn2_w_out, final_norm=v_final_norm)
    return _step(x, loss_target, w, m, v)
```

```python
import math

import jax
import jax.numpy as jnp
from jax import lax
from jax.experimental import pallas as pl
from jax.experimental.pallas import tpu as pltpu

F32 = jnp.float32
BF = jnp.bfloat16
EPS = 1e-6
LANES = 128
VMEM_LIMIT_V7X = 56 * 1024 * 1024
MESH = pl.DeviceIdType.MESH
N_DEV = 8
LRU_C = 8.0
ADAM_LR, ADAM_B1, ADAM_B2, ADAM_EPS, ADAM_WD, ADAM_STEP = 0.001, 0.9, 0.999, 1e-08, 0.01, 10
GELU_C = math.sqrt(2.0 / math.pi)
GELU_K = 0.044715


def _params(*sem):
    return pltpu.CompilerParams(dimension_semantics=sem, vmem_limit_bytes=VMEM_LIMIT_V7X)


def _any():
    return pl.BlockSpec(memory_space=pl.ANY)


def _tile(n, target):
    if n <= target:
        return n
    t = target - target % 16
    while t >= 16:
        if n % t == 0:
            return t
        t -= 16
    return n


def _mm(name, grid, a, b, o, dims, scale=1.0, resid=None):
    nk = grid[-1]
    acc_shape = tuple(d for d in o[1] if d is not None)
    has_resid = resid is not None

    def body(*refs):
        a_ref, b_ref = refs[0], refs[1]
        r_ref = refs[2] if has_resid else None
        o_ref = refs[3] if has_resid else refs[2]

        def finish(acc):
            r = acc * scale if scale != 1.0 else acc
            if has_resid:
                r = r + r_ref[...]
            o_ref[...] = r.astype(o_ref.dtype)

        part = lax.dot_general(a_ref[...].astype(BF), b_ref[...].astype(BF), (dims, ((), ())),
                               preferred_element_type=F32)
        if nk == 1:
            finish(part)
        else:
            acc_ref = refs[-1]
            k = pl.program_id(len(grid) - 1)

            @pl.when(k == 0)
            def _():
                acc_ref[...] = part

            @pl.when(k > 0)
            def _():
                acc_ref[...] += part

            @pl.when(k == nk - 1)
            def _():
                finish(acc_ref[...])

    ins = [a, b] + ([resid] if has_resid else [])
    return pl.pallas_call(
        body, name=name, grid=grid,
        in_specs=[pl.BlockSpec(blk, idx) for (_, blk, idx) in ins],
        out_specs=pl.BlockSpec(o[1], o[2]),
        out_shape=o[0],
        scratch_shapes=[pltpu.VMEM(acc_shape, F32)] if nk > 1 else [],
        compiler_params=_params(*(["parallel"] * (len(grid) - 1) + ["arbitrary"])),
    )(*[x[0] for x in ins])


NN = ((1,), (0,))
NT = ((1,), (1,))
TN = ((0,), (0,))


def _rmsnorm(x, gain):
    T, D = x.shape
    tr = _tile(T, 256)

    def body(x_ref, g_ref, o_ref):
        xv = x_ref[...]
        r = lax.rsqrt(jnp.mean(xv * xv, axis=-1, keepdims=True) + EPS)
        o_ref[...] = (xv * r * g_ref[...]).astype(BF)

    return pl.pallas_call(
        body, name="rmsnorm_fwd", grid=(T // tr,),
        in_specs=[pl.BlockSpec((tr, D), lambda i: (i, 0)), pl.BlockSpec((1, D), lambda i: (0, 0))],
        out_specs=pl.BlockSpec((tr, D), lambda i: (i, 0)),
        out_shape=jax.ShapeDtypeStruct((T, D), BF),
        compiler_params=_params("parallel"),
    )(x, gain)


def _rmsnorm_bwd(dh, x, gain, dres):
    T, D = x.shape
    tr = _tile(T, 256)

    def body(dh_ref, x_ref, g_ref, dres_ref, dx_ref, dg_ref):
        i = pl.program_id(0)
        xv = x_ref[...]
        r = lax.rsqrt(jnp.mean(xv * xv, axis=-1, keepdims=True) + EPS)
        xh = xv * r
        dy = dh_ref[...].astype(F32)
        dgp = jnp.sum(dy * xh, axis=0, keepdims=True)

        @pl.when(i == 0)
        def _():
            dg_ref[...] = dgp

        @pl.when(i > 0)
        def _():
            dg_ref[...] += dgp

        dxh = dy * g_ref[...]
        dx_ref[...] = dres_ref[...] + r * (dxh - xh * jnp.mean(dxh * xh, axis=-1, keepdims=True))

    return pl.pallas_call(
        body, name="rmsnorm_bwd", grid=(T // tr,),
        in_specs=[pl.BlockSpec((tr, D), lambda i: (i, 0)), pl.BlockSpec((tr, D), lambda i: (i, 0)),
                  pl.BlockSpec((1, D), lambda i: (0, 0)), pl.BlockSpec((tr, D), lambda i: (i, 0))],
        out_specs=[pl.BlockSpec((tr, D), lambda i: (i, 0)), pl.BlockSpec((1, D), lambda i: (0, 0))],
        out_shape=[jax.ShapeDtypeStruct((T, D), F32), jax.ShapeDtypeStruct((1, D), F32)],
        compiler_params=_params("arbitrary"),
    )(dh, x, gain, dres)


def _loss_head(x, gain, target):
    T, D = x.shape
    tr = _tile(T, 256)

    def body(x_ref, g_ref, t_ref, loss_ref, dx_ref, dg_ref):
        i = pl.program_id(0)
        xv = x_ref[...]
        g = g_ref[...]
        r = lax.rsqrt(jnp.mean(xv * xv, axis=-1, keepdims=True) + EPS)
        xh = xv * r
        err = xh * g - t_ref[...]
        lp = 0.5 * jnp.sum(jnp.mean(err * err, axis=-1, keepdims=True), axis=0, keepdims=True)
        dy = err * (1.0 / D)
        dgp = jnp.sum(dy * xh, axis=0, keepdims=True)

        @pl.when(i == 0)
        def _():
            loss_ref[...] = jnp.broadcast_to(lp, loss_ref.shape)
            dg_ref[...] = dgp

        @pl.when(i > 0)
        def _():
            loss_ref[...] += jnp.broadcast_to(lp, loss_ref.shape)
            dg_ref[...] += dgp

        dxh = dy * g
        dx_ref[...] = r * (dxh - xh * jnp.mean(dxh * xh, axis=-1, keepdims=True))

    return pl.pallas_call(
        body, name="loss_head", grid=(T // tr,),
        in_specs=[pl.BlockSpec((tr, D), lambda i: (i, 0)), pl.BlockSpec((1, D), lambda i: (0, 0)),
                  pl.BlockSpec((tr, D), lambda i: (i, 0))],
        out_specs=[pl.BlockSpec((1, LANES), lambda i: (0, 0)), pl.BlockSpec((tr, D), lambda i: (i, 0)),
                   pl.BlockSpec((1, D), lambda i: (0, 0))],
        out_shape=[jax.ShapeDtypeStruct((1, LANES), F32), jax.ShapeDtypeStruct((T, D), F32),
                   jax.ShapeDtypeStruct((1, D), F32)],
        compiler_params=_params("arbitrary"),
    )(x, gain, target)


def _group_slices(D):
    dc, da = D // 4, D // 2
    return [(0, dc), (dc, dc + da), (dc + da, D)]


def _groupnorm(yc, ya, yl, gain):
    T = yc.shape[0]
    D = yc.shape[1] + ya.shape[1] + yl.shape[1]
    tr = _tile(T, 256)
    sl = _group_slices(D)

    def body(yc_ref, ya_ref, yl_ref, g_ref, o_ref):
        for y_ref, (lo, hi) in zip((yc_ref, ya_ref, yl_ref), sl):
            y = y_ref[...]
            r = lax.rsqrt(jnp.mean(y * y, axis=-1, keepdims=True) + EPS)
            o_ref[:, lo:hi] = (y * r * g_ref[:, lo:hi]).astype(BF)

    return pl.pallas_call(
        body, name="groupnorm_fwd", grid=(T // tr,),
        in_specs=[pl.BlockSpec((tr, y.shape[1]), lambda i: (i, 0)) for y in (yc, ya, yl)]
        + [pl.BlockSpec((1, D), lambda i: (0, 0))],
        out_specs=pl.BlockSpec((tr, D), lambda i: (i, 0)),
        out_shape=jax.ShapeDtypeStruct((T, D), BF),
        compiler_params=_params("parallel"),
    )(yc, ya, yl, gain)


def _groupnorm_bwd(dyn, yc, ya, yl, gain):
    T, D = dyn.shape
    tr = _tile(T, 256)
    sl = _group_slices(D)

    def body(dyn_ref, yc_ref, ya_ref, yl_ref, g_ref, dc_ref, da_ref, dl_ref, dg_ref):
        i = pl.program_id(0)
        for y_ref, d_ref, (lo, hi) in zip((yc_ref, ya_ref, yl_ref), (dc_ref, da_ref, dl_ref), sl):
            y = y_ref[...]
            r = lax.rsqrt(jnp.mean(y * y, axis=-1, keepdims=True) + EPS)
            yh = y * r
            dy = dyn_ref[:, lo:hi]
            dgp = jnp.sum(dy * yh, axis=0, keepdims=True)

            @pl.when(i == 0)
            def _():
                dg_ref[:, lo:hi] = dgp

            @pl.when(i > 0)
            def _():
                dg_ref[:, lo:hi] += dgp

            dyh = dy * g_ref[:, lo:hi]
            d_ref[...] = r * (dyh - yh * jnp.mean(dyh * yh, axis=-1, keepdims=True))

    return pl.pallas_call(
        body, name="groupnorm_bwd", grid=(T // tr,),
        in_specs=[pl.BlockSpec((tr, D), lambda i: (i, 0))]
        + [pl.BlockSpec((tr, y.shape[1]), lambda i: (i, 0)) for y in (yc, ya, yl)]
        + [pl.BlockSpec((1, D), lambda i: (0, 0))],
        out_specs=[pl.BlockSpec((tr, y.shape[1]), lambda i: (i, 0)) for y in (yc, ya, yl)]
        + [pl.BlockSpec((1, D), lambda i: (0, 0))],
        out_shape=[jax.ShapeDtypeStruct(y.shape, F32) for y in (yc, ya, yl)] + [jax.ShapeDtypeStruct((1, D), F32)],
        compiler_params=_params("arbitrary"),
    )(dyn, yc, ya, yl, gain)


def _ffn_in(h, w3):
    T, D = h.shape
    tn = w3.shape[2]
    F = 4 * tn
    tm = _tile(T, 512)

    def body(h_ref, wg_ref, wu_ref, g_ref, u_ref, a_ref):
        hv = h_ref[...]
        g = jnp.dot(hv, wg_ref[...], preferred_element_type=F32)
        u = jnp.dot(hv, wu_ref[...], preferred_element_type=F32)
        g_ref[...] = g.astype(BF)
        u_ref[...] = u.astype(BF)
        a_ref[...] = (g * jax.nn.sigmoid(g) * u).astype(BF)

    out = jax.ShapeDtypeStruct((T, F), BF)
    return pl.pallas_call(
        body, name="ffn_in_swiglu", grid=(4, T // tm),
        in_specs=[pl.BlockSpec((tm, D), lambda j, i: (i, 0)),
                  pl.BlockSpec((None, D, tn), lambda j, i: (j, 0, 0)),
                  pl.BlockSpec((None, D, tn), lambda j, i: (j + 4, 0, 0))],
        out_specs=[pl.BlockSpec((tm, tn), lambda j, i: (i, j))] * 3,
        out_shape=[out, out, out],
        compiler_params=_params("parallel", "parallel"),
    )(h, w3, w3)


def _ffn_bwd_in(dx, wout, g, u):
    T, D = dx.shape
    F = wout.shape[0]
    tn = F // 4
    tm = _tile(T, 512)

    def body(dx_ref, w_ref, g_ref, u_ref, o_ref):
        da = 0.5 * lax.dot_general(dx_ref[...].astype(BF), w_ref[...], (NT, ((), ())), preferred_element_type=F32)
        gv = g_ref[...].astype(F32)
        s = jax.nn.sigmoid(gv)
        o_ref[0] = (da * u_ref[...].astype(F32) * (s * (1.0 + gv * (1.0 - s)))).astype(BF)
        o_ref[1] = (da * gv * s).astype(BF)

    return pl.pallas_call(
        body, name="ffn_bwd_swiglu", grid=(4, T // tm),
        in_specs=[pl.BlockSpec((tm, D), lambda j, i: (i, 0)), pl.BlockSpec((tn, D), lambda j, i: (j, 0)),
                  pl.BlockSpec((tm, tn), lambda j, i: (i, j)), pl.BlockSpec((tm, tn), lambda j, i: (i, j))],
        out_specs=pl.BlockSpec((2, tm, tn), lambda j, i: (0, i, j)),
        out_shape=jax.ShapeDtypeStruct((2, T, F), BF),
        compiler_params=_params("parallel", "parallel"),
    )(dx, wout, g, u)


def _ffn_forward(x, gain, w3, wout):
    T, D = x.shape
    F = wout.shape[0]
    h = _rmsnorm(x, gain)
    g, u, a = _ffn_in(h, w3)
    tm, tk = _tile(T, 512), F // 4
    xn = _mm("ffn_out", (T // tm, 1, F // tk),
             (a, (tm, tk), lambda i, j, k: (i, k)), (wout, (tk, D), lambda i, j, k: (k, 0)),
             (jax.ShapeDtypeStruct((T, D), F32), (tm, D), lambda i, j, k: (i, 0)), NN, scale=0.5,
             resid=(x, (tm, D), lambda i, j, k: (i, 0)))
    return xn, (x, h, g, u, a)


def _ffn_backward(dx, saved, w3, wout):
    x, h, g, u, a = saved
    T, D = x.shape
    F = wout.shape[0]
    tn3 = F // 4
    dgu = _ffn_bwd_in(dx, wout, g, u)
    tnd = _tile(D, 512)
    dwout = _mm("ffn_dwout", (F // tn3, D // tnd, 1),
                (a, (T, tn3), lambda i, j, k: (0, i)), (dx, (T, tnd), lambda i, j, k: (0, j)),
                (jax.ShapeDtypeStruct((F, D), BF), (tn3, tnd), lambda i, j, k: (i, j)), TN, scale=0.5)
    dw3 = _mm("ffn_dwin", (8, D // tnd, 1),
              (h, (T, tnd), lambda s, i, k: (0, i)), (dgu, (None, T, tn3), lambda s, i, k: (s // 4, 0, s % 4)),
              (jax.ShapeDtypeStruct((8, D, tn3), BF), (None, tnd, tn3), lambda s, i, k: (s, i, 0)), TN)
    tm = _tile(T, 512)
    dh = _mm("ffn_dh", (T // tm, 1, 8),
             (dgu, (None, tm, tn3), lambda i, j, k: (k // 4, i, k % 4)), (w3, (None, D, tn3), lambda i, j, k: (k, 0, 0)),
             (jax.ShapeDtypeStruct((T, D), F32), (tm, D), lambda i, j, k: (i, 0)), NT)
    return dh, dw3, dwout.reshape(N_DEV, F // N_DEV, D)


def _rows(shape):
    return lax.broadcasted_iota(jnp.int32, shape, 0)


def _down(x, s, fill, rows):
    return jnp.where(rows >= s, pltpu.roll(x, s, 0), fill)


def _up(x, s, fill, rows):
    T = x.shape[0]
    return jnp.where(rows < T - s, pltpu.roll(x, T - s, 0), fill)


def _scan_linear(a, b, rows, shift):
    T = a.shape[0]
    s = 1
    while s < T:
        b = a * shift(b, s, 0.0, rows) + b
        if 2 * s < T:
            a = a * shift(a, s, 1.0, rows)
        s *= 2
    return b


def _cumsum(c, rows, shift):
    T = c.shape[0]
    s = 1
    while s < T:
        c = c + shift(c, s, 0.0, rows)
        s *= 2
    return c


def _log1p_small(e):
    return jnp.where(e < 0.01, e * (1.0 - e * (0.5 - e * (1.0 / 3.0))), jnp.log(1.0 + e))


def _softplus(x):
    return jnp.maximum(x, 0.0) + _log1p_small(jnp.exp(-jnp.abs(x)))


def _one_minus_exp_neg(z):
    return jnp.where(z < 0.1, z * (1.0 - z * (0.5 - z * (1.0 / 6.0 - z * (1.0 / 24.0)))), 1.0 - jnp.exp(-z))


def _fgate_cum(f, b):
    T = f.shape[0]

    def body(f_ref, b_ref, o_ref):
        z = f_ref[...] + b_ref[...]
        o_ref[...] = _cumsum(-_softplus(-z), _rows(z.shape), _down)

    return pl.pallas_call(
        body, name="fgate_cumsum",
        out_shape=jax.ShapeDtypeStruct((T, LANES), F32),
        compiler_params=pltpu.CompilerParams(vmem_limit_bytes=VMEM_LIMIT_V7X),
    )(f, b)


def _fgate_cum_bwd(drow, dcol, f, b):
    T = f.shape[0]

    def body(dr_ref, dc_ref, f_ref, b_ref, df_ref, db_ref):
        z = f_ref[...] + b_ref[...]
        dlogf = _cumsum(dr_ref[...] - dc_ref[...], _rows(z.shape), _up)
        dz = dlogf * jax.nn.sigmoid(-z)
        df_ref[...] = dz.astype(BF)
        db_ref[...] = jnp.sum(dz, axis=0, keepdims=True)

    return pl.pallas_call(
        body, name="fgate_cumsum_bwd",
        out_shape=[jax.ShapeDtypeStruct((T, LANES), BF), jax.ShapeDtypeStruct((1, LANES), F32)],
        compiler_params=pltpu.CompilerParams(vmem_limit_bytes=VMEM_LIMIT_V7X),
    )(drow, dcol, f, b)


def _col(blk0):
    return lambda g: (0, blk0 + g)


def _conv_fwd(p, w, nb):
    T = p.shape[0]

    def body(b_ref, c_ref, v_ref, w_ref, o_ref):
        z = c_ref[...].astype(F32) * v_ref[...].astype(F32)
        rows = _rows(z.shape)
        conv = w_ref[2:3, :] * z + w_ref[1:2, :] * _down(z, 1, 0.0, rows) + w_ref[0:1, :] * _down(z, 2, 0.0, rows)
        o_ref[...] = b_ref[...].astype(F32) * conv

    return pl.pallas_call(
        body, name="conv_fwd", grid=(nb,),
        in_specs=[pl.BlockSpec((T, LANES), _col(0)), pl.BlockSpec((T, LANES), _col(nb)),
                  pl.BlockSpec((T, LANES), _col(2 * nb)), pl.BlockSpec((3, LANES), lambda g: (0, g))],
        out_specs=pl.BlockSpec((T, LANES), lambda g: (0, g)),
        out_shape=jax.ShapeDtypeStruct((T, nb * LANES), F32),
        compiler_params=_params("parallel"),
    )(p, p, p, w)


def _conv_bwd(dy, p, w, nb):
    T = p.shape[0]

    def body(dy_ref, b_ref, c_ref, v_ref, w_ref, db_ref, dc_ref, dv_ref, dw_ref):
        cv, vv = c_ref[...].astype(F32), v_ref[...].astype(F32)
        z = cv * vv
        rows = _rows(z.shape)
        z1, z2 = _down(z, 1, 0.0, rows), _down(z, 2, 0.0, rows)
        dyv = dy_ref[...]
        db_ref[...] = (dyv * (w_ref[2:3, :] * z + w_ref[1:2, :] * z1 + w_ref[0:1, :] * z2)).astype(BF)
        dconv = dyv * b_ref[...].astype(F32)
        dz = (w_ref[2:3, :] * dconv + w_ref[1:2, :] * _up(dconv, 1, 0.0, rows)
              + w_ref[0:1, :] * _up(dconv, 2, 0.0, rows))
        dc_ref[...] = (dz * vv).astype(BF)
        dv_ref[...] = (dz * cv).astype(BF)
        dw_ref[0:1, :] = jnp.sum(dconv * z2, axis=0, keepdims=True)
        dw_ref[1:2, :] = jnp.sum(dconv * z1, axis=0, keepdims=True)
        dw_ref[2:3, :] = jnp.sum(dconv * z, axis=0, keepdims=True)

    return pl.pallas_call(
        body, name="conv_bwd", grid=(nb,),
        in_specs=[pl.BlockSpec((T, LANES), lambda g: (0, g)), pl.BlockSpec((T, LANES), _col(0)),
                  pl.BlockSpec((T, LANES), _col(nb)), pl.BlockSpec((T, LANES), _col(2 * nb)),
                  pl.BlockSpec((3, LANES), lambda g: (0, g))],
        out_specs=[pl.BlockSpec((T, LANES), lambda g: (0, g))] * 3 + [pl.BlockSpec((3, LANES), lambda g: (0, g))],
        out_shape=[jax.ShapeDtypeStruct((T, nb * LANES), BF)] * 3 + [jax.ShapeDtypeStruct((3, nb * LANES), F32)],
        compiler_params=_params("parallel"),
    )(dy, p, p, p, w)


def _gelu(x):
    t = jnp.tanh(GELU_C * (x + GELU_K * x * x * x))
    return 0.5 * x * (1.0 + t), t


def _lru_common(x, cw_ref, cb_ref, wa_ref, ba_ref, wx_ref, bx_ref, lam_ref, rows):
    xr = (cb_ref[...] + cw_ref[3:4, :] * x + cw_ref[2:3, :] * _down(x, 1, 0.0, rows)
          + cw_ref[1:2, :] * _down(x, 2, 0.0, rows) + cw_ref[0:1, :] * _down(x, 3, 0.0, rows))
    xrb = xr.astype(BF)
    r = jax.nn.sigmoid(jnp.dot(xrb, wa_ref[...].astype(BF), preferred_element_type=F32) + ba_ref[...])
    i = jax.nn.sigmoid(jnp.dot(xrb, wx_ref[...].astype(BF), preferred_element_type=F32) + bx_ref[...])
    sp = _softplus(-lam_ref[...])
    log_a = -LRU_C * r * sp
    a = jnp.exp(log_a)
    m = jnp.sqrt(_one_minus_exp_neg(-2.0 * log_a))
    return xr, xrb, r, i, sp, a, m


def _lru_specs(T, nb, gate_blk0, x_blk0):
    vec = pl.BlockSpec((1, LANES), lambda g: (0, g))
    mat = pl.BlockSpec((None, LANES, LANES), lambda g: (g, 0, 0))
    return [pl.BlockSpec((T, LANES), _col(gate_blk0)), pl.BlockSpec((T, LANES), _col(x_blk0)),
            pl.BlockSpec((4, LANES), lambda g: (0, g)), vec, mat, vec, mat, vec, vec]


def _lru_fwd(p, cw, cb, wa, ba, wx, bx, lam, nb, gate_blk0):
    T = p.shape[0]

    def body(gate_ref, x_ref, cw_ref, cb_ref, wa_ref, ba_ref, wx_ref, bx_ref, lam_ref, y_ref, h_ref):
        x = x_ref[...].astype(F32)
        rows = _rows(x.shape)
        xr, _, _, i, _, a, m = _lru_common(x, cw_ref, cb_ref, wa_ref, ba_ref, wx_ref, bx_ref, lam_ref, rows)
        h = _scan_linear(a, m * (i * xr), rows, _down)
        h_ref[...] = h
        y_ref[...] = _gelu(gate_ref[...].astype(F32))[0] * h

    out = jax.ShapeDtypeStruct((T, nb * LANES), F32)
    return pl.pallas_call(
        body, name="lru_fwd", grid=(nb,),
        in_specs=_lru_specs(T, nb, gate_blk0, gate_blk0 + nb),
        out_specs=[pl.BlockSpec((T, LANES), lambda g: (0, g))] * 2,
        out_shape=[out, out],
        compiler_params=_params("parallel"),
    )(p, p, cw, cb, wa, ba, wx, bx, lam)


def _lru_bwd(dy, hs, p, cw, cb, wa, ba, wx, bx, lam, nb, gate_blk0):
    T = p.shape[0]

    def body(dy_ref, hs_ref, gate_ref, x_ref, cw_ref, cb_ref, wa_ref, ba_ref, wx_ref, bx_ref, lam_ref,
             dgate_ref, dx_ref, dcw_ref, dcb_ref, dwa_ref, dba_ref, dwx_ref, dbx_ref, dlam_ref):
        x = x_ref[...].astype(F32)
        rows = _rows(x.shape)
        xr, xrb, r, i, sp, a, m = _lru_common(x, cw_ref, cb_ref, wa_ref, ba_ref, wx_ref, bx_ref, lam_ref, rows)
        gate = gate_ref[...].astype(F32)
        gl, t = _gelu(gate)
        h = hs_ref[...]
        dyv = dy_ref[...]
        dgelu = 0.5 * (1.0 + t) + 0.5 * gate * (1.0 - t * t) * GELU_C * (1.0 + 3.0 * GELU_K * gate * gate)
        dgate_ref[...] = (dyv * h * dgelu).astype(BF)
        lam_adj = _scan_linear(_up(a, 1, 0.0, rows), dyv * gl, rows, _up)
        da = lam_adj * _down(h, 1, 0.0, rows)
        ix = i * xr
        dix = lam_adj * m
        dm = lam_adj * ix
        dlog_a = da * a - dm * (a * a) / jnp.maximum(m, 1e-30)
        dr = dlog_a * (-LRU_C * sp)
        dsp = jnp.sum(dlog_a * (-LRU_C * r), axis=0, keepdims=True)
        dlam_ref[...] = -dsp * jax.nn.sigmoid(-lam_ref[...])
        dpa = dr * r * (1.0 - r)
        dpx = dix * xr * i * (1.0 - i)
        dpab, dpxb = dpa.astype(BF), dpx.astype(BF)
        dxr = (dix * i
               + lax.dot_general(dpab, wa_ref[...].astype(BF), (NT, ((), ())), preferred_element_type=F32)
               + lax.dot_general(dpxb, wx_ref[...].astype(BF), (NT, ((), ())), preferred_element_type=F32))
        dwa_ref[...] = lax.dot_general(xrb, dpab, (TN, ((), ())), preferred_element_type=F32)
        dwx_ref[...] = lax.dot_general(xrb, dpxb, (TN, ((), ())), preferred_element_type=F32)
        dba_ref[...] = jnp.sum(dpa, axis=0, keepdims=True)
        dbx_ref[...] = jnp.sum(dpx, axis=0, keepdims=True)
        dcb_ref[...] = jnp.sum(dxr, axis=0, keepdims=True)
        dx_ref[...] = (cw_ref[3:4, :] * dxr + cw_ref[2:3, :] * _up(dxr, 1, 0.0, rows)
                       + cw_ref[1:2, :] * _up(dxr, 2, 0.0, rows) + cw_ref[0:1, :] * _up(dxr, 3, 0.0, rows)).astype(BF)
        for k in range(4):
            xs = x if k == 3 else _down(x, 3 - k, 0.0, rows)
            dcw_ref[k:k + 1, :] = jnp.sum(dxr * xs, axis=0, keepdims=True)

    C = nb * LANES
    seq = jax.ShapeDtypeStruct((T, C), BF)
    vec = jax.ShapeDtypeStruct((1, C), F32)
    mat = jax.ShapeDtypeStruct((nb, LANES, LANES), F32)
    vspec = pl.BlockSpec((1, LANES), lambda g: (0, g))
    mspec = pl.BlockSpec((None, LANES, LANES), lambda g: (g, 0, 0))
    sspec = pl.BlockSpec((T, LANES), lambda g: (0, g))
    return pl.pallas_call(
        body, name="lru_bwd", grid=(nb,),
        in_specs=[sspec, sspec] + _lru_specs(T, nb, gate_blk0, gate_blk0 + nb),
        out_specs=[sspec, sspec, pl.BlockSpec((4, LANES), lambda g: (0, g)), vspec, mspec, vspec, mspec, vspec, vspec],
        out_shape=[seq, seq, jax.ShapeDtypeStruct((4, C), F32), vec, mat, vec, mat, vec, vec],
        compiler_params=_params("parallel"),
    )(dy, hs, p, p, cw, cb, wa, ba, wx, bx, lam)


def _causal(shape, transposed=False):
    r = lax.broadcasted_iota(jnp.int32, shape, 0)
    c = lax.broadcasted_iota(jnp.int32, shape, 1)
    return r <= c if transposed else c <= r


def _attn_fwd(p, cq, ck, nh, q_blk0):
    T = p.shape[0]
    tq = _tile(T, 512)
    nq = T // tq
    scale = LANES ** -0.5

    def body(q_ref, k_ref, v_ref, cq_ref, ck_ref, o_ref, lse_ref, m_ref, l_ref, acc_ref):
        i, j = pl.program_id(1), pl.program_id(2)

        @pl.when(j == 0)
        def _():
            m_ref[...] = jnp.full(m_ref.shape, -jnp.inf, F32)
            l_ref[...] = jnp.zeros(l_ref.shape, F32)
            acc_ref[...] = jnp.zeros(acc_ref.shape, F32)

        def block(diagonal):
            s = lax.dot_general(q_ref[...], k_ref[...], (NT, ((), ())), preferred_element_type=F32) * scale
            s = s + cq_ref[...] - ck_ref[...]
            if diagonal:
                s = jnp.where(_causal(s.shape), s, -jnp.inf)
            m_new = jnp.maximum(m_ref[...], jnp.max(s, axis=-1, keepdims=True))
            alpha = jnp.exp(m_ref[...] - m_new)
            pr = jnp.exp(s - m_new)
            l_ref[...] = alpha * l_ref[...] + jnp.sum(pr, axis=-1, keepdims=True)
            acc_ref[...] = alpha * acc_ref[...] + jnp.dot(pr.astype(BF), v_ref[...], preferred_element_type=F32)
            m_ref[...] = m_new

        pl.when(j < i)(lambda: block(False))
        pl.when(j == i)(lambda: block(True))

        @pl.when(j == nq - 1)
        def _():
            o_ref[...] = acc_ref[...] / l_ref[...]
            lse_ref[...] = m_ref[...] + jnp.log(l_ref[...])

    def kv(off):
        return pl.BlockSpec((tq, LANES), lambda h, i, j: (jnp.minimum(j, i), q_blk0 + off * nh + h))

    return pl.pallas_call(
        body, name="attn_fwd", grid=(nh, nq, nq),
        in_specs=[pl.BlockSpec((tq, LANES), lambda h, i, j: (i, q_blk0 + h)), kv(1), kv(2),
                  pl.BlockSpec((None, tq, 1), lambda h, i, j: (h, i, 0)),
                  pl.BlockSpec((None, 1, tq), lambda h, i, j: (h, 0, jnp.minimum(j, i)))],
        out_specs=[pl.BlockSpec((tq, LANES), lambda h, i, j: (i, h)),
                   pl.BlockSpec((None, tq, 1), lambda h, i, j: (h, i, 0))],
        out_shape=[jax.ShapeDtypeStruct((T, nh * LANES), F32), jax.ShapeDtypeStruct((nh, T, 1), F32)],
        scratch_shapes=[pltpu.VMEM((tq, 1), F32), pltpu.VMEM((tq, 1), F32), pltpu.VMEM((tq, LANES), F32)],
        compiler_params=_params("parallel", "parallel", "arbitrary"),
    )(p, p, p, cq, ck)


def _attn_bwd_q(p, cq, ck, lse, do, o, nh, q_blk0):
    T = p.shape[0]
    tq = _tile(T, 512)
    nq = T // tq
    scale = LANES ** -0.5

    def body(q_ref, k_ref, v_ref, cq_ref, ck_ref, lse_ref, do_ref, o_ref, dq_ref, dl_ref, dr_ref, acc_ref):
        i, j = pl.program_id(1), pl.program_id(2)

        @pl.when(j == 0)
        def _():
            dl_ref[...] = jnp.sum(do_ref[...] * o_ref[...], axis=-1, keepdims=True)
            dr_ref[...] = jnp.zeros(dr_ref.shape, F32)
            acc_ref[...] = jnp.zeros(acc_ref.shape, F32)

        def block(diagonal):
            s = lax.dot_general(q_ref[...], k_ref[...], (NT, ((), ())), preferred_element_type=F32) * scale
            s = s + cq_ref[...] - ck_ref[...]
            pr = jnp.exp(s - lse_ref[...])
            if diagonal:
                pr = jnp.where(_causal(s.shape), pr, 0.0)
            dp = lax.dot_general(do_ref[...].astype(BF), v_ref[...], (NT, ((), ())), preferred_element_type=F32)
            ds = pr * (dp - dl_ref[...])
            dr_ref[...] += jnp.sum(ds, axis=-1, keepdims=True)
            acc_ref[...] += jnp.dot(ds.astype(BF), k_ref[...], preferred_element_type=F32)

        pl.when(j < i)(lambda: block(False))
        pl.when(j == i)(lambda: block(True))

        @pl.when(j == nq - 1)
        def _():
            dq_ref[...] = (acc_ref[...] * scale).astype(BF)

    def kv(off):
        return pl.BlockSpec((tq, LANES), lambda h, i, j: (jnp.minimum(j, i), q_blk0 + off * nh + h))

    col = pl.BlockSpec((None, tq, 1), lambda h, i, j: (h, i, 0))
    head = pl.BlockSpec((tq, LANES), lambda h, i, j: (i, h))
    return pl.pallas_call(
        body, name="attn_bwd_q", grid=(nh, nq, nq),
        in_specs=[pl.BlockSpec((tq, LANES), lambda h, i, j: (i, q_blk0 + h)), kv(1), kv(2), col,
                  pl.BlockSpec((None, 1, tq), lambda h, i, j: (h, 0, jnp.minimum(j, i))), col, head, head],
        out_specs=[head, col, col],
        out_shape=[jax.ShapeDtypeStruct((T, nh * LANES), BF), jax.ShapeDtypeStruct((nh, T, 1), F32),
                   jax.ShapeDtypeStruct((nh, T, 1), F32)],
        scratch_shapes=[pltpu.VMEM((tq, LANES), F32)],
        compiler_params=_params("parallel", "parallel", "arbitrary"),
    )(p, p, p, cq, ck, lse, do, o)


def _attn_bwd_kv(p, cq_row, ck_col, lse_row, delta_row, do, nh, q_blk0):
    T = p.shape[0]
    tk = _tile(T, 512)
    nk = T // tk
    scale = LANES ** -0.5

    def body(q_ref, k_ref, v_ref, cq_ref, ck_ref, lse_ref, dl_ref, do_ref, dk_ref, dv_ref, dc_ref,
             dk_acc, dv_acc, dc_acc):
        j, i = pl.program_id(1), pl.program_id(2)

        @pl.when(i == 0)
        def _():
            dk_acc[...] = jnp.zeros(dk_acc.shape, F32)
            dv_acc[...] = jnp.zeros(dv_acc.shape, F32)
            dc_acc[...] = jnp.zeros(dc_acc.shape, F32)

        def block(diagonal):
            st = lax.dot_general(k_ref[...], q_ref[...], (NT, ((), ())), preferred_element_type=F32) * scale
            st = st + cq_ref[...] - ck_ref[...]
            pt = jnp.exp(st - lse_ref[...])
            if diagonal:
                pt = jnp.where(_causal(st.shape, transposed=True), pt, 0.0)
            dob = do_ref[...].astype(BF)
            dv_acc[...] += jnp.dot(pt.astype(BF), dob, preferred_element_type=F32)
            dpt = lax.dot_general(v_ref[...], dob, (NT, ((), ())), preferred_element_type=F32)
            dst = pt * (dpt - dl_ref[...])
            dk_acc[...] += jnp.dot(dst.astype(BF), q_ref[...], preferred_element_type=F32)
            dc_acc[...] += jnp.sum(dst, axis=-1, keepdims=True)

        pl.when(i > j)(lambda: block(False))
        pl.when(i == j)(lambda: block(True))

        @pl.when(i == nk - 1)
        def _():
            dk_ref[...] = (dk_acc[...] * scale).astype(BF)
            dv_ref[...] = dv_acc[...].astype(BF)
            dc_ref[...] = dc_acc[...]

    def qside(blk):
        return pl.BlockSpec((tk, LANES), lambda h, j, i: (jnp.maximum(i, j), blk + h))

    def kside(off):
        return pl.BlockSpec((tk, LANES), lambda h, j, i: (j, q_blk0 + off * nh + h))

    row = pl.BlockSpec((None, 1, tk), lambda h, j, i: (h, 0, jnp.maximum(i, j)))
    col = pl.BlockSpec((None, tk, 1), lambda h, j, i: (h, j, 0))
    head = pl.BlockSpec((tk, LANES), lambda h, j, i: (j, h))
    return pl.pallas_call(
        body, name="attn_bwd_kv", grid=(nh, nk, nk),
        in_specs=[qside(q_blk0), kside(1), kside(2), row, col, row, row, qside(0)],
        out_specs=[head, head, col],
        out_shape=[jax.ShapeDtypeStruct((T, nh * LANES), BF), jax.ShapeDtypeStruct((T, nh * LANES), BF),
                   jax.ShapeDtypeStruct((nh, T, 1), F32)],
        scratch_shapes=[pltpu.VMEM((tk, LANES), F32), pltpu.VMEM((tk, LANES), F32), pltpu.VMEM((tk, 1), F32)],
        compiler_params=_params("parallel", "parallel", "arbitrary"),
    )(p, p, p, cq_row, ck_col, lse_row, delta_row, do)


def _mixer_dims(D):
    dc, da, dl = D // 4, D // 2, D // 4
    nh = da // LANES
    n_main = 3 * dc + 3 * da + 2 * dl
    return dc, da, dl, nh, n_main


def _pad_mix_w_in(wfull):
    D = wfull.shape[0]
    dc, da, dl, nh, n_main = _mixer_dims(D)
    a = 3 * dc + 3 * da
    return jnp.concatenate([wfull[:, :a], wfull[:, a + nh:], wfull[:, a:a + nh],
                            jnp.zeros((D, LANES - nh), wfull.dtype)], axis=1)


def _unpad_mix_w_in(wp):
    D = wp.shape[0]
    dc, da, dl, nh, n_main = _mixer_dims(D)
    a = 3 * dc + 3 * da
    return jnp.concatenate([wp[:, :a], wp[:, n_main:n_main + nh], wp[:, a:n_main]], axis=1)


def _head_cols(c, nh):
    t = jnp.transpose(c[:, :nh])
    return t[:, :, None], t[:, None, :]


def _mixer_forward(x, gain, wp, wo, sp):
    T, D = x.shape
    dc, da, dl, nh, n_main = _mixer_dims(D)
    nbc, nbl = dc // LANES, dl // LANES
    h = _rmsnorm(x, gain)
    tm = _tile(T, 512)
    tn = n_main // 4
    p = _mm("mix_in", (4, T // tm, 1),
            (h, (tm, D), lambda j, i, k: (i, 0)), (wp, (D, tn), lambda j, i, k: (0, j)),
            (jax.ShapeDtypeStruct((T, n_main), BF), (tm, tn), lambda j, i, k: (i, j)), NN)
    f = _mm("mix_in_fgate", (T // tm, 1, 1),
            (h, (tm, D), lambda i, j, k: (i, 0)), (wp, (D, LANES), lambda i, j, k: (0, n_main // LANES)),
            (jax.ShapeDtypeStruct((T, LANES), F32), (tm, LANES), lambda i, j, k: (i, 0)), NN)
    cum = _fgate_cum(f, sp["fgate_b"])
    cq, ck = _head_cols(cum, nh)
    yc = _conv_fwd(p, sp["conv_w"], nbc)
    q_blk0 = 3 * nbc
    ya, lse = _attn_fwd(p, cq, ck, nh, q_blk0)
    gate_blk0 = q_blk0 + 3 * nh
    yl, hs = _lru_fwd(p, sp["lru_conv_w"], sp["lru_conv_b"], sp["lru_w_a"], sp["lru_b_a"], sp["lru_w_x"],
                      sp["lru_b_x"], sp["lru_lambda"], nbl, gate_blk0)
    yn = _groupnorm(yc, ya, yl, sp["mix_out_norm"])
    xn = _mm("mix_out", (T // tm, 1, 1),
             (yn, (tm, D), lambda i, j, k: (i, 0)), (wo, (D, D), lambda i, j, k: (0, 0)),
             (jax.ShapeDtypeStruct((T, D), F32), (tm, D), lambda i, j, k: (i, 0)), NN,
             resid=(x, (tm, D), lambda i, j, k: (i, 0)))
    return xn, (x, h, p, f, cq, ck, yc, ya, lse, yl, hs, yn)


def _mixer_backward(dx, saved, gain, wp, wo, sp):
    x, h, p, f, cq, ck, yc, ya, lse, yl, hs, yn = saved
    T, D = x.shape
    dc, da, dl, nh, n_main = _mixer_dims(D)
    nbc, nbl = dc // LANES, dl // LANES
    q_blk0 = 3 * nbc
    gate_blk0 = q_blk0 + 3 * nh
    tm = _tile(T, 512)
    tnd, tnd2 = _tile(D, 512), _tile(D, 1024)
    dyn = _mm("mix_out_dy", (T // tm, 1, 1),
              (dx, (tm, D), lambda i, j, k: (i, 0)), (wo, (D, D), lambda i, j, k: (0, 0)),
              (jax.ShapeDtypeStruct((T, D), F32), (tm, D), lambda i, j, k: (i, 0)), NT)
    dwo = _mm("mix_out_dw", (D // tnd, D // tnd2, 1),
              (yn, (T, tnd), lambda i, j, k: (0, i)), (dx, (T, tnd2), lambda i, j, k: (0, j)),
              (jax.ShapeDtypeStruct((D, D), BF), (tnd, tnd2), lambda i, j, k: (i, j)), TN)
    dyc, dya, dyl, dgn = _groupnorm_bwd(dyn, yc, ya, yl, sp["mix_out_norm"])
    dcb, dcc, dcv, dconv_w = _conv_bwd(dyc, p, sp["conv_w"], nbc)
    dq, delta, drow = _attn_bwd_q(p, cq, ck, lse, dya, ya, nh, q_blk0)
    as_row = lambda t: t.reshape(nh, 1, T)
    dk, dv, dcol = _attn_bwd_kv(p, ck, cq, as_row(lse), as_row(delta), dya, nh, q_blk0)
    lanes = lambda t: jnp.pad(jnp.transpose(t.reshape(nh, T)), ((0, 0), (0, LANES - nh)))
    df, dfb = _fgate_cum_bwd(lanes(drow), lanes(dcol), f, sp["fgate_b"])
    dgate, dlx, dcw, dcbias, dwa, dba, dwx, dbx, dlam = _lru_bwd(
        dyl, hs, p, sp["lru_conv_w"], sp["lru_conv_b"], sp["lru_w_a"], sp["lru_b_a"], sp["lru_w_x"],
        sp["lru_b_x"], sp["lru_lambda"], nbl, gate_blk0)
    dp = jnp.concatenate([dcb, dcc, dcv, dq, dk, dv, dgate, dlx, df], axis=1)
    n_pad = n_main + LANES
    tn, tkp = n_pad // 5, n_pad // 3
    dwp = _mm("mix_in_dw", (n_pad // tn, D // tnd, 1),
              (h, (T, tnd), lambda j, i, k: (0, i)), (dp, (T, tn), lambda j, i, k: (0, j)),
              (jax.ShapeDtypeStruct((D, n_pad), BF), (tnd, tn), lambda j, i, k: (i, j)), TN)
    dh = _mm("mix_in_dh", (T // tm, 1, n_pad // tkp),
             (dp, (tm, tkp), lambda i, j, k: (i, k)), (wp, (D, tkp), lambda i, j, k: (0, k)),
             (jax.ShapeDtypeStruct((T, D), F32), (tm, D), lambda i, j, k: (i, 0)), NT)
    dxn, dgain = _rmsnorm_bwd(dh, x, gain, dx)
    small = {"norm_mix": dgain, "mix_out_norm": dgn, "conv_w": dconv_w, "fgate_b": dfb[:, :nh],
             "lru_conv_w": dcw, "lru_conv_b": dcbias, "lru_w_a": dwa, "lru_b_a": dba, "lru_w_x": dwx,
             "lru_b_x": dbx, "lru_lambda": dlam}
    return dxn, dwp, dwo, small


def _position():
    x, y, c = lax.axis_index("x"), lax.axis_index("y"), lax.axis_index("c")
    return x, y, c, [(1 - x, y), (x, 1 - y), (1 - x, 1 - y)]


def _all_gather(name, xs):
    n = len(xs)

    def body(*refs):
        x_refs, o_refs = refs[:n], refs[n:2 * n]
        send_sems, recv_sems, local_sems = refs[2 * n:]
        x, y, c, chips = _position()
        me, sibling = (x, y, c), (x, y, 1 - c)

        def copy(t, k, block, to, src=None):
            dst = o_refs[t].at[4 * block[0] + 2 * block[1] + block[2]]
            return pltpu.make_async_remote_copy(
                src_ref=dst if src is None else src, dst_ref=dst, send_sem=send_sems.at[t, k],
                recv_sem=recv_sems.at[t, k], device_id=to, device_id_type=MESH)

        mine = [pltpu.make_async_copy(x_refs[t], o_refs[t].at[4 * x + 2 * y + c], local_sems.at[t]) for t in range(n)]
        for cp in mine:
            cp.start()
        first = []
        for t in range(n):
            first.append(copy(t, 0, me, sibling, src=x_refs[t]))
            first += [copy(t, 1 + j, me, (*chip, c), src=x_refs[t]) for j, chip in enumerate(chips)]
        for cp in first:
            cp.start()
        passed = []
        for j, chip in enumerate(chips):
            for t in range(n):
                copy(t, 1 + j, (*chip, c), me).wait_recv()
                passed.append(copy(t, 4 + j, (*chip, c), sibling))
                passed[-1].start()
        for t in range(n):
            copy(t, 0, sibling, me).wait_recv()
            for j, chip in enumerate(chips):
                copy(t, 4 + j, (*chip, 1 - c), me).wait_recv()
        for cp in first + passed:
            cp.wait_send()
        for cp in mine:
            cp.wait()

    return pl.pallas_call(
        body, name=name,
        in_specs=[_any()] * n, out_specs=[_any()] * n,
        out_shape=[jax.ShapeDtypeStruct((N_DEV,) + v.shape, v.dtype) for v in xs],
        scratch_shapes=[pltpu.SemaphoreType.DMA((n, 7)), pltpu.SemaphoreType.DMA((n, 7)),
                        pltpu.SemaphoreType.DMA((n,))],
    )(*xs)


def _exchange_sibling(name, gs):
    n = len(gs)
    g4 = [g.reshape((4, 2) + g.shape[1:]) for g in gs]

    def body(*refs):
        g_refs, o_refs = refs[:n], refs[n:2 * n]
        send_sems, recv_sems = refs[2 * n:]
        x, y, c, _ = _position()
        copies = [pltpu.make_async_remote_copy(
            src_ref=g_refs[t].at[:, pl.ds(1 - c, 1)], dst_ref=o_refs[t], send_sem=send_sems.at[t],
            recv_sem=recv_sems.at[t], device_id=(x, y, 1 - c), device_id_type=MESH) for t in range(n)]
        for cp in copies:
            cp.start()
        for cp in copies:
            cp.wait()

    return pl.pallas_call(
        body, name=name,
        in_specs=[_any()] * n, out_specs=[_any()] * n,
        out_shape=[jax.ShapeDtypeStruct((4, 1) + g.shape[2:], g.dtype) for g in g4],
        scratch_shapes=[pltpu.SemaphoreType.DMA((n,)), pltpu.SemaphoreType.DMA((n,))],
    )(*g4), g4


def _add_sibling(g4, recv, c_idx):
    _, _, R, C = g4.shape
    tr = _tile(R, 256)

    def body(c_ref, g_ref, r_ref, o_ref):
        o_ref[...] = (g_ref[...].astype(F32) + r_ref[...].astype(F32)).astype(o_ref.dtype)

    return pl.pallas_call(
        body, name="rs_add_sibling",
        grid_spec=pltpu.PrefetchScalarGridSpec(
            num_scalar_prefetch=1, grid=(4, R // tr),
            in_specs=[pl.BlockSpec((None, None, tr, C), lambda q, i, c_ref: (q, c_ref[0], i, 0)),
                      pl.BlockSpec((None, None, tr, C), lambda q, i, c_ref: (q, 0, i, 0))],
            out_specs=pl.BlockSpec((None, tr, C), lambda q, i, c_ref: (q, i, 0))),
        out_shape=jax.ShapeDtypeStruct((4, R, C), g4.dtype),
        compiler_params=_params("parallel", "parallel"),
    )(c_idx, g4, recv)


def _exchange_chips(name, ps):
    n = len(ps)

    def body(*refs):
        p_refs, o_refs = refs[:n], refs[n:2 * n]
        send_sems, recv_sems, local_sems = refs[2 * n:]
        x, y, c, chips = _position()
        mine = [pltpu.make_async_copy(p_refs[t].at[2 * x + y], o_refs[t].at[3], local_sems.at[t]) for t in range(n)]
        for cp in mine:
            cp.start()
        copies = [pltpu.make_async_remote_copy(
            src_ref=p_refs[t].at[2 * chip[0] + chip[1]], dst_ref=o_refs[t].at[k], send_sem=send_sems.at[t, k],
            recv_sem=recv_sems.at[t, k], device_id=(*chip, c), device_id_type=MESH)
            for t in range(n) for k, chip in enumerate(chips)]
        for cp in copies:
            cp.start()
        for cp in copies:
            cp.wait()
        for cp in mine:
            cp.wait()

    return pl.pallas_call(
        body, name=name,
        in_specs=[_any()] * n, out_specs=[_any()] * n,
        out_shape=[jax.ShapeDtypeStruct(v.shape, v.dtype) for v in ps],
        scratch_shapes=[pltpu.SemaphoreType.DMA((n, 3)), pltpu.SemaphoreType.DMA((n, 3)),
                        pltpu.SemaphoreType.DMA((n,))],
    )(*ps)


def _hbm_spec():
    return pl.BlockSpec(memory_space=pltpu.HBM)


def _sem_spec():
    return pl.BlockSpec(memory_space=pltpu.SEMAPHORE)


def _side_effects():
    return pltpu.CompilerParams(has_side_effects=pltpu.SideEffectType.DATAFLOW_SIDE_EFFECTING)


def _in_hbm(v):
    return pltpu.with_memory_space_constraint(v, pltpu.HBM)


def _split_start(name, srcs, lands, n_copies, copies_of, after):
    n = len(srcs)
    ns = n * n_copies

    def body(*refs):
        src_refs, land_refs = refs[:n], refs[n:2 * n]
        send_sems, recv_sems = refs[2 * n + 1:2 * n + 1 + ns], refs[2 * n + 1 + ns:2 * n + 1 + 2 * ns]
        token = refs[-1]
        for t in range(n):
            for k, (s, d, to) in enumerate(copies_of(t, src_refs[t], land_refs[t])):
                pltpu.make_async_remote_copy(src_ref=s, dst_ref=d, send_sem=send_sems[t * n_copies + k],
                                             recv_sem=recv_sems[t * n_copies + k], device_id=to,
                                             device_id_type=MESH).start()
        token[...] = jnp.zeros(token.shape, token.dtype)

    thru = [pltpu.HBM(v.shape, v.dtype) for v in list(srcs) + list(lands)]
    out = pl.pallas_call(
        body, name=name,
        in_specs=[_hbm_spec()] * (2 * n) + [_any()],
        out_specs=[_sem_spec()] * (2 * ns) + [_hbm_spec()] * (2 * n) + [pl.BlockSpec(memory_space=pltpu.VMEM)],
        out_shape=[pltpu.SemaphoreType.DMA(())] * (2 * ns) + thru + [jax.ShapeDtypeStruct((8, LANES), F32)],
        input_output_aliases={i: 2 * ns + i for i in range(2 * n)},
        compiler_params=_side_effects(),
    )(*[_in_hbm(v) for v in list(srcs) + list(lands)], after)
    return out[:ns], out[ns:2 * ns], out[2 * ns:2 * ns + n], out[2 * ns + n:2 * ns + 2 * n], out[-1]


def _split_wait(name, send_sems, recv_sems, srcs, lands, n_copies, waits_of, after):
    n = len(srcs)
    ns = n * n_copies

    def body(*refs):
        src_refs, land_refs = refs[:n], refs[n:2 * n]
        send_refs, recv_refs = refs[2 * n:2 * n + ns], refs[2 * n + ns:2 * n + 2 * ns]
        x, y, c, _ = _position()
        for t in range(n):
            for k, (s, d) in enumerate(waits_of(t, src_refs[t], land_refs[t])):
                cp = pltpu.make_async_remote_copy(src_ref=s, dst_ref=d, send_sem=send_refs[t * n_copies + k],
                                                  recv_sem=recv_refs[t * n_copies + k], device_id=(x, y, 1 - c),
                                                  device_id_type=MESH)
                cp.wait_send()
                cp.wait_recv()

    out = pl.pallas_call(
        body, name=name,
        in_specs=[_hbm_spec()] * (2 * n) + [_sem_spec()] * (2 * ns) + [_any()],
        out_specs=[_hbm_spec()] * (2 * n),
        out_shape=[pltpu.HBM(v.shape, v.dtype) for v in list(srcs) + list(lands)],
        input_output_aliases={i: i for i in range(2 * n)},
        compiler_params=_side_effects(),
    )(*srcs, *lands, *send_sems, *recv_sems, after)
    return out[:n], out[n:]


def _block_of(px, py, pc):
    return 4 * px + 2 * py + pc


def _gather_phase1_start(name, xs, after):
    lands = [lax.empty((N_DEV,) + v.shape, v.dtype) for v in xs]

    def copies_of(t, x_ref, land_ref):
        x, y, c, chips = _position()
        dst = land_ref.at[_block_of(x, y, c)]
        return [(x_ref, dst, (x, y, 1 - c))] + [(x_ref, dst, (*chip, c)) for chip in chips]

    return _split_start(name, xs, lands, 4, copies_of, after)


def _gather_phase1_wait(name, started, after):
    send_sems, recv_sems, xs, lands, _ = started

    def waits_of(t, x_ref, land_ref):
        x, y, c, chips = _position()
        return [(x_ref, land_ref.at[_block_of(x, y, 1 - c)])] + [(x_ref, land_ref.at[_block_of(*chip, c)])
                                                                  for chip in chips]

    return _split_wait(name, send_sems, recv_sems, xs, lands, 4, waits_of, after)


def _gather_phase2_start(name, lands, after):
    keep = [lax.empty((8, LANES), v.dtype) for v in lands]

    def copies_of(t, _, land_ref):
        x, y, c, chips = _position()
        return [(land_ref.at[_block_of(*chip, c)], land_ref.at[_block_of(*chip, c)], (x, y, 1 - c)) for chip in chips]

    return _split_start(name, keep, lands, 3, copies_of, after)


def _gather_phase2_wait(name, started, after):
    send_sems, recv_sems, keep, lands, _ = started

    def waits_of(t, _, land_ref):
        x, y, c, chips = _position()
        return [(land_ref.at[_block_of(*chip, c)], land_ref.at[_block_of(*chip, 1 - c)]) for chip in chips]

    return _split_wait(name, send_sems, recv_sems, keep, lands, 3, waits_of, after)[1]


def _place_local(name, lands, srcs, dst_index, src_index):
    n = len(lands)

    def body(*refs):
        land_in, src_refs, out_refs, sems = refs[:n], refs[n:2 * n], refs[2 * n:3 * n], refs[3 * n]
        x, y, c, _ = _position()
        cps = []
        for t in range(n):
            s = src_refs[t] if src_index is None else src_refs[t].at[src_index(x, y, c)]
            cps.append(pltpu.make_async_copy(s, out_refs[t].at[dst_index(x, y, c)], sems.at[t]))
            cps[-1].start()
        for cp in cps:
            cp.wait()

    return pl.pallas_call(
        body, name=name,
        in_specs=[_any()] * (2 * n), out_specs=[_any()] * n,
        out_shape=[jax.ShapeDtypeStruct(v.shape, v.dtype) for v in lands],
        scratch_shapes=[pltpu.SemaphoreType.DMA((n,))],
        input_output_aliases={i: i for i in range(n)},
    )(*lands, *srcs)


def _scatter_chips_start(name, ps, after):
    lands = [lax.empty(v.shape, v.dtype) for v in ps]

    def copies_of(t, p_ref, land_ref):
        x, y, c, chips = _position()
        return [(p_ref.at[2 * chip[0] + chip[1]], land_ref.at[k], (*chip, c)) for k, chip in enumerate(chips)]

    return _split_start(name, ps, lands, 3, copies_of, after)


def _scatter_chips_wait(name, started, after):
    send_sems, recv_sems, ps, lands, _ = started

    def waits_of(t, p_ref, land_ref):
        x, y, c, chips = _position()
        return [(p_ref.at[2 * chip[0] + chip[1]], land_ref.at[k]) for k, chip in enumerate(chips)]

    ps, lands = _split_wait(name, send_sems, recv_sems, ps, lands, 3, waits_of, after)
    return _place_local(name + "_own", lands, ps, lambda x, y, c: 3, lambda x, y, c: 2 * x + y)


def _reduce_scatter_start(tag, gs, c_idx):
    recv, g4 = _exchange_sibling("rs_sibling_" + tag, gs)
    ps = [_add_sibling(g, r, c_idx) for g, r in zip(g4, recv)]
    return _scatter_chips_start("rs_chips_start_" + tag, ps, ps[0])


def _sum_devices(parts):
    _, R, C = parts.shape
    tr = _tile(R, 512)

    def body(p_ref, o_ref):
        acc = p_ref[0]
        for d in range(1, N_DEV):
            acc = acc + p_ref[d]
        o_ref[...] = acc

    return pl.pallas_call(
        body, name="sum_devices", grid=(R // tr,),
        in_specs=[pl.BlockSpec((N_DEV, tr, C), lambda i: (0, i, 0))],
        out_specs=pl.BlockSpec((tr, C), lambda i: (i, 0)),
        out_shape=jax.ShapeDtypeStruct((R, C), F32),
        compiler_params=_params("parallel"),
    )(parts)


def _adam_math(w, g, m, v):
    m = ADAM_B1 * m + (1.0 - ADAM_B1) * g
    v = ADAM_B2 * v + (1.0 - ADAM_B2) * (g * g)
    m_hat = m / (1.0 - ADAM_B1 ** ADAM_STEP)
    v_hat = v / (1.0 - ADAM_B2 ** ADAM_STEP)
    return -ADAM_LR * (m_hat / (jnp.sqrt(v_hat) + ADAM_EPS) + ADAM_WD * w), m, v


def _adam_layer(name, layer, w, m, v, parts, prev):
    L, R, C = w.shape
    tr = _tile(R, 128)
    slab = pl.BlockSpec((None, tr, C), lambda i: (layer, i, 0))

    def body(w_ref, m_ref, v_ref, p_ref, *rest):
        g_ref, d_ref, nm_ref, nv_ref = rest[-4:]
        g = p_ref[0].astype(F32)
        for k in range(1, 4):
            g = g + p_ref[k].astype(F32)
        d, nm, nv = _adam_math(w_ref[...], g, m_ref[...], v_ref[...])
        g_ref[...] = g
        d_ref[...] = d
        nm_ref[...] = nm
        nv_ref[...] = nv

    n_prev = 0 if prev is None else 4
    out = jax.ShapeDtypeStruct((L, R, C), F32)
    return pl.pallas_call(
        body, name=name, grid=(R // tr,),
        in_specs=[slab, slab, slab, pl.BlockSpec((4, tr, C), lambda i: (0, i, 0))] + [_any()] * n_prev,
        out_specs=[slab] * 4,
        out_shape=[out] * 4,
        input_output_aliases={4 + k: k for k in range(n_prev)},
        compiler_params=_params("parallel"),
    )(w, m, v, parts, *(prev or ()))


def _adam_small(w, g, m, v):
    R, C = w.shape
    tr = _tile(R, 512)
    spec = pl.BlockSpec((tr, C), lambda i: (i, 0))

    def body(w_ref, g_ref, m_ref, v_ref, d_ref, nm_ref, nv_ref):
        d, nm, nv = _adam_math(w_ref[...], g_ref[...], m_ref[...], v_ref[...])
        d_ref[...] = d
        nm_ref[...] = nm
        nv_ref[...] = nv

    out = jax.ShapeDtypeStruct((R, C), F32)
    return pl.pallas_call(
        body, name="adam_small", grid=(R // tr,),
        in_specs=[spec] * 4, out_specs=[spec] * 3, out_shape=[out] * 3,
        compiler_params=_params("parallel"),
    )(w, g, m, v)


def _pack(arrays, row_multiple=512):
    flat = jnp.concatenate([a.reshape(-1).astype(F32) for a in arrays])
    per = row_multiple * LANES
    total = -(-flat.shape[0] // per) * per
    return jnp.pad(flat, (0, total - flat.shape[0])).reshape(total // LANES, LANES)


def _unpack(packed, shapes):
    flat = packed.reshape(-1)
    out, off = [], 0
    for s in shapes:
        n = math.prod(s)
        out.append(flat[off:off + n].reshape(s))
        off += n
    return out


BIG = ["ffn1_w_in", "ffn1_w_out", "mix_w_in", "mix_w_out", "ffn2_w_in", "ffn2_w_out"]
SHARDED_SMALL = ["conv_w", "lru_conv_w"]
WEIGHTS = ["norm_ffn1", "ffn1_w_in", "ffn1_w_out", "norm_mix", "mix_w_in", "conv_w", "fgate_b", "lru_conv_w",
           "lru_conv_b", "lru_w_a", "lru_b_a", "lru_w_x", "lru_b_x", "lru_lambda", "mix_out_norm", "mix_w_out",
           "norm_ffn2", "ffn2_w_in", "ffn2_w_out", "final_norm"]
REPLICATED = [n for n in WEIGHTS if n not in BIG and n not in SHARDED_SMALL]


def _step(x, target, w, m, v):
    L = w["norm_ffn1"].shape[0]
    T, D = x.shape[1], x.shape[2]
    dc, da, dl, nh, n_main = _mixer_dims(D)
    xi, yi, ci = lax.axis_index("x"), lax.axis_index("y"), lax.axis_index("c")
    me = 4 * xi + 2 * yi + ci
    c_idx = ci.astype(jnp.int32).reshape(1)

    taps = jnp.concatenate([w["conv_w"].reshape(-1), w["lru_conv_w"].reshape(-1)])
    taps = jnp.pad(taps, (0, (-taps.shape[0]) % (8 * LANES))).reshape(-1, LANES)
    def shards_of(l):
        return [w[n][l].astype(BF) for n in BIG] + ([taps] if l == 0 else [])

    own_block = lambda x_, y_, c_: _block_of(x_, y_, c_)
    gathered = [None] * L
    phase1 = [None] * L
    phase1[0] = _gather_phase1_start("gather_p1_start_0", shards_of(0), taps)
    own, landed = _gather_phase1_wait("gather_p1_wait_0", phase1[0], phase1[0][-1])
    phase2 = _gather_phase2_start("gather_p2_start_0", landed, landed[0])
    order_token = phase2[-1]
    if L > 1:
        phase1[1] = _gather_phase1_start("gather_p1_start_1", shards_of(1), phase2[-1])
        order_token = phase1[1][-1]
    landed = _gather_phase2_wait("gather_p2_wait_0", phase2, order_token)
    gathered[0] = _place_local("gather_own_0", landed, own, own_block, None)
    taps_all = gathered[0][len(BIG)].reshape(N_DEV, -1)
    n_cw = math.prod(w["conv_w"].shape)
    ch = w["conv_w"].shape[-1]
    conv_w_full = jnp.moveaxis(taps_all[:, :n_cw].reshape((N_DEV,) + w["conv_w"].shape), 0, -2).reshape(L, 3, N_DEV * ch)
    n_lw = math.prod(w["lru_conv_w"].shape)
    lru_conv_w_full = jnp.moveaxis(taps_all[:, n_cw:n_cw + n_lw].reshape((N_DEV,) + w["lru_conv_w"].shape), 0, -2
                                   ).reshape(L, 4, N_DEV * ch)

    def layer_weights(l):
        g = dict(zip(BIG, gathered[l]))
        F = g["ffn1_w_out"].shape[1] * N_DEV
        wmix = jnp.transpose(g["mix_w_in"], (1, 0, 2)).reshape(D, -1)
        return {"ffn1_w_in": g["ffn1_w_in"], "ffn1_w_out": g["ffn1_w_out"].reshape(F, D),
                "wp": _pad_mix_w_in(wmix), "wo": g["mix_w_out"].reshape(D, D),
                "ffn2_w_in": g["ffn2_w_in"], "ffn2_w_out": g["ffn2_w_out"].reshape(F, D)}

    def small_params(l):
        return {"fgate_b": jnp.pad(w["fgate_b"][l], (0, LANES - nh)).reshape(1, LANES),
                "conv_w": conv_w_full[l], "lru_conv_w": lru_conv_w_full[l],
                "lru_conv_b": w["lru_conv_b"][l].reshape(1, dl), "lru_w_a": w["lru_w_a"][l],
                "lru_b_a": w["lru_b_a"][l].reshape(1, dl), "lru_w_x": w["lru_w_x"][l],
                "lru_b_x": w["lru_b_x"][l].reshape(1, dl), "lru_lambda": w["lru_lambda"][l].reshape(1, dl),
                "mix_out_norm": w["mix_out_norm"][l].reshape(1, D)}

    gain = lambda n, l, token=None: w[n][l].reshape(1, D) + (0.0 if token is None else token[0:1, 0:1])

    xc = x[0]
    saved, lw, sps = [], [], []
    for l in range(L):
        lw.append(layer_weights(l))
        sps.append(small_params(l))
        xc, s1 = _ffn_forward(xc, gain("norm_ffn1", l), lw[l]["ffn1_w_in"], lw[l]["ffn1_w_out"])
        xc, s2 = _mixer_forward(xc, gain("norm_mix", l), lw[l]["wp"], lw[l]["wo"], sps[l])
        order_token = None
        if l + 1 < L:
            own, landed = _gather_phase1_wait("gather_p1_wait_%d" % (l + 1), phase1[l + 1], xc)
            phase2 = _gather_phase2_start("gather_p2_start_%d" % (l + 1), landed, landed[0])
            order_token = phase2[-1]
            if l + 2 < L:
                phase1[l + 2] = _gather_phase1_start("gather_p1_start_%d" % (l + 2), shards_of(l + 2), phase2[-1])
                order_token = phase1[l + 2][-1]
        xc, s3 = _ffn_forward(xc, gain("norm_ffn2", l, order_token), lw[l]["ffn2_w_in"], lw[l]["ffn2_w_out"])
        if l + 1 < L:
            landed = _gather_phase2_wait("gather_p2_wait_%d" % (l + 1), phase2, xc)
            gathered[l + 1] = _place_local("gather_own_%d" % (l + 1), landed, own, own_block, None)
        saved.append((s1, s2, s3))
    loss_part, dx, d_final = _loss_head(xc, w["final_norm"].reshape(1, D), target[0])

    small_grads = [None] * L
    parts = [None] * L
    scatter = [None] * L
    for l in reversed(range(L)):
        s1, s2, s3 = saved[l]
        dh, dw3_2, dwout_2 = _ffn_backward(dx, s3, lw[l]["ffn2_w_in"], lw[l]["ffn2_w_out"])
        dx, dg3 = _rmsnorm_bwd(dh, s3[0], gain("norm_ffn2", l), dx)
        dx, dwp, dwo, sg = _mixer_backward(dx, s2, gain("norm_mix", l), lw[l]["wp"], lw[l]["wo"], sps[l])
        dh, dw3_1, dwout_1 = _ffn_backward(dx, s1, lw[l]["ffn1_w_in"], lw[l]["ffn1_w_out"])
        if l + 1 < L:
            parts[l + 1] = dict(zip(BIG, _scatter_chips_wait("rs_chips_wait_%d" % (l + 1), scatter[l + 1], dh)))
        dmix = jnp.transpose(_unpad_mix_w_in(dwp).reshape(D, N_DEV, -1), (1, 0, 2))
        grads = [dw3_1, dwout_1, dmix, dwo.reshape(N_DEV, D // N_DEV, D), dw3_2, dwout_2]
        scatter[l] = _reduce_scatter_start(str(l), grads, c_idx)
        dx, dg1 = _rmsnorm_bwd(dh, s1[0], gain("norm_ffn1", l, scatter[l][-1]), dx)
        sg["norm_ffn2"], sg["norm_ffn1"] = dg3, dg1
        small_grads[l] = sg

    out_g, out_d, out_m, out_v = {}, {}, {}, {}
    prev = {n: None for n in BIG}

    def adam_big(l):
        for n in BIG:
            prev[n] = _adam_layer("adam_%s_%d" % (n, l), l, w[n], m[n], v[n], parts[l][n], prev[n])

    for l in reversed(range(1, L)):
        adam_big(l)

    small_names = REPLICATED + SHARDED_SMALL
    partial = []
    for n in small_names:
        if n == "final_norm":
            partial.append(d_final)
        else:
            partial.append(jnp.stack([small_grads[l][n].reshape(w[n].shape[1:]) if n not in SHARDED_SMALL
                                      else small_grads[l][n] for l in range(L)]))
    partial.append(loss_part[0, :1])
    packed = _pack(partial)
    summed = _sum_devices(_all_gather("gather_small_grads", [packed])[0])
    full_shapes = [w[n].shape for n in REPLICATED] + [(L, 3, N_DEV * ch), (L, 4, N_DEV * ch), (1,)]
    full = _unpack(summed, full_shapes)
    loss = full[-1][0]
    g_small = dict(zip(small_names, full[:-1]))
    for n in SHARDED_SMALL:
        g_small[n] = lax.dynamic_slice_in_dim(g_small[n], me * ch, ch, axis=2)
    shapes = [w[n].shape for n in small_names]
    d_s, m_s, v_s = _adam_small(_pack([w[n] for n in small_names]), _pack([g_small[n] for n in small_names]),
                                _pack([m[n] for n in small_names]), _pack([v[n] for n in small_names]))
    for n, d_, m_, v_ in zip(small_names, _unpack(d_s, shapes), _unpack(m_s, shapes), _unpack(v_s, shapes)):
        out_g[n], out_d[n], out_m[n], out_v[n] = g_small[n], d_, m_, v_

    parts[0] = dict(zip(BIG, _scatter_chips_wait("rs_chips_wait_0", scatter[0], d_s)))
    adam_big(0)
    for n in BIG:
        out_g[n], out_d[n], out_m[n], out_v[n] = prev[n]

    return (loss, dx[None], *[out_g[n] for n in WEIGHTS], *[out_d[n] for n in WEIGHTS],
            *[out_m[n] for n in WEIGHTS], *[out_v[n] for n in WEIGHTS])


def kernel(x, norm_ffn1, ffn1_w_in, ffn1_w_out, norm_mix, mix_w_in, conv_w, fgate_b, lru_conv_w, lru_conv_b, lru_w_a, lru_b_a, lru_w_x, lru_b_x, lru_lambda, mix_out_norm, mix_w_out, norm_ffn2, ffn2_w_in, ffn2_w_out, final_norm, loss_target, m_norm_ffn1, m_ffn1_w_in, m_ffn1_w_out, m_norm_mix, m_mix_w_in, m_conv_w, m_fgate_b, m_lru_conv_w, m_lru_conv_b, m_lru_w_a, m_lru_b_a, m_lru_w_x, m_lru_b_x, m_lru_lambda, m_mix_out_norm, m_mix_w_out, m_norm_ffn2, m_ffn2_w_in, m_ffn2_w_out, m_final_norm, v_norm_ffn1, v_ffn1_w_in, v_ffn1_w_out, v_norm_mix, v_mix_w_in, v_conv_w, v_fgate_b, v_lru_conv_w, v_lru_conv_b, v_lru_w_a, v_lru_b_a, v_lru_w_x, v_lru_b_x, v_lru_lambda, v_mix_out_norm, v_mix_w_out, v_norm_ffn2, v_ffn2_w_in, v_ffn2_w_out, v_final_norm):
    w = dict(norm_ffn1=norm_ffn1, ffn1_w_in=ffn1_w_in, ffn1_w_out=ffn1_w_out, norm_mix=norm_mix, mix_w_in=mix_w_in,
             conv_w=conv_w, fgate_b=fgate_b, lru_conv_w=lru_conv_w, lru_conv_b=lru_conv_b, lru_w_a=lru_w_a,
             lru_b_a=lru_b_a, lru_w_x=lru_w_x, lru_b_x=lru_b_x, lru_lambda=lru_lambda, mix_out_norm=mix_out_norm,
             mix_w_out=mix_w_out, norm_ffn2=norm_ffn2, ffn2_w_in=ffn2_w_in, ffn2_w_out=ffn2_w_out,
             final_norm=final_norm)
    m = dict(norm_ffn1=m_norm_ffn1, ffn1_w_in=m_ffn1_w_in, ffn1_w_out=m_ffn1_w_out, norm_mix=m_norm_mix,
             mix_w_in=m_mix_w_in, conv_w=m_conv_w, fgate_b=m_fgate_b, lru_conv_w=m_lru_conv_w,
             lru_conv_b=m_lru_conv_b, lru_w_a=m_lru_w_a, lru_b_a=m_lru_b_a, lru_w_x=m_lru_w_x, lru_b_x=m_lru_b_x,
             lru_lambda=m_lru_lambda, mix_out_norm=m_mix_out_norm, mix_w_out=m_mix_w_out, norm_ffn2=m_norm_ffn2,
             ffn2_w_in=m_ffn2_w_in, ffn2_w_out=m_ffn2_w_out, final_norm=m_final_norm)
    v = dict(norm_ffn1=v_norm_ffn1, ffn1_w_in=v_ffn1_w_in, ffn1_w_out=v_ffn1_w_out, norm_mix=v_norm_mix,
             mix_w_in=v_mix_w_in, conv_w=v_conv_w, fgate_b=v_fgate_b, lru_conv_w=v_lru_conv_w,
             lru_conv_b=v_lru_conv_b, lru_w_a=v_lru_w_a, lru_b_a=v_lru_b_a, lru_w_x=v_lru_w_x, lru_b_x=v_lru_b_x,
             lru_lambda=v_lru_lambda, mix_out_norm=v_mix_out_norm, mix_w_out=v_mix_w_out, norm_ffn2=v_norm_ffn2,
             ffn2_w_in=v_ffn2_w_in, ffn2_w_out=v_ffn2_w_out, final_norm=v_final_norm)
    return _step(x, loss_target, w, m, v)
```

```python
import math

import jax
import jax.numpy as jnp
from jax import lax
from jax.experimental import pallas as pl
from jax.experimental.pallas import tpu as pltpu

F32 = jnp.float32
BF = jnp.bfloat16
EPS = 1e-6
LANES = 128
VMEM_LIMIT_V7X = 56 * 1024 * 1024
MESH = pl.DeviceIdType.MESH
N_DEV = 8
LRU_C = 8.0
ADAM_LR, ADAM_B1, ADAM_B2, ADAM_EPS, ADAM_WD, ADAM_STEP = 0.001, 0.9, 0.999, 1e-08, 0.01, 10
GELU_C = math.sqrt(2.0 / math.pi)
GELU_K = 0.044715


def _params(*sem):
    return pltpu.CompilerParams(dimension_semantics=sem, vmem_limit_bytes=VMEM_LIMIT_V7X)


def _any():
    return pl.BlockSpec(memory_space=pl.ANY)


def _tile(n, target):
    if n <= target:
        return n
    t = target - target % 16
    while t >= 16:
        if n % t == 0:
            return t
        t -= 16
    return n


def _mm(name, grid, a, b, o, dims, scale=1.0, resid=None):
    nk = grid[-1]
    acc_shape = tuple(d for d in o[1] if d is not None)
    has_resid = resid is not None

    def body(*refs):
        a_ref, b_ref = refs[0], refs[1]
        r_ref = refs[2] if has_resid else None
        o_ref = refs[3] if has_resid else refs[2]

        def finish(acc):
            r = acc * scale if scale != 1.0 else acc
            if has_resid:
                r = r + r_ref[...]
            o_ref[...] = r.astype(o_ref.dtype)

        part = lax.dot_general(a_ref[...].astype(BF), b_ref[...].astype(BF), (dims, ((), ())),
                               preferred_element_type=F32)
        if nk == 1:
            finish(part)
        else:
            acc_ref = refs[-1]
            k = pl.program_id(len(grid) - 1)

            @pl.when(k == 0)
            def _():
                acc_ref[...] = part

            @pl.when(k > 0)
            def _():
                acc_ref[...] += part

            @pl.when(k == nk - 1)
            def _():
                finish(acc_ref[...])

    ins = [a, b] + ([resid] if has_resid else [])
    return pl.pallas_call(
        body, name=name, grid=grid,
        in_specs=[pl.BlockSpec(blk, idx) for (_, blk, idx) in ins],
        out_specs=pl.BlockSpec(o[1], o[2]),
        out_shape=o[0],
        scratch_shapes=[pltpu.VMEM(acc_shape, F32)] if nk > 1 else [],
        compiler_params=_params(*(["parallel"] * (len(grid) - 1) + ["arbitrary"])),
    )(*[x[0] for x in ins])


NN = ((1,), (0,))
NT = ((1,), (1,))
TN = ((0,), (0,))


def _rmsnorm(x, gain):
    T, D = x.shape
    tr = _tile(T, 256)

    def body(x_ref, g_ref, o_ref):
        xv = x_ref[...]
        r = lax.rsqrt(jnp.mean(xv * xv, axis=-1, keepdims=True) + EPS)
        o_ref[...] = (xv * r * g_ref[...]).astype(BF)

    return pl.pallas_call(
        body, name="rmsnorm_fwd", grid=(T // tr,),
        in_specs=[pl.BlockSpec((tr, D), lambda i: (i, 0)), pl.BlockSpec((1, D), lambda i: (0, 0))],
        out_specs=pl.BlockSpec((tr, D), lambda i: (i, 0)),
        out_shape=jax.ShapeDtypeStruct((T, D), BF),
        compiler_params=_params("parallel"),
    )(x, gain)


def _rmsnorm_bwd(dh, x, gain, dres):
    T, D = x.shape
    tr = _tile(T, 256)

    def body(dh_ref, x_ref, g_ref, dres_ref, dx_ref, dg_ref):
        i = pl.program_id(0)
        xv = x_ref[...]
        r = lax.rsqrt(jnp.mean(xv * xv, axis=-1, keepdims=True) + EPS)
        xh = xv * r
        dy = dh_ref[...].astype(F32)
        dgp = jnp.sum(dy * xh, axis=0, keepdims=True)

        @pl.when(i == 0)
        def _():
            dg_ref[...] = dgp

        @pl.when(i > 0)
        def _():
            dg_ref[...] += dgp

        dxh = dy * g_ref[...]
        dx_ref[...] = dres_ref[...] + r * (dxh - xh * jnp.mean(dxh * xh, axis=-1, keepdims=True))

    return pl.pallas_call(
        body, name="rmsnorm_bwd", grid=(T // tr,),
        in_specs=[pl.BlockSpec((tr, D), lambda i: (i, 0)), pl.BlockSpec((tr, D), lambda i: (i, 0)),
                  pl.BlockSpec((1, D), lambda i: (0, 0)), pl.BlockSpec((tr, D), lambda i: (i, 0))],
        out_specs=[pl.BlockSpec((tr, D), lambda i: (i, 0)), pl.BlockSpec((1, D), lambda i: (0, 0))],
        out_shape=[jax.ShapeDtypeStruct((T, D), F32), jax.ShapeDtypeStruct((1, D), F32)],
        compiler_params=_params("arbitrary"),
    )(dh, x, gain, dres)


def _loss_head(x, gain, target):
    T, D = x.shape
    tr = _tile(T, 256)

    def body(x_ref, g_ref, t_ref, loss_ref, dx_ref, dg_ref):
        i = pl.program_id(0)
        xv = x_ref[...]
        g = g_ref[...]
        r = lax.rsqrt(jnp.mean(xv * xv, axis=-1, keepdims=True) + EPS)
        xh = xv * r
        err = xh * g - t_ref[...]
        lp = 0.5 * jnp.sum(jnp.mean(err * err, axis=-1, keepdims=True), axis=0, keepdims=True)
        dy = err * (1.0 / D)
        dgp = jnp.sum(dy * xh, axis=0, keepdims=True)

        @pl.when(i == 0)
        def _():
            loss_ref[...] = jnp.broadcast_to(lp, loss_ref.shape)
            dg_ref[...] = dgp

        @pl.when(i > 0)
        def _():
            loss_ref[...] += jnp.broadcast_to(lp, loss_ref.shape)
            dg_ref[...] += dgp

        dxh = dy * g
        dx_ref[...] = r * (dxh - xh * jnp.mean(dxh * xh, axis=-1, keepdims=True))

    return pl.pallas_call(
        body, name="loss_head", grid=(T // tr,),
        in_specs=[pl.BlockSpec((tr, D), lambda i: (i, 0)), pl.BlockSpec((1, D), lambda i: (0, 0)),
                  pl.BlockSpec((tr, D), lambda i: (i, 0))],
        out_specs=[pl.BlockSpec((1, LANES), lambda i: (0, 0)), pl.BlockSpec((tr, D), lambda i: (i, 0)),
                   pl.BlockSpec((1, D), lambda i: (0, 0))],
        out_shape=[jax.ShapeDtypeStruct((1, LANES), F32), jax.ShapeDtypeStruct((T, D), F32),
                   jax.ShapeDtypeStruct((1, D), F32)],
        compiler_params=_params("arbitrary"),
    )(x, gain, target)


def _group_slices(D):
    dc, da = D // 4, D // 2
    return [(0, dc), (dc, dc + da), (dc + da, D)]


def _groupnorm(yc, ya, yl, gain):
    T = yc.shape[0]
    D = yc.shape[1] + ya.shape[1] + yl.shape[1]
    tr = _tile(T, 256)
    sl = _group_slices(D)

    def body(yc_ref, ya_ref, yl_ref, g_ref, o_ref):
        for y_ref, (lo, hi) in zip((yc_ref, ya_ref, yl_ref), sl):
            y = y_ref[...]
            r = lax.rsqrt(jnp.mean(y * y, axis=-1, keepdims=True) + EPS)
            o_ref[:, lo:hi] = (y * r * g_ref[:, lo:hi]).astype(BF)

    return pl.pallas_call(
        body, name="groupnorm_fwd", grid=(T // tr,),
        in_specs=[pl.BlockSpec((tr, y.shape[1]), lambda i: (i, 0)) for y in (yc, ya, yl)]
        + [pl.BlockSpec((1, D), lambda i: (0, 0))],
        out_specs=pl.BlockSpec((tr, D), lambda i: (i, 0)),
        out_shape=jax.ShapeDtypeStruct((T, D), BF),
        compiler_params=_params("parallel"),
    )(yc, ya, yl, gain)


def _groupnorm_bwd(dyn, yc, ya, yl, gain):
    T, D = dyn.shape
    tr = _tile(T, 256)
    sl = _group_slices(D)

    def body(dyn_ref, yc_ref, ya_ref, yl_ref, g_ref, dc_ref, da_ref, dl_ref, dg_ref):
        i = pl.program_id(0)
        for y_ref, d_ref, (lo, hi) in zip((yc_ref, ya_ref, yl_ref), (dc_ref, da_ref, dl_ref), sl):
            y = y_ref[...]
            r = lax.rsqrt(jnp.mean(y * y, axis=-1, keepdims=True) + EPS)
            yh = y * r
            dy = dyn_ref[:, lo:hi]
            dgp = jnp.sum(dy * yh, axis=0, keepdims=True)

            @pl.when(i == 0)
            def _():
                dg_ref[:, lo:hi] = dgp

            @pl.when(i > 0)
            def _():
                dg_ref[:, lo:hi] += dgp

            dyh = dy * g_ref[:, lo:hi]
            d_ref[...] = r * (dyh - yh * jnp.mean(dyh * yh, axis=-1, keepdims=True))

    return pl.pallas_call(
        body, name="groupnorm_bwd", grid=(T // tr,),
        in_specs=[pl.BlockSpec((tr, D), lambda i: (i, 0))]
        + [pl.BlockSpec((tr, y.shape[1]), lambda i: (i, 0)) for y in (yc, ya, yl)]
        + [pl.BlockSpec((1, D), lambda i: (0, 0))],
        out_specs=[pl.BlockSpec((tr, y.shape[1]), lambda i: (i, 0)) for y in (yc, ya, yl)]
        + [pl.BlockSpec((1, D), lambda i: (0, 0))],
        out_shape=[jax.ShapeDtypeStruct(y.shape, F32) for y in (yc, ya, yl)] + [jax.ShapeDtypeStruct((1, D), F32)],
        compiler_params=_params("arbitrary"),
    )(dyn, yc, ya, yl, gain)


def _ffn_in(h, w3):
    T, D = h.shape
    tn = w3.shape[2]
    F = 4 * tn
    tm = _tile(T, 512)

    def body(h_ref, wg_ref, wu_ref, g_ref, u_ref, a_ref):
        hv = h_ref[...]
        g = jnp.dot(hv, wg_ref[...], preferred_element_type=F32)
        u = jnp.dot(hv, wu_ref[...], preferred_element_type=F32)
        g_ref[...] = g.astype(BF)
        u_ref[...] = u.astype(BF)
        a_ref[...] = (g * jax.nn.sigmoid(g) * u).astype(BF)

    out = jax.ShapeDtypeStruct((T, F), BF)
    return pl.pallas_call(
        body, name="ffn_in_swiglu", grid=(4, T // tm),
        in_specs=[pl.BlockSpec((tm, D), lambda j, i: (i, 0)),
                  pl.BlockSpec((None, D, tn), lambda j, i: (j, 0, 0)),
                  pl.BlockSpec((None, D, tn), lambda j, i: (j + 4, 0, 0))],
        out_specs=[pl.BlockSpec((tm, tn), lambda j, i: (i, j))] * 3,
        out_shape=[out, out, out],
        compiler_params=_params("parallel", "parallel"),
    )(h, w3, w3)


def _ffn_bwd_in(dx, wout, g, u):
    T, D = dx.shape
    F = wout.shape[0]
    tn = F // 4
    tm = _tile(T, 512)

    def body(dx_ref, w_ref, g_ref, u_ref, o_ref):
        da = 0.5 * lax.dot_general(dx_ref[...].astype(BF), w_ref[...], (NT, ((), ())), preferred_element_type=F32)
        gv = g_ref[...].astype(F32)
        s = jax.nn.sigmoid(gv)
        o_ref[0] = (da * u_ref[...].astype(F32) * (s * (1.0 + gv * (1.0 - s)))).astype(BF)
        o_ref[1] = (da * gv * s).astype(BF)

    return pl.pallas_call(
        body, name="ffn_bwd_swiglu", grid=(4, T // tm),
        in_specs=[pl.BlockSpec((tm, D), lambda j, i: (i, 0)), pl.BlockSpec((tn, D), lambda j, i: (j, 0)),
                  pl.BlockSpec((tm, tn), lambda j, i: (i, j)), pl.BlockSpec((tm, tn), lambda j, i: (i, j))],
        out_specs=pl.BlockSpec((2, tm, tn), lambda j, i: (0, i, j)),
        out_shape=jax.ShapeDtypeStruct((2, T, F), BF),
        compiler_params=_params("parallel", "parallel"),
    )(dx, wout, g, u)


def _ffn_forward(x, gain, w3, wout):
    T, D = x.shape
    F = wout.shape[0]
    h = _rmsnorm(x, gain)
    g, u, a = _ffn_in(h, w3)
    tm, tk = _tile(T, 512), F // 4
    xn = _mm("ffn_out", (T // tm, 1, F // tk),
             (a, (tm, tk), lambda i, j, k: (i, k)), (wout, (tk, D), lambda i, j, k: (k, 0)),
             (jax.ShapeDtypeStruct((T, D), F32), (tm, D), lambda i, j, k: (i, 0)), NN, scale=0.5,
             resid=(x, (tm, D), lambda i, j, k: (i, 0)))
    return xn, (x, h, g, u, a)


def _ffn_backward(dx, saved, w3, wout):
    x, h, g, u, a = saved
    T, D = x.shape
    F = wout.shape[0]
    tn3 = F // 4
    dgu = _ffn_bwd_in(dx, wout, g, u)
    tnd = _tile(D, 512)
    dwout = _mm("ffn_dwout", (F // tn3, D // tnd, 1),
                (a, (T, tn3), lambda i, j, k: (0, i)), (dx, (T, tnd), lambda i, j, k: (0, j)),
                (jax.ShapeDtypeStruct((F, D), BF), (tn3, tnd), lambda i, j, k: (i, j)), TN, scale=0.5)
    dw3 = _mm("ffn_dwin", (8, D // tnd, 1),
              (h, (T, tnd), lambda s, i, k: (0, i)), (dgu, (None, T, tn3), lambda s, i, k: (s // 4, 0, s % 4)),
              (jax.ShapeDtypeStruct((8, D, tn3), BF), (None, tnd, tn3), lambda s, i, k: (s, i, 0)), TN)
    tm = _tile(T, 512)
    dh = _mm("ffn_dh", (T // tm, 1, 8),
             (dgu, (None, tm, tn3), lambda i, j, k: (k // 4, i, k % 4)), (w3, (None, D, tn3), lambda i, j, k: (k, 0, 0)),
             (jax.ShapeDtypeStruct((T, D), F32), (tm, D), lambda i, j, k: (i, 0)), NT)
    return dh, dw3, dwout.reshape(N_DEV, F // N_DEV, D)


def _rows(shape):
    return lax.broadcasted_iota(jnp.int32, shape, 0)


def _down(x, s, fill, rows):
    return jnp.where(rows >= s, pltpu.roll(x, s, 0), fill)


def _up(x, s, fill, rows):
    T = x.shape[0]
    return jnp.where(rows < T - s, pltpu.roll(x, T - s, 0), fill)


def _scan_linear(a, b, rows, shift):
    T = a.shape[0]
    s = 1
    while s < T:
        b = a * shift(b, s, 0.0, rows) + b
        if 2 * s < T:
            a = a * shift(a, s, 1.0, rows)
        s *= 2
    return b


def _cumsum(c, rows, shift):
    T = c.shape[0]
    s = 1
    while s < T:
        c = c + shift(c, s, 0.0, rows)
        s *= 2
    return c


def _log1p_small(e):
    return jnp.where(e < 0.01, e * (1.0 - e * (0.5 - e * (1.0 / 3.0))), jnp.log(1.0 + e))


def _softplus(x):
    return jnp.maximum(x, 0.0) + _log1p_small(jnp.exp(-jnp.abs(x)))


def _one_minus_exp_neg(z):
    return jnp.where(z < 0.1, z * (1.0 - z * (0.5 - z * (1.0 / 6.0 - z * (1.0 / 24.0)))), 1.0 - jnp.exp(-z))


def _fgate_cum(f, b):
    T = f.shape[0]

    def body(f_ref, b_ref, o_ref):
        z = f_ref[...] + b_ref[...]
        o_ref[...] = _cumsum(-_softplus(-z), _rows(z.shape), _down)

    return pl.pallas_call(
        body, name="fgate_cumsum",
        out_shape=jax.ShapeDtypeStruct((T, LANES), F32),
        compiler_params=pltpu.CompilerParams(vmem_limit_bytes=VMEM_LIMIT_V7X),
    )(f, b)


def _fgate_cum_bwd(drow, dcol, f, b):
    T = f.shape[0]

    def body(dr_ref, dc_ref, f_ref, b_ref, df_ref, db_ref):
        z = f_ref[...] + b_ref[...]
        dlogf = _cumsum(dr_ref[...] - dc_ref[...], _rows(z.shape), _up)
        dz = dlogf * jax.nn.sigmoid(-z)
        df_ref[...] = dz.astype(BF)
        db_ref[...] = jnp.sum(dz, axis=0, keepdims=True)

    return pl.pallas_call(
        body, name="fgate_cumsum_bwd",
        out_shape=[jax.ShapeDtypeStruct((T, LANES), BF), jax.ShapeDtypeStruct((1, LANES), F32)],
        compiler_params=pltpu.CompilerParams(vmem_limit_bytes=VMEM_LIMIT_V7X),
    )(drow, dcol, f, b)


def _col(blk0):
    return lambda g: (0, blk0 + g)


def _conv_fwd(p, w, nb):
    T = p.shape[0]

    def body(b_ref, c_ref, v_ref, w_ref, o_ref):
        z = c_ref[...].astype(F32) * v_ref[...].astype(F32)
        rows = _rows(z.shape)
        conv = w_ref[2:3, :] * z + w_ref[1:2, :] * _down(z, 1, 0.0, rows) + w_ref[0:1, :] * _down(z, 2, 0.0, rows)
        o_ref[...] = b_ref[...].astype(F32) * conv

    return pl.pallas_call(
        body, name="conv_fwd", grid=(nb,),
        in_specs=[pl.BlockSpec((T, LANES), _col(0)), pl.BlockSpec((T, LANES), _col(nb)),
                  pl.BlockSpec((T, LANES), _col(2 * nb)), pl.BlockSpec((3, LANES), lambda g: (0, g))],
        out_specs=pl.BlockSpec((T, LANES), lambda g: (0, g)),
        out_shape=jax.ShapeDtypeStruct((T, nb * LANES), F32),
        compiler_params=_params("parallel"),
    )(p, p, p, w)


def _conv_bwd(dy, p, w, nb):
    T = p.shape[0]

    def body(dy_ref, b_ref, c_ref, v_ref, w_ref, db_ref, dc_ref, dv_ref, dw_ref):
        cv, vv = c_ref[...].astype(F32), v_ref[...].astype(F32)
        z = cv * vv
        rows = _rows(z.shape)
        z1, z2 = _down(z, 1, 0.0, rows), _down(z, 2, 0.0, rows)
        dyv = dy_ref[...]
        db_ref[...] = (dyv * (w_ref[2:3, :] * z + w_ref[1:2, :] * z1 + w_ref[0:1, :] * z2)).astype(BF)
        dconv = dyv * b_ref[...].astype(F32)
        dz = (w_ref[2:3, :] * dconv + w_ref[1:2, :] * _up(dconv, 1, 0.0, rows)
              + w_ref[0:1, :] * _up(dconv, 2, 0.0, rows))
        dc_ref[...] = (dz * vv).astype(BF)
        dv_ref[...] = (dz * cv).astype(BF)
        dw_ref[0:1, :] = jnp.sum(dconv * z2, axis=0, keepdims=True)
        dw_ref[1:2, :] = jnp.sum(dconv * z1, axis=0, keepdims=True)
        dw_ref[2:3, :] = jnp.sum(dconv * z, axis=0, keepdims=True)

    return pl.pallas_call(
        body, name="conv_bwd", grid=(nb,),
        in_specs=[pl.BlockSpec((T, LANES), lambda g: (0, g)), pl.BlockSpec((T, LANES), _col(0)),
                  pl.BlockSpec((T, LANES), _col(nb)), pl.BlockSpec((T, LANES), _col(2 * nb)),
                  pl.BlockSpec((3, LANES), lambda g: (0, g))],
        out_specs=[pl.BlockSpec((T, LANES), lambda g: (0, g))] * 3 + [pl.BlockSpec((3, LANES), lambda g: (0, g))],
        out_shape=[jax.ShapeDtypeStruct((T, nb * LANES), BF)] * 3 + [jax.ShapeDtypeStruct((3, nb * LANES), F32)],
        compiler_params=_params("parallel"),
    )(dy, p, p, p, w)


def _gelu(x):
    t = jnp.tanh(GELU_C * (x + GELU_K * x * x * x))
    return 0.5 * x * (1.0 + t), t


def _lru_common(x, cw_ref, cb_ref, wa_ref, ba_ref, wx_ref, bx_ref, lam_ref, rows):
    xr = (cb_ref[...] + cw_ref[3:4, :] * x + cw_ref[2:3, :] * _down(x, 1, 0.0, rows)
          + cw_ref[1:2, :] * _down(x, 2, 0.0, rows) + cw_ref[0:1, :] * _down(x, 3, 0.0, rows))
    xrb = xr.astype(BF)
    r = jax.nn.sigmoid(jnp.dot(xrb, wa_ref[...].astype(BF), preferred_element_type=F32) + ba_ref[...])
    i = jax.nn.sigmoid(jnp.dot(xrb, wx_ref[...].astype(BF), preferred_element_type=F32) + bx_ref[...])
    sp = _softplus(-lam_ref[...])
    log_a = -LRU_C * r * sp
    a = jnp.exp(log_a)
    m = jnp.sqrt(_one_minus_exp_neg(-2.0 * log_a))
    return xr, xrb, r, i, sp, a, m


def _lru_specs(T, nb, gate_blk0, x_blk0):
    vec = pl.BlockSpec((1, LANES), lambda g: (0, g))
    mat = pl.BlockSpec((None, LANES, LANES), lambda g: (g, 0, 0))
    return [pl.BlockSpec((T, LANES), _col(gate_blk0)), pl.BlockSpec((T, LANES), _col(x_blk0)),
            pl.BlockSpec((4, LANES), lambda g: (0, g)), vec, mat, vec, mat, vec, vec]


def _lru_fwd(p, cw, cb, wa, ba, wx, bx, lam, nb, gate_blk0):
    T = p.shape[0]

    def body(gate_ref, x_ref, cw_ref, cb_ref, wa_ref, ba_ref, wx_ref, bx_ref, lam_ref, y_ref, h_ref):
        x = x_ref[...].astype(F32)
        rows = _rows(x.shape)
        xr, _, _, i, _, a, m = _lru_common(x, cw_ref, cb_ref, wa_ref, ba_ref, wx_ref, bx_ref, lam_ref, rows)
        h = _scan_linear(a, m * (i * xr), rows, _down)
        h_ref[...] = h
        y_ref[...] = _gelu(gate_ref[...].astype(F32))[0] * h

    out = jax.ShapeDtypeStruct((T, nb * LANES), F32)
    return pl.pallas_call(
        body, name="lru_fwd", grid=(nb,),
        in_specs=_lru_specs(T, nb, gate_blk0, gate_blk0 + nb),
        out_specs=[pl.BlockSpec((T, LANES), lambda g: (0, g))] * 2,
        out_shape=[out, out],
        compiler_params=_params("parallel"),
    )(p, p, cw, cb, wa, ba, wx, bx, lam)


def _lru_bwd(dy, hs, p, cw, cb, wa, ba, wx, bx, lam, nb, gate_blk0):
    T = p.shape[0]

    def body(dy_ref, hs_ref, gate_ref, x_ref, cw_ref, cb_ref, wa_ref, ba_ref, wx_ref, bx_ref, lam_ref,
             dgate_ref, dx_ref, dcw_ref, dcb_ref, dwa_ref, dba_ref, dwx_ref, dbx_ref, dlam_ref):
        x = x_ref[...].astype(F32)
        rows = _rows(x.shape)
        xr, xrb, r, i, sp, a, m = _lru_common(x, cw_ref, cb_ref, wa_ref, ba_ref, wx_ref, bx_ref, lam_ref, rows)
        gate = gate_ref[...].astype(F32)
        gl, t = _gelu(gate)
        h = hs_ref[...]
        dyv = dy_ref[...]
        dgelu = 0.5 * (1.0 + t) + 0.5 * gate * (1.0 - t * t) * GELU_C * (1.0 + 3.0 * GELU_K * gate * gate)
        dgate_ref[...] = (dyv * h * dgelu).astype(BF)
        lam_adj = _scan_linear(_up(a, 1, 0.0, rows), dyv * gl, rows, _up)
        da = lam_adj * _down(h, 1, 0.0, rows)
        ix = i * xr
        dix = lam_adj * m
        dm = lam_adj * ix
        dlog_a = da * a - dm * (a * a) / jnp.maximum(m, 1e-30)
        dr = dlog_a * (-LRU_C * sp)
        dsp = jnp.sum(dlog_a * (-LRU_C * r), axis=0, keepdims=True)
        dlam_ref[...] = -dsp * jax.nn.sigmoid(-lam_ref[...])
        dpa = dr * r * (1.0 - r)
        dpx = dix * xr * i * (1.0 - i)
        dpab, dpxb = dpa.astype(BF), dpx.astype(BF)
        dxr = (dix * i
               + lax.dot_general(dpab, wa_ref[...].astype(BF), (NT, ((), ())), preferred_element_type=F32)
               + lax.dot_general(dpxb, wx_ref[...].astype(BF), (NT, ((), ())), preferred_element_type=F32))
        dwa_ref[...] = lax.dot_general(xrb, dpab, (TN, ((), ())), preferred_element_type=F32)
        dwx_ref[...] = lax.dot_general(xrb, dpxb, (TN, ((), ())), preferred_element_type=F32)
        dba_ref[...] = jnp.sum(dpa, axis=0, keepdims=True)
        dbx_ref[...] = jnp.sum(dpx, axis=0, keepdims=True)
        dcb_ref[...] = jnp.sum(dxr, axis=0, keepdims=True)
        dx_ref[...] = (cw_ref[3:4, :] * dxr + cw_ref[2:3, :] * _up(dxr, 1, 0.0, rows)
                       + cw_ref[1:2, :] * _up(dxr, 2, 0.0, rows) + cw_ref[0:1, :] * _up(dxr, 3, 0.0, rows)).astype(BF)
        for k in range(4):
            xs = x if k == 3 else _down(x, 3 - k, 0.0, rows)
            dcw_ref[k:k + 1, :] = jnp.sum(dxr * xs, axis=0, keepdims=True)

    C = nb * LANES
    seq = jax.ShapeDtypeStruct((T, C), BF)
    vec = jax.ShapeDtypeStruct((1, C), F32)
    mat = jax.ShapeDtypeStruct((nb, LANES, LANES), F32)
    vspec = pl.BlockSpec((1, LANES), lambda g: (0, g))
    mspec = pl.BlockSpec((None, LANES, LANES), lambda g: (g, 0, 0))
    sspec = pl.BlockSpec((T, LANES), lambda g: (0, g))
    return pl.pallas_call(
        body, name="lru_bwd", grid=(nb,),
        in_specs=[sspec, sspec] + _lru_specs(T, nb, gate_blk0, gate_blk0 + nb),
        out_specs=[sspec, sspec, pl.BlockSpec((4, LANES), lambda g: (0, g)), vspec, mspec, vspec, mspec, vspec, vspec],
        out_shape=[seq, seq, jax.ShapeDtypeStruct((4, C), F32), vec, mat, vec, mat, vec, vec],
        compiler_params=_params("parallel"),
    )(dy, hs, p, p, cw, cb, wa, ba, wx, bx, lam)


def _causal(shape, transposed=False):
    r = lax.broadcasted_iota(jnp.int32, shape, 0)
    c = lax.broadcasted_iota(jnp.int32, shape, 1)
    return r <= c if transposed else c <= r


def _attn_fwd(p, cq, ck, nh, q_blk0):
    T = p.shape[0]
    tq = _tile(T, 512)
    nq = T // tq
    scale = LANES ** -0.5

    def body(q_ref, k_ref, v_ref, cq_ref, ck_ref, o_ref, lse_ref, m_ref, l_ref, acc_ref):
        i, j = pl.program_id(1), pl.program_id(2)

        @pl.when(j == 0)
        def _():
            m_ref[...] = jnp.full(m_ref.shape, -jnp.inf, F32)
            l_ref[...] = jnp.zeros(l_ref.shape, F32)
            acc_ref[...] = jnp.zeros(acc_ref.shape, F32)

        def block(diagonal):
            s = lax.dot_general(q_ref[...], k_ref[...], (NT, ((), ())), preferred_element_type=F32) * scale
            s = s + cq_ref[...] - ck_ref[...]
            if diagonal:
                s = jnp.where(_causal(s.shape), s, -jnp.inf)
            m_new = jnp.maximum(m_ref[...], jnp.max(s, axis=-1, keepdims=True))
            alpha = jnp.exp(m_ref[...] - m_new)
            pr = jnp.exp(s - m_new)
            l_ref[...] = alpha * l_ref[...] + jnp.sum(pr, axis=-1, keepdims=True)
            acc_ref[...] = alpha * acc_ref[...] + jnp.dot(pr.astype(BF), v_ref[...], preferred_element_type=F32)
            m_ref[...] = m_new

        pl.when(j < i)(lambda: block(False))
        pl.when(j == i)(lambda: block(True))

        @pl.when(j == nq - 1)
        def _():
            o_ref[...] = acc_ref[...] / l_ref[...]
            lse_ref[...] = m_ref[...] + jnp.log(l_ref[...])

    def kv(off):
        return pl.BlockSpec((tq, LANES), lambda h, i, j: (jnp.minimum(j, i), q_blk0 + off * nh + h))

    return pl.pallas_call(
        body, name="attn_fwd", grid=(nh, nq, nq),
        in_specs=[pl.BlockSpec((tq, LANES), lambda h, i, j: (i, q_blk0 + h)), kv(1), kv(2),
                  pl.BlockSpec((None, tq, 1), lambda h, i, j: (h, i, 0)),
                  pl.BlockSpec((None, 1, tq), lambda h, i, j: (h, 0, jnp.minimum(j, i)))],
        out_specs=[pl.BlockSpec((tq, LANES), lambda h, i, j: (i, h)),
                   pl.BlockSpec((None, tq, 1), lambda h, i, j: (h, i, 0))],
        out_shape=[jax.ShapeDtypeStruct((T, nh * LANES), F32), jax.ShapeDtypeStruct((nh, T, 1), F32)],
        scratch_shapes=[pltpu.VMEM((tq, 1), F32), pltpu.VMEM((tq, 1), F32), pltpu.VMEM((tq, LANES), F32)],
        compiler_params=_params("parallel", "parallel", "arbitrary"),
    )(p, p, p, cq, ck)


def _attn_bwd_q(p, cq, ck, lse, do, o, nh, q_blk0):
    T = p.shape[0]
    tq = _tile(T, 512)
    nq = T // tq
    scale = LANES ** -0.5

    def body(q_ref, k_ref, v_ref, cq_ref, ck_ref, lse_ref, do_ref, o_ref, dq_ref, dl_ref, dr_ref, acc_ref):
        i, j = pl.program_id(1), pl.program_id(2)

        @pl.when(j == 0)
        def _():
            dl_ref[...] = jnp.sum(do_ref[...] * o_ref[...], axis=-1, keepdims=True)
            dr_ref[...] = jnp.zeros(dr_ref.shape, F32)
            acc_ref[...] = jnp.zeros(acc_ref.shape, F32)

        def block(diagonal):
            s = lax.dot_general(q_ref[...], k_ref[...], (NT, ((), ())), preferred_element_type=F32) * scale
            s = s + cq_ref[...] - ck_ref[...]
            pr = jnp.exp(s - lse_ref[...])
            if diagonal:
                pr = jnp.where(_causal(s.shape), pr, 0.0)
            dp = lax.dot_general(do_ref[...].astype(BF), v_ref[...], (NT, ((), ())), preferred_element_type=F32)
            ds = pr * (dp - dl_ref[...])
            dr_ref[...] += jnp.sum(ds, axis=-1, keepdims=True)
            acc_ref[...] += jnp.dot(ds.astype(BF), k_ref[...], preferred_element_type=F32)

        pl.when(j < i)(lambda: block(False))
        pl.when(j == i)(lambda: block(True))

        @pl.when(j == nq - 1)
        def _():
            dq_ref[...] = (acc_ref[...] * scale).astype(BF)

    def kv(off):
        return pl.BlockSpec((tq, LANES), lambda h, i, j: (jnp.minimum(j, i), q_blk0 + off * nh + h))

    col = pl.BlockSpec((None, tq, 1), lambda h, i, j: (h, i, 0))
    head = pl.BlockSpec((tq, LANES), lambda h, i, j: (i, h))
    return pl.pallas_call(
        body, name="attn_bwd_q", grid=(nh, nq, nq),
        in_specs=[pl.BlockSpec((tq, LANES), lambda h, i, j: (i, q_blk0 + h)), kv(1), kv(2), col,
                  pl.BlockSpec((None, 1, tq), lambda h, i, j: (h, 0, jnp.minimum(j, i))), col, head, head],
        out_specs=[head, col, col],
        out_shape=[jax.ShapeDtypeStruct((T, nh * LANES), BF), jax.ShapeDtypeStruct((nh, T, 1), F32),
                   jax.ShapeDtypeStruct((nh, T, 1), F32)],
        scratch_shapes=[pltpu.VMEM((tq, LANES), F32)],
        compiler_params=_params("parallel", "parallel", "arbitrary"),
    )(p, p, p, cq, ck, lse, do, o)


def _attn_bwd_kv(p, cq_row, ck_col, lse_row, delta_row, do, nh, q_blk0):
    T = p.shape[0]
    tk = _tile(T, 512)
    nk = T // tk
    scale = LANES ** -0.5

    def body(q_ref, k_ref, v_ref, cq_ref, ck_ref, lse_ref, dl_ref, do_ref, dk_ref, dv_ref, dc_ref,
             dk_acc, dv_acc, dc_acc):
        j, i = pl.program_id(1), pl.program_id(2)

        @pl.when(i == 0)
        def _():
            dk_acc[...] = jnp.zeros(dk_acc.shape, F32)
            dv_acc[...] = jnp.zeros(dv_acc.shape, F32)
            dc_acc[...] = jnp.zeros(dc_acc.shape, F32)

        def block(diagonal):
            st = lax.dot_general(k_ref[...], q_ref[...], (NT, ((), ())), preferred_element_type=F32) * scale
            st = st + cq_ref[...] - ck_ref[...]
            pt = jnp.exp(st - lse_ref[...])
            if diagonal:
                pt = jnp.where(_causal(st.shape, transposed=True), pt, 0.0)
            dob = do_ref[...].astype(BF)
            dv_acc[...] += jnp.dot(pt.astype(BF), dob, preferred_element_type=F32)
            dpt = lax.dot_general(v_ref[...], dob, (NT, ((), ())), preferred_element_type=F32)
            dst = pt * (dpt - dl_ref[...])
            dk_acc[...] += jnp.dot(dst.astype(BF), q_ref[...], preferred_element_type=F32)
            dc_acc[...] += jnp.sum(dst, axis=-1, keepdims=True)

        pl.when(i > j)(lambda: block(False))
        pl.when(i == j)(lambda: block(True))

        @pl.when(i == nk - 1)
        def _():
            dk_ref[...] = (dk_acc[...] * scale).astype(BF)
            dv_ref[...] = dv_acc[...].astype(BF)
            dc_ref[...] = dc_acc[...]

    def qside(blk):
        return pl.BlockSpec((tk, LANES), lambda h, j, i: (jnp.maximum(i, j), blk + h))

    def kside(off):
        return pl.BlockSpec((tk, LANES), lambda h, j, i: (j, q_blk0 + off * nh + h))

    row = pl.BlockSpec((None, 1, tk), lambda h, j, i: (h, 0, jnp.maximum(i, j)))
    col = pl.BlockSpec((None, tk, 1), lambda h, j, i: (h, j, 0))
    head = pl.BlockSpec((tk, LANES), lambda h, j, i: (j, h))
    return pl.pallas_call(
        body, name="attn_bwd_kv", grid=(nh, nk, nk),
        in_specs=[qside(q_blk0), kside(1), kside(2), row, col, row, row, qside(0)],
        out_specs=[head, head, col],
        out_shape=[jax.ShapeDtypeStruct((T, nh * LANES), BF), jax.ShapeDtypeStruct((T, nh * LANES), BF),
                   jax.ShapeDtypeStruct((nh, T, 1), F32)],
        scratch_shapes=[pltpu.VMEM((tk, LANES), F32), pltpu.VMEM((tk, LANES), F32), pltpu.VMEM((tk, 1), F32)],
        compiler_params=_params("parallel", "parallel", "arbitrary"),
    )(p, p, p, cq_row, ck_col, lse_row, delta_row, do)


def _mixer_dims(D):
    dc, da, dl = D // 4, D // 2, D // 4
    nh = da // LANES
    n_main = 3 * dc + 3 * da + 2 * dl
    return dc, da, dl, nh, n_main


def _pad_mix_w_in(wfull):
    D = wfull.shape[0]
    dc, da, dl, nh, n_main = _mixer_dims(D)
    a = 3 * dc + 3 * da
    return jnp.concatenate([wfull[:, :a], wfull[:, a + nh:], wfull[:, a:a + nh],
                            jnp.zeros((D, LANES - nh), wfull.dtype)], axis=1)


def _unpad_mix_w_in(wp):
    D = wp.shape[0]
    dc, da, dl, nh, n_main = _mixer_dims(D)
    a = 3 * dc + 3 * da
    return jnp.concatenate([wp[:, :a], wp[:, n_main:n_main + nh], wp[:, a:n_main]], axis=1)


def _head_cols(c, nh):
    t = jnp.transpose(c[:, :nh])
    return t[:, :, None], t[:, None, :]


def _mixer_forward(x, gain, wp, wo, sp):
    T, D = x.shape
    dc, da, dl, nh, n_main = _mixer_dims(D)
    nbc, nbl = dc // LANES, dl // LANES
    h = _rmsnorm(x, gain)
    tm = _tile(T, 512)
    tn = n_main // 4
    p = _mm("mix_in", (4, T // tm, 1),
            (h, (tm, D), lambda j, i, k: (i, 0)), (wp, (D, tn), lambda j, i, k: (0, j)),
            (jax.ShapeDtypeStruct((T, n_main), BF), (tm, tn), lambda j, i, k: (i, j)), NN)
    f = _mm("mix_in_fgate", (T // tm, 1, 1),
            (h, (tm, D), lambda i, j, k: (i, 0)), (wp, (D, LANES), lambda i, j, k: (0, n_main // LANES)),
            (jax.ShapeDtypeStruct((T, LANES), F32), (tm, LANES), lambda i, j, k: (i, 0)), NN)
    cum = _fgate_cum(f, sp["fgate_b"])
    cq, ck = _head_cols(cum, nh)
    yc = _conv_fwd(p, sp["conv_w"], nbc)
    q_blk0 = 3 * nbc
    ya, lse = _attn_fwd(p, cq, ck, nh, q_blk0)
    gate_blk0 = q_blk0 + 3 * nh
    yl, hs = _lru_fwd(p, sp["lru_conv_w"], sp["lru_conv_b"], sp["lru_w_a"], sp["lru_b_a"], sp["lru_w_x"],
                      sp["lru_b_x"], sp["lru_lambda"], nbl, gate_blk0)
    yn = _groupnorm(yc, ya, yl, sp["mix_out_norm"])
    xn = _mm("mix_out", (T // tm, 1, 1),
             (yn, (tm, D), lambda i, j, k: (i, 0)), (wo, (D, D), lambda i, j, k: (0, 0)),
             (jax.ShapeDtypeStruct((T, D), F32), (tm, D), lambda i, j, k: (i, 0)), NN,
             resid=(x, (tm, D), lambda i, j, k: (i, 0)))
    return xn, (x, h, p, f, cq, ck, yc, ya, lse, yl, hs, yn)


def _mixer_backward(dx, saved, gain, wp, wo, sp):
    x, h, p, f, cq, ck, yc, ya, lse, yl, hs, yn = saved
    T, D = x.shape
    dc, da, dl, nh, n_main = _mixer_dims(D)
    nbc, nbl = dc // LANES, dl // LANES
    q_blk0 = 3 * nbc
    gate_blk0 = q_blk0 + 3 * nh
    tm = _tile(T, 512)
    tnd, tnd2 = _tile(D, 512), _tile(D, 1024)
    dyn = _mm("mix_out_dy", (T // tm, 1, 1),
              (dx, (tm, D), lambda i, j, k: (i, 0)), (wo, (D, D), lambda i, j, k: (0, 0)),
              (jax.ShapeDtypeStruct((T, D), F32), (tm, D), lambda i, j, k: (i, 0)), NT)
    dwo = _mm("mix_out_dw", (D // tnd, D // tnd2, 1),
              (yn, (T, tnd), lambda i, j, k: (0, i)), (dx, (T, tnd2), lambda i, j, k: (0, j)),
              (jax.ShapeDtypeStruct((D, D), BF), (tnd, tnd2), lambda i, j, k: (i, j)), TN)
    dyc, dya, dyl, dgn = _groupnorm_bwd(dyn, yc, ya, yl, sp["mix_out_norm"])
    dcb, dcc, dcv, dconv_w = _conv_bwd(dyc, p, sp["conv_w"], nbc)
    dq, delta, drow = _attn_bwd_q(p, cq, ck, lse, dya, ya, nh, q_blk0)
    as_row = lambda t: t.reshape(nh, 1, T)
    dk, dv, dcol = _attn_bwd_kv(p, ck, cq, as_row(lse), as_row(delta), dya, nh, q_blk0)
    lanes = lambda t: jnp.pad(jnp.transpose(t.reshape(nh, T)), ((0, 0), (0, LANES - nh)))
    df, dfb = _fgate_cum_bwd(lanes(drow), lanes(dcol), f, sp["fgate_b"])
    dgate, dlx, dcw, dcbias, dwa, dba, dwx, dbx, dlam = _lru_bwd(
        dyl, hs, p, sp["lru_conv_w"], sp["lru_conv_b"], sp["lru_w_a"], sp["lru_b_a"], sp["lru_w_x"],
        sp["lru_b_x"], sp["lru_lambda"], nbl, gate_blk0)
    dp = jnp.concatenate([dcb, dcc, dcv, dq, dk, dv, dgate, dlx, df], axis=1)
    n_pad = n_main + LANES
    tn, tkp = n_pad // 5, n_pad // 3
    dwp = _mm("mix_in_dw", (n_pad // tn, D // tnd, 1),
              (h, (T, tnd), lambda j, i, k: (0, i)), (dp, (T, tn), lambda j, i, k: (0, j)),
              (jax.ShapeDtypeStruct((D, n_pad), BF), (tnd, tn), lambda j, i, k: (i, j)), TN)
    dh = _mm("mix_in_dh", (T // tm, 1, n_pad // tkp),
             (dp, (tm, tkp), lambda i, j, k: (i, k)), (wp, (D, tkp), lambda i, j, k: (0, k)),
             (jax.ShapeDtypeStruct((T, D), F32), (tm, D), lambda i, j, k: (i, 0)), NT)
    dxn, dgain = _rmsnorm_bwd(dh, x, gain, dx)
    small = {"norm_mix": dgain, "mix_out_norm": dgn, "conv_w": dconv_w, "fgate_b": dfb[:, :nh],
             "lru_conv_w": dcw, "lru_conv_b": dcbias, "lru_w_a": dwa, "lru_b_a": dba, "lru_w_x": dwx,
             "lru_b_x": dbx, "lru_lambda": dlam}
    return dxn, dwp, dwo, small


def _position():
    x, y, c = lax.axis_index("x"), lax.axis_index("y"), lax.axis_index("c")
    return x, y, c, [(1 - x, y), (x, 1 - y), (1 - x, 1 - y)]


def _all_gather(name, xs):
    n = len(xs)

    def body(*refs):
        x_refs, o_refs = refs[:n], refs[n:2 * n]
        send_sems, recv_sems, local_sems = refs[2 * n:]
        x, y, c, chips = _position()
        me, sibling = (x, y, c), (x, y, 1 - c)

        def copy(t, k, block, to, src=None):
            dst = o_refs[t].at[4 * block[0] + 2 * block[1] + block[2]]
            return pltpu.make_async_remote_copy(
                src_ref=dst if src is None else src, dst_ref=dst, send_sem=send_sems.at[t, k],
                recv_sem=recv_sems.at[t, k], device_id=to, device_id_type=MESH)

        mine = [pltpu.make_async_copy(x_refs[t], o_refs[t].at[4 * x + 2 * y + c], local_sems.at[t]) for t in range(n)]
        for cp in mine:
            cp.start()
        first = []
        for t in range(n):
            first.append(copy(t, 0, me, sibling, src=x_refs[t]))
            first += [copy(t, 1 + j, me, (*chip, c), src=x_refs[t]) for j, chip in enumerate(chips)]
        for cp in first:
            cp.start()
        passed = []
        for j, chip in enumerate(chips):
            for t in range(n):
                copy(t, 1 + j, (*chip, c), me).wait_recv()
                passed.append(copy(t, 4 + j, (*chip, c), sibling))
                passed[-1].start()
        for t in range(n):
            copy(t, 0, sibling, me).wait_recv()
            for j, chip in enumerate(chips):
                copy(t, 4 + j, (*chip, 1 - c), me).wait_recv()
        for cp in first + passed:
            cp.wait_send()
        for cp in mine:
            cp.wait()

    return pl.pallas_call(
        body, name=name,
        in_specs=[_any()] * n, out_specs=[_any()] * n,
        out_shape=[jax.ShapeDtypeStruct((N_DEV,) + v.shape, v.dtype) for v in xs],
        scratch_shapes=[pltpu.SemaphoreType.DMA((n, 7)), pltpu.SemaphoreType.DMA((n, 7)),
                        pltpu.SemaphoreType.DMA((n,))],
    )(*xs)


def _add_sibling(g4, recv, c_idx):
    _, _, R, C = g4.shape
    tr = _tile(R, 256)

    def body(c_ref, g_ref, r_ref, o_ref):
        o_ref[...] = (g_ref[...].astype(F32) + r_ref[...].astype(F32)).astype(o_ref.dtype)

    return pl.pallas_call(
        body, name="rs_add_sibling",
        grid_spec=pltpu.PrefetchScalarGridSpec(
            num_scalar_prefetch=1, grid=(4, R // tr),
            in_specs=[pl.BlockSpec((None, None, tr, C), lambda q, i, c_ref: (q, c_ref[0], i, 0)),
                      pl.BlockSpec((None, None, tr, C), lambda q, i, c_ref: (q, 0, i, 0))],
            out_specs=pl.BlockSpec((None, tr, C), lambda q, i, c_ref: (q, i, 0))),
        out_shape=jax.ShapeDtypeStruct((4, R, C), g4.dtype),
        compiler_params=_params("parallel", "parallel"),
    )(c_idx, g4, recv)


def _hbm_spec():
    return pl.BlockSpec(memory_space=pltpu.HBM)


def _sem_spec():
    return pl.BlockSpec(memory_space=pltpu.SEMAPHORE)


def _side_effects():
    return pltpu.CompilerParams(has_side_effects=pltpu.SideEffectType.DATAFLOW_SIDE_EFFECTING)


def _in_hbm(v):
    return pltpu.with_memory_space_constraint(v, pltpu.HBM)


def _split_start(name, srcs, lands, n_copies, copies_of, after):
    n = len(srcs)
    ns = n * n_copies

    def body(*refs):
        src_refs, land_refs = refs[:n], refs[n:2 * n]
        send_sems, recv_sems = refs[2 * n + 1:2 * n + 1 + ns], refs[2 * n + 1 + ns:2 * n + 1 + 2 * ns]
        token = refs[-1]
        for t in range(n):
            for k, (s, d, to) in enumerate(copies_of(t, src_refs[t], land_refs[t])):
                pltpu.make_async_remote_copy(src_ref=s, dst_ref=d, send_sem=send_sems[t * n_copies + k],
                                             recv_sem=recv_sems[t * n_copies + k], device_id=to,
                                             device_id_type=MESH).start()
        token[...] = jnp.zeros(token.shape, token.dtype)

    thru = [pltpu.HBM(v.shape, v.dtype) for v in list(srcs) + list(lands)]
    out = pl.pallas_call(
        body, name=name,
        in_specs=[_hbm_spec()] * (2 * n) + [_any()],
        out_specs=[_sem_spec()] * (2 * ns) + [_hbm_spec()] * (2 * n) + [pl.BlockSpec(memory_space=pltpu.VMEM)],
        out_shape=[pltpu.SemaphoreType.DMA(())] * (2 * ns) + thru + [jax.ShapeDtypeStruct((8, LANES), F32)],
        input_output_aliases={i: 2 * ns + i for i in range(2 * n)},
        compiler_params=_side_effects(),
    )(*[_in_hbm(v) for v in list(srcs) + list(lands)], after)
    return out[:ns], out[ns:2 * ns], out[2 * ns:2 * ns + n], out[2 * ns + n:2 * ns + 2 * n], out[-1]


def _split_wait(name, send_sems, recv_sems, srcs, lands, n_copies, waits_of, after):
    n = len(srcs)
    ns = n * n_copies

    def body(*refs):
        src_refs, land_refs = refs[:n], refs[n:2 * n]
        send_refs, recv_refs = refs[2 * n:2 * n + ns], refs[2 * n + ns:2 * n + 2 * ns]
        x, y, c, _ = _position()
        for t in range(n):
            for k, (s, d) in enumerate(waits_of(t, src_refs[t], land_refs[t])):
                cp = pltpu.make_async_remote_copy(src_ref=s, dst_ref=d, send_sem=send_refs[t * n_copies + k],
                                                  recv_sem=recv_refs[t * n_copies + k], device_id=(x, y, 1 - c),
                                                  device_id_type=MESH)
                cp.wait_send()
                cp.wait_recv()

    out = pl.pallas_call(
        body, name=name,
        in_specs=[_hbm_spec()] * (2 * n) + [_sem_spec()] * (2 * ns) + [_any()],
        out_specs=[_hbm_spec()] * (2 * n),
        out_shape=[pltpu.HBM(v.shape, v.dtype) for v in list(srcs) + list(lands)],
        input_output_aliases={i: i for i in range(2 * n)},
        compiler_params=_side_effects(),
    )(*srcs, *lands, *send_sems, *recv_sems, after)
    return out[:n], out[n:]


def _block_of(px, py, pc):
    return 4 * px + 2 * py + pc


def _gather_phase1_start(name, xs, after):
    lands = [lax.empty((N_DEV,) + v.shape, v.dtype) for v in xs]

    def copies_of(t, x_ref, land_ref):
        x, y, c, chips = _position()
        dst = land_ref.at[_block_of(x, y, c)]
        return [(x_ref, dst, (x, y, 1 - c))] + [(x_ref, dst, (*chip, c)) for chip in chips]

    return _split_start(name, xs, lands, 4, copies_of, after)


def _gather_phase1_wait(name, started, after):
    send_sems, recv_sems, xs, lands, _ = started

    def waits_of(t, x_ref, land_ref):
        x, y, c, chips = _position()
        return [(x_ref, land_ref.at[_block_of(x, y, 1 - c)])] + [(x_ref, land_ref.at[_block_of(*chip, c)])
                                                                  for chip in chips]

    return _split_wait(name, send_sems, recv_sems, xs, lands, 4, waits_of, after)


def _gather_phase2_start(name, lands, after):
    keep = [lax.empty((8, LANES), v.dtype) for v in lands]

    def copies_of(t, _, land_ref):
        x, y, c, chips = _position()
        return [(land_ref.at[_block_of(*chip, c)], land_ref.at[_block_of(*chip, c)], (x, y, 1 - c)) for chip in chips]

    return _split_start(name, keep, lands, 3, copies_of, after)


def _gather_phase2_wait(name, started, after):
    send_sems, recv_sems, keep, lands, _ = started

    def waits_of(t, _, land_ref):
        x, y, c, chips = _position()
        return [(land_ref.at[_block_of(*chip, c)], land_ref.at[_block_of(*chip, 1 - c)]) for chip in chips]

    return _split_wait(name, send_sems, recv_sems, keep, lands, 3, waits_of, after)[1]


def _place_block(name, land, src, block_idx):
    R, C = src.shape
    tr = _tile(R, 512)

    def body(b_ref, land_ref, s_ref, o_ref):
        o_ref[...] = s_ref[...]

    return pl.pallas_call(
        body, name=name,
        grid_spec=pltpu.PrefetchScalarGridSpec(
            num_scalar_prefetch=1, grid=(R // tr,),
            in_specs=[_any(), pl.BlockSpec((tr, C), lambda i, b: (i, 0))],
            out_specs=pl.BlockSpec((None, tr, C), lambda i, b: (b[0], i, 0))),
        out_shape=jax.ShapeDtypeStruct(land.shape, land.dtype),
        input_output_aliases={1: 0},
        compiler_params=_params("parallel"),
    )(block_idx, land, src)


def _sibling_start(name, gs, after):
    g4 = [g.reshape((4, 2) + g.shape[1:]) for g in gs]
    lands = [lax.empty((4, 1) + g.shape[2:], g.dtype) for g in g4]

    def copies_of(t, g_ref, land_ref):
        x, y, c, _ = _position()
        return [(g_ref.at[:, pl.ds(1 - c, 1)], land_ref, (x, y, 1 - c))]

    return _split_start(name, g4, lands, 1, copies_of, after)


def _sibling_wait(name, started, after):
    send_sems, recv_sems, g4, lands, _ = started

    def waits_of(t, g_ref, land_ref):
        x, y, c, _ = _position()
        return [(g_ref.at[:, pl.ds(1 - c, 1)], land_ref)]

    return _split_wait(name, send_sems, recv_sems, g4, lands, 1, waits_of, after)


def _scatter_chips_start(name, ps, after):
    lands = [lax.empty((3,) + v.shape[1:], v.dtype) for v in ps]

    def copies_of(t, p_ref, land_ref):
        x, y, c, chips = _position()
        return [(p_ref.at[2 * chip[0] + chip[1]], land_ref.at[k], (*chip, c)) for k, chip in enumerate(chips)]

    return _split_start(name, ps, lands, 3, copies_of, after)


def _scatter_chips_wait(name, started, after):
    send_sems, recv_sems, ps, lands, _ = started

    def waits_of(t, p_ref, land_ref):
        x, y, c, chips = _position()
        return [(p_ref.at[2 * chip[0] + chip[1]], land_ref.at[k]) for k, chip in enumerate(chips)]

    return _split_wait(name, send_sems, recv_sems, ps, lands, 3, waits_of, after)


def _scatter_chips_after_sibling(tag, sibling, c_idx, after):
    g4, recv = _sibling_wait("rs_sibling_wait_" + tag, sibling, after)
    ps = [_add_sibling(g, r, c_idx) for g, r in zip(g4, recv)]
    return _scatter_chips_start("rs_chips_start_" + tag, ps, ps[0])


def _sum_devices(parts):
    _, R, C = parts.shape
    tr = _tile(R, 512)

    def body(p_ref, o_ref):
        acc = p_ref[0]
        for d in range(1, N_DEV):
            acc = acc + p_ref[d]
        o_ref[...] = acc

    return pl.pallas_call(
        body, name="sum_devices", grid=(R // tr,),
        in_specs=[pl.BlockSpec((N_DEV, tr, C), lambda i: (0, i, 0))],
        out_specs=pl.BlockSpec((tr, C), lambda i: (i, 0)),
        out_shape=jax.ShapeDtypeStruct((R, C), F32),
        compiler_params=_params("parallel"),
    )(parts)


def _adam_math(w, g, m, v):
    m = ADAM_B1 * m + (1.0 - ADAM_B1) * g
    v = ADAM_B2 * v + (1.0 - ADAM_B2) * (g * g)
    m_hat = m / (1.0 - ADAM_B1 ** ADAM_STEP)
    v_hat = v / (1.0 - ADAM_B2 ** ADAM_STEP)
    return -ADAM_LR * (m_hat / (jnp.sqrt(v_hat) + ADAM_EPS) + ADAM_WD * w), m, v


def _adam_layer(name, layer, w, m, v, own, landed, chip_idx, prev, after):
    L, R, C = w.shape
    tr = _tile(R, 128)
    slab = pl.BlockSpec((None, tr, C), lambda i, q: (layer, i, 0))

    def body(q_ref, w_ref, m_ref, v_ref, own_ref, land_ref, after_ref, *rest):
        g_ref, d_ref, nm_ref, nv_ref = rest[-4:]
        g = land_ref[0].astype(F32)
        for k in range(1, 3):
            g = g + land_ref[k].astype(F32)
        g = g + own_ref[...].astype(F32)
        d, nm, nv = _adam_math(w_ref[...], g, m_ref[...], v_ref[...])
        g_ref[...] = g
        d_ref[...] = d
        nm_ref[...] = nm
        nv_ref[...] = nv

    n_prev = 0 if prev is None else 4
    out = jax.ShapeDtypeStruct((L, R, C), F32)
    return pl.pallas_call(
        body, name=name,
        grid_spec=pltpu.PrefetchScalarGridSpec(
            num_scalar_prefetch=1, grid=(R // tr,),
            in_specs=[slab, slab, slab, pl.BlockSpec((None, tr, C), lambda i, q: (q[0], i, 0)),
                      pl.BlockSpec((3, tr, C), lambda i, q: (0, i, 0)), _any()] + [_any()] * n_prev,
            out_specs=[slab] * 4),
        out_shape=[out] * 4,
        input_output_aliases={7 + k: k for k in range(n_prev)},
        compiler_params=_params("parallel"),
    )(chip_idx, w, m, v, own, landed, after, *(prev or ()))


def _adam_small(w, g, m, v):
    R, C = w.shape
    tr = _tile(R, 512)
    spec = pl.BlockSpec((tr, C), lambda i: (i, 0))

    def body(w_ref, g_ref, m_ref, v_ref, d_ref, nm_ref, nv_ref):
        d, nm, nv = _adam_math(w_ref[...], g_ref[...], m_ref[...], v_ref[...])
        d_ref[...] = d
        nm_ref[...] = nm
        nv_ref[...] = nv

    out = jax.ShapeDtypeStruct((R, C), F32)
    return pl.pallas_call(
        body, name="adam_small", grid=(R // tr,),
        in_specs=[spec] * 4, out_specs=[spec] * 3, out_shape=[out] * 3,
        compiler_params=_params("parallel"),
    )(w, g, m, v)


def _pack(arrays, row_multiple=512):
    flat = jnp.concatenate([a.reshape(-1).astype(F32) for a in arrays])
    per = row_multiple * LANES
    total = -(-flat.shape[0] // per) * per
    return jnp.pad(flat, (0, total - flat.shape[0])).reshape(total // LANES, LANES)


def _unpack(packed, shapes):
    flat = packed.reshape(-1)
    out, off = [], 0
    for s in shapes:
        n = math.prod(s)
        out.append(flat[off:off + n].reshape(s))
        off += n
    return out


BIG = ["ffn1_w_in", "ffn1_w_out", "mix_w_in", "mix_w_out", "ffn2_w_in", "ffn2_w_out"]
SHARDED_SMALL = ["conv_w", "lru_conv_w"]
WEIGHTS = ["norm_ffn1", "ffn1_w_in", "ffn1_w_out", "norm_mix", "mix_w_in", "conv_w", "fgate_b", "lru_conv_w",
           "lru_conv_b", "lru_w_a", "lru_b_a", "lru_w_x", "lru_b_x", "lru_lambda", "mix_out_norm", "mix_w_out",
           "norm_ffn2", "ffn2_w_in", "ffn2_w_out", "final_norm"]
REPLICATED = [n for n in WEIGHTS if n not in BIG and n not in SHARDED_SMALL]


def _step(x, target, w, m, v):
    L = w["norm_ffn1"].shape[0]
    T, D = x.shape[1], x.shape[2]
    dc, da, dl, nh, n_main = _mixer_dims(D)
    xi, yi, ci = lax.axis_index("x"), lax.axis_index("y"), lax.axis_index("c")
    me = 4 * xi + 2 * yi + ci
    c_idx = ci.astype(jnp.int32).reshape(1)

    taps = jnp.concatenate([w["conv_w"].reshape(-1), w["lru_conv_w"].reshape(-1)])
    taps = jnp.pad(taps, (0, (-taps.shape[0]) % (8 * LANES))).reshape(-1, LANES)
    def shards_of(l):
        return [w[n][l].astype(BF) for n in BIG] + ([taps] if l == 0 else [])

    me_idx = me.astype(jnp.int32).reshape(1)
    chip_idx = (2 * xi + yi).astype(jnp.int32).reshape(1)

    def place_own(tag, lands, own):
        return [_place_block("gather_own_" + tag, land, src, me_idx) for land, src in zip(lands, own)]

    gathered = [None] * L
    phase1 = [None] * L
    phase1[0] = _gather_phase1_start("gather_p1_start_0", shards_of(0), taps)
    own, landed = _gather_phase1_wait("gather_p1_wait_0", phase1[0], phase1[0][-1])
    phase2 = _gather_phase2_start("gather_p2_start_0", landed, landed[0])
    order_token = phase2[-1]
    if L > 1:
        phase1[1] = _gather_phase1_start("gather_p1_start_1", shards_of(1), phase2[-1])
        order_token = phase1[1][-1]
    landed = _gather_phase2_wait("gather_p2_wait_0", phase2, order_token)
    gathered[0] = place_own("0", landed, own)
    taps_all = gathered[0][len(BIG)].reshape(N_DEV, -1)
    n_cw = math.prod(w["conv_w"].shape)
    ch = w["conv_w"].shape[-1]
    conv_w_full = jnp.moveaxis(taps_all[:, :n_cw].reshape((N_DEV,) + w["conv_w"].shape), 0, -2).reshape(L, 3, N_DEV * ch)
    n_lw = math.prod(w["lru_conv_w"].shape)
    lru_conv_w_full = jnp.moveaxis(taps_all[:, n_cw:n_cw + n_lw].reshape((N_DEV,) + w["lru_conv_w"].shape), 0, -2
                                   ).reshape(L, 4, N_DEV * ch)

    def layer_weights(l):
        g = dict(zip(BIG, gathered[l]))
        F = g["ffn1_w_out"].shape[1] * N_DEV
        wmix = jnp.transpose(g["mix_w_in"], (1, 0, 2)).reshape(D, -1)
        return {"ffn1_w_in": g["ffn1_w_in"], "ffn1_w_out": g["ffn1_w_out"].reshape(F, D),
                "wp": _pad_mix_w_in(wmix), "wo": g["mix_w_out"].reshape(D, D),
                "ffn2_w_in": g["ffn2_w_in"], "ffn2_w_out": g["ffn2_w_out"].reshape(F, D)}

    def small_params(l):
        return {"fgate_b": jnp.pad(w["fgate_b"][l], (0, LANES - nh)).reshape(1, LANES),
                "conv_w": conv_w_full[l], "lru_conv_w": lru_conv_w_full[l],
                "lru_conv_b": w["lru_conv_b"][l].reshape(1, dl), "lru_w_a": w["lru_w_a"][l],
                "lru_b_a": w["lru_b_a"][l].reshape(1, dl), "lru_w_x": w["lru_w_x"][l],
                "lru_b_x": w["lru_b_x"][l].reshape(1, dl), "lru_lambda": w["lru_lambda"][l].reshape(1, dl),
                "mix_out_norm": w["mix_out_norm"][l].reshape(1, D)}

    gain = lambda n, l, token=None: w[n][l].reshape(1, D) + (0.0 if token is None else token[0:1, 0:1])

    xc = x[0]
    saved, lw, sps = [], [], []
    for l in range(L):
        lw.append(layer_weights(l))
        sps.append(small_params(l))
        xc, s1 = _ffn_forward(xc, gain("norm_ffn1", l), lw[l]["ffn1_w_in"], lw[l]["ffn1_w_out"])
        xc, s2 = _mixer_forward(xc, gain("norm_mix", l), lw[l]["wp"], lw[l]["wo"], sps[l])
        order_token = None
        if l + 1 < L:
            own, landed = _gather_phase1_wait("gather_p1_wait_%d" % (l + 1), phase1[l + 1], xc)
            phase2 = _gather_phase2_start("gather_p2_start_%d" % (l + 1), landed, landed[0])
            order_token = phase2[-1]
            if l + 2 < L:
                phase1[l + 2] = _gather_phase1_start("gather_p1_start_%d" % (l + 2), shards_of(l + 2), phase2[-1])
                order_token = phase1[l + 2][-1]
        xc, s3 = _ffn_forward(xc, gain("norm_ffn2", l, order_token), lw[l]["ffn2_w_in"], lw[l]["ffn2_w_out"])
        if l + 1 < L:
            landed = _gather_phase2_wait("gather_p2_wait_%d" % (l + 1), phase2, xc)
            gathered[l + 1] = place_own(str(l + 1), landed, own)
        saved.append((s1, s2, s3))
    loss_part, dx, d_final = _loss_head(xc, w["final_norm"].reshape(1, D), target[0])

    small_grads = [None] * L
    parts = [None] * L
    sibling = [None] * L
    scatter = [None] * L
    for l in reversed(range(L)):
        s1, s2, s3 = saved[l]
        dh, dw3_2, dwout_2 = _ffn_backward(dx, s3, lw[l]["ffn2_w_in"], lw[l]["ffn2_w_out"])
        dx, dg3 = _rmsnorm_bwd(dh, s3[0], gain("norm_ffn2", l), dx)
        sp_l = sps[l]
        if l + 1 < L:
            scatter[l + 1] = _scatter_chips_after_sibling(str(l + 1), sibling[l + 1], c_idx, dx)
            sp_l = dict(sp_l, mix_out_norm=sp_l["mix_out_norm"] + scatter[l + 1][-1][0:1, 0:1])
        dx, dwp, dwo, sg = _mixer_backward(dx, s2, gain("norm_mix", l), lw[l]["wp"], lw[l]["wo"], sp_l)
        dh, dw3_1, dwout_1 = _ffn_backward(dx, s1, lw[l]["ffn1_w_in"], lw[l]["ffn1_w_out"])
        if l + 1 < L:
            parts[l + 1] = _scatter_chips_wait("rs_chips_wait_%d" % (l + 1), scatter[l + 1], dh)
        dmix = jnp.transpose(_unpad_mix_w_in(dwp).reshape(D, N_DEV, -1), (1, 0, 2))
        grads = [dw3_1, dwout_1, dmix, dwo.reshape(N_DEV, D // N_DEV, D), dw3_2, dwout_2]
        sibling[l] = _sibling_start("rs_sibling_start_%d" % l, grads, dh)
        dx, dg1 = _rmsnorm_bwd(dh, s1[0], gain("norm_ffn1", l, sibling[l][-1]), dx)
        sg["norm_ffn2"], sg["norm_ffn1"] = dg3, dg1
        small_grads[l] = sg

    out_g, out_d, out_m, out_v = {}, {}, {}, {}
    prev = {n: None for n in BIG}

    def adam_big(l, after):
        own, landed = parts[l]
        for t, n in enumerate(BIG):
            prev[n] = _adam_layer("adam_%s_%d" % (n, l), l, w[n], m[n], v[n], own[t], landed[t], chip_idx, prev[n], after)

    small_names = REPLICATED + SHARDED_SMALL
    partial = []
    for n in small_names:
        if n == "final_norm":
            partial.append(d_final)
        else:
            partial.append(jnp.stack([small_grads[l][n].reshape(w[n].shape[1:]) if n not in SHARDED_SMALL
                                      else small_grads[l][n] for l in range(L)]))
    partial.append(loss_part[0, :1])
    packed = _pack(partial)
    small_gathered = _all_gather("gather_small_grads", [packed])[0]
    scatter[0] = _scatter_chips_after_sibling("0", sibling[0], c_idx, small_gathered)
    for l in reversed(range(1, L)):
        adam_big(l, scatter[0][-1])
    summed = _sum_devices(small_gathered)
    full_shapes = [w[n].shape for n in REPLICATED] + [(L, 3, N_DEV * ch), (L, 4, N_DEV * ch), (1,)]
    full = _unpack(summed, full_shapes)
    loss = full[-1][0]
    g_small = dict(zip(small_names, full[:-1]))
    for n in SHARDED_SMALL:
        g_small[n] = lax.dynamic_slice_in_dim(g_small[n], me * ch, ch, axis=2)
    shapes = [w[n].shape for n in small_names]
    d_s, m_s, v_s = _adam_small(_pack([w[n] for n in small_names]), _pack([g_small[n] for n in small_names]),
                                _pack([m[n] for n in small_names]), _pack([v[n] for n in small_names]))
    for n, d_, m_, v_ in zip(small_names, _unpack(d_s, shapes), _unpack(m_s, shapes), _unpack(v_s, shapes)):
        out_g[n], out_d[n], out_m[n], out_v[n] = g_small[n], d_, m_, v_

    parts[0] = _scatter_chips_wait("rs_chips_wait_0", scatter[0], d_s if L == 1 else prev[BIG[-1]][0])
    adam_big(0, d_s)
    for n in BIG:
        out_g[n], out_d[n], out_m[n], out_v[n] = prev[n]

    return (loss, dx[None], *[out_g[n] for n in WEIGHTS], *[out_d[n] for n in WEIGHTS],
            *[out_m[n] for n in WEIGHTS], *[out_v[n] for n in WEIGHTS])


def kernel(x, norm_ffn1, ffn1_w_in, ffn1_w_out, norm_mix, mix_w_in, conv_w, fgate_b, lru_conv_w, lru_conv_b, lru_w_a, lru_b_a, lru_w_x, lru_b_x, lru_lambda, mix_out_norm, mix_w_out, norm_ffn2, ffn2_w_in, ffn2_w_out, final_norm, loss_target, m_norm_ffn1, m_ffn1_w_in, m_ffn1_w_out, m_norm_mix, m_mix_w_in, m_conv_w, m_fgate_b, m_lru_conv_w, m_lru_conv_b, m_lru_w_a, m_lru_b_a, m_lru_w_x, m_lru_b_x, m_lru_lambda, m_mix_out_norm, m_mix_w_out, m_norm_ffn2, m_ffn2_w_in, m_ffn2_w_out, m_final_norm, v_norm_ffn1, v_ffn1_w_in, v_ffn1_w_out, v_norm_mix, v_mix_w_in, v_conv_w, v_fgate_b, v_lru_conv_w, v_lru_conv_b, v_lru_w_a, v_lru_b_a, v_lru_w_x, v_lru_b_x, v_lru_lambda, v_mix_out_norm, v_mix_w_out, v_norm_ffn2, v_ffn2_w_in, v_ffn2_w_out, v_final_norm):
    w = dict(norm_ffn1=norm_ffn1, ffn1_w_in=ffn1_w_in, ffn1_w_out=ffn1_w_out, norm_mix=norm_mix, mix_w_in=mix_w_in,
             conv_w=conv_w, fgate_b=fgate_b, lru_conv_w=lru_conv_w, lru_conv_b=lru_conv_b, lru_w_a=lru_w_a,
             lru_b_a=lru_b_a, lru_w_x=lru_w_x, lru_b_x=lru_b_x, lru_lambda=lru_lambda, mix_out_norm=mix_out_norm,
             mix_w_out=mix_w_out, norm_ffn2=norm_ffn2, ffn2_w_in=ffn2_w_in, ffn2_w_out=ffn2_w_out,
             final_norm=final_norm)
    m = dict(norm_ffn1=m_norm_ffn1, ffn1_w_in=m_ffn1_w_in, ffn1_w_out=m_ffn1_w_out, norm_mix=m_norm_mix,
             mix_w_in=m_mix_w_in, conv_w=m_conv_w, fgate_b=m_fgate_b, lru_conv_w=m_lru_conv_w,
             lru_conv_b=m_lru_conv_b, lru_w_a=m_lru_w_a, lru_b_a=m_lru_b_a, lru_w_x=m_lru_w_x, lru_b_x=m_lru_b_x,
             lru_lambda=m_lru_lambda, mix_out_norm=m_mix_out_norm, mix_w_out=m_mix_w_out, norm_ffn2=m_norm_ffn2,
             ffn2_w_in=m_ffn2_w_in, ffn2_w_out=m_ffn2_w_out, final_norm=m_final_norm)
    v = dict(norm_ffn1=v_norm_ffn1, ffn1_w_in=v_ffn1_w_in, ffn1_w_out=v_ffn1_w_out, norm_mix=v_norm_mix,
             mix_w_in=v_mix_w_in, conv_w=v_conv_w, fgate_b=v_fgate_b, lru_conv_w=v_lru_conv_w,
             lru_conv_b=v_lru_conv_b, lru_w_a=v_lru_w_a, lru_b_a=v_lru_b_a, lru_w_x=v_lru_w_x, lru_b_x=v_lru_b_x,
             lru_lambda=v_lru_lambda, mix_out_norm=v_mix_out_norm, mix_w_out=v_mix_w_out, norm_ffn2=v_norm_ffn2,
             ffn2_w_in=v_ffn2_w_in, ffn2_w_out=v_ffn2_w_out, final_norm=v_final_norm)
    return _step(x, loss_target, w, m, v)
```

```python
import math

import jax
import jax.numpy as jnp
from jax import lax
from jax.experimental import pallas as pl
from jax.experimental.pallas import tpu as pltpu

F32 = jnp.float32
BF = jnp.bfloat16
EPS = 1e-6
LANES = 128
VMEM_LIMIT_V7X = 56 * 1024 * 1024
MESH = pl.DeviceIdType.MESH
N_DEV = 8
LRU_C = 8.0
ADAM_LR, ADAM_B1, ADAM_B2, ADAM_EPS, ADAM_WD, ADAM_STEP = 0.001, 0.9, 0.999, 1e-08, 0.01, 10
GELU_C = math.sqrt(2.0 / math.pi)
GELU_K = 0.044715


def _params(*sem):
    return pltpu.CompilerParams(dimension_semantics=sem, vmem_limit_bytes=VMEM_LIMIT_V7X)


def _any():
    return pl.BlockSpec(memory_space=pl.ANY)


def _tile(n, target):
    if n <= target:
        return n
    t = target - target % 16
    while t >= 16:
        if n % t == 0:
            return t
        t -= 16
    return n


def _mm(name, grid, a, b, o, dims, scale=1.0, resid=None, after=None):
    nk = grid[-1]
    acc_shape = tuple(d for d in o[1] if d is not None)
    has_resid = resid is not None
    n_in = 2 + has_resid + (after is not None)

    def body(*refs):
        a_ref, b_ref = refs[0], refs[1]
        r_ref = refs[2] if has_resid else None
        o_ref = refs[n_in]

        def finish(acc):
            r = acc * scale if scale != 1.0 else acc
            if has_resid:
                r = r + r_ref[...]
            o_ref[...] = r.astype(o_ref.dtype)

        part = lax.dot_general(a_ref[...].astype(BF), b_ref[...].astype(BF), (dims, ((), ())),
                               preferred_element_type=F32)
        if nk == 1:
            finish(part)
        else:
            acc_ref = refs[-1]
            k = pl.program_id(len(grid) - 1)

            @pl.when(k == 0)
            def _():
                acc_ref[...] = part

            @pl.when(k > 0)
            def _():
                acc_ref[...] += part

            @pl.when(k == nk - 1)
            def _():
                finish(acc_ref[...])

    ins = [a, b] + ([resid] if has_resid else [])
    return pl.pallas_call(
        body, name=name, grid=grid,
        in_specs=[pl.BlockSpec(blk, idx) for (_, blk, idx) in ins] + ([_any()] if after is not None else []),
        out_specs=pl.BlockSpec(o[1], o[2]),
        out_shape=o[0],
        scratch_shapes=[pltpu.VMEM(acc_shape, F32)] if nk > 1 else [],
        compiler_params=_params(*(["parallel"] * (len(grid) - 1) + ["arbitrary"])),
    )(*[x[0] for x in ins], *([after] if after is not None else []))


NN = ((1,), (0,))
NT = ((1,), (1,))
TN = ((0,), (0,))


def _rmsnorm(x, gain):
    T, D = x.shape
    tr = _tile(T, 256)

    def body(x_ref, g_ref, o_ref):
        xv = x_ref[...]
        r = lax.rsqrt(jnp.mean(xv * xv, axis=-1, keepdims=True) + EPS)
        o_ref[...] = (xv * r * g_ref[...]).astype(BF)

    return pl.pallas_call(
        body, name="rmsnorm_fwd", grid=(T // tr,),
        in_specs=[pl.BlockSpec((tr, D), lambda i: (i, 0)), pl.BlockSpec((1, D), lambda i: (0, 0))],
        out_specs=pl.BlockSpec((tr, D), lambda i: (i, 0)),
        out_shape=jax.ShapeDtypeStruct((T, D), BF),
        compiler_params=_params("parallel"),
    )(x, gain)


def _rmsnorm_bwd(dh, x, gain, dres):
    T, D = x.shape
    tr = _tile(T, 256)

    def body(dh_ref, x_ref, g_ref, dres_ref, dx_ref, dg_ref):
        i = pl.program_id(0)
        xv = x_ref[...]
        r = lax.rsqrt(jnp.mean(xv * xv, axis=-1, keepdims=True) + EPS)
        xh = xv * r
        dy = dh_ref[...].astype(F32)
        dgp = jnp.sum(dy * xh, axis=0, keepdims=True)

        @pl.when(i == 0)
        def _():
            dg_ref[...] = dgp

        @pl.when(i > 0)
        def _():
            dg_ref[...] += dgp

        dxh = dy * g_ref[...]
        dx_ref[...] = dres_ref[...] + r * (dxh - xh * jnp.mean(dxh * xh, axis=-1, keepdims=True))

    return pl.pallas_call(
        body, name="rmsnorm_bwd", grid=(T // tr,),
        in_specs=[pl.BlockSpec((tr, D), lambda i: (i, 0)), pl.BlockSpec((tr, D), lambda i: (i, 0)),
                  pl.BlockSpec((1, D), lambda i: (0, 0)), pl.BlockSpec((tr, D), lambda i: (i, 0))],
        out_specs=[pl.BlockSpec((tr, D), lambda i: (i, 0)), pl.BlockSpec((1, D), lambda i: (0, 0))],
        out_shape=[jax.ShapeDtypeStruct((T, D), F32), jax.ShapeDtypeStruct((1, D), F32)],
        compiler_params=_params("arbitrary"),
    )(dh, x, gain, dres)


def _loss_head(x, gain, target):
    T, D = x.shape
    tr = _tile(T, 256)

    def body(x_ref, g_ref, t_ref, loss_ref, dx_ref, dg_ref):
        i = pl.program_id(0)
        xv = x_ref[...]
        g = g_ref[...]
        r = lax.rsqrt(jnp.mean(xv * xv, axis=-1, keepdims=True) + EPS)
        xh = xv * r
        err = xh * g - t_ref[...]
        lp = 0.5 * jnp.sum(jnp.mean(err * err, axis=-1, keepdims=True), axis=0, keepdims=True)
        dy = err * (1.0 / D)
        dgp = jnp.sum(dy * xh, axis=0, keepdims=True)

        @pl.when(i == 0)
        def _():
            loss_ref[...] = jnp.broadcast_to(lp, loss_ref.shape)
            dg_ref[...] = dgp

        @pl.when(i > 0)
        def _():
            loss_ref[...] += jnp.broadcast_to(lp, loss_ref.shape)
            dg_ref[...] += dgp

        dxh = dy * g
        dx_ref[...] = r * (dxh - xh * jnp.mean(dxh * xh, axis=-1, keepdims=True))

    return pl.pallas_call(
        body, name="loss_head", grid=(T // tr,),
        in_specs=[pl.BlockSpec((tr, D), lambda i: (i, 0)), pl.BlockSpec((1, D), lambda i: (0, 0)),
                  pl.BlockSpec((tr, D), lambda i: (i, 0))],
        out_specs=[pl.BlockSpec((1, LANES), lambda i: (0, 0)), pl.BlockSpec((tr, D), lambda i: (i, 0)),
                   pl.BlockSpec((1, D), lambda i: (0, 0))],
        out_shape=[jax.ShapeDtypeStruct((1, LANES), F32), jax.ShapeDtypeStruct((T, D), F32),
                   jax.ShapeDtypeStruct((1, D), F32)],
        compiler_params=_params("arbitrary"),
    )(x, gain, target)


def _group_slices(D):
    dc, da = D // 4, D // 2
    return [(0, dc), (dc, dc + da), (dc + da, D)]


def _groupnorm(yc, ya, yl, gain):
    T = yc.shape[0]
    D = yc.shape[1] + ya.shape[1] + yl.shape[1]
    tr = _tile(T, 256)
    sl = _group_slices(D)

    def body(yc_ref, ya_ref, yl_ref, g_ref, o_ref):
        for y_ref, (lo, hi) in zip((yc_ref, ya_ref, yl_ref), sl):
            y = y_ref[...]
            r = lax.rsqrt(jnp.mean(y * y, axis=-1, keepdims=True) + EPS)
            o_ref[:, lo:hi] = (y * r * g_ref[:, lo:hi]).astype(BF)

    return pl.pallas_call(
        body, name="groupnorm_fwd", grid=(T // tr,),
        in_specs=[pl.BlockSpec((tr, y.shape[1]), lambda i: (i, 0)) for y in (yc, ya, yl)]
        + [pl.BlockSpec((1, D), lambda i: (0, 0))],
        out_specs=pl.BlockSpec((tr, D), lambda i: (i, 0)),
        out_shape=jax.ShapeDtypeStruct((T, D), BF),
        compiler_params=_params("parallel"),
    )(yc, ya, yl, gain)


def _groupnorm_bwd(dyn, yc, ya, yl, gain):
    T, D = dyn.shape
    tr = _tile(T, 256)
    sl = _group_slices(D)

    def body(dyn_ref, yc_ref, ya_ref, yl_ref, g_ref, dc_ref, da_ref, dl_ref, dg_ref):
        i = pl.program_id(0)
        for y_ref, d_ref, (lo, hi) in zip((yc_ref, ya_ref, yl_ref), (dc_ref, da_ref, dl_ref), sl):
            y = y_ref[...]
            r = lax.rsqrt(jnp.mean(y * y, axis=-1, keepdims=True) + EPS)
            yh = y * r
            dy = dyn_ref[:, lo:hi]
            dgp = jnp.sum(dy * yh, axis=0, keepdims=True)

            @pl.when(i == 0)
            def _():
                dg_ref[:, lo:hi] = dgp

            @pl.when(i > 0)
            def _():
                dg_ref[:, lo:hi] += dgp

            dyh = dy * g_ref[:, lo:hi]
            d_ref[...] = r * (dyh - yh * jnp.mean(dyh * yh, axis=-1, keepdims=True))

    return pl.pallas_call(
        body, name="groupnorm_bwd", grid=(T // tr,),
        in_specs=[pl.BlockSpec((tr, D), lambda i: (i, 0))]
        + [pl.BlockSpec((tr, y.shape[1]), lambda i: (i, 0)) for y in (yc, ya, yl)]
        + [pl.BlockSpec((1, D), lambda i: (0, 0))],
        out_specs=[pl.BlockSpec((tr, y.shape[1]), lambda i: (i, 0)) for y in (yc, ya, yl)]
        + [pl.BlockSpec((1, D), lambda i: (0, 0))],
        out_shape=[jax.ShapeDtypeStruct(y.shape, F32) for y in (yc, ya, yl)] + [jax.ShapeDtypeStruct((1, D), F32)],
        compiler_params=_params("arbitrary"),
    )(dyn, yc, ya, yl, gain)


def _ffn_in(h, w3):
    T, D = h.shape
    tn = w3.shape[2]
    F = 4 * tn
    tm = _tile(T, 512)

    def body(h_ref, wg_ref, wu_ref, g_ref, u_ref, a_ref):
        hv = h_ref[...]
        g = jnp.dot(hv, wg_ref[...], preferred_element_type=F32)
        u = jnp.dot(hv, wu_ref[...], preferred_element_type=F32)
        g_ref[...] = g.astype(BF)
        u_ref[...] = u.astype(BF)
        a_ref[...] = (g * jax.nn.sigmoid(g) * u).astype(BF)

    out = jax.ShapeDtypeStruct((T, F), BF)
    return pl.pallas_call(
        body, name="ffn_in_swiglu", grid=(4, T // tm),
        in_specs=[pl.BlockSpec((tm, D), lambda j, i: (i, 0)),
                  pl.BlockSpec((None, D, tn), lambda j, i: (j, 0, 0)),
                  pl.BlockSpec((None, D, tn), lambda j, i: (j + 4, 0, 0))],
        out_specs=[pl.BlockSpec((tm, tn), lambda j, i: (i, j))] * 3,
        out_shape=[out, out, out],
        compiler_params=_params("parallel", "parallel"),
    )(h, w3, w3)


def _ffn_bwd_in(dx, wout, g, u):
    T, D = dx.shape
    F = wout.shape[0]
    tn = F // 4
    tm = _tile(T, 512)

    def body(dx_ref, w_ref, g_ref, u_ref, o_ref):
        da = 0.5 * lax.dot_general(dx_ref[...].astype(BF), w_ref[...], (NT, ((), ())), preferred_element_type=F32)
        gv = g_ref[...].astype(F32)
        s = jax.nn.sigmoid(gv)
        o_ref[0] = (da * u_ref[...].astype(F32) * (s * (1.0 + gv * (1.0 - s)))).astype(BF)
        o_ref[1] = (da * gv * s).astype(BF)

    return pl.pallas_call(
        body, name="ffn_bwd_swiglu", grid=(4, T // tm),
        in_specs=[pl.BlockSpec((tm, D), lambda j, i: (i, 0)), pl.BlockSpec((tn, D), lambda j, i: (j, 0)),
                  pl.BlockSpec((tm, tn), lambda j, i: (i, j)), pl.BlockSpec((tm, tn), lambda j, i: (i, j))],
        out_specs=pl.BlockSpec((2, tm, tn), lambda j, i: (0, i, j)),
        out_shape=jax.ShapeDtypeStruct((2, T, F), BF),
        compiler_params=_params("parallel", "parallel"),
    )(dx, wout, g, u)


def _ffn_forward(x, gain, w3, wout):
    T, D = x.shape
    F = wout.shape[0]
    h = _rmsnorm(x, gain)
    g, u, a = _ffn_in(h, w3)
    tm, tk = _tile(T, 512), F // 4
    xn = _mm("ffn_out", (T // tm, 1, F // tk),
             (a, (tm, tk), lambda i, j, k: (i, k)), (wout, (tk, D), lambda i, j, k: (k, 0)),
             (jax.ShapeDtypeStruct((T, D), F32), (tm, D), lambda i, j, k: (i, 0)), NN, scale=0.5,
             resid=(x, (tm, D), lambda i, j, k: (i, 0)))
    return xn, (x, h, g, u, a)


def _ffn_backward(dx, saved, w3, wout, between=None):
    x, h, g, u, a = saved
    T, D = x.shape
    F = wout.shape[0]
    tn3 = F // 4
    dgu = _ffn_bwd_in(dx, wout, g, u)
    behind = None if between is None else between(dgu)
    tnd = _tile(D, 512)
    dwout = _mm("ffn_dwout", (F // tn3, D // tnd, 1),
                (a, (T, tn3), lambda i, j, k: (0, i)), (dx, (T, tnd), lambda i, j, k: (0, j)),
                (jax.ShapeDtypeStruct((F, D), BF), (tn3, tnd), lambda i, j, k: (i, j)), TN, scale=0.5, after=behind)
    dw3 = _mm("ffn_dwin", (8, D // tnd, 1),
              (h, (T, tnd), lambda s, i, k: (0, i)), (dgu, (None, T, tn3), lambda s, i, k: (s // 4, 0, s % 4)),
              (jax.ShapeDtypeStruct((8, D, tn3), BF), (None, tnd, tn3), lambda s, i, k: (s, i, 0)), TN, after=behind)
    tm = _tile(T, 1024)
    dh = _mm("ffn_dh", (T // tm, 1, 8),
             (dgu, (None, tm, tn3), lambda i, j, k: (k // 4, i, k % 4)), (w3, (None, D, tn3), lambda i, j, k: (k, 0, 0)),
             (jax.ShapeDtypeStruct((T, D), F32), (tm, D), lambda i, j, k: (i, 0)), NT, after=behind)
    return dh, dw3, dwout.reshape(N_DEV, F // N_DEV, D)


def _rows(shape):
    return lax.broadcasted_iota(jnp.int32, shape, 0)


def _down(x, s, fill, rows):
    return jnp.where(rows >= s, pltpu.roll(x, s, 0), fill)


def _up(x, s, fill, rows):
    T = x.shape[0]
    return jnp.where(rows < T - s, pltpu.roll(x, T - s, 0), fill)


def _scan_linear(a, b, rows, shift):
    T = a.shape[0]
    s = 1
    while s < T:
        b = a * shift(b, s, 0.0, rows) + b
        if 2 * s < T:
            a = a * shift(a, s, 1.0, rows)
        s *= 2
    return b


def _cumsum(c, rows, shift):
    T = c.shape[0]
    s = 1
    while s < T:
        c = c + shift(c, s, 0.0, rows)
        s *= 2
    return c


def _log1p_small(e):
    return jnp.where(e < 0.01, e * (1.0 - e * (0.5 - e * (1.0 / 3.0))), jnp.log(1.0 + e))


def _softplus(x):
    return jnp.maximum(x, 0.0) + _log1p_small(jnp.exp(-jnp.abs(x)))


def _one_minus_exp_neg(z):
    return jnp.where(z < 0.1, z * (1.0 - z * (0.5 - z * (1.0 / 6.0 - z * (1.0 / 24.0)))), 1.0 - jnp.exp(-z))


def _fgate_cum(f, b):
    T = f.shape[0]

    def body(f_ref, b_ref, o_ref):
        z = f_ref[...] + b_ref[...]
        o_ref[...] = _cumsum(-_softplus(-z), _rows(z.shape), _down)

    return pl.pallas_call(
        body, name="fgate_cumsum",
        out_shape=jax.ShapeDtypeStruct((T, LANES), F32),
        compiler_params=pltpu.CompilerParams(vmem_limit_bytes=VMEM_LIMIT_V7X),
    )(f, b)


def _fgate_cum_bwd(drow, dcol, f, b):
    T = f.shape[0]

    def body(dr_ref, dc_ref, f_ref, b_ref, df_ref, db_ref):
        z = f_ref[...] + b_ref[...]
        dlogf = _cumsum(dr_ref[...] - dc_ref[...], _rows(z.shape), _up)
        dz = dlogf * jax.nn.sigmoid(-z)
        df_ref[...] = dz.astype(BF)
        db_ref[...] = jnp.sum(dz, axis=0, keepdims=True)

    return pl.pallas_call(
        body, name="fgate_cumsum_bwd",
        out_shape=[jax.ShapeDtypeStruct((T, LANES), BF), jax.ShapeDtypeStruct((1, LANES), F32)],
        compiler_params=pltpu.CompilerParams(vmem_limit_bytes=VMEM_LIMIT_V7X),
    )(drow, dcol, f, b)


def _col(blk0):
    return lambda g: (0, blk0 + g)


def _conv_fwd(p, w, nb):
    T = p.shape[0]

    def body(b_ref, c_ref, v_ref, w_ref, o_ref):
        z = c_ref[...].astype(F32) * v_ref[...].astype(F32)
        rows = _rows(z.shape)
        conv = w_ref[2:3, :] * z + w_ref[1:2, :] * _down(z, 1, 0.0, rows) + w_ref[0:1, :] * _down(z, 2, 0.0, rows)
        o_ref[...] = b_ref[...].astype(F32) * conv

    return pl.pallas_call(
        body, name="conv_fwd", grid=(nb,),
        in_specs=[pl.BlockSpec((T, LANES), _col(0)), pl.BlockSpec((T, LANES), _col(nb)),
                  pl.BlockSpec((T, LANES), _col(2 * nb)), pl.BlockSpec((3, LANES), lambda g: (0, g))],
        out_specs=pl.BlockSpec((T, LANES), lambda g: (0, g)),
        out_shape=jax.ShapeDtypeStruct((T, nb * LANES), F32),
        compiler_params=_params("parallel"),
    )(p, p, p, w)


def _conv_bwd(dy, p, w, nb):
    T = p.shape[0]

    def body(dy_ref, b_ref, c_ref, v_ref, w_ref, db_ref, dc_ref, dv_ref, dw_ref):
        cv, vv = c_ref[...].astype(F32), v_ref[...].astype(F32)
        z = cv * vv
        rows = _rows(z.shape)
        z1, z2 = _down(z, 1, 0.0, rows), _down(z, 2, 0.0, rows)
        dyv = dy_ref[...]
        db_ref[...] = (dyv * (w_ref[2:3, :] * z + w_ref[1:2, :] * z1 + w_ref[0:1, :] * z2)).astype(BF)
        dconv = dyv * b_ref[...].astype(F32)
        dz = (w_ref[2:3, :] * dconv + w_ref[1:2, :] * _up(dconv, 1, 0.0, rows)
              + w_ref[0:1, :] * _up(dconv, 2, 0.0, rows))
        dc_ref[...] = (dz * vv).astype(BF)
        dv_ref[...] = (dz * cv).astype(BF)
        dw_ref[0:1, :] = jnp.sum(dconv * z2, axis=0, keepdims=True)
        dw_ref[1:2, :] = jnp.sum(dconv * z1, axis=0, keepdims=True)
        dw_ref[2:3, :] = jnp.sum(dconv * z, axis=0, keepdims=True)

    return pl.pallas_call(
        body, name="conv_bwd", grid=(nb,),
        in_specs=[pl.BlockSpec((T, LANES), lambda g: (0, g)), pl.BlockSpec((T, LANES), _col(0)),
                  pl.BlockSpec((T, LANES), _col(nb)), pl.BlockSpec((T, LANES), _col(2 * nb)),
                  pl.BlockSpec((3, LANES), lambda g: (0, g))],
        out_specs=[pl.BlockSpec((T, LANES), lambda g: (0, g))] * 3 + [pl.BlockSpec((3, LANES), lambda g: (0, g))],
        out_shape=[jax.ShapeDtypeStruct((T, nb * LANES), BF)] * 3 + [jax.ShapeDtypeStruct((3, nb * LANES), F32)],
        compiler_params=_params("parallel"),
    )(dy, p, p, p, w)


def _gelu(x):
    t = jnp.tanh(GELU_C * (x + GELU_K * x * x * x))
    return 0.5 * x * (1.0 + t), t


def _lru_common(x, cw_ref, cb_ref, wa_ref, ba_ref, wx_ref, bx_ref, lam_ref, rows):
    xr = (cb_ref[...] + cw_ref[3:4, :] * x + cw_ref[2:3, :] * _down(x, 1, 0.0, rows)
          + cw_ref[1:2, :] * _down(x, 2, 0.0, rows) + cw_ref[0:1, :] * _down(x, 3, 0.0, rows))
    xrb = xr.astype(BF)
    r = jax.nn.sigmoid(jnp.dot(xrb, wa_ref[...].astype(BF), preferred_element_type=F32) + ba_ref[...])
    i = jax.nn.sigmoid(jnp.dot(xrb, wx_ref[...].astype(BF), preferred_element_type=F32) + bx_ref[...])
    sp = _softplus(-lam_ref[...])
    log_a = -LRU_C * r * sp
    a = jnp.exp(log_a)
    m = jnp.sqrt(_one_minus_exp_neg(-2.0 * log_a))
    return xr, xrb, r, i, sp, a, m


def _lru_specs(T, nb, gate_blk0, x_blk0):
    vec = pl.BlockSpec((1, LANES), lambda g: (0, g))
    mat = pl.BlockSpec((None, LANES, LANES), lambda g: (g, 0, 0))
    return [pl.BlockSpec((T, LANES), _col(gate_blk0)), pl.BlockSpec((T, LANES), _col(x_blk0)),
            pl.BlockSpec((4, LANES), lambda g: (0, g)), vec, mat, vec, mat, vec, vec]


def _lru_fwd(p, cw, cb, wa, ba, wx, bx, lam, nb, gate_blk0):
    T = p.shape[0]

    def body(gate_ref, x_ref, cw_ref, cb_ref, wa_ref, ba_ref, wx_ref, bx_ref, lam_ref, y_ref, h_ref):
        x = x_ref[...].astype(F32)
        rows = _rows(x.shape)
        xr, _, _, i, _, a, m = _lru_common(x, cw_ref, cb_ref, wa_ref, ba_ref, wx_ref, bx_ref, lam_ref, rows)
        h = _scan_linear(a, m * (i * xr), rows, _down)
        h_ref[...] = h
        y_ref[...] = _gelu(gate_ref[...].astype(F32))[0] * h

    out = jax.ShapeDtypeStruct((T, nb * LANES), F32)
    return pl.pallas_call(
        body, name="lru_fwd", grid=(nb,),
        in_specs=_lru_specs(T, nb, gate_blk0, gate_blk0 + nb),
        out_specs=[pl.BlockSpec((T, LANES), lambda g: (0, g))] * 2,
        out_shape=[out, out],
        compiler_params=_params("parallel"),
    )(p, p, cw, cb, wa, ba, wx, bx, lam)


def _lru_bwd(dy, hs, p, cw, cb, wa, ba, wx, bx, lam, nb, gate_blk0):
    T = p.shape[0]

    def body(dy_ref, hs_ref, gate_ref, x_ref, cw_ref, cb_ref, wa_ref, ba_ref, wx_ref, bx_ref, lam_ref,
             dgate_ref, dx_ref, dcw_ref, dcb_ref, dwa_ref, dba_ref, dwx_ref, dbx_ref, dlam_ref):
        x = x_ref[...].astype(F32)
        rows = _rows(x.shape)
        xr, xrb, r, i, sp, a, m = _lru_common(x, cw_ref, cb_ref, wa_ref, ba_ref, wx_ref, bx_ref, lam_ref, rows)
        gate = gate_ref[...].astype(F32)
        gl, t = _gelu(gate)
        h = hs_ref[...]
        dyv = dy_ref[...]
        dgelu = 0.5 * (1.0 + t) + 0.5 * gate * (1.0 - t * t) * GELU_C * (1.0 + 3.0 * GELU_K * gate * gate)
        dgate_ref[...] = (dyv * h * dgelu).astype(BF)
        lam_adj = _scan_linear(_up(a, 1, 0.0, rows), dyv * gl, rows, _up)
        da = lam_adj * _down(h, 1, 0.0, rows)
        ix = i * xr
        dix = lam_adj * m
        dm = lam_adj * ix
        dlog_a = da * a - dm * (a * a) / jnp.maximum(m, 1e-30)
        dr = dlog_a * (-LRU_C * sp)
        dsp = jnp.sum(dlog_a * (-LRU_C * r), axis=0, keepdims=True)
        dlam_ref[...] = -dsp * jax.nn.sigmoid(-lam_ref[...])
        dpa = dr * r * (1.0 - r)
        dpx = dix * xr * i * (1.0 - i)
        dpab, dpxb = dpa.astype(BF), dpx.astype(BF)
        dxr = (dix * i
               + lax.dot_general(dpab, wa_ref[...].astype(BF), (NT, ((), ())), preferred_element_type=F32)
               + lax.dot_general(dpxb, wx_ref[...].astype(BF), (NT, ((), ())), preferred_element_type=F32))
        dwa_ref[...] = lax.dot_general(xrb, dpab, (TN, ((), ())), preferred_element_type=F32)
        dwx_ref[...] = lax.dot_general(xrb, dpxb, (TN, ((), ())), preferred_element_type=F32)
        dba_ref[...] = jnp.sum(dpa, axis=0, keepdims=True)
        dbx_ref[...] = jnp.sum(dpx, axis=0, keepdims=True)
        dcb_ref[...] = jnp.sum(dxr, axis=0, keepdims=True)
        dx_ref[...] = (cw_ref[3:4, :] * dxr + cw_ref[2:3, :] * _up(dxr, 1, 0.0, rows)
                       + cw_ref[1:2, :] * _up(dxr, 2, 0.0, rows) + cw_ref[0:1, :] * _up(dxr, 3, 0.0, rows)).astype(BF)
        for k in range(4):
            xs = x if k == 3 else _down(x, 3 - k, 0.0, rows)
            dcw_ref[k:k + 1, :] = jnp.sum(dxr * xs, axis=0, keepdims=True)

    C = nb * LANES
    seq = jax.ShapeDtypeStruct((T, C), BF)
    vec = jax.ShapeDtypeStruct((1, C), F32)
    mat = jax.ShapeDtypeStruct((nb, LANES, LANES), F32)
    vspec = pl.BlockSpec((1, LANES), lambda g: (0, g))
    mspec = pl.BlockSpec((None, LANES, LANES), lambda g: (g, 0, 0))
    sspec = pl.BlockSpec((T, LANES), lambda g: (0, g))
    return pl.pallas_call(
        body, name="lru_bwd", grid=(nb,),
        in_specs=[sspec, sspec] + _lru_specs(T, nb, gate_blk0, gate_blk0 + nb),
        out_specs=[sspec, sspec, pl.BlockSpec((4, LANES), lambda g: (0, g)), vspec, mspec, vspec, mspec, vspec, vspec],
        out_shape=[seq, seq, jax.ShapeDtypeStruct((4, C), F32), vec, mat, vec, mat, vec, vec],
        compiler_params=_params("parallel"),
    )(dy, hs, p, p, cw, cb, wa, ba, wx, bx, lam)


def _causal(shape, transposed=False):
    r = lax.broadcasted_iota(jnp.int32, shape, 0)
    c = lax.broadcasted_iota(jnp.int32, shape, 1)
    return r <= c if transposed else c <= r


def _causal_pairs(n, by_query):
    if by_query:
        pairs = [(i, j) for i in range(n) for j in range(i + 1)]
    else:
        pairs = [(i, j) for j in range(n) for i in range(j, n)]
    return jnp.asarray([a for a, _ in pairs], jnp.int32), jnp.asarray([b for _, b in pairs], jnp.int32)


def _attn_fwd(p, cq, ck, nh, q_blk0):
    T = p.shape[0]
    tq = _tile(T, 512)
    nq = T // tq
    scale = LANES ** -0.5

    qi, kj = _causal_pairs(nq, by_query=True)

    def body(qi_ref, kj_ref, q_ref, k_ref, v_ref, cq_ref, ck_ref, o_ref, lse_ref, m_ref, l_ref, acc_ref):
        i, j = qi_ref[pl.program_id(1)], kj_ref[pl.program_id(1)]

        @pl.when(j == 0)
        def _():
            m_ref[...] = jnp.full(m_ref.shape, -jnp.inf, F32)
            l_ref[...] = jnp.zeros(l_ref.shape, F32)
            acc_ref[...] = jnp.zeros(acc_ref.shape, F32)

        def block(diagonal):
            s = lax.dot_general(q_ref[...], k_ref[...], (NT, ((), ())), preferred_element_type=F32) * scale
            s = s + cq_ref[...] - ck_ref[...]
            if diagonal:
                s = jnp.where(_causal(s.shape), s, -jnp.inf)
            m_new = jnp.maximum(m_ref[...], jnp.max(s, axis=-1, keepdims=True))
            alpha = jnp.exp(m_ref[...] - m_new)
            pr = jnp.exp(s - m_new)
            l_ref[...] = alpha * l_ref[...] + jnp.sum(pr, axis=-1, keepdims=True)
            acc_ref[...] = alpha * acc_ref[...] + jnp.dot(pr.astype(BF), v_ref[...], preferred_element_type=F32)
            m_ref[...] = m_new

        pl.when(j < i)(lambda: block(False))

        @pl.when(j == i)
        def _():
            block(True)
            o_ref[...] = acc_ref[...] / l_ref[...]
            lse_ref[...] = m_ref[...] + jnp.log(l_ref[...])

    def kv(off):
        return pl.BlockSpec((tq, LANES), lambda h, s, qi, kj: (kj[s], q_blk0 + off * nh + h))

    return pl.pallas_call(
        body, name="attn_fwd",
        grid_spec=pltpu.PrefetchScalarGridSpec(
            num_scalar_prefetch=2, grid=(nh, qi.shape[0]),
            in_specs=[pl.BlockSpec((tq, LANES), lambda h, s, qi, kj: (qi[s], q_blk0 + h)), kv(1), kv(2),
                      pl.BlockSpec((None, tq, 1), lambda h, s, qi, kj: (h, qi[s], 0)),
                      pl.BlockSpec((None, 1, tq), lambda h, s, qi, kj: (h, 0, kj[s]))],
            out_specs=[pl.BlockSpec((tq, LANES), lambda h, s, qi, kj: (qi[s], h)),
                       pl.BlockSpec((None, tq, 1), lambda h, s, qi, kj: (h, qi[s], 0))],
            scratch_shapes=[pltpu.VMEM((tq, 1), F32), pltpu.VMEM((tq, 1), F32), pltpu.VMEM((tq, LANES), F32)]),
        out_shape=[jax.ShapeDtypeStruct((T, nh * LANES), F32), jax.ShapeDtypeStruct((nh, T, 1), F32)],
        compiler_params=_params("parallel", "arbitrary"),
    )(qi, kj, p, p, p, cq, ck)


def _attn_bwd_q(p, cq, ck, lse, do, o, nh, q_blk0):
    T = p.shape[0]
    tq = _tile(T, 512)
    nq = T // tq
    scale = LANES ** -0.5

    qi, kj = _causal_pairs(nq, by_query=True)

    def body(qi_ref, kj_ref, q_ref, k_ref, v_ref, cq_ref, ck_ref, lse_ref, do_ref, o_ref, dq_ref, dl_ref, dr_ref,
             acc_ref):
        i, j = qi_ref[pl.program_id(1)], kj_ref[pl.program_id(1)]

        @pl.when(j == 0)
        def _():
            dl_ref[...] = jnp.sum(do_ref[...] * o_ref[...], axis=-1, keepdims=True)
            dr_ref[...] = jnp.zeros(dr_ref.shape, F32)
            acc_ref[...] = jnp.zeros(acc_ref.shape, F32)

        def block(diagonal):
            s = lax.dot_general(q_ref[...], k_ref[...], (NT, ((), ())), preferred_element_type=F32) * scale
            s = s + cq_ref[...] - ck_ref[...]
            pr = jnp.exp(s - lse_ref[...])
            if diagonal:
                pr = jnp.where(_causal(s.shape), pr, 0.0)
            dp = lax.dot_general(do_ref[...].astype(BF), v_ref[...], (NT, ((), ())), preferred_element_type=F32)
            ds = pr * (dp - dl_ref[...])
            dr_ref[...] += jnp.sum(ds, axis=-1, keepdims=True)
            acc_ref[...] += jnp.dot(ds.astype(BF), k_ref[...], preferred_element_type=F32)

        pl.when(j < i)(lambda: block(False))

        @pl.when(j == i)
        def _():
            block(True)
            dq_ref[...] = (acc_ref[...] * scale).astype(BF)

    def kv(off):
        return pl.BlockSpec((tq, LANES), lambda h, s, qi, kj: (kj[s], q_blk0 + off * nh + h))

    col = pl.BlockSpec((None, tq, 1), lambda h, s, qi, kj: (h, qi[s], 0))
    head = pl.BlockSpec((tq, LANES), lambda h, s, qi, kj: (qi[s], h))
    return pl.pallas_call(
        body, name="attn_bwd_q",
        grid_spec=pltpu.PrefetchScalarGridSpec(
            num_scalar_prefetch=2, grid=(nh, qi.shape[0]),
            in_specs=[pl.BlockSpec((tq, LANES), lambda h, s, qi, kj: (qi[s], q_blk0 + h)), kv(1), kv(2), col,
                      pl.BlockSpec((None, 1, tq), lambda h, s, qi, kj: (h, 0, kj[s])), col, head, head],
            out_specs=[head, col, col],
            scratch_shapes=[pltpu.VMEM((tq, LANES), F32)]),
        out_shape=[jax.ShapeDtypeStruct((T, nh * LANES), BF), jax.ShapeDtypeStruct((nh, T, 1), F32),
                   jax.ShapeDtypeStruct((nh, T, 1), F32)],
        compiler_params=_params("parallel", "arbitrary"),
    )(qi, kj, p, p, p, cq, ck, lse, do, o)


def _attn_bwd_kv(p, cq_row, ck_col, lse_row, delta_row, do, nh, q_blk0):
    T = p.shape[0]
    tk = _tile(T, 512)
    nk = T // tk
    scale = LANES ** -0.5

    qi, kj = _causal_pairs(nk, by_query=False)

    def body(qi_ref, kj_ref, q_ref, k_ref, v_ref, cq_ref, ck_ref, lse_ref, dl_ref, do_ref, dk_ref, dv_ref, dc_ref,
             dk_acc, dv_acc, dc_acc):
        i, j = qi_ref[pl.program_id(1)], kj_ref[pl.program_id(1)]

        @pl.when(i == j)
        def _():
            dk_acc[...] = jnp.zeros(dk_acc.shape, F32)
            dv_acc[...] = jnp.zeros(dv_acc.shape, F32)
            dc_acc[...] = jnp.zeros(dc_acc.shape, F32)

        def block(diagonal):
            st = lax.dot_general(k_ref[...], q_ref[...], (NT, ((), ())), preferred_element_type=F32) * scale
            st = st + cq_ref[...] - ck_ref[...]
            pt = jnp.exp(st - lse_ref[...])
            if diagonal:
                pt = jnp.where(_causal(st.shape, transposed=True), pt, 0.0)
            dob = do_ref[...].astype(BF)
            dv_acc[...] += jnp.dot(pt.astype(BF), dob, preferred_element_type=F32)
            dpt = lax.dot_general(v_ref[...], dob, (NT, ((), ())), preferred_element_type=F32)
            dst = pt * (dpt - dl_ref[...])
            dk_acc[...] += jnp.dot(dst.astype(BF), q_ref[...], preferred_element_type=F32)
            dc_acc[...] += jnp.sum(dst, axis=-1, keepdims=True)

        pl.when(i == j)(lambda: block(True))
        pl.when(i > j)(lambda: block(False))

        @pl.when(i == nk - 1)
        def _():
            dk_ref[...] = (dk_acc[...] * scale).astype(BF)
            dv_ref[...] = dv_acc[...].astype(BF)
            dc_ref[...] = dc_acc[...]

    def qside(blk):
        return pl.BlockSpec((tk, LANES), lambda h, s, qi, kj: (qi[s], blk + h))

    def kside(off):
        return pl.BlockSpec((tk, LANES), lambda h, s, qi, kj: (kj[s], q_blk0 + off * nh + h))

    row = pl.BlockSpec((None, 1, tk), lambda h, s, qi, kj: (h, 0, qi[s]))
    col = pl.BlockSpec((None, tk, 1), lambda h, s, qi, kj: (h, kj[s], 0))
    head = pl.BlockSpec((tk, LANES), lambda h, s, qi, kj: (kj[s], h))
    return pl.pallas_call(
        body, name="attn_bwd_kv",
        grid_spec=pltpu.PrefetchScalarGridSpec(
            num_scalar_prefetch=2, grid=(nh, qi.shape[0]),
            in_specs=[qside(q_blk0), kside(1), kside(2), row, col, row, row, qside(0)],
            out_specs=[head, head, col],
            scratch_shapes=[pltpu.VMEM((tk, LANES), F32), pltpu.VMEM((tk, LANES), F32), pltpu.VMEM((tk, 1), F32)]),
        out_shape=[jax.ShapeDtypeStruct((T, nh * LANES), BF), jax.ShapeDtypeStruct((T, nh * LANES), BF),
                   jax.ShapeDtypeStruct((nh, T, 1), F32)],
        compiler_params=_params("parallel", "arbitrary"),
    )(qi, kj, p, p, p, cq_row, ck_col, lse_row, delta_row, do)


def _mixer_dims(D):
    dc, da, dl = D // 4, D // 2, D // 4
    nh = da // LANES
    n_main = 3 * dc + 3 * da + 2 * dl
    return dc, da, dl, nh, n_main


def _pad_mix_w_in(wfull):
    D = wfull.shape[0]
    dc, da, dl, nh, n_main = _mixer_dims(D)
    a = 3 * dc + 3 * da
    return jnp.concatenate([wfull[:, :a], wfull[:, a + nh:], wfull[:, a:a + nh],
                            jnp.zeros((D, LANES - nh), wfull.dtype)], axis=1)


def _unpad_mix_w_in(wp):
    D = wp.shape[0]
    dc, da, dl, nh, n_main = _mixer_dims(D)
    a = 3 * dc + 3 * da
    return jnp.concatenate([wp[:, :a], wp[:, n_main:n_main + nh], wp[:, a:n_main]], axis=1)


def _head_cols(c, nh):
    t = jnp.transpose(c[:, :nh])
    return t[:, :, None], t[:, None, :]


def _mixer_forward(x, gain, wp, wo, sp):
    T, D = x.shape
    dc, da, dl, nh, n_main = _mixer_dims(D)
    nbc, nbl = dc // LANES, dl // LANES
    h = _rmsnorm(x, gain)
    tm = _tile(T, 512)
    tn = n_main // 4
    p = _mm("mix_in", (4, T // tm, 1),
            (h, (tm, D), lambda j, i, k: (i, 0)), (wp, (D, tn), lambda j, i, k: (0, j)),
            (jax.ShapeDtypeStruct((T, n_main), BF), (tm, tn), lambda j, i, k: (i, j)), NN)
    f = _mm("mix_in_fgate", (T // tm, 1, 1),
            (h, (tm, D), lambda i, j, k: (i, 0)), (wp, (D, LANES), lambda i, j, k: (0, n_main // LANES)),
            (jax.ShapeDtypeStruct((T, LANES), F32), (tm, LANES), lambda i, j, k: (i, 0)), NN)
    cum = _fgate_cum(f, sp["fgate_b"])
    cq, ck = _head_cols(cum, nh)
    yc = _conv_fwd(p, sp["conv_w"], nbc)
    q_blk0 = 3 * nbc
    ya, lse = _attn_fwd(p, cq, ck, nh, q_blk0)
    gate_blk0 = q_blk0 + 3 * nh
    yl, hs = _lru_fwd(p, sp["lru_conv_w"], sp["lru_conv_b"], sp["lru_w_a"], sp["lru_b_a"], sp["lru_w_x"],
                      sp["lru_b_x"], sp["lru_lambda"], nbl, gate_blk0)
    yn = _groupnorm(yc, ya, yl, sp["mix_out_norm"])
    xn = _mm("mix_out", (T // tm, 1, 1),
             (yn, (tm, D), lambda i, j, k: (i, 0)), (wo, (D, D), lambda i, j, k: (0, 0)),
             (jax.ShapeDtypeStruct((T, D), F32), (tm, D), lambda i, j, k: (i, 0)), NN,
             resid=(x, (tm, D), lambda i, j, k: (i, 0)))
    return xn, (x, h, p, f, cq, ck, yc, ya, lse, yl, hs, yn)


def _mixer_backward(dx, saved, gain, wp, wo, sp):
    x, h, p, f, cq, ck, yc, ya, lse, yl, hs, yn = saved
    T, D = x.shape
    dc, da, dl, nh, n_main = _mixer_dims(D)
    nbc, nbl = dc // LANES, dl // LANES
    q_blk0 = 3 * nbc
    gate_blk0 = q_blk0 + 3 * nh
    tm = _tile(T, 512)
    tnd, tnd2 = _tile(D, 512), _tile(D, 1024)
    dyn = _mm("mix_out_dy", (T // tm, 1, 1),
              (dx, (tm, D), lambda i, j, k: (i, 0)), (wo, (D, D), lambda i, j, k: (0, 0)),
              (jax.ShapeDtypeStruct((T, D), F32), (tm, D), lambda i, j, k: (i, 0)), NT)
    dwo = _mm("mix_out_dw", (D // tnd, D // tnd2, 1),
              (yn, (T, tnd), lambda i, j, k: (0, i)), (dx, (T, tnd2), lambda i, j, k: (0, j)),
              (jax.ShapeDtypeStruct((D, D), BF), (tnd, tnd2), lambda i, j, k: (i, j)), TN)
    dyc, dya, dyl, dgn = _groupnorm_bwd(dyn, yc, ya, yl, sp["mix_out_norm"])
    dcb, dcc, dcv, dconv_w = _conv_bwd(dyc, p, sp["conv_w"], nbc)
    dq, delta, drow = _attn_bwd_q(p, cq, ck, lse, dya, ya, nh, q_blk0)
    as_row = lambda t: t.reshape(nh, 1, T)
    dk, dv, dcol = _attn_bwd_kv(p, ck, cq, as_row(lse), as_row(delta), dya, nh, q_blk0)
    lanes = lambda t: jnp.pad(jnp.transpose(t.reshape(nh, T)), ((0, 0), (0, LANES - nh)))
    df, dfb = _fgate_cum_bwd(lanes(drow), lanes(dcol), f, sp["fgate_b"])
    dgate, dlx, dcw, dcbias, dwa, dba, dwx, dbx, dlam = _lru_bwd(
        dyl, hs, p, sp["lru_conv_w"], sp["lru_conv_b"], sp["lru_w_a"], sp["lru_b_a"], sp["lru_w_x"],
        sp["lru_b_x"], sp["lru_lambda"], nbl, gate_blk0)
    dp = jnp.concatenate([dcb, dcc, dcv, dq, dk, dv, dgate, dlx, df], axis=1)
    n_pad = n_main + LANES
    tn, tkp = n_pad // 5, n_pad // 3
    dwp = _mm("mix_in_dw", (n_pad // tn, D // tnd, 1),
              (h, (T, tnd), lambda j, i, k: (0, i)), (dp, (T, tn), lambda j, i, k: (0, j)),
              (jax.ShapeDtypeStruct((D, n_pad), BF), (tnd, tn), lambda j, i, k: (i, j)), TN)
    dh = _mm("mix_in_dh", (T // tm, 1, n_pad // tkp),
             (dp, (tm, tkp), lambda i, j, k: (i, k)), (wp, (D, tkp), lambda i, j, k: (0, k)),
             (jax.ShapeDtypeStruct((T, D), F32), (tm, D), lambda i, j, k: (i, 0)), NT)
    dxn, dgain = _rmsnorm_bwd(dh, x, gain, dx)
    small = {"norm_mix": dgain, "mix_out_norm": dgn, "conv_w": dconv_w, "fgate_b": dfb[:, :nh],
             "lru_conv_w": dcw, "lru_conv_b": dcbias, "lru_w_a": dwa, "lru_b_a": dba, "lru_w_x": dwx,
             "lru_b_x": dbx, "lru_lambda": dlam}
    return dxn, dwp, dwo, small


def _position():
    x, y, c = lax.axis_index("x"), lax.axis_index("y"), lax.axis_index("c")
    return x, y, c, [(1 - x, y), (x, 1 - y), (1 - x, 1 - y)]


def _all_gather(name, xs):
    n = len(xs)

    def body(*refs):
        x_refs, o_refs = refs[:n], refs[n:2 * n]
        send_sems, recv_sems, local_sems = refs[2 * n:]
        x, y, c, chips = _position()
        me, sibling = (x, y, c), (x, y, 1 - c)

        def copy(t, k, block, to, src=None):
            dst = o_refs[t].at[4 * block[0] + 2 * block[1] + block[2]]
            return pltpu.make_async_remote_copy(
                src_ref=dst if src is None else src, dst_ref=dst, send_sem=send_sems.at[t, k],
                recv_sem=recv_sems.at[t, k], device_id=to, device_id_type=MESH)

        mine = [pltpu.make_async_copy(x_refs[t], o_refs[t].at[4 * x + 2 * y + c], local_sems.at[t]) for t in range(n)]
        for cp in mine:
            cp.start()
        first = []
        for t in range(n):
            first.append(copy(t, 0, me, sibling, src=x_refs[t]))
            first += [copy(t, 1 + j, me, (*chip, c), src=x_refs[t]) for j, chip in enumerate(chips)]
        for cp in first:
            cp.start()
        passed = []
        for j, chip in enumerate(chips):
            for t in range(n):
                copy(t, 1 + j, (*chip, c), me).wait_recv()
                passed.append(copy(t, 4 + j, (*chip, c), sibling))
                passed[-1].start()
        for t in range(n):
            copy(t, 0, sibling, me).wait_recv()
            for j, chip in enumerate(chips):
                copy(t, 4 + j, (*chip, 1 - c), me).wait_recv()
        for cp in first + passed:
            cp.wait_send()
        for cp in mine:
            cp.wait()

    return pl.pallas_call(
        body, name=name,
        in_specs=[_any()] * n, out_specs=[_any()] * n,
        out_shape=[jax.ShapeDtypeStruct((N_DEV,) + v.shape, v.dtype) for v in xs],
        scratch_shapes=[pltpu.SemaphoreType.DMA((n, 7)), pltpu.SemaphoreType.DMA((n, 7)),
                        pltpu.SemaphoreType.DMA((n,))],
    )(*xs)


def _add_sibling(g4, recv, c_idx):
    _, _, R, C = g4.shape
    tr = _tile(R, 256)

    def body(c_ref, g_ref, r_ref, o_ref):
        o_ref[...] = (g_ref[...].astype(F32) + r_ref[...].astype(F32)).astype(o_ref.dtype)

    return pl.pallas_call(
        body, name="rs_add_sibling",
        grid_spec=pltpu.PrefetchScalarGridSpec(
            num_scalar_prefetch=1, grid=(4, R // tr),
            in_specs=[pl.BlockSpec((None, None, tr, C), lambda q, i, c_ref: (q, c_ref[0], i, 0)),
                      pl.BlockSpec((None, None, tr, C), lambda q, i, c_ref: (q, 0, i, 0))],
            out_specs=pl.BlockSpec((None, tr, C), lambda q, i, c_ref: (q, i, 0))),
        out_shape=jax.ShapeDtypeStruct((4, R, C), g4.dtype),
        compiler_params=_params("parallel", "parallel"),
    )(c_idx, g4, recv)


def _hbm_spec():
    return pl.BlockSpec(memory_space=pltpu.HBM)


def _sem_spec():
    return pl.BlockSpec(memory_space=pltpu.SEMAPHORE)


def _side_effects():
    return pltpu.CompilerParams(has_side_effects=pltpu.SideEffectType.DATAFLOW_SIDE_EFFECTING)


def _in_hbm(v):
    return pltpu.with_memory_space_constraint(v, pltpu.HBM)


def _split_start(name, srcs, lands, n_copies, copies_of, after):
    n = len(srcs)
    ns = n * n_copies

    def body(*refs):
        src_refs, land_refs = refs[:n], refs[n:2 * n]
        send_sems, recv_sems = refs[2 * n + 1:2 * n + 1 + ns], refs[2 * n + 1 + ns:2 * n + 1 + 2 * ns]
        token = refs[-1]
        for t in range(n):
            for k, (s, d, to) in enumerate(copies_of(t, src_refs[t], land_refs[t])):
                pltpu.make_async_remote_copy(src_ref=s, dst_ref=d, send_sem=send_sems[t * n_copies + k],
                                             recv_sem=recv_sems[t * n_copies + k], device_id=to,
                                             device_id_type=MESH).start()
        token[...] = jnp.zeros(token.shape, token.dtype)

    thru = [pltpu.HBM(v.shape, v.dtype) for v in list(srcs) + list(lands)]
    out = pl.pallas_call(
        body, name=name,
        in_specs=[_hbm_spec()] * (2 * n) + [_any()],
        out_specs=[_sem_spec()] * (2 * ns) + [_hbm_spec()] * (2 * n) + [pl.BlockSpec(memory_space=pltpu.VMEM)],
        out_shape=[pltpu.SemaphoreType.DMA(())] * (2 * ns) + thru + [jax.ShapeDtypeStruct((8, LANES), F32)],
        input_output_aliases={i: 2 * ns + i for i in range(2 * n)},
        compiler_params=_side_effects(),
    )(*[_in_hbm(v) for v in list(srcs) + list(lands)], after)
    return out[:ns], out[ns:2 * ns], out[2 * ns:2 * ns + n], out[2 * ns + n:2 * ns + 2 * n], out[-1]


def _split_wait(name, send_sems, recv_sems, srcs, lands, n_copies, waits_of, after):
    n = len(srcs)
    ns = n * n_copies

    def body(*refs):
        src_refs, land_refs = refs[:n], refs[n:2 * n]
        send_refs, recv_refs = refs[2 * n:2 * n + ns], refs[2 * n + ns:2 * n + 2 * ns]
        x, y, c, _ = _position()
        for t in range(n):
            for k, (s, d) in enumerate(waits_of(t, src_refs[t], land_refs[t])):
                cp = pltpu.make_async_remote_copy(src_ref=s, dst_ref=d, send_sem=send_refs[t * n_copies + k],
                                                  recv_sem=recv_refs[t * n_copies + k], device_id=(x, y, 1 - c),
                                                  device_id_type=MESH)
                cp.wait_send()
                cp.wait_recv()

    out = pl.pallas_call(
        body, name=name,
        in_specs=[_hbm_spec()] * (2 * n) + [_sem_spec()] * (2 * ns) + [_any()],
        out_specs=[_hbm_spec()] * (2 * n),
        out_shape=[pltpu.HBM(v.shape, v.dtype) for v in list(srcs) + list(lands)],
        input_output_aliases={i: i for i in range(2 * n)},
        compiler_params=_side_effects(),
    )(*srcs, *lands, *send_sems, *recv_sems, after)
    return out[:n], out[n:]


def _block_of(px, py, pc):
    return 4 * px + 2 * py + pc


def _gather_phase1_start(name, xs, after):
    lands = [lax.empty((N_DEV,) + v.shape, v.dtype) for v in xs]

    def copies_of(t, x_ref, land_ref):
        x, y, c, chips = _position()
        dst = land_ref.at[_block_of(x, y, c)]
        return [(x_ref, dst, (x, y, 1 - c))] + [(x_ref, dst, (*chip, c)) for chip in chips]

    return _split_start(name, xs, lands, 4, copies_of, after)


def _gather_phase1_wait(name, started, after):
    send_sems, recv_sems, xs, lands, _ = started

    def waits_of(t, x_ref, land_ref):
        x, y, c, chips = _position()
        return [(x_ref, land_ref.at[_block_of(x, y, 1 - c)])] + [(x_ref, land_ref.at[_block_of(*chip, c)])
                                                                  for chip in chips]

    return _split_wait(name, send_sems, recv_sems, xs, lands, 4, waits_of, after)


def _gather_phase2_start(name, lands, after):
    keep = [lax.empty((8, LANES), v.dtype) for v in lands]

    def copies_of(t, _, land_ref):
        x, y, c, chips = _position()
        return [(land_ref.at[_block_of(*chip, c)], land_ref.at[_block_of(*chip, c)], (x, y, 1 - c)) for chip in chips]

    return _split_start(name, keep, lands, 3, copies_of, after)


def _gather_phase2_wait(name, started, after):
    send_sems, recv_sems, keep, lands, _ = started

    def waits_of(t, _, land_ref):
        x, y, c, chips = _position()
        return [(land_ref.at[_block_of(*chip, c)], land_ref.at[_block_of(*chip, 1 - c)]) for chip in chips]

    return _split_wait(name, send_sems, recv_sems, keep, lands, 3, waits_of, after)[1]


def _place_block(name, land, src, block_idx):
    R, C = src.shape
    tr = _tile(R, 512)

    def body(b_ref, land_ref, s_ref, o_ref):
        o_ref[...] = s_ref[...]

    return pl.pallas_call(
        body, name=name,
        grid_spec=pltpu.PrefetchScalarGridSpec(
            num_scalar_prefetch=1, grid=(R // tr,),
            in_specs=[_any(), pl.BlockSpec((tr, C), lambda i, b: (i, 0))],
            out_specs=pl.BlockSpec((None, tr, C), lambda i, b: (b[0], i, 0))),
        out_shape=jax.ShapeDtypeStruct(land.shape, land.dtype),
        input_output_aliases={1: 0},
        compiler_params=_params("parallel"),
    )(block_idx, land, src)


def _sibling_start(name, gs, after):
    g4 = [g.reshape((4, 2) + g.shape[1:]) for g in gs]
    lands = [lax.empty((4, 1) + g.shape[2:], g.dtype) for g in g4]

    def copies_of(t, g_ref, land_ref):
        x, y, c, _ = _position()
        return [(g_ref.at[:, pl.ds(1 - c, 1)], land_ref, (x, y, 1 - c))]

    return _split_start(name, g4, lands, 1, copies_of, after)


def _sibling_wait(name, started, after):
    send_sems, recv_sems, g4, lands, _ = started

    def waits_of(t, g_ref, land_ref):
        x, y, c, _ = _position()
        return [(g_ref.at[:, pl.ds(1 - c, 1)], land_ref)]

    return _split_wait(name, send_sems, recv_sems, g4, lands, 1, waits_of, after)


def _scatter_chips_start(name, ps, after):
    lands = [lax.empty((3,) + v.shape[1:], v.dtype) for v in ps]

    def copies_of(t, p_ref, land_ref):
        x, y, c, chips = _position()
        return [(p_ref.at[2 * chip[0] + chip[1]], land_ref.at[k], (*chip, c)) for k, chip in enumerate(chips)]

    return _split_start(name, ps, lands, 3, copies_of, after)


def _scatter_chips_wait(name, started, after):
    send_sems, recv_sems, ps, lands, _ = started

    def waits_of(t, p_ref, land_ref):
        x, y, c, chips = _position()
        return [(p_ref.at[2 * chip[0] + chip[1]], land_ref.at[k]) for k, chip in enumerate(chips)]

    return _split_wait(name, send_sems, recv_sems, ps, lands, 3, waits_of, after)


def _scatter_chips_after_sibling(tag, sibling, c_idx, after):
    g4, recv = _sibling_wait("rs_sibling_wait_" + tag, sibling, after)
    ps = [_add_sibling(g, r, c_idx) for g, r in zip(g4, recv)]
    return _scatter_chips_start("rs_chips_start_" + tag, ps, ps[0])


def _sum_devices(parts):
    _, R, C = parts.shape
    tr = _tile(R, 512)

    def body(p_ref, o_ref):
        acc = p_ref[0]
        for d in range(1, N_DEV):
            acc = acc + p_ref[d]
        o_ref[...] = acc

    return pl.pallas_call(
        body, name="sum_devices", grid=(R // tr,),
        in_specs=[pl.BlockSpec((N_DEV, tr, C), lambda i: (0, i, 0))],
        out_specs=pl.BlockSpec((tr, C), lambda i: (i, 0)),
        out_shape=jax.ShapeDtypeStruct((R, C), F32),
        compiler_params=_params("parallel"),
    )(parts)


def _adam_math(w, g, m, v):
    m = ADAM_B1 * m + (1.0 - ADAM_B1) * g
    v = ADAM_B2 * v + (1.0 - ADAM_B2) * (g * g)
    m_hat = m / (1.0 - ADAM_B1 ** ADAM_STEP)
    v_hat = v / (1.0 - ADAM_B2 ** ADAM_STEP)
    return -ADAM_LR * (m_hat / (jnp.sqrt(v_hat) + ADAM_EPS) + ADAM_WD * w), m, v


def _adam_layer(name, layer, w, m, v, own, landed, chip_idx, prev, after):
    L, R, C = w.shape
    tr = _tile(R, 128)
    slab = pl.BlockSpec((None, tr, C), lambda i, q: (layer, i, 0))

    def body(q_ref, w_ref, m_ref, v_ref, own_ref, land_ref, after_ref, *rest):
        g_ref, d_ref, nm_ref, nv_ref = rest[-4:]
        g = land_ref[0].astype(F32)
        for k in range(1, 3):
            g = g + land_ref[k].astype(F32)
        g = g + own_ref[...].astype(F32)
        d, nm, nv = _adam_math(w_ref[...], g, m_ref[...], v_ref[...])
        g_ref[...] = g
        d_ref[...] = d
        nm_ref[...] = nm
        nv_ref[...] = nv

    n_prev = 0 if prev is None else 4
    out = jax.ShapeDtypeStruct((L, R, C), F32)
    return pl.pallas_call(
        body, name=name,
        grid_spec=pltpu.PrefetchScalarGridSpec(
            num_scalar_prefetch=1, grid=(R // tr,),
            in_specs=[slab, slab, slab, pl.BlockSpec((None, tr, C), lambda i, q: (q[0], i, 0)),
                      pl.BlockSpec((3, tr, C), lambda i, q: (0, i, 0)), _any()] + [_any()] * n_prev,
            out_specs=[slab] * 4),
        out_shape=[out] * 4,
        input_output_aliases={7 + k: k for k in range(n_prev)},
        compiler_params=_params("parallel"),
    )(chip_idx, w, m, v, own, landed, after, *(prev or ()))


def _adam_small(w, g, m, v):
    R, C = w.shape
    tr = _tile(R, 512)
    spec = pl.BlockSpec((tr, C), lambda i: (i, 0))

    def body(w_ref, g_ref, m_ref, v_ref, d_ref, nm_ref, nv_ref):
        d, nm, nv = _adam_math(w_ref[...], g_ref[...], m_ref[...], v_ref[...])
        d_ref[...] = d
        nm_ref[...] = nm
        nv_ref[...] = nv

    out = jax.ShapeDtypeStruct((R, C), F32)
    return pl.pallas_call(
        body, name="adam_small", grid=(R // tr,),
        in_specs=[spec] * 4, out_specs=[spec] * 3, out_shape=[out] * 3,
        compiler_params=_params("parallel"),
    )(w, g, m, v)


def _pack(arrays, row_multiple=512):
    flat = jnp.concatenate([a.reshape(-1).astype(F32) for a in arrays])
    per = row_multiple * LANES
    total = -(-flat.shape[0] // per) * per
    return jnp.pad(flat, (0, total - flat.shape[0])).reshape(total // LANES, LANES)


def _unpack(packed, shapes):
    flat = packed.reshape(-1)
    out, off = [], 0
    for s in shapes:
        n = math.prod(s)
        out.append(flat[off:off + n].reshape(s))
        off += n
    return out


BIG = ["ffn1_w_in", "ffn1_w_out", "mix_w_in", "mix_w_out", "ffn2_w_in", "ffn2_w_out"]
SHARDED_SMALL = ["conv_w", "lru_conv_w"]
WEIGHTS = ["norm_ffn1", "ffn1_w_in", "ffn1_w_out", "norm_mix", "mix_w_in", "conv_w", "fgate_b", "lru_conv_w",
           "lru_conv_b", "lru_w_a", "lru_b_a", "lru_w_x", "lru_b_x", "lru_lambda", "mix_out_norm", "mix_w_out",
           "norm_ffn2", "ffn2_w_in", "ffn2_w_out", "final_norm"]
REPLICATED = [n for n in WEIGHTS if n not in BIG and n not in SHARDED_SMALL]


def _step(x, target, w, m, v):
    L = w["norm_ffn1"].shape[0]
    T, D = x.shape[1], x.shape[2]
    dc, da, dl, nh, n_main = _mixer_dims(D)
    xi, yi, ci = lax.axis_index("x"), lax.axis_index("y"), lax.axis_index("c")
    me = 4 * xi + 2 * yi + ci
    c_idx = ci.astype(jnp.int32).reshape(1)

    taps = jnp.concatenate([w["conv_w"].reshape(-1), w["lru_conv_w"].reshape(-1)])
    taps = jnp.pad(taps, (0, (-taps.shape[0]) % (8 * LANES))).reshape(-1, LANES)
    def shards_of(l):
        return [w[n][l].astype(BF) for n in BIG] + ([taps] if l == 0 else [])

    me_idx = me.astype(jnp.int32).reshape(1)
    chip_idx = (2 * xi + yi).astype(jnp.int32).reshape(1)

    def place_own(tag, lands, own):
        return [_place_block("gather_own_" + tag, land, src, me_idx) for land, src in zip(lands, own)]

    gathered = [None] * L
    phase1 = [None] * L
    shards0 = shards_of(0)
    first_start = _gather_phase1_start("gather_p1_start_0_first", shards0[:2] + [taps], taps)
    rest_start = _gather_phase1_start("gather_p1_start_0_rest", shards0[2:len(BIG)], first_start[-1])
    own, landed = _gather_phase1_wait("gather_p1_wait_0_first", first_start, rest_start[-1])
    phase2 = _gather_phase2_start("gather_p2_start_0_first", landed, landed[0])
    landed = _gather_phase2_wait("gather_p2_wait_0_first", phase2, phase2[-1])
    first0 = place_own("0_first", landed, own)
    taps_all = first0[2].reshape(N_DEV, -1)
    n_cw = math.prod(w["conv_w"].shape)
    ch = w["conv_w"].shape[-1]
    conv_w_full = jnp.moveaxis(taps_all[:, :n_cw].reshape((N_DEV,) + w["conv_w"].shape), 0, -2).reshape(L, 3, N_DEV * ch)
    n_lw = math.prod(w["lru_conv_w"].shape)
    lru_conv_w_full = jnp.moveaxis(taps_all[:, n_cw:n_cw + n_lw].reshape((N_DEV,) + w["lru_conv_w"].shape), 0, -2
                                   ).reshape(L, 4, N_DEV * ch)

    def layer_weights(l):
        g = dict(zip(BIG, gathered[l]))
        F = g["ffn1_w_out"].shape[1] * N_DEV
        wmix = jnp.transpose(g["mix_w_in"], (1, 0, 2)).reshape(D, -1)
        return {"ffn1_w_in": g["ffn1_w_in"], "ffn1_w_out": g["ffn1_w_out"].reshape(F, D),
                "wp": _pad_mix_w_in(wmix), "wo": g["mix_w_out"].reshape(D, D),
                "ffn2_w_in": g["ffn2_w_in"], "ffn2_w_out": g["ffn2_w_out"].reshape(F, D)}

    def small_params(l):
        return {"fgate_b": jnp.pad(w["fgate_b"][l], (0, LANES - nh)).reshape(1, LANES),
                "conv_w": conv_w_full[l], "lru_conv_w": lru_conv_w_full[l],
                "lru_conv_b": w["lru_conv_b"][l].reshape(1, dl), "lru_w_a": w["lru_w_a"][l],
                "lru_b_a": w["lru_b_a"][l].reshape(1, dl), "lru_w_x": w["lru_w_x"][l],
                "lru_b_x": w["lru_b_x"][l].reshape(1, dl), "lru_lambda": w["lru_lambda"][l].reshape(1, dl),
                "mix_out_norm": w["mix_out_norm"][l].reshape(1, D)}

    gain = lambda n, l, token=None: w[n][l].reshape(1, D) + (0.0 if token is None else token[0:1, 0:1])

    xc = x[0]
    saved, lw, sps = [], [], []
    for l in range(L):
        w_in1, w_out1 = first0[:2] if l == 0 else gathered[l][:2]
        xc, s1 = _ffn_forward(xc, gain("norm_ffn1", l), w_in1, w_out1.reshape(-1, D))
        if l == 0:
            own, landed = _gather_phase1_wait("gather_p1_wait_0_rest", rest_start, xc)
            phase2 = _gather_phase2_start("gather_p2_start_0_rest", landed, landed[0])
            order_token = phase2[-1]
            if L > 1:
                phase1[1] = _gather_phase1_start("gather_p1_start_1", shards_of(1), phase2[-1])
                order_token = phase1[1][-1]
            landed = _gather_phase2_wait("gather_p2_wait_0_rest", phase2, order_token)
            gathered[0] = first0[:2] + place_own("0_rest", landed, own)
        lw.append(layer_weights(l))
        sps.append(small_params(l))
        xc, s2 = _mixer_forward(xc, gain("norm_mix", l), lw[l]["wp"], lw[l]["wo"], sps[l])
        order_token = None
        if l + 1 < L:
            own, landed = _gather_phase1_wait("gather_p1_wait_%d" % (l + 1), phase1[l + 1], xc)
            phase2 = _gather_phase2_start("gather_p2_start_%d" % (l + 1), landed, landed[0])
            order_token = phase2[-1]
            if l + 2 < L:
                phase1[l + 2] = _gather_phase1_start("gather_p1_start_%d" % (l + 2), shards_of(l + 2), phase2[-1])
                order_token = phase1[l + 2][-1]
        xc, s3 = _ffn_forward(xc, gain("norm_ffn2", l, order_token), lw[l]["ffn2_w_in"], lw[l]["ffn2_w_out"])
        if l + 1 < L:
            landed = _gather_phase2_wait("gather_p2_wait_%d" % (l + 1), phase2, xc)
            gathered[l + 1] = place_own(str(l + 1), landed, own)
        saved.append((s1, s2, s3))
    loss_part, dx, d_final = _loss_head(xc, w["final_norm"].reshape(1, D), target[0])

    small_grads = [None] * L
    parts = [None] * L
    sibling = [None] * L
    scatter = [None] * L
    scatter_early = []
    for l in reversed(range(L)):
        s1, s2, s3 = saved[l]
        dh, dw3_2, dwout_2 = _ffn_backward(dx, s3, lw[l]["ffn2_w_in"], lw[l]["ffn2_w_out"])
        dx, dg3 = _rmsnorm_bwd(dh, s3[0], gain("norm_ffn2", l), dx)
        sp_l = sps[l]
        if l + 1 < L:
            scatter[l + 1] = _scatter_chips_after_sibling(str(l + 1), sibling[l + 1], c_idx, dx)
            sp_l = dict(sp_l, mix_out_norm=sp_l["mix_out_norm"] + scatter[l + 1][-1][0:1, 0:1])
        dx, dwp, dwo, sg = _mixer_backward(dx, s2, gain("norm_mix", l), lw[l]["wp"], lw[l]["wo"], sp_l)
        dmix = jnp.transpose(_unpad_mix_w_in(dwp).reshape(D, N_DEV, -1), (1, 0, 2))
        grads = [dmix, dwo.reshape(N_DEV, D // N_DEV, D), dw3_2, dwout_2]
        between = None
        if l == 0:
            early = _sibling_start("rs_sibling_start_0_early", grads, dwp)
            grads = []

            def between(dgu):
                scatter_early.append(_scatter_chips_after_sibling("0_early", early, c_idx, dgu))
                return scatter_early[0][-1]

        dh, dw3_1, dwout_1 = _ffn_backward(dx, s1, lw[l]["ffn1_w_in"], lw[l]["ffn1_w_out"], between)
        if l + 1 < L:
            parts[l + 1] = _scatter_chips_wait("rs_chips_wait_%d" % (l + 1), scatter[l + 1], dh)
        sibling[l] = _sibling_start("rs_sibling_start_%d" % l, [dw3_1, dwout_1] + grads, dh)
        dx, dg1 = _rmsnorm_bwd(dh, s1[0], gain("norm_ffn1", l, sibling[l][-1]), dx)
        sg["norm_ffn2"], sg["norm_ffn1"] = dg3, dg1
        small_grads[l] = sg

    out_g, out_d, out_m, out_v = {}, {}, {}, {}
    prev = {n: None for n in BIG}

    def adam_big(l, after):
        own, landed = parts[l]
        for t, n in enumerate(BIG):
            prev[n] = _adam_layer("adam_%s_%d" % (n, l), l, w[n], m[n], v[n], own[t], landed[t], chip_idx, prev[n], after)

    small_names = REPLICATED + SHARDED_SMALL
    partial = []
    for n in small_names:
        if n == "final_norm":
            partial.append(d_final)
        else:
            partial.append(jnp.stack([small_grads[l][n].reshape(w[n].shape[1:]) if n not in SHARDED_SMALL
                                      else small_grads[l][n] for l in range(L)]))
    partial.append(loss_part[0, :1])
    packed = _pack(partial)
    small_gathered = _all_gather("gather_small_grads", [packed])[0]
    scatter[0] = _scatter_chips_after_sibling("0", sibling[0], c_idx, small_gathered)
    for l in reversed(range(1, L)):
        adam_big(l, scatter[0][-1])
    summed = _sum_devices(small_gathered)
    full_shapes = [w[n].shape for n in REPLICATED] + [(L, 3, N_DEV * ch), (L, 4, N_DEV * ch), (1,)]
    full = _unpack(summed, full_shapes)
    loss = full[-1][0]
    g_small = dict(zip(small_names, full[:-1]))
    for n in SHARDED_SMALL:
        g_small[n] = lax.dynamic_slice_in_dim(g_small[n], me * ch, ch, axis=2)
    shapes = [w[n].shape for n in small_names]
    d_s, m_s, v_s = _adam_small(_pack([w[n] for n in small_names]), _pack([g_small[n] for n in small_names]),
                                _pack([m[n] for n in small_names]), _pack([v[n] for n in small_names]))
    for n, d_, m_, v_ in zip(small_names, _unpack(d_s, shapes), _unpack(m_s, shapes), _unpack(v_s, shapes)):
        out_g[n], out_d[n], out_m[n], out_v[n] = g_small[n], d_, m_, v_

    behind = d_s if L == 1 else prev[BIG[-1]][0]
    own_early, landed_early = _scatter_chips_wait("rs_chips_wait_0_early", scatter_early[0], behind)
    own_late, landed_late = _scatter_chips_wait("rs_chips_wait_0", scatter[0], behind)
    parts[0] = (list(own_late) + list(own_early), list(landed_late) + list(landed_early))
    adam_big(0, d_s)
    for n in BIG:
        out_g[n], out_d[n], out_m[n], out_v[n] = prev[n]

    return (loss, dx[None], *[out_g[n] for n in WEIGHTS], *[out_d[n] for n in WEIGHTS],
            *[out_m[n] for n in WEIGHTS], *[out_v[n] for n in WEIGHTS])


def kernel(x, norm_ffn1, ffn1_w_in, ffn1_w_out, norm_mix, mix_w_in, conv_w, fgate_b, lru_conv_w, lru_conv_b, lru_w_a, lru_b_a, lru_w_x, lru_b_x, lru_lambda, mix_out_norm, mix_w_out, norm_ffn2, ffn2_w_in, ffn2_w_out, final_norm, loss_target, m_norm_ffn1, m_ffn1_w_in, m_ffn1_w_out, m_norm_mix, m_mix_w_in, m_conv_w, m_fgate_b, m_lru_conv_w, m_lru_conv_b, m_lru_w_a, m_lru_b_a, m_lru_w_x, m_lru_b_x, m_lru_lambda, m_mix_out_norm, m_mix_w_out, m_norm_ffn2, m_ffn2_w_in, m_ffn2_w_out, m_final_norm, v_norm_ffn1, v_ffn1_w_in, v_ffn1_w_out, v_norm_mix, v_mix_w_in, v_conv_w, v_fgate_b, v_lru_conv_w, v_lru_conv_b, v_lru_w_a, v_lru_b_a, v_lru_w_x, v_lru_b_x, v_lru_lambda, v_mix_out_norm, v_mix_w_out, v_norm_ffn2, v_ffn2_w_in, v_ffn2_w_out, v_final_norm):
    w = dict(norm_ffn1=norm_ffn1, ffn1_w_in=ffn1_w_in, ffn1_w_out=ffn1_w_out, norm_mix=norm_mix, mix_w_in=mix_w_in,
             conv_w=conv_w, fgate_b=fgate_b, lru_conv_w=lru_conv_w, lru_conv_b=lru_conv_b, lru_w_a=lru_w_a,
             lru_b_a=lru_b_a, lru_w_x=lru_w_x, lru_b_x=lru_b_x, lru_lambda=lru_lambda, mix_out_norm=mix_out_norm,
             mix_w_out=mix_w_out, norm_ffn2=norm_ffn2, ffn2_w_in=ffn2_w_in, ffn2_w_out=ffn2_w_out,
             final_norm=final_norm)
    m = dict(norm_ffn1=m_norm_ffn1, ffn1_w_in=m_ffn1_w_in, ffn1_w_out=m_ffn1_w_out, norm_mix=m_norm_mix,
             mix_w_in=m_mix_w_in, conv_w=m_conv_w, fgate_b=m_fgate_b, lru_conv_w=m_lru_conv_w,
             lru_conv_b=m_lru_conv_b, lru_w_a=m_lru_w_a, lru_b_a=m_lru_b_a, lru_w_x=m_lru_w_x, lru_b_x=m_lru_b_x,
             lru_lambda=m_lru_lambda, mix_out_norm=m_mix_out_norm, mix_w_out=m_mix_w_out, norm_ffn2=m_norm_ffn2,
             ffn2_w_in=m_ffn2_w_in, ffn2_w_out=m_ffn2_w_out, final_norm=m_final_norm)
    v = dict(norm_ffn1=v_norm_ffn1, ffn1_w_in=v_ffn1_w_in, ffn1_w_out=v_ffn1_w_out, norm_mix=v_norm_mix,
             mix_w_in=v_mix_w_in, conv_w=v_conv_w, fgate_b=v_fgate_b, lru_conv_w=v_lru_conv_w,
             lru_conv_b=v_lru_conv_b, lru_w_a=v_lru_w_a, lru_b_a=v_lru_b_a, lru_w_x=v_lru_w_x, lru_b_x=v_lru_b_x,
             lru_lambda=v_lru_lambda, mix_out_norm=v_mix_out_norm, mix_w_out=v_mix_w_out, norm_ffn2=v_norm_ffn2,
             ffn2_w_in=v_ffn2_w_in, ffn2_w_out=v_ffn2_w_out, final_norm=v_final_norm)
    return _step(x, loss_target, w, m, v)
```

```python
import math

import jax
import jax.numpy as jnp
from jax import lax
from jax.experimental import pallas as pl
from jax.experimental.pallas import tpu as pltpu

F32 = jnp.float32
BF = jnp.bfloat16
EPS = 1e-6
LANES = 128
VMEM_LIMIT_V7X = 56 * 1024 * 1024
MESH = pl.DeviceIdType.MESH
N_DEV = 8
LRU_C = 8.0
ADAM_LR, ADAM_B1, ADAM_B2, ADAM_EPS, ADAM_WD, ADAM_STEP = 0.001, 0.9, 0.999, 1e-08, 0.01, 10
GELU_C = math.sqrt(2.0 / math.pi)
GELU_K = 0.044715


def _params(*sem):
    return pltpu.CompilerParams(dimension_semantics=sem, vmem_limit_bytes=VMEM_LIMIT_V7X)


def _any():
    return pl.BlockSpec(memory_space=pl.ANY)


def _tile(n, target):
    if n <= target:
        return n
    t = target - target % 16
    while t >= 16:
        if n % t == 0:
            return t
        t -= 16
    return n


def _mm(name, grid, a, b, o, dims, scale=1.0, resid=None, after=None):
    nk = grid[-1]
    acc_shape = tuple(d for d in o[1] if d is not None)
    has_resid = resid is not None
    n_in = 2 + has_resid + (after is not None)

    def body(*refs):
        a_ref, b_ref = refs[0], refs[1]
        r_ref = refs[2] if has_resid else None
        o_ref = refs[n_in]

        def finish(acc):
            r = acc * scale if scale != 1.0 else acc
            if has_resid:
                r = r + r_ref[...]
            o_ref[...] = r.astype(o_ref.dtype)

        part = lax.dot_general(a_ref[...].astype(BF), b_ref[...].astype(BF), (dims, ((), ())),
                               preferred_element_type=F32)
        if nk == 1:
            finish(part)
        else:
            acc_ref = refs[-1]
            k = pl.program_id(len(grid) - 1)

            @pl.when(k == 0)
            def _():
                acc_ref[...] = part

            @pl.when(k > 0)
            def _():
                acc_ref[...] += part

            @pl.when(k == nk - 1)
            def _():
                finish(acc_ref[...])

    ins = [a, b] + ([resid] if has_resid else [])
    return pl.pallas_call(
        body, name=name, grid=grid,
        in_specs=[pl.BlockSpec(blk, idx) for (_, blk, idx) in ins] + ([_any()] if after is not None else []),
        out_specs=pl.BlockSpec(o[1], o[2]),
        out_shape=o[0],
        scratch_shapes=[pltpu.VMEM(acc_shape, F32)] if nk > 1 else [],
        compiler_params=_params(*(["parallel"] * (len(grid) - 1) + ["arbitrary"])),
    )(*[x[0] for x in ins], *([after] if after is not None else []))


NN = ((1,), (0,))
NT = ((1,), (1,))
TN = ((0,), (0,))


def _rmsnorm(x, gain):
    T, D = x.shape
    tr = _tile(T, 256)

    def body(x_ref, g_ref, o_ref):
        xv = x_ref[...]
        r = lax.rsqrt(jnp.mean(xv * xv, axis=-1, keepdims=True) + EPS)
        o_ref[...] = (xv * r * g_ref[...]).astype(BF)

    return pl.pallas_call(
        body, name="rmsnorm_fwd", grid=(T // tr,),
        in_specs=[pl.BlockSpec((tr, D), lambda i: (i, 0)), pl.BlockSpec((1, D), lambda i: (0, 0))],
        out_specs=pl.BlockSpec((tr, D), lambda i: (i, 0)),
        out_shape=jax.ShapeDtypeStruct((T, D), BF),
        compiler_params=_params("parallel"),
    )(x, gain)


def _rmsnorm_bwd(dh, x, gain, dres):
    T, D = x.shape
    tr = _tile(T, 256)

    def body(dh_ref, x_ref, g_ref, dres_ref, dx_ref, dg_ref):
        i = pl.program_id(0)
        xv = x_ref[...]
        r = lax.rsqrt(jnp.mean(xv * xv, axis=-1, keepdims=True) + EPS)
        xh = xv * r
        dy = dh_ref[...].astype(F32)
        dgp = jnp.sum(dy * xh, axis=0, keepdims=True)

        @pl.when(i == 0)
        def _():
            dg_ref[...] = dgp

        @pl.when(i > 0)
        def _():
            dg_ref[...] += dgp

        dxh = dy * g_ref[...]
        dx_ref[...] = dres_ref[...] + r * (dxh - xh * jnp.mean(dxh * xh, axis=-1, keepdims=True))

    return pl.pallas_call(
        body, name="rmsnorm_bwd", grid=(T // tr,),
        in_specs=[pl.BlockSpec((tr, D), lambda i: (i, 0)), pl.BlockSpec((tr, D), lambda i: (i, 0)),
                  pl.BlockSpec((1, D), lambda i: (0, 0)), pl.BlockSpec((tr, D), lambda i: (i, 0))],
        out_specs=[pl.BlockSpec((tr, D), lambda i: (i, 0)), pl.BlockSpec((1, D), lambda i: (0, 0))],
        out_shape=[jax.ShapeDtypeStruct((T, D), F32), jax.ShapeDtypeStruct((1, D), F32)],
        compiler_params=_params("arbitrary"),
    )(dh, x, gain, dres)


def _loss_head(x, gain, target):
    T, D = x.shape
    tr = _tile(T, 256)

    def body(x_ref, g_ref, t_ref, loss_ref, dx_ref, dg_ref):
        i = pl.program_id(0)
        xv = x_ref[...]
        g = g_ref[...]
        r = lax.rsqrt(jnp.mean(xv * xv, axis=-1, keepdims=True) + EPS)
        xh = xv * r
        err = xh * g - t_ref[...]
        lp = 0.5 * jnp.sum(jnp.mean(err * err, axis=-1, keepdims=True), axis=0, keepdims=True)
        dy = err * (1.0 / D)
        dgp = jnp.sum(dy * xh, axis=0, keepdims=True)

        @pl.when(i == 0)
        def _():
            loss_ref[...] = jnp.broadcast_to(lp, loss_ref.shape)
            dg_ref[...] = dgp

        @pl.when(i > 0)
        def _():
            loss_ref[...] += jnp.broadcast_to(lp, loss_ref.shape)
            dg_ref[...] += dgp

        dxh = dy * g
        dx_ref[...] = r * (dxh - xh * jnp.mean(dxh * xh, axis=-1, keepdims=True))

    return pl.pallas_call(
        body, name="loss_head", grid=(T // tr,),
        in_specs=[pl.BlockSpec((tr, D), lambda i: (i, 0)), pl.BlockSpec((1, D), lambda i: (0, 0)),
                  pl.BlockSpec((tr, D), lambda i: (i, 0))],
        out_specs=[pl.BlockSpec((1, LANES), lambda i: (0, 0)), pl.BlockSpec((tr, D), lambda i: (i, 0)),
                   pl.BlockSpec((1, D), lambda i: (0, 0))],
        out_shape=[jax.ShapeDtypeStruct((1, LANES), F32), jax.ShapeDtypeStruct((T, D), F32),
                   jax.ShapeDtypeStruct((1, D), F32)],
        compiler_params=_params("arbitrary"),
    )(x, gain, target)


def _group_slices(D):
    dc, da = D // 4, D // 2
    return [(0, dc), (dc, dc + da), (dc + da, D)]


def _groupnorm(yc, ya, yl, gain):
    T = yc.shape[0]
    D = yc.shape[1] + ya.shape[1] + yl.shape[1]
    tr = _tile(T, 256)
    sl = _group_slices(D)

    def body(yc_ref, ya_ref, yl_ref, g_ref, o_ref):
        for y_ref, (lo, hi) in zip((yc_ref, ya_ref, yl_ref), sl):
            y = y_ref[...]
            r = lax.rsqrt(jnp.mean(y * y, axis=-1, keepdims=True) + EPS)
            o_ref[:, lo:hi] = (y * r * g_ref[:, lo:hi]).astype(BF)

    return pl.pallas_call(
        body, name="groupnorm_fwd", grid=(T // tr,),
        in_specs=[pl.BlockSpec((tr, y.shape[1]), lambda i: (i, 0)) for y in (yc, ya, yl)]
        + [pl.BlockSpec((1, D), lambda i: (0, 0))],
        out_specs=pl.BlockSpec((tr, D), lambda i: (i, 0)),
        out_shape=jax.ShapeDtypeStruct((T, D), BF),
        compiler_params=_params("parallel"),
    )(yc, ya, yl, gain)


def _groupnorm_bwd(dyn, yc, ya, yl, gain):
    T, D = dyn.shape
    tr = _tile(T, 256)
    sl = _group_slices(D)

    def body(dyn_ref, yc_ref, ya_ref, yl_ref, g_ref, dc_ref, da_ref, dl_ref, dg_ref):
        i = pl.program_id(0)
        for y_ref, d_ref, (lo, hi) in zip((yc_ref, ya_ref, yl_ref), (dc_ref, da_ref, dl_ref), sl):
            y = y_ref[...]
            r = lax.rsqrt(jnp.mean(y * y, axis=-1, keepdims=True) + EPS)
            yh = y * r
            dy = dyn_ref[:, lo:hi]
            dgp = jnp.sum(dy * yh, axis=0, keepdims=True)

            @pl.when(i == 0)
            def _():
                dg_ref[:, lo:hi] = dgp

            @pl.when(i > 0)
            def _():
                dg_ref[:, lo:hi] += dgp

            dyh = dy * g_ref[:, lo:hi]
            d_ref[...] = r * (dyh - yh * jnp.mean(dyh * yh, axis=-1, keepdims=True))

    return pl.pallas_call(
        body, name="groupnorm_bwd", grid=(T // tr,),
        in_specs=[pl.BlockSpec((tr, D), lambda i: (i, 0))]
        + [pl.BlockSpec((tr, y.shape[1]), lambda i: (i, 0)) for y in (yc, ya, yl)]
        + [pl.BlockSpec((1, D), lambda i: (0, 0))],
        out_specs=[pl.BlockSpec((tr, y.shape[1]), lambda i: (i, 0)) for y in (yc, ya, yl)]
        + [pl.BlockSpec((1, D), lambda i: (0, 0))],
        out_shape=[jax.ShapeDtypeStruct(y.shape, F32) for y in (yc, ya, yl)] + [jax.ShapeDtypeStruct((1, D), F32)],
        compiler_params=_params("arbitrary"),
    )(dyn, yc, ya, yl, gain)


def _col_chunks(n, width=512):
    return [slice(c, min(c + width, n)) for c in range(0, n, width)]


def _ffn_in(h, w3):
    T, D = h.shape
    tn = w3.shape[2]
    F = 4 * tn
    tm = _tile(T, 512)

    def body(h_ref, wg_ref, wu_ref, g_ref, u_ref, a_ref):
        hv = h_ref[...]
        for cols in _col_chunks(tn):
            g = jnp.dot(hv, wg_ref[:, cols], preferred_element_type=F32)
            u = jnp.dot(hv, wu_ref[:, cols], preferred_element_type=F32)
            g_ref[:, cols] = g.astype(BF)
            u_ref[:, cols] = u.astype(BF)
            a_ref[:, cols] = (g * jax.nn.sigmoid(g) * u).astype(BF)

    out = jax.ShapeDtypeStruct((T, F), BF)
    return pl.pallas_call(
        body, name="ffn_in_swiglu", grid=(4, T // tm),
        in_specs=[pl.BlockSpec((tm, D), lambda j, i: (i, 0)),
                  pl.BlockSpec((None, D, tn), lambda j, i: (j, 0, 0)),
                  pl.BlockSpec((None, D, tn), lambda j, i: (j + 4, 0, 0))],
        out_specs=[pl.BlockSpec((tm, tn), lambda j, i: (i, j))] * 3,
        out_shape=[out, out, out],
        compiler_params=_params("parallel", "parallel"),
    )(h, w3, w3)


def _ffn_bwd_in(dx, wout, g, u):
    T, D = dx.shape
    F = wout.shape[0]
    tn = F // 4
    tm = _tile(T, 512)

    def body(dx_ref, w_ref, g_ref, u_ref, o_ref):
        dxb = dx_ref[...].astype(BF)
        for cols in _col_chunks(tn):
            da = 0.5 * lax.dot_general(dxb, w_ref[cols, :], (NT, ((), ())), preferred_element_type=F32)
            gv = g_ref[:, cols].astype(F32)
            s = jax.nn.sigmoid(gv)
            o_ref[0, :, cols] = (da * u_ref[:, cols].astype(F32) * (s * (1.0 + gv * (1.0 - s)))).astype(BF)
            o_ref[1, :, cols] = (da * gv * s).astype(BF)

    return pl.pallas_call(
        body, name="ffn_bwd_swiglu", grid=(4, T // tm),
        in_specs=[pl.BlockSpec((tm, D), lambda j, i: (i, 0)), pl.BlockSpec((tn, D), lambda j, i: (j, 0)),
                  pl.BlockSpec((tm, tn), lambda j, i: (i, j)), pl.BlockSpec((tm, tn), lambda j, i: (i, j))],
        out_specs=pl.BlockSpec((2, tm, tn), lambda j, i: (0, i, j)),
        out_shape=jax.ShapeDtypeStruct((2, T, F), BF),
        compiler_params=_params("parallel", "parallel"),
    )(dx, wout, g, u)


def _ffn_forward(x, gain, w3, wout):
    T, D = x.shape
    F = wout.shape[0]
    h = _rmsnorm(x, gain)
    g, u, a = _ffn_in(h, w3)
    tm, tk = _tile(T, 512), F // 4
    xn = _mm("ffn_out", (T // tm, 1, F // tk),
             (a, (tm, tk), lambda i, j, k: (i, k)), (wout, (tk, D), lambda i, j, k: (k, 0)),
             (jax.ShapeDtypeStruct((T, D), F32), (tm, D), lambda i, j, k: (i, 0)), NN, scale=0.5,
             resid=(x, (tm, D), lambda i, j, k: (i, 0)))
    return xn, (x, h, g, u, a)


def _ffn_backward(dx, saved, w3, wout, between=None):
    x, h, g, u, a = saved
    T, D = x.shape
    F = wout.shape[0]
    tn3 = F // 4
    dgu = _ffn_bwd_in(dx, wout, g, u)
    behind = None if between is None else between(dgu)
    tnd = _tile(D, 512)
    dwout = _mm("ffn_dwout", (F // tn3, D // tnd, 1),
                (a, (T, tn3), lambda i, j, k: (0, i)), (dx, (T, tnd), lambda i, j, k: (0, j)),
                (jax.ShapeDtypeStruct((F, D), BF), (tn3, tnd), lambda i, j, k: (i, j)), TN, scale=0.5, after=behind)
    dw3 = _mm("ffn_dwin", (8, D // tnd, 1),
              (h, (T, tnd), lambda s, i, k: (0, i)), (dgu, (None, T, tn3), lambda s, i, k: (s // 4, 0, s % 4)),
              (jax.ShapeDtypeStruct((8, D, tn3), BF), (None, tnd, tn3), lambda s, i, k: (s, i, 0)), TN, after=behind)
    tm = _tile(T, 1024)
    dh = _mm("ffn_dh", (T // tm, 1, 8),
             (dgu, (None, tm, tn3), lambda i, j, k: (k // 4, i, k % 4)), (w3, (None, D, tn3), lambda i, j, k: (k, 0, 0)),
             (jax.ShapeDtypeStruct((T, D), F32), (tm, D), lambda i, j, k: (i, 0)), NT, after=behind)
    return dh, dw3, dwout.reshape(N_DEV, F // N_DEV, D)


def _rows(shape):
    return lax.broadcasted_iota(jnp.int32, shape, 0)


def _down(x, s, fill, rows):
    return jnp.where(rows >= s, pltpu.roll(x, s, 0), fill)


def _up(x, s, fill, rows):
    T = x.shape[0]
    return jnp.where(rows < T - s, pltpu.roll(x, T - s, 0), fill)


def _scan_linear(a, b, rows, shift):
    T = a.shape[0]
    s = 1
    while s < T:
        b = a * shift(b, s, 0.0, rows) + b
        if 2 * s < T:
            a = a * shift(a, s, 1.0, rows)
        s *= 2
    return b


def _cumsum(c, rows, shift):
    T = c.shape[0]
    s = 1
    while s < T:
        c = c + shift(c, s, 0.0, rows)
        s *= 2
    return c


def _log1p_small(e):
    return jnp.where(e < 0.01, e * (1.0 - e * (0.5 - e * (1.0 / 3.0))), jnp.log(1.0 + e))


def _softplus(x):
    return jnp.maximum(x, 0.0) + _log1p_small(jnp.exp(-jnp.abs(x)))


def _one_minus_exp_neg(z):
    return jnp.where(z < 0.1, z * (1.0 - z * (0.5 - z * (1.0 / 6.0 - z * (1.0 / 24.0)))), 1.0 - jnp.exp(-z))


def _fgate_cum(f, b):
    T = f.shape[0]

    def body(f_ref, b_ref, o_ref):
        z = f_ref[...] + b_ref[...]
        o_ref[...] = _cumsum(-_softplus(-z), _rows(z.shape), _down)

    return pl.pallas_call(
        body, name="fgate_cumsum",
        out_shape=jax.ShapeDtypeStruct((T, LANES), F32),
        compiler_params=pltpu.CompilerParams(vmem_limit_bytes=VMEM_LIMIT_V7X),
    )(f, b)


def _fgate_cum_bwd(drow, dcol, f, b):
    T = f.shape[0]

    def body(dr_ref, dc_ref, f_ref, b_ref, df_ref, db_ref):
        z = f_ref[...] + b_ref[...]
        dlogf = _cumsum(dr_ref[...] - dc_ref[...], _rows(z.shape), _up)
        dz = dlogf * jax.nn.sigmoid(-z)
        df_ref[...] = dz.astype(BF)
        db_ref[...] = jnp.sum(dz, axis=0, keepdims=True)

    return pl.pallas_call(
        body, name="fgate_cumsum_bwd",
        out_shape=[jax.ShapeDtypeStruct((T, LANES), BF), jax.ShapeDtypeStruct((1, LANES), F32)],
        compiler_params=pltpu.CompilerParams(vmem_limit_bytes=VMEM_LIMIT_V7X),
    )(drow, dcol, f, b)


def _col(blk0):
    return lambda g: (0, blk0 + g)


def _conv_fwd(p, w, nb):
    T = p.shape[0]

    def body(b_ref, c_ref, v_ref, w_ref, o_ref):
        z = c_ref[...].astype(F32) * v_ref[...].astype(F32)
        rows = _rows(z.shape)
        conv = w_ref[2:3, :] * z + w_ref[1:2, :] * _down(z, 1, 0.0, rows) + w_ref[0:1, :] * _down(z, 2, 0.0, rows)
        o_ref[...] = b_ref[...].astype(F32) * conv

    return pl.pallas_call(
        body, name="conv_fwd", grid=(nb,),
        in_specs=[pl.BlockSpec((T, LANES), _col(0)), pl.BlockSpec((T, LANES), _col(nb)),
                  pl.BlockSpec((T, LANES), _col(2 * nb)), pl.BlockSpec((3, LANES), lambda g: (0, g))],
        out_specs=pl.BlockSpec((T, LANES), lambda g: (0, g)),
        out_shape=jax.ShapeDtypeStruct((T, nb * LANES), F32),
        compiler_params=_params("parallel"),
    )(p, p, p, w)


def _conv_bwd(dy, p, w, nb):
    T = p.shape[0]

    def body(dy_ref, b_ref, c_ref, v_ref, w_ref, db_ref, dc_ref, dv_ref, dw_ref):
        cv, vv = c_ref[...].astype(F32), v_ref[...].astype(F32)
        z = cv * vv
        rows = _rows(z.shape)
        z1, z2 = _down(z, 1, 0.0, rows), _down(z, 2, 0.0, rows)
        dyv = dy_ref[...]
        db_ref[...] = (dyv * (w_ref[2:3, :] * z + w_ref[1:2, :] * z1 + w_ref[0:1, :] * z2)).astype(BF)
        dconv = dyv * b_ref[...].astype(F32)
        dz = (w_ref[2:3, :] * dconv + w_ref[1:2, :] * _up(dconv, 1, 0.0, rows)
              + w_ref[0:1, :] * _up(dconv, 2, 0.0, rows))
        dc_ref[...] = (dz * vv).astype(BF)
        dv_ref[...] = (dz * cv).astype(BF)
        dw_ref[0:1, :] = jnp.sum(dconv * z2, axis=0, keepdims=True)
        dw_ref[1:2, :] = jnp.sum(dconv * z1, axis=0, keepdims=True)
        dw_ref[2:3, :] = jnp.sum(dconv * z, axis=0, keepdims=True)

    return pl.pallas_call(
        body, name="conv_bwd", grid=(nb,),
        in_specs=[pl.BlockSpec((T, LANES), lambda g: (0, g)), pl.BlockSpec((T, LANES), _col(0)),
                  pl.BlockSpec((T, LANES), _col(nb)), pl.BlockSpec((T, LANES), _col(2 * nb)),
                  pl.BlockSpec((3, LANES), lambda g: (0, g))],
        out_specs=[pl.BlockSpec((T, LANES), lambda g: (0, g))] * 3 + [pl.BlockSpec((3, LANES), lambda g: (0, g))],
        out_shape=[jax.ShapeDtypeStruct((T, nb * LANES), BF)] * 3 + [jax.ShapeDtypeStruct((3, nb * LANES), F32)],
        compiler_params=_params("parallel"),
    )(dy, p, p, p, w)


def _gelu(x):
    t = jnp.tanh(GELU_C * (x + GELU_K * x * x * x))
    return 0.5 * x * (1.0 + t), t


def _lru_common(x, cw_ref, cb_ref, wa_ref, ba_ref, wx_ref, bx_ref, lam_ref, rows):
    xr = (cb_ref[...] + cw_ref[3:4, :] * x + cw_ref[2:3, :] * _down(x, 1, 0.0, rows)
          + cw_ref[1:2, :] * _down(x, 2, 0.0, rows) + cw_ref[0:1, :] * _down(x, 3, 0.0, rows))
    xrb = xr.astype(BF)
    r = jax.nn.sigmoid(jnp.dot(xrb, wa_ref[...].astype(BF), preferred_element_type=F32) + ba_ref[...])
    i = jax.nn.sigmoid(jnp.dot(xrb, wx_ref[...].astype(BF), preferred_element_type=F32) + bx_ref[...])
    sp = _softplus(-lam_ref[...])
    log_a = -LRU_C * r * sp
    a = jnp.exp(log_a)
    m = jnp.sqrt(_one_minus_exp_neg(-2.0 * log_a))
    return xr, xrb, r, i, sp, a, m


def _lru_specs(T, nb, gate_blk0, x_blk0):
    vec = pl.BlockSpec((1, LANES), lambda g: (0, g))
    mat = pl.BlockSpec((None, LANES, LANES), lambda g: (g, 0, 0))
    return [pl.BlockSpec((T, LANES), _col(gate_blk0)), pl.BlockSpec((T, LANES), _col(x_blk0)),
            pl.BlockSpec((4, LANES), lambda g: (0, g)), vec, mat, vec, mat, vec, vec]


def _lru_fwd(p, cw, cb, wa, ba, wx, bx, lam, nb, gate_blk0):
    T = p.shape[0]

    def body(gate_ref, x_ref, cw_ref, cb_ref, wa_ref, ba_ref, wx_ref, bx_ref, lam_ref, y_ref, h_ref):
        x = x_ref[...].astype(F32)
        rows = _rows(x.shape)
        xr, _, _, i, _, a, m = _lru_common(x, cw_ref, cb_ref, wa_ref, ba_ref, wx_ref, bx_ref, lam_ref, rows)
        h = _scan_linear(a, m * (i * xr), rows, _down)
        h_ref[...] = h
        y_ref[...] = _gelu(gate_ref[...].astype(F32))[0] * h

    out = jax.ShapeDtypeStruct((T, nb * LANES), F32)
    return pl.pallas_call(
        body, name="lru_fwd", grid=(nb,),
        in_specs=_lru_specs(T, nb, gate_blk0, gate_blk0 + nb),
        out_specs=[pl.BlockSpec((T, LANES), lambda g: (0, g))] * 2,
        out_shape=[out, out],
        compiler_params=_params("parallel"),
    )(p, p, cw, cb, wa, ba, wx, bx, lam)


def _lru_bwd(dy, hs, p, cw, cb, wa, ba, wx, bx, lam, nb, gate_blk0):
    T = p.shape[0]

    def body(dy_ref, hs_ref, gate_ref, x_ref, cw_ref, cb_ref, wa_ref, ba_ref, wx_ref, bx_ref, lam_ref,
             dgate_ref, dx_ref, dcw_ref, dcb_ref, dwa_ref, dba_ref, dwx_ref, dbx_ref, dlam_ref):
        x = x_ref[...].astype(F32)
        rows = _rows(x.shape)
        xr, xrb, r, i, sp, a, m = _lru_common(x, cw_ref, cb_ref, wa_ref, ba_ref, wx_ref, bx_ref, lam_ref, rows)
        gate = gate_ref[...].astype(F32)
        gl, t = _gelu(gate)
        h = hs_ref[...]
        dyv = dy_ref[...]
        dgelu = 0.5 * (1.0 + t) + 0.5 * gate * (1.0 - t * t) * GELU_C * (1.0 + 3.0 * GELU_K * gate * gate)
        dgate_ref[...] = (dyv * h * dgelu).astype(BF)
        lam_adj = _scan_linear(_up(a, 1, 0.0, rows), dyv * gl, rows, _up)
        da = lam_adj * _down(h, 1, 0.0, rows)
        ix = i * xr
        dix = lam_adj * m
        dm = lam_adj * ix
        dlog_a = da * a - dm * (a * a) / jnp.maximum(m, 1e-30)
        dr = dlog_a * (-LRU_C * sp)
        dsp = jnp.sum(dlog_a * (-LRU_C * r), axis=0, keepdims=True)
        dlam_ref[...] = -dsp * jax.nn.sigmoid(-lam_ref[...])
        dpa = dr * r * (1.0 - r)
        dpx = dix * xr * i * (1.0 - i)
        dpab, dpxb = dpa.astype(BF), dpx.astype(BF)
        dxr = (dix * i
               + lax.dot_general(dpab, wa_ref[...].astype(BF), (NT, ((), ())), preferred_element_type=F32)
               + lax.dot_general(dpxb, wx_ref[...].astype(BF), (NT, ((), ())), preferred_element_type=F32))
        dwa_ref[...] = lax.dot_general(xrb, dpab, (TN, ((), ())), preferred_element_type=F32)
        dwx_ref[...] = lax.dot_general(xrb, dpxb, (TN, ((), ())), preferred_element_type=F32)
        dba_ref[...] = jnp.sum(dpa, axis=0, keepdims=True)
        dbx_ref[...] = jnp.sum(dpx, axis=0, keepdims=True)
        dcb_ref[...] = jnp.sum(dxr, axis=0, keepdims=True)
        dx_ref[...] = (cw_ref[3:4, :] * dxr + cw_ref[2:3, :] * _up(dxr, 1, 0.0, rows)
                       + cw_ref[1:2, :] * _up(dxr, 2, 0.0, rows) + cw_ref[0:1, :] * _up(dxr, 3, 0.0, rows)).astype(BF)
        for k in range(4):
            xs = x if k == 3 else _down(x, 3 - k, 0.0, rows)
            dcw_ref[k:k + 1, :] = jnp.sum(dxr * xs, axis=0, keepdims=True)

    C = nb * LANES
    seq = jax.ShapeDtypeStruct((T, C), BF)
    vec = jax.ShapeDtypeStruct((1, C), F32)
    mat = jax.ShapeDtypeStruct((nb, LANES, LANES), F32)
    vspec = pl.BlockSpec((1, LANES), lambda g: (0, g))
    mspec = pl.BlockSpec((None, LANES, LANES), lambda g: (g, 0, 0))
    sspec = pl.BlockSpec((T, LANES), lambda g: (0, g))
    return pl.pallas_call(
        body, name="lru_bwd", grid=(nb,),
        in_specs=[sspec, sspec] + _lru_specs(T, nb, gate_blk0, gate_blk0 + nb),
        out_specs=[sspec, sspec, pl.BlockSpec((4, LANES), lambda g: (0, g)), vspec, mspec, vspec, mspec, vspec, vspec],
        out_shape=[seq, seq, jax.ShapeDtypeStruct((4, C), F32), vec, mat, vec, mat, vec, vec],
        compiler_params=_params("parallel"),
    )(dy, hs, p, p, cw, cb, wa, ba, wx, bx, lam)


ATTN_ROWS = 128


def _causal(shape, row0, transposed=False):
    r = row0 + lax.broadcasted_iota(jnp.int32, shape, 0)
    c = lax.broadcasted_iota(jnp.int32, shape, 1)
    return r <= c if transposed else c <= r


def _row_chunks(n):
    step = min(ATTN_ROWS, n)
    return [(r, slice(r, r + step)) for r in range(0, n, step)]


def _causal_pairs(n, by_query):
    if by_query:
        pairs = [(i, j) for i in range(n) for j in range(i + 1)]
    else:
        pairs = [(i, j) for j in range(n) for i in range(j, n)]
    return jnp.asarray([a for a, _ in pairs], jnp.int32), jnp.asarray([b for _, b in pairs], jnp.int32)


def _attn_fwd(p, cq, ck, nh, q_blk0):
    T = p.shape[0]
    tq = _tile(T, 512)
    nq = T // tq
    scale = LANES ** -0.5

    qi, kj = _causal_pairs(nq, by_query=True)

    def body(qi_ref, kj_ref, q_ref, k_ref, v_ref, cq_ref, ck_ref, o_ref, lse_ref, m_ref, l_ref, acc_ref):
        i, j = qi_ref[pl.program_id(1)], kj_ref[pl.program_id(1)]

        @pl.when(j == 0)
        def _():
            m_ref[...] = jnp.full(m_ref.shape, -jnp.inf, F32)
            l_ref[...] = jnp.zeros(l_ref.shape, F32)
            acc_ref[...] = jnp.zeros(acc_ref.shape, F32)

        def block(diagonal):
            for r0, rows in _row_chunks(tq):
                s = lax.dot_general(q_ref[rows, :], k_ref[...], (NT, ((), ())), preferred_element_type=F32) * scale
                s = s + cq_ref[rows, :] - ck_ref[...]
                if diagonal:
                    s = jnp.where(_causal(s.shape, r0), s, -jnp.inf)
                m_old = m_ref[rows, :]
                m_new = jnp.maximum(m_old, jnp.max(s, axis=-1, keepdims=True))
                alpha = jnp.exp(m_old - m_new)
                pr = jnp.exp(s - m_new)
                l_ref[rows, :] = alpha * l_ref[rows, :] + jnp.sum(pr, axis=-1, keepdims=True)
                acc_ref[rows, :] = alpha * acc_ref[rows, :] + jnp.dot(pr.astype(BF), v_ref[...],
                                                                      preferred_element_type=F32)
                m_ref[rows, :] = m_new

        pl.when(j < i)(lambda: block(False))

        @pl.when(j == i)
        def _():
            block(True)
            o_ref[...] = acc_ref[...] / l_ref[...]
            lse_ref[...] = m_ref[...] + jnp.log(l_ref[...])

    def kv(off):
        return pl.BlockSpec((tq, LANES), lambda h, s, qi, kj: (kj[s], q_blk0 + off * nh + h))

    return pl.pallas_call(
        body, name="attn_fwd",
        grid_spec=pltpu.PrefetchScalarGridSpec(
            num_scalar_prefetch=2, grid=(nh, qi.shape[0]),
            in_specs=[pl.BlockSpec((tq, LANES), lambda h, s, qi, kj: (qi[s], q_blk0 + h)), kv(1), kv(2),
                      pl.BlockSpec((None, tq, 1), lambda h, s, qi, kj: (h, qi[s], 0)),
                      pl.BlockSpec((None, 1, tq), lambda h, s, qi, kj: (h, 0, kj[s]))],
            out_specs=[pl.BlockSpec((tq, LANES), lambda h, s, qi, kj: (qi[s], h)),
                       pl.BlockSpec((None, tq, 1), lambda h, s, qi, kj: (h, qi[s], 0))],
            scratch_shapes=[pltpu.VMEM((tq, 1), F32), pltpu.VMEM((tq, 1), F32), pltpu.VMEM((tq, LANES), F32)]),
        out_shape=[jax.ShapeDtypeStruct((T, nh * LANES), F32), jax.ShapeDtypeStruct((nh, T, 1), F32)],
        compiler_params=_params("parallel", "arbitrary"),
    )(qi, kj, p, p, p, cq, ck)


def _attn_bwd_q(p, cq, ck, lse, do, o, nh, q_blk0):
    T = p.shape[0]
    tq = _tile(T, 512)
    nq = T // tq
    scale = LANES ** -0.5

    qi, kj = _causal_pairs(nq, by_query=True)

    def body(qi_ref, kj_ref, q_ref, k_ref, v_ref, cq_ref, ck_ref, lse_ref, do_ref, o_ref, dq_ref, dl_ref, dr_ref,
             acc_ref):
        i, j = qi_ref[pl.program_id(1)], kj_ref[pl.program_id(1)]

        @pl.when(j == 0)
        def _():
            dl_ref[...] = jnp.sum(do_ref[...] * o_ref[...], axis=-1, keepdims=True)
            dr_ref[...] = jnp.zeros(dr_ref.shape, F32)
            acc_ref[...] = jnp.zeros(acc_ref.shape, F32)

        def block(diagonal):
            for r0, rows in _row_chunks(tq):
                s = lax.dot_general(q_ref[rows, :], k_ref[...], (NT, ((), ())), preferred_element_type=F32) * scale
                s = s + cq_ref[rows, :] - ck_ref[...]
                pr = jnp.exp(s - lse_ref[rows, :])
                if diagonal:
                    pr = jnp.where(_causal(s.shape, r0), pr, 0.0)
                dp = lax.dot_general(do_ref[rows, :].astype(BF), v_ref[...], (NT, ((), ())),
                                     preferred_element_type=F32)
                ds = pr * (dp - dl_ref[rows, :])
                dr_ref[rows, :] += jnp.sum(ds, axis=-1, keepdims=True)
                acc_ref[rows, :] += jnp.dot(ds.astype(BF), k_ref[...], preferred_element_type=F32)

        pl.when(j < i)(lambda: block(False))

        @pl.when(j == i)
        def _():
            block(True)
            dq_ref[...] = (acc_ref[...] * scale).astype(BF)

    def kv(off):
        return pl.BlockSpec((tq, LANES), lambda h, s, qi, kj: (kj[s], q_blk0 + off * nh + h))

    col = pl.BlockSpec((None, tq, 1), lambda h, s, qi, kj: (h, qi[s], 0))
    head = pl.BlockSpec((tq, LANES), lambda h, s, qi, kj: (qi[s], h))
    return pl.pallas_call(
        body, name="attn_bwd_q",
        grid_spec=pltpu.PrefetchScalarGridSpec(
            num_scalar_prefetch=2, grid=(nh, qi.shape[0]),
            in_specs=[pl.BlockSpec((tq, LANES), lambda h, s, qi, kj: (qi[s], q_blk0 + h)), kv(1), kv(2), col,
                      pl.BlockSpec((None, 1, tq), lambda h, s, qi, kj: (h, 0, kj[s])), col, head, head],
            out_specs=[head, col, col],
            scratch_shapes=[pltpu.VMEM((tq, LANES), F32)]),
        out_shape=[jax.ShapeDtypeStruct((T, nh * LANES), BF), jax.ShapeDtypeStruct((nh, T, 1), F32),
                   jax.ShapeDtypeStruct((nh, T, 1), F32)],
        compiler_params=_params("parallel", "arbitrary"),
    )(qi, kj, p, p, p, cq, ck, lse, do, o)


def _attn_bwd_kv(p, cq_row, ck_col, lse_row, delta_row, do, nh, q_blk0):
    T = p.shape[0]
    tk = _tile(T, 512)
    nk = T // tk
    scale = LANES ** -0.5

    qi, kj = _causal_pairs(nk, by_query=False)

    def body(qi_ref, kj_ref, q_ref, k_ref, v_ref, cq_ref, ck_ref, lse_ref, dl_ref, do_ref, dk_ref, dv_ref, dc_ref,
             dk_acc, dv_acc, dc_acc):
        i, j = qi_ref[pl.program_id(1)], kj_ref[pl.program_id(1)]

        @pl.when(i == j)
        def _():
            dk_acc[...] = jnp.zeros(dk_acc.shape, F32)
            dv_acc[...] = jnp.zeros(dv_acc.shape, F32)
            dc_acc[...] = jnp.zeros(dc_acc.shape, F32)

        def block(diagonal):
            dob = do_ref[...].astype(BF)
            for r0, rows in _row_chunks(tk):
                st = lax.dot_general(k_ref[rows, :], q_ref[...], (NT, ((), ())), preferred_element_type=F32) * scale
                st = st + cq_ref[...] - ck_ref[rows, :]
                pt = jnp.exp(st - lse_ref[...])
                if diagonal:
                    pt = jnp.where(_causal(st.shape, r0, transposed=True), pt, 0.0)
                dv_acc[rows, :] += jnp.dot(pt.astype(BF), dob, preferred_element_type=F32)
                dpt = lax.dot_general(v_ref[rows, :], dob, (NT, ((), ())), preferred_element_type=F32)
                dst = pt * (dpt - dl_ref[...])
                dk_acc[rows, :] += jnp.dot(dst.astype(BF), q_ref[...], preferred_element_type=F32)
                dc_acc[rows, :] += jnp.sum(dst, axis=-1, keepdims=True)

        pl.when(i == j)(lambda: block(True))
        pl.when(i > j)(lambda: block(False))

        @pl.when(i == nk - 1)
        def _():
            dk_ref[...] = (dk_acc[...] * scale).astype(BF)
            dv_ref[...] = dv_acc[...].astype(BF)
            dc_ref[...] = dc_acc[...]

    def qside(blk):
        return pl.BlockSpec((tk, LANES), lambda h, s, qi, kj: (qi[s], blk + h))

    def kside(off):
        return pl.BlockSpec((tk, LANES), lambda h, s, qi, kj: (kj[s], q_blk0 + off * nh + h))

    row = pl.BlockSpec((None, 1, tk), lambda h, s, qi, kj: (h, 0, qi[s]))
    col = pl.BlockSpec((None, tk, 1), lambda h, s, qi, kj: (h, kj[s], 0))
    head = pl.BlockSpec((tk, LANES), lambda h, s, qi, kj: (kj[s], h))
    return pl.pallas_call(
        body, name="attn_bwd_kv",
        grid_spec=pltpu.PrefetchScalarGridSpec(
            num_scalar_prefetch=2, grid=(nh, qi.shape[0]),
            in_specs=[qside(q_blk0), kside(1), kside(2), row, col, row, row, qside(0)],
            out_specs=[head, head, col],
            scratch_shapes=[pltpu.VMEM((tk, LANES), F32), pltpu.VMEM((tk, LANES), F32), pltpu.VMEM((tk, 1), F32)]),
        out_shape=[jax.ShapeDtypeStruct((T, nh * LANES), BF), jax.ShapeDtypeStruct((T, nh * LANES), BF),
                   jax.ShapeDtypeStruct((nh, T, 1), F32)],
        compiler_params=_params("parallel", "arbitrary"),
    )(qi, kj, p, p, p, cq_row, ck_col, lse_row, delta_row, do)


def _mixer_dims(D):
    dc, da, dl = D // 4, D // 2, D // 4
    nh = da // LANES
    n_main = 3 * dc + 3 * da + 2 * dl
    return dc, da, dl, nh, n_main


def _pad_mix_w_in(wfull):
    D = wfull.shape[0]
    dc, da, dl, nh, n_main = _mixer_dims(D)
    a = 3 * dc + 3 * da
    return jnp.concatenate([wfull[:, :a], wfull[:, a + nh:], wfull[:, a:a + nh],
                            jnp.zeros((D, LANES - nh), wfull.dtype)], axis=1)


def _unpad_mix_w_in(wp):
    D = wp.shape[0]
    dc, da, dl, nh, n_main = _mixer_dims(D)
    a = 3 * dc + 3 * da
    return jnp.concatenate([wp[:, :a], wp[:, n_main:n_main + nh], wp[:, a:n_main]], axis=1)


def _head_cols(c, nh):
    t = jnp.transpose(c[:, :nh])
    return t[:, :, None], t[:, None, :]


def _mixer_forward(x, gain, wp, wo, sp):
    T, D = x.shape
    dc, da, dl, nh, n_main = _mixer_dims(D)
    nbc, nbl = dc // LANES, dl // LANES
    h = _rmsnorm(x, gain)
    tm = _tile(T, 512)
    tn = n_main // 4
    p = _mm("mix_in", (4, T // tm, 1),
            (h, (tm, D), lambda j, i, k: (i, 0)), (wp, (D, tn), lambda j, i, k: (0, j)),
            (jax.ShapeDtypeStruct((T, n_main), BF), (tm, tn), lambda j, i, k: (i, j)), NN)
    f = _mm("mix_in_fgate", (T // tm, 1, 1),
            (h, (tm, D), lambda i, j, k: (i, 0)), (wp, (D, LANES), lambda i, j, k: (0, n_main // LANES)),
            (jax.ShapeDtypeStruct((T, LANES), F32), (tm, LANES), lambda i, j, k: (i, 0)), NN)
    cum = _fgate_cum(f, sp["fgate_b"])
    cq, ck = _head_cols(cum, nh)
    yc = _conv_fwd(p, sp["conv_w"], nbc)
    q_blk0 = 3 * nbc
    ya, lse = _attn_fwd(p, cq, ck, nh, q_blk0)
    gate_blk0 = q_blk0 + 3 * nh
    yl, hs = _lru_fwd(p, sp["lru_conv_w"], sp["lru_conv_b"], sp["lru_w_a"], sp["lru_b_a"], sp["lru_w_x"],
                      sp["lru_b_x"], sp["lru_lambda"], nbl, gate_blk0)
    yn = _groupnorm(yc, ya, yl, sp["mix_out_norm"])
    xn = _mm("mix_out", (T // tm, 1, 1),
             (yn, (tm, D), lambda i, j, k: (i, 0)), (wo, (D, D), lambda i, j, k: (0, 0)),
             (jax.ShapeDtypeStruct((T, D), F32), (tm, D), lambda i, j, k: (i, 0)), NN,
             resid=(x, (tm, D), lambda i, j, k: (i, 0)))
    return xn, (x, h, p, f, cq, ck, yc, ya, lse, yl, hs, yn)


def _mixer_backward(dx, saved, gain, wp, wo, sp):
    x, h, p, f, cq, ck, yc, ya, lse, yl, hs, yn = saved
    T, D = x.shape
    dc, da, dl, nh, n_main = _mixer_dims(D)
    nbc, nbl = dc // LANES, dl // LANES
    q_blk0 = 3 * nbc
    gate_blk0 = q_blk0 + 3 * nh
    tm = _tile(T, 512)
    tnd, tnd2 = _tile(D, 512), _tile(D, 1024)
    dyn = _mm("mix_out_dy", (T // tm, 1, 1),
              (dx, (tm, D), lambda i, j, k: (i, 0)), (wo, (D, D), lambda i, j, k: (0, 0)),
              (jax.ShapeDtypeStruct((T, D), F32), (tm, D), lambda i, j, k: (i, 0)), NT)
    dwo = _mm("mix_out_dw", (D // tnd, D // tnd2, 1),
              (yn, (T, tnd), lambda i, j, k: (0, i)), (dx, (T, tnd2), lambda i, j, k: (0, j)),
              (jax.ShapeDtypeStruct((D, D), BF), (tnd, tnd2), lambda i, j, k: (i, j)), TN)
    dyc, dya, dyl, dgn = _groupnorm_bwd(dyn, yc, ya, yl, sp["mix_out_norm"])
    dcb, dcc, dcv, dconv_w = _conv_bwd(dyc, p, sp["conv_w"], nbc)
    dq, delta, drow = _attn_bwd_q(p, cq, ck, lse, dya, ya, nh, q_blk0)
    as_row = lambda t: t.reshape(nh, 1, T)
    dk, dv, dcol = _attn_bwd_kv(p, ck, cq, as_row(lse), as_row(delta), dya, nh, q_blk0)
    lanes = lambda t: jnp.pad(jnp.transpose(t.reshape(nh, T)), ((0, 0), (0, LANES - nh)))
    df, dfb = _fgate_cum_bwd(lanes(drow), lanes(dcol), f, sp["fgate_b"])
    dgate, dlx, dcw, dcbias, dwa, dba, dwx, dbx, dlam = _lru_bwd(
        dyl, hs, p, sp["lru_conv_w"], sp["lru_conv_b"], sp["lru_w_a"], sp["lru_b_a"], sp["lru_w_x"],
        sp["lru_b_x"], sp["lru_lambda"], nbl, gate_blk0)
    dp = jnp.concatenate([dcb, dcc, dcv, dq, dk, dv, dgate, dlx, df], axis=1)
    n_pad = n_main + LANES
    tn, tkp = n_pad // 5, n_pad // 3
    dwp = _mm("mix_in_dw", (n_pad // tn, D // tnd, 1),
              (h, (T, tnd), lambda j, i, k: (0, i)), (dp, (T, tn), lambda j, i, k: (0, j)),
              (jax.ShapeDtypeStruct((D, n_pad), BF), (tnd, tn), lambda j, i, k: (i, j)), TN)
    dh = _mm("mix_in_dh", (T // tm, 1, n_pad // tkp),
             (dp, (tm, tkp), lambda i, j, k: (i, k)), (wp, (D, tkp), lambda i, j, k: (0, k)),
             (jax.ShapeDtypeStruct((T, D), F32), (tm, D), lambda i, j, k: (i, 0)), NT)
    dxn, dgain = _rmsnorm_bwd(dh, x, gain, dx)
    small = {"norm_mix": dgain, "mix_out_norm": dgn, "conv_w": dconv_w, "fgate_b": dfb[:, :nh],
             "lru_conv_w": dcw, "lru_conv_b": dcbias, "lru_w_a": dwa, "lru_b_a": dba, "lru_w_x": dwx,
             "lru_b_x": dbx, "lru_lambda": dlam}
    return dxn, dwp, dwo, small


def _position():
    x, y, c = lax.axis_index("x"), lax.axis_index("y"), lax.axis_index("c")
    return x, y, c, [(1 - x, y), (x, 1 - y), (1 - x, 1 - y)]


def _all_gather(name, xs):
    n = len(xs)

    def body(*refs):
        x_refs, o_refs = refs[:n], refs[n:2 * n]
        send_sems, recv_sems, local_sems = refs[2 * n:]
        x, y, c, chips = _position()
        me, sibling = (x, y, c), (x, y, 1 - c)

        def copy(t, k, block, to, src=None):
            dst = o_refs[t].at[4 * block[0] + 2 * block[1] + block[2]]
            return pltpu.make_async_remote_copy(
                src_ref=dst if src is None else src, dst_ref=dst, send_sem=send_sems.at[t, k],
                recv_sem=recv_sems.at[t, k], device_id=to, device_id_type=MESH)

        mine = [pltpu.make_async_copy(x_refs[t], o_refs[t].at[4 * x + 2 * y + c], local_sems.at[t]) for t in range(n)]
        for cp in mine:
            cp.start()
        first = []
        for t in range(n):
            first.append(copy(t, 0, me, sibling, src=x_refs[t]))
            first += [copy(t, 1 + j, me, (*chip, c), src=x_refs[t]) for j, chip in enumerate(chips)]
        for cp in first:
            cp.start()
        passed = []
        for j, chip in enumerate(chips):
            for t in range(n):
                copy(t, 1 + j, (*chip, c), me).wait_recv()
                passed.append(copy(t, 4 + j, (*chip, c), sibling))
                passed[-1].start()
        for t in range(n):
            copy(t, 0, sibling, me).wait_recv()
            for j, chip in enumerate(chips):
                copy(t, 4 + j, (*chip, 1 - c), me).wait_recv()
        for cp in first + passed:
            cp.wait_send()
        for cp in mine:
            cp.wait()

    return pl.pallas_call(
        body, name=name,
        in_specs=[_any()] * n, out_specs=[_any()] * n,
        out_shape=[jax.ShapeDtypeStruct((N_DEV,) + v.shape, v.dtype) for v in xs],
        scratch_shapes=[pltpu.SemaphoreType.DMA((n, 7)), pltpu.SemaphoreType.DMA((n, 7)),
                        pltpu.SemaphoreType.DMA((n,))],
    )(*xs)


def _add_sibling(g4, recv, c_idx):
    _, _, R, C = g4.shape
    tr = _tile(R, 256)

    def body(c_ref, g_ref, r_ref, o_ref):
        o_ref[...] = (g_ref[...].astype(F32) + r_ref[...].astype(F32)).astype(o_ref.dtype)

    return pl.pallas_call(
        body, name="rs_add_sibling",
        grid_spec=pltpu.PrefetchScalarGridSpec(
            num_scalar_prefetch=1, grid=(4, R // tr),
            in_specs=[pl.BlockSpec((None, None, tr, C), lambda q, i, c_ref: (q, c_ref[0], i, 0)),
                      pl.BlockSpec((None, None, tr, C), lambda q, i, c_ref: (q, 0, i, 0))],
            out_specs=pl.BlockSpec((None, tr, C), lambda q, i, c_ref: (q, i, 0))),
        out_shape=jax.ShapeDtypeStruct((4, R, C), g4.dtype),
        compiler_params=_params("parallel", "parallel"),
    )(c_idx, g4, recv)


def _hbm_spec():
    return pl.BlockSpec(memory_space=pltpu.HBM)


def _sem_spec():
    return pl.BlockSpec(memory_space=pltpu.SEMAPHORE)


def _side_effects():
    return pltpu.CompilerParams(has_side_effects=pltpu.SideEffectType.DATAFLOW_SIDE_EFFECTING)


def _in_hbm(v):
    return pltpu.with_memory_space_constraint(v, pltpu.HBM)


def _split_start(name, srcs, lands, n_copies, copies_of, after):
    n = len(srcs)
    ns = n * n_copies

    def body(*refs):
        src_refs, land_refs = refs[:n], refs[n:2 * n]
        send_sems, recv_sems = refs[2 * n + 1:2 * n + 1 + ns], refs[2 * n + 1 + ns:2 * n + 1 + 2 * ns]
        token = refs[-1]
        for t in range(n):
            for k, (s, d, to) in enumerate(copies_of(t, src_refs[t], land_refs[t])):
                pltpu.make_async_remote_copy(src_ref=s, dst_ref=d, send_sem=send_sems[t * n_copies + k],
                                             recv_sem=recv_sems[t * n_copies + k], device_id=to,
                                             device_id_type=MESH).start()
        token[...] = jnp.zeros(token.shape, token.dtype)

    thru = [pltpu.HBM(v.shape, v.dtype) for v in list(srcs) + list(lands)]
    out = pl.pallas_call(
        body, name=name,
        in_specs=[_hbm_spec()] * (2 * n) + [_any()],
        out_specs=[_sem_spec()] * (2 * ns) + [_hbm_spec()] * (2 * n) + [pl.BlockSpec(memory_space=pltpu.VMEM)],
        out_shape=[pltpu.SemaphoreType.DMA(())] * (2 * ns) + thru + [jax.ShapeDtypeStruct((8, LANES), F32)],
        input_output_aliases={i: 2 * ns + i for i in range(2 * n)},
        compiler_params=_side_effects(),
    )(*[_in_hbm(v) for v in list(srcs) + list(lands)], after)
    return out[:ns], out[ns:2 * ns], out[2 * ns:2 * ns + n], out[2 * ns + n:2 * ns + 2 * n], out[-1]


def _split_wait(name, send_sems, recv_sems, srcs, lands, n_copies, waits_of, after):
    n = len(srcs)
    ns = n * n_copies

    def body(*refs):
        src_refs, land_refs = refs[:n], refs[n:2 * n]
        send_refs, recv_refs = refs[2 * n:2 * n + ns], refs[2 * n + ns:2 * n + 2 * ns]
        x, y, c, _ = _position()
        for t in range(n):
            for k, (s, d) in enumerate(waits_of(t, src_refs[t], land_refs[t])):
                cp = pltpu.make_async_remote_copy(src_ref=s, dst_ref=d, send_sem=send_refs[t * n_copies + k],
                                                  recv_sem=recv_refs[t * n_copies + k], device_id=(x, y, 1 - c),
                                                  device_id_type=MESH)
                cp.wait_send()
                cp.wait_recv()

    out = pl.pallas_call(
        body, name=name,
        in_specs=[_hbm_spec()] * (2 * n) + [_sem_spec()] * (2 * ns) + [_any()],
        out_specs=[_hbm_spec()] * (2 * n),
        out_shape=[pltpu.HBM(v.shape, v.dtype) for v in list(srcs) + list(lands)],
        input_output_aliases={i: i for i in range(2 * n)},
        compiler_params=_side_effects(),
    )(*srcs, *lands, *send_sems, *recv_sems, after)
    return out[:n], out[n:]


def _block_of(px, py, pc):
    return 4 * px + 2 * py + pc


def _gather_phase1_start(name, xs, after):
    lands = [lax.empty((N_DEV,) + v.shape, v.dtype) for v in xs]

    def copies_of(t, x_ref, land_ref):
        x, y, c, chips = _position()
        dst = land_ref.at[_block_of(x, y, c)]
        return [(x_ref, dst, (x, y, 1 - c))] + [(x_ref, dst, (*chip, c)) for chip in chips]

    return _split_start(name, xs, lands, 4, copies_of, after)


def _gather_phase1_wait(name, started, after):
    send_sems, recv_sems, xs, lands, _ = started

    def waits_of(t, x_ref, land_ref):
        x, y, c, chips = _position()
        return [(x_ref, land_ref.at[_block_of(x, y, 1 - c)])] + [(x_ref, land_ref.at[_block_of(*chip, c)])
                                                                  for chip in chips]

    return _split_wait(name, send_sems, recv_sems, xs, lands, 4, waits_of, after)


def _gather_phase2_start(name, lands, after):
    keep = [lax.empty((8, LANES), v.dtype) for v in lands]

    def copies_of(t, _, land_ref):
        x, y, c, chips = _position()
        return [(land_ref.at[_block_of(*chip, c)], land_ref.at[_block_of(*chip, c)], (x, y, 1 - c)) for chip in chips]

    return _split_start(name, keep, lands, 3, copies_of, after)


def _gather_phase2_wait(name, started, after):
    send_sems, recv_sems, keep, lands, _ = started

    def waits_of(t, _, land_ref):
        x, y, c, chips = _position()
        return [(land_ref.at[_block_of(*chip, c)], land_ref.at[_block_of(*chip, 1 - c)]) for chip in chips]

    return _split_wait(name, send_sems, recv_sems, keep, lands, 3, waits_of, after)[1]


def _place_block(name, land, src, block_idx):
    R, C = src.shape
    tr = _tile(R, 512)

    def body(b_ref, land_ref, s_ref, o_ref):
        o_ref[...] = s_ref[...]

    return pl.pallas_call(
        body, name=name,
        grid_spec=pltpu.PrefetchScalarGridSpec(
            num_scalar_prefetch=1, grid=(R // tr,),
            in_specs=[_any(), pl.BlockSpec((tr, C), lambda i, b: (i, 0))],
            out_specs=pl.BlockSpec((None, tr, C), lambda i, b: (b[0], i, 0))),
        out_shape=jax.ShapeDtypeStruct(land.shape, land.dtype),
        input_output_aliases={1: 0},
        compiler_params=_params("parallel"),
    )(block_idx, land, src)


def _sibling_start(name, gs, after):
    g4 = [g.reshape((4, 2) + g.shape[1:]) for g in gs]
    lands = [lax.empty((4, 1) + g.shape[2:], g.dtype) for g in g4]

    def copies_of(t, g_ref, land_ref):
        x, y, c, _ = _position()
        return [(g_ref.at[:, pl.ds(1 - c, 1)], land_ref, (x, y, 1 - c))]

    return _split_start(name, g4, lands, 1, copies_of, after)


def _sibling_wait(name, started, after):
    send_sems, recv_sems, g4, lands, _ = started

    def waits_of(t, g_ref, land_ref):
        x, y, c, _ = _position()
        return [(g_ref.at[:, pl.ds(1 - c, 1)], land_ref)]

    return _split_wait(name, send_sems, recv_sems, g4, lands, 1, waits_of, after)


def _scatter_chips_start(name, ps, after):
    lands = [lax.empty((3,) + v.shape[1:], v.dtype) for v in ps]

    def copies_of(t, p_ref, land_ref):
        x, y, c, chips = _position()
        return [(p_ref.at[2 * chip[0] + chip[1]], land_ref.at[k], (*chip, c)) for k, chip in enumerate(chips)]

    return _split_start(name, ps, lands, 3, copies_of, after)


def _scatter_chips_wait(name, started, after):
    send_sems, recv_sems, ps, lands, _ = started

    def waits_of(t, p_ref, land_ref):
        x, y, c, chips = _position()
        return [(p_ref.at[2 * chip[0] + chip[1]], land_ref.at[k]) for k, chip in enumerate(chips)]

    return _split_wait(name, send_sems, recv_sems, ps, lands, 3, waits_of, after)


def _scatter_chips_after_sibling(tag, sibling, c_idx, after):
    g4, recv = _sibling_wait("rs_sibling_wait_" + tag, sibling, after)
    ps = [_add_sibling(g, r, c_idx) for g, r in zip(g4, recv)]
    return _scatter_chips_start("rs_chips_start_" + tag, ps, ps[0])


def _sum_devices(parts):
    _, R, C = parts.shape
    tr = _tile(R, 512)

    def body(p_ref, o_ref):
        acc = p_ref[0]
        for d in range(1, N_DEV):
            acc = acc + p_ref[d]
        o_ref[...] = acc

    return pl.pallas_call(
        body, name="sum_devices", grid=(R // tr,),
        in_specs=[pl.BlockSpec((N_DEV, tr, C), lambda i: (0, i, 0))],
        out_specs=pl.BlockSpec((tr, C), lambda i: (i, 0)),
        out_shape=jax.ShapeDtypeStruct((R, C), F32),
        compiler_params=_params("parallel"),
    )(parts)


def _adam_math(w, g, m, v):
    m = ADAM_B1 * m + (1.0 - ADAM_B1) * g
    v = ADAM_B2 * v + (1.0 - ADAM_B2) * (g * g)
    m_hat = m / (1.0 - ADAM_B1 ** ADAM_STEP)
    v_hat = v / (1.0 - ADAM_B2 ** ADAM_STEP)
    return -ADAM_LR * (m_hat / (jnp.sqrt(v_hat) + ADAM_EPS) + ADAM_WD * w), m, v


def _adam_layer(name, layer, w, m, v, own, landed, chip_idx, prev, after):
    L, R, C = w.shape
    tr = _tile(R, 128)
    slab = pl.BlockSpec((None, tr, C), lambda i, q: (layer, i, 0))

    def body(q_ref, w_ref, m_ref, v_ref, own_ref, land_ref, after_ref, *rest):
        g_ref, d_ref, nm_ref, nv_ref = rest[-4:]
        g = land_ref[0].astype(F32)
        for k in range(1, 3):
            g = g + land_ref[k].astype(F32)
        g = g + own_ref[...].astype(F32)
        d, nm, nv = _adam_math(w_ref[...], g, m_ref[...], v_ref[...])
        g_ref[...] = g
        d_ref[...] = d
        nm_ref[...] = nm
        nv_ref[...] = nv

    n_prev = 0 if prev is None else 4
    out = jax.ShapeDtypeStruct((L, R, C), F32)
    return pl.pallas_call(
        body, name=name,
        grid_spec=pltpu.PrefetchScalarGridSpec(
            num_scalar_prefetch=1, grid=(R // tr,),
            in_specs=[slab, slab, slab, pl.BlockSpec((None, tr, C), lambda i, q: (q[0], i, 0)),
                      pl.BlockSpec((3, tr, C), lambda i, q: (0, i, 0)), _any()] + [_any()] * n_prev,
            out_specs=[slab] * 4),
        out_shape=[out] * 4,
        input_output_aliases={7 + k: k for k in range(n_prev)},
        compiler_params=_params("parallel"),
    )(chip_idx, w, m, v, own, landed, after, *(prev or ()))


def _adam_small(w, g, m, v):
    R, C = w.shape
    tr = _tile(R, 512)
    spec = pl.BlockSpec((tr, C), lambda i: (i, 0))

    def body(w_ref, g_ref, m_ref, v_ref, d_ref, nm_ref, nv_ref):
        d, nm, nv = _adam_math(w_ref[...], g_ref[...], m_ref[...], v_ref[...])
        d_ref[...] = d
        nm_ref[...] = nm
        nv_ref[...] = nv

    out = jax.ShapeDtypeStruct((R, C), F32)
    return pl.pallas_call(
        body, name="adam_small", grid=(R // tr,),
        in_specs=[spec] * 4, out_specs=[spec] * 3, out_shape=[out] * 3,
        compiler_params=_params("parallel"),
    )(w, g, m, v)


def _pack(arrays, row_multiple=512):
    flat = jnp.concatenate([a.reshape(-1).astype(F32) for a in arrays])
    per = row_multiple * LANES
    total = -(-flat.shape[0] // per) * per
    return jnp.pad(flat, (0, total - flat.shape[0])).reshape(total // LANES, LANES)


def _unpack(packed, shapes):
    flat = packed.reshape(-1)
    out, off = [], 0
    for s in shapes:
        n = math.prod(s)
        out.append(flat[off:off + n].reshape(s))
        off += n
    return out


BIG = ["ffn1_w_in", "ffn1_w_out", "mix_w_in", "mix_w_out", "ffn2_w_in", "ffn2_w_out"]
SHARDED_SMALL = ["conv_w", "lru_conv_w"]
WEIGHTS = ["norm_ffn1", "ffn1_w_in", "ffn1_w_out", "norm_mix", "mix_w_in", "conv_w", "fgate_b", "lru_conv_w",
           "lru_conv_b", "lru_w_a", "lru_b_a", "lru_w_x", "lru_b_x", "lru_lambda", "mix_out_norm", "mix_w_out",
           "norm_ffn2", "ffn2_w_in", "ffn2_w_out", "final_norm"]
REPLICATED = [n for n in WEIGHTS if n not in BIG and n not in SHARDED_SMALL]


def _step(x, target, w, m, v):
    L = w["norm_ffn1"].shape[0]
    T, D = x.shape[1], x.shape[2]
    dc, da, dl, nh, n_main = _mixer_dims(D)
    xi, yi, ci = lax.axis_index("x"), lax.axis_index("y"), lax.axis_index("c")
    me = 4 * xi + 2 * yi + ci
    c_idx = ci.astype(jnp.int32).reshape(1)

    taps = jnp.concatenate([w["conv_w"].reshape(-1), w["lru_conv_w"].reshape(-1)])
    taps = jnp.pad(taps, (0, (-taps.shape[0]) % (8 * LANES))).reshape(-1, LANES)
    def shards_of(l):
        return [w[n][l].astype(BF) for n in BIG] + ([taps] if l == 0 else [])

    me_idx = me.astype(jnp.int32).reshape(1)
    chip_idx = (2 * xi + yi).astype(jnp.int32).reshape(1)

    def place_own(tag, lands, own):
        return [_place_block("gather_own_" + tag, land, src, me_idx) for land, src in zip(lands, own)]

    gathered = [None] * L
    phase1 = [None] * L
    shards0 = shards_of(0)
    first_start = _gather_phase1_start("gather_p1_start_0_first", shards0[:2] + [taps], taps)
    rest_start = _gather_phase1_start("gather_p1_start_0_rest", shards0[2:len(BIG)], first_start[-1])
    own, landed = _gather_phase1_wait("gather_p1_wait_0_first", first_start, rest_start[-1])
    phase2 = _gather_phase2_start("gather_p2_start_0_first", landed, landed[0])
    landed = _gather_phase2_wait("gather_p2_wait_0_first", phase2, phase2[-1])
    first0 = place_own("0_first", landed, own)
    taps_all = first0[2].reshape(N_DEV, -1)
    n_cw = math.prod(w["conv_w"].shape)
    ch = w["conv_w"].shape[-1]
    conv_w_full = jnp.moveaxis(taps_all[:, :n_cw].reshape((N_DEV,) + w["conv_w"].shape), 0, -2).reshape(L, 3, N_DEV * ch)
    n_lw = math.prod(w["lru_conv_w"].shape)
    lru_conv_w_full = jnp.moveaxis(taps_all[:, n_cw:n_cw + n_lw].reshape((N_DEV,) + w["lru_conv_w"].shape), 0, -2
                                   ).reshape(L, 4, N_DEV * ch)

    def layer_weights(l):
        g = dict(zip(BIG, gathered[l]))
        F = g["ffn1_w_out"].shape[1] * N_DEV
        wmix = jnp.transpose(g["mix_w_in"], (1, 0, 2)).reshape(D, -1)
        return {"ffn1_w_in": g["ffn1_w_in"], "ffn1_w_out": g["ffn1_w_out"].reshape(F, D),
                "wp": _pad_mix_w_in(wmix), "wo": g["mix_w_out"].reshape(D, D),
                "ffn2_w_in": g["ffn2_w_in"], "ffn2_w_out": g["ffn2_w_out"].reshape(F, D)}

    def small_params(l):
        return {"fgate_b": jnp.pad(w["fgate_b"][l], (0, LANES - nh)).reshape(1, LANES),
                "conv_w": conv_w_full[l], "lru_conv_w": lru_conv_w_full[l],
                "lru_conv_b": w["lru_conv_b"][l].reshape(1, dl), "lru_w_a": w["lru_w_a"][l],
                "lru_b_a": w["lru_b_a"][l].reshape(1, dl), "lru_w_x": w["lru_w_x"][l],
                "lru_b_x": w["lru_b_x"][l].reshape(1, dl), "lru_lambda": w["lru_lambda"][l].reshape(1, dl),
                "mix_out_norm": w["mix_out_norm"][l].reshape(1, D)}

    gain = lambda n, l, token=None: w[n][l].reshape(1, D) + (0.0 if token is None else token[0:1, 0:1])

    xc = x[0]
    saved, lw, sps = [], [], []
    for l in range(L):
        w_in1, w_out1 = first0[:2] if l == 0 else gathered[l][:2]
        xc, s1 = _ffn_forward(xc, gain("norm_ffn1", l), w_in1, w_out1.reshape(-1, D))
        if l == 0:
            own, landed = _gather_phase1_wait("gather_p1_wait_0_rest", rest_start, xc)
            phase2 = _gather_phase2_start("gather_p2_start_0_rest", landed, landed[0])
            order_token = phase2[-1]
            if L > 1:
                phase1[1] = _gather_phase1_start("gather_p1_start_1", shards_of(1), phase2[-1])
                order_token = phase1[1][-1]
            landed = _gather_phase2_wait("gather_p2_wait_0_rest", phase2, order_token)
            gathered[0] = first0[:2] + place_own("0_rest", landed, own)
        lw.append(layer_weights(l))
        sps.append(small_params(l))
        xc, s2 = _mixer_forward(xc, gain("norm_mix", l), lw[l]["wp"], lw[l]["wo"], sps[l])
        order_token = None
        if l + 1 < L:
            own, landed = _gather_phase1_wait("gather_p1_wait_%d" % (l + 1), phase1[l + 1], xc)
            phase2 = _gather_phase2_start("gather_p2_start_%d" % (l + 1), landed, landed[0])
            order_token = phase2[-1]
            if l + 2 < L:
                phase1[l + 2] = _gather_phase1_start("gather_p1_start_%d" % (l + 2), shards_of(l + 2), phase2[-1])
                order_token = phase1[l + 2][-1]
        xc, s3 = _ffn_forward(xc, gain("norm_ffn2", l, order_token), lw[l]["ffn2_w_in"], lw[l]["ffn2_w_out"])
        if l + 1 < L:
            landed = _gather_phase2_wait("gather_p2_wait_%d" % (l + 1), phase2, xc)
            gathered[l + 1] = place_own(str(l + 1), landed, own)
        saved.append((s1, s2, s3))
    loss_part, dx, d_final = _loss_head(xc, w["final_norm"].reshape(1, D), target[0])

    small_grads = [None] * L
    parts = [None] * L
    sibling = [None] * L
    scatter = [None] * L
    scatter_early = []
    for l in reversed(range(L)):
        s1, s2, s3 = saved[l]
        dh, dw3_2, dwout_2 = _ffn_backward(dx, s3, lw[l]["ffn2_w_in"], lw[l]["ffn2_w_out"])
        dx, dg3 = _rmsnorm_bwd(dh, s3[0], gain("norm_ffn2", l), dx)
        sp_l = sps[l]
        if l + 1 < L:
            scatter[l + 1] = _scatter_chips_after_sibling(str(l + 1), sibling[l + 1], c_idx, dx)
            sp_l = dict(sp_l, mix_out_norm=sp_l["mix_out_norm"] + scatter[l + 1][-1][0:1, 0:1])
        dx, dwp, dwo, sg = _mixer_backward(dx, s2, gain("norm_mix", l), lw[l]["wp"], lw[l]["wo"], sp_l)
        dmix = jnp.transpose(_unpad_mix_w_in(dwp).reshape(D, N_DEV, -1), (1, 0, 2))
        grads = [dmix, dwo.reshape(N_DEV, D // N_DEV, D), dw3_2, dwout_2]
        between = None
        if l == 0:
            early = _sibling_start("rs_sibling_start_0_early", grads, dwp)
            grads = []

            def between(dgu):
                scatter_early.append(_scatter_chips_after_sibling("0_early", early, c_idx, dgu))
                return scatter_early[0][-1]

        dh, dw3_1, dwout_1 = _ffn_backward(dx, s1, lw[l]["ffn1_w_in"], lw[l]["ffn1_w_out"], between)
        if l + 1 < L:
            parts[l + 1] = _scatter_chips_wait("rs_chips_wait_%d" % (l + 1), scatter[l + 1], dh)
        sibling[l] = _sibling_start("rs_sibling_start_%d" % l, [dw3_1, dwout_1] + grads, dh)
        dx, dg1 = _rmsnorm_bwd(dh, s1[0], gain("norm_ffn1", l, sibling[l][-1]), dx)
        sg["norm_ffn2"], sg["norm_ffn1"] = dg3, dg1
        small_grads[l] = sg

    out_g, out_d, out_m, out_v = {}, {}, {}, {}
    prev = {n: None for n in BIG}

    def adam_big(l, after):
        own, landed = parts[l]
        for t, n in enumerate(BIG):
            prev[n] = _adam_layer("adam_%s_%d" % (n, l), l, w[n], m[n], v[n], own[t], landed[t], chip_idx, prev[n], after)

    small_names = REPLICATED + SHARDED_SMALL
    partial = []
    for n in small_names:
        if n == "final_norm":
            partial.append(d_final)
        else:
            partial.append(jnp.stack([small_grads[l][n].reshape(w[n].shape[1:]) if n not in SHARDED_SMALL
                                      else small_grads[l][n] for l in range(L)]))
    partial.append(loss_part[0, :1])
    packed = _pack(partial)
    small_gathered = _all_gather("gather_small_grads", [packed])[0]
    scatter[0] = _scatter_chips_after_sibling("0", sibling[0], c_idx, small_gathered)
    for l in reversed(range(1, L)):
        adam_big(l, scatter[0][-1])
    summed = _sum_devices(small_gathered)
    full_shapes = [w[n].shape for n in REPLICATED] + [(L, 3, N_DEV * ch), (L, 4, N_DEV * ch), (1,)]
    full = _unpack(summed, full_shapes)
    loss = full[-1][0]
    g_small = dict(zip(small_names, full[:-1]))
    for n in SHARDED_SMALL:
        g_small[n] = lax.dynamic_slice_in_dim(g_small[n], me * ch, ch, axis=2)
    shapes = [w[n].shape for n in small_names]
    d_s, m_s, v_s = _adam_small(_pack([w[n] for n in small_names]), _pack([g_small[n] for n in small_names]),
                                _pack([m[n] for n in small_names]), _pack([v[n] for n in small_names]))
    for n, d_, m_, v_ in zip(small_names, _unpack(d_s, shapes), _unpack(m_s, shapes), _unpack(v_s, shapes)):
        out_g[n], out_d[n], out_m[n], out_v[n] = g_small[n], d_, m_, v_

    behind = d_s if L == 1 else prev[BIG[-1]][0]
    own_early, landed_early = _scatter_chips_wait("rs_chips_wait_0_early", scatter_early[0], behind)
    own_late, landed_late = _scatter_chips_wait("rs_chips_wait_0", scatter[0], behind)
    parts[0] = (list(own_late) + list(own_early), list(landed_late) + list(landed_early))
    adam_big(0, d_s)
    for n in BIG:
        out_g[n], out_d[n], out_m[n], out_v[n] = prev[n]

    return (loss, dx[None], *[out_g[n] for n in WEIGHTS], *[out_d[n] for n in WEIGHTS],
            *[out_m[n] for n in WEIGHTS], *[out_v[n] for n in WEIGHTS])


def kernel(x, norm_ffn1, ffn1_w_in, ffn1_w_out, norm_mix, mix_w_in, conv_w, fgate_b, lru_conv_w, lru_conv_b, lru_w_a, lru_b_a, lru_w_x, lru_b_x, lru_lambda, mix_out_norm, mix_w_out, norm_ffn2, ffn2_w_in, ffn2_w_out, final_norm, loss_target, m_norm_ffn1, m_ffn1_w_in, m_ffn1_w_out, m_norm_mix, m_mix_w_in, m_conv_w, m_fgate_b, m_lru_conv_w, m_lru_conv_b, m_lru_w_a, m_lru_b_a, m_lru_w_x, m_lru_b_x, m_lru_lambda, m_mix_out_norm, m_mix_w_out, m_norm_ffn2, m_ffn2_w_in, m_ffn2_w_out, m_final_norm, v_norm_ffn1, v_ffn1_w_in, v_ffn1_w_out, v_norm_mix, v_mix_w_in, v_conv_w, v_fgate_b, v_lru_conv_w, v_lru_conv_b, v_lru_w_a, v_lru_b_a, v_lru_w_x, v_lru_b_x, v_lru_lambda, v_mix_out_norm, v_mix_w_out, v_norm_ffn2, v_ffn2_w_in, v_ffn2_w_out, v_final_norm):
    w = dict(norm_ffn1=norm_ffn1, ffn1_w_in=ffn1_w_in, ffn1_w_out=ffn1_w_out, norm_mix=norm_mix, mix_w_in=mix_w_in,
             conv_w=conv_w, fgate_b=fgate_b, lru_conv_w=lru_conv_w, lru_conv_b=lru_conv_b, lru_w_a=lru_w_a,
             lru_b_a=lru_b_a, lru_w_x=lru_w_x, lru_b_x=lru_b_x, lru_lambda=lru_lambda, mix_out_norm=mix_out_norm,
             mix_w_out=mix_w_out, norm_ffn2=norm_ffn2, ffn2_w_in=ffn2_w_in, ffn2_w_out=ffn2_w_out,
             final_norm=final_norm)
    m = dict(norm_ffn1=m_norm_ffn1, ffn1_w_in=m_ffn1_w_in, ffn1_w_out=m_ffn1_w_out, norm_mix=m_norm_mix,
             mix_w_in=m_mix_w_in, conv_w=m_conv_w, fgate_b=m_fgate_b, lru_conv_w=m_lru_conv_w,
             lru_conv_b=m_lru_conv_b, lru_w_a=m_lru_w_a, lru_b_a=m_lru_b_a, lru_w_x=m_lru_w_x, lru_b_x=m_lru_b_x,
             lru_lambda=m_lru_lambda, mix_out_norm=m_mix_out_norm, mix_w_out=m_mix_w_out, norm_ffn2=m_norm_ffn2,
             ffn2_w_in=m_ffn2_w_in, ffn2_w_out=m_ffn2_w_out, final_norm=m_final_norm)
    v = dict(norm_ffn1=v_norm_ffn1, ffn1_w_in=v_ffn1_w_in, ffn1_w_out=v_ffn1_w_out, norm_mix=v_norm_mix,
             mix_w_in=v_mix_w_in, conv_w=v_conv_w, fgate_b=v_fgate_b, lru_conv_w=v_lru_conv_w,
             lru_conv_b=v_lru_conv_b, lru_w_a=v_lru_w_a, lru_b_a=v_lru_b_a, lru_w_x=v_lru_w_x, lru_b_x=v_lru_b_x,
             lru_lambda=v_lru_lambda, mix_out_norm=v_mix_out_norm, mix_w_out=v_mix_w_out, norm_ffn2=v_norm_ffn2,
             ffn2_w_in=v_ffn2_w_in, ffn2_w_out=v_ffn2_w_out, final_norm=v_final_norm)
    return _step(x, loss_target, w, m, v)
```

```python
import math

import jax
import jax.numpy as jnp
from jax import lax
from jax.experimental import pallas as pl
from jax.experimental.pallas import tpu as pltpu

F32 = jnp.float32
BF = jnp.bfloat16
EPS = 1e-6
LANES = 128
VMEM_LIMIT_V7X = 56 * 1024 * 1024
MESH = pl.DeviceIdType.MESH
N_DEV = 8
LRU_C = 8.0
ADAM_LR, ADAM_B1, ADAM_B2, ADAM_EPS, ADAM_WD, ADAM_STEP = 0.001, 0.9, 0.999, 1e-08, 0.01, 10
GELU_C = math.sqrt(2.0 / math.pi)
GELU_K = 0.044715


def _params(*sem):
    return pltpu.CompilerParams(dimension_semantics=sem, vmem_limit_bytes=VMEM_LIMIT_V7X)


def _any():
    return pl.BlockSpec(memory_space=pl.ANY)


def _tile(n, target):
    if n <= target:
        return n
    t = target - target % 16
    while t >= 16:
        if n % t == 0:
            return t
        t -= 16
    return n


def _mm(name, grid, a, b, o, dims, scale=1.0, resid=None, after=None):
    nk = grid[-1]
    acc_shape = tuple(d for d in o[1] if d is not None)
    has_resid = resid is not None
    n_in = 2 + has_resid + (after is not None)

    def body(*refs):
        a_ref, b_ref = refs[0], refs[1]
        r_ref = refs[2] if has_resid else None
        o_ref = refs[n_in]

        def finish(acc):
            r = acc * scale if scale != 1.0 else acc
            if has_resid:
                r = r + r_ref[...]
            o_ref[...] = r.astype(o_ref.dtype)

        part = lax.dot_general(a_ref[...].astype(BF), b_ref[...].astype(BF), (dims, ((), ())),
                               preferred_element_type=F32)
        if nk == 1:
            finish(part)
        else:
            acc_ref = refs[-1]
            k = pl.program_id(len(grid) - 1)

            @pl.when(k == 0)
            def _():
                acc_ref[...] = part

            @pl.when(k > 0)
            def _():
                acc_ref[...] += part

            @pl.when(k == nk - 1)
            def _():
                finish(acc_ref[...])

    ins = [a, b] + ([resid] if has_resid else [])
    return pl.pallas_call(
        body, name=name, grid=grid,
        in_specs=[pl.BlockSpec(blk, idx) for (_, blk, idx) in ins] + ([_any()] if after is not None else []),
        out_specs=pl.BlockSpec(o[1], o[2]),
        out_shape=o[0],
        scratch_shapes=[pltpu.VMEM(acc_shape, F32)] if nk > 1 else [],
        compiler_params=_params(*(["parallel"] * (len(grid) - 1) + ["arbitrary"])),
    )(*[x[0] for x in ins], *([after] if after is not None else []))


NN = ((1,), (0,))
NT = ((1,), (1,))
TN = ((0,), (0,))


def _rmsnorm(x, gain):
    T, D = x.shape
    tr = _tile(T, 256)

    def body(x_ref, g_ref, o_ref):
        xv = x_ref[...]
        r = lax.rsqrt(jnp.mean(xv * xv, axis=-1, keepdims=True) + EPS)
        o_ref[...] = (xv * r * g_ref[...]).astype(BF)

    return pl.pallas_call(
        body, name="rmsnorm_fwd", grid=(T // tr,),
        in_specs=[pl.BlockSpec((tr, D), lambda i: (i, 0)), pl.BlockSpec((1, D), lambda i: (0, 0))],
        out_specs=pl.BlockSpec((tr, D), lambda i: (i, 0)),
        out_shape=jax.ShapeDtypeStruct((T, D), BF),
        compiler_params=_params("parallel"),
    )(x, gain)


def _rmsnorm_bwd(dh, x, gain, dres):
    T, D = x.shape
    tr = _tile(T, 256)

    def body(dh_ref, x_ref, g_ref, dres_ref, dx_ref, dg_ref):
        i = pl.program_id(0)
        xv = x_ref[...]
        r = lax.rsqrt(jnp.mean(xv * xv, axis=-1, keepdims=True) + EPS)
        xh = xv * r
        dy = dh_ref[...].astype(F32)
        dgp = jnp.sum(dy * xh, axis=0, keepdims=True)

        @pl.when(i == 0)
        def _():
            dg_ref[...] = dgp

        @pl.when(i > 0)
        def _():
            dg_ref[...] += dgp

        dxh = dy * g_ref[...]
        dx_ref[...] = dres_ref[...] + r * (dxh - xh * jnp.mean(dxh * xh, axis=-1, keepdims=True))

    return pl.pallas_call(
        body, name="rmsnorm_bwd", grid=(T // tr,),
        in_specs=[pl.BlockSpec((tr, D), lambda i: (i, 0)), pl.BlockSpec((tr, D), lambda i: (i, 0)),
                  pl.BlockSpec((1, D), lambda i: (0, 0)), pl.BlockSpec((tr, D), lambda i: (i, 0))],
        out_specs=[pl.BlockSpec((tr, D), lambda i: (i, 0)), pl.BlockSpec((1, D), lambda i: (0, 0))],
        out_shape=[jax.ShapeDtypeStruct((T, D), F32), jax.ShapeDtypeStruct((1, D), F32)],
        compiler_params=_params("arbitrary"),
    )(dh, x, gain, dres)


def _loss_head(x, gain, target):
    T, D = x.shape
    tr = _tile(T, 256)

    def body(x_ref, g_ref, t_ref, loss_ref, dx_ref, dg_ref):
        i = pl.program_id(0)
        xv = x_ref[...]
        g = g_ref[...]
        r = lax.rsqrt(jnp.mean(xv * xv, axis=-1, keepdims=True) + EPS)
        xh = xv * r
        err = xh * g - t_ref[...]
        lp = 0.5 * jnp.sum(jnp.mean(err * err, axis=-1, keepdims=True), axis=0, keepdims=True)
        dy = err * (1.0 / D)
        dgp = jnp.sum(dy * xh, axis=0, keepdims=True)

        @pl.when(i == 0)
        def _():
            loss_ref[...] = jnp.broadcast_to(lp, loss_ref.shape)
            dg_ref[...] = dgp

        @pl.when(i > 0)
        def _():
            loss_ref[...] += jnp.broadcast_to(lp, loss_ref.shape)
            dg_ref[...] += dgp

        dxh = dy * g
        dx_ref[...] = r * (dxh - xh * jnp.mean(dxh * xh, axis=-1, keepdims=True))

    return pl.pallas_call(
        body, name="loss_head", grid=(T // tr,),
        in_specs=[pl.BlockSpec((tr, D), lambda i: (i, 0)), pl.BlockSpec((1, D), lambda i: (0, 0)),
                  pl.BlockSpec((tr, D), lambda i: (i, 0))],
        out_specs=[pl.BlockSpec((1, LANES), lambda i: (0, 0)), pl.BlockSpec((tr, D), lambda i: (i, 0)),
                   pl.BlockSpec((1, D), lambda i: (0, 0))],
        out_shape=[jax.ShapeDtypeStruct((1, LANES), F32), jax.ShapeDtypeStruct((T, D), F32),
                   jax.ShapeDtypeStruct((1, D), F32)],
        compiler_params=_params("arbitrary"),
    )(x, gain, target)


def _group_slices(D):
    dc, da = D // 4, D // 2
    return [(0, dc), (dc, dc + da), (dc + da, D)]


def _groupnorm(yc, ya, yl, gain):
    T = yc.shape[0]
    D = yc.shape[1] + ya.shape[1] + yl.shape[1]
    tr = _tile(T, 256)
    sl = _group_slices(D)

    def body(yc_ref, ya_ref, yl_ref, g_ref, o_ref):
        for y_ref, (lo, hi) in zip((yc_ref, ya_ref, yl_ref), sl):
            y = y_ref[...]
            r = lax.rsqrt(jnp.mean(y * y, axis=-1, keepdims=True) + EPS)
            o_ref[:, lo:hi] = (y * r * g_ref[:, lo:hi]).astype(BF)

    return pl.pallas_call(
        body, name="groupnorm_fwd", grid=(T // tr,),
        in_specs=[pl.BlockSpec((tr, y.shape[1]), lambda i: (i, 0)) for y in (yc, ya, yl)]
        + [pl.BlockSpec((1, D), lambda i: (0, 0))],
        out_specs=pl.BlockSpec((tr, D), lambda i: (i, 0)),
        out_shape=jax.ShapeDtypeStruct((T, D), BF),
        compiler_params=_params("parallel"),
    )(yc, ya, yl, gain)


def _groupnorm_bwd(dyn, yc, ya, yl, gain):
    T, D = dyn.shape
    tr = _tile(T, 256)
    sl = _group_slices(D)

    def body(dyn_ref, yc_ref, ya_ref, yl_ref, g_ref, dc_ref, da_ref, dl_ref, dg_ref):
        i = pl.program_id(0)
        for y_ref, d_ref, (lo, hi) in zip((yc_ref, ya_ref, yl_ref), (dc_ref, da_ref, dl_ref), sl):
            y = y_ref[...]
            r = lax.rsqrt(jnp.mean(y * y, axis=-1, keepdims=True) + EPS)
            yh = y * r
            dy = dyn_ref[:, lo:hi]
            dgp = jnp.sum(dy * yh, axis=0, keepdims=True)

            @pl.when(i == 0)
            def _():
                dg_ref[:, lo:hi] = dgp

            @pl.when(i > 0)
            def _():
                dg_ref[:, lo:hi] += dgp

            dyh = dy * g_ref[:, lo:hi]
            d_ref[...] = r * (dyh - yh * jnp.mean(dyh * yh, axis=-1, keepdims=True))

    return pl.pallas_call(
        body, name="groupnorm_bwd", grid=(T // tr,),
        in_specs=[pl.BlockSpec((tr, D), lambda i: (i, 0))]
        + [pl.BlockSpec((tr, y.shape[1]), lambda i: (i, 0)) for y in (yc, ya, yl)]
        + [pl.BlockSpec((1, D), lambda i: (0, 0))],
        out_specs=[pl.BlockSpec((tr, y.shape[1]), lambda i: (i, 0)) for y in (yc, ya, yl)]
        + [pl.BlockSpec((1, D), lambda i: (0, 0))],
        out_shape=[jax.ShapeDtypeStruct(y.shape, F32) for y in (yc, ya, yl)] + [jax.ShapeDtypeStruct((1, D), F32)],
        compiler_params=_params("arbitrary"),
    )(dyn, yc, ya, yl, gain)


def _ffn_in(h, w3):
    T, D = h.shape
    tn = w3.shape[2]
    F = 4 * tn
    tm = _tile(T, 512)

    def body(h_ref, wg_ref, wu_ref, g_ref, u_ref, a_ref):
        hv = h_ref[...]
        g = jnp.dot(hv, wg_ref[...], preferred_element_type=F32)
        u = jnp.dot(hv, wu_ref[...], preferred_element_type=F32)
        g_ref[...] = g.astype(BF)
        u_ref[...] = u.astype(BF)
        a_ref[...] = (g * jax.nn.sigmoid(g) * u).astype(BF)

    out = jax.ShapeDtypeStruct((T, F), BF)
    return pl.pallas_call(
        body, name="ffn_in_swiglu", grid=(4, T // tm),
        in_specs=[pl.BlockSpec((tm, D), lambda j, i: (i, 0)),
                  pl.BlockSpec((None, D, tn), lambda j, i: (j, 0, 0)),
                  pl.BlockSpec((None, D, tn), lambda j, i: (j + 4, 0, 0))],
        out_specs=[pl.BlockSpec((tm, tn), lambda j, i: (i, j))] * 3,
        out_shape=[out, out, out],
        compiler_params=_params("parallel", "parallel"),
    )(h, w3, w3)


def _ffn_bwd_in(dx, wout, g, u):
    T, D = dx.shape
    F = wout.shape[0]
    tn = F // 4
    tm = _tile(T, 512)

    def body(dx_ref, w_ref, g_ref, u_ref, o_ref):
        da = 0.5 * lax.dot_general(dx_ref[...].astype(BF), w_ref[...], (NT, ((), ())), preferred_element_type=F32)
        gv = g_ref[...].astype(F32)
        s = jax.nn.sigmoid(gv)
        o_ref[0] = (da * u_ref[...].astype(F32) * (s * (1.0 + gv * (1.0 - s)))).astype(BF)
        o_ref[1] = (da * gv * s).astype(BF)

    return pl.pallas_call(
        body, name="ffn_bwd_swiglu", grid=(4, T // tm),
        in_specs=[pl.BlockSpec((tm, D), lambda j, i: (i, 0)), pl.BlockSpec((tn, D), lambda j, i: (j, 0)),
                  pl.BlockSpec((tm, tn), lambda j, i: (i, j)), pl.BlockSpec((tm, tn), lambda j, i: (i, j))],
        out_specs=pl.BlockSpec((2, tm, tn), lambda j, i: (0, i, j)),
        out_shape=jax.ShapeDtypeStruct((2, T, F), BF),
        compiler_params=_params("parallel", "parallel"),
    )(dx, wout, g, u)


def _ffn_forward(x, gain, w3, wout):
    T, D = x.shape
    F = wout.shape[0]
    h = _rmsnorm(x, gain)
    g, u, a = _ffn_in(h, w3)
    tm, tk = _tile(T, 512), F // 4
    xn = _mm("ffn_out", (T // tm, 1, F // tk),
             (a, (tm, tk), lambda i, j, k: (i, k)), (wout, (tk, D), lambda i, j, k: (k, 0)),
             (jax.ShapeDtypeStruct((T, D), F32), (tm, D), lambda i, j, k: (i, 0)), NN, scale=0.5,
             resid=(x, (tm, D), lambda i, j, k: (i, 0)))
    return xn, (x, h, g, u, a)


def _ffn_backward(dx, saved, w3, wout, between=None):
    x, h, g, u, a = saved
    T, D = x.shape
    F = wout.shape[0]
    tn3 = F // 4
    dgu = _ffn_bwd_in(dx, wout, g, u)
    behind = None if between is None else between(dgu)
    tnd = _tile(D, 512)
    dwout = _mm("ffn_dwout", (F // tn3, D // tnd, 1),
                (a, (T, tn3), lambda i, j, k: (0, i)), (dx, (T, tnd), lambda i, j, k: (0, j)),
                (jax.ShapeDtypeStruct((F, D), BF), (tn3, tnd), lambda i, j, k: (i, j)), TN, scale=0.5, after=behind)
    dw3 = _mm("ffn_dwin", (8, D // tnd, 1),
              (h, (T, tnd), lambda s, i, k: (0, i)), (dgu, (None, T, tn3), lambda s, i, k: (s // 4, 0, s % 4)),
              (jax.ShapeDtypeStruct((8, D, tn3), BF), (None, tnd, tn3), lambda s, i, k: (s, i, 0)), TN, after=behind)
    tm = _tile(T, 1024)
    dh = _mm("ffn_dh", (T // tm, 1, 8),
             (dgu, (None, tm, tn3), lambda i, j, k: (k // 4, i, k % 4)), (w3, (None, D, tn3), lambda i, j, k: (k, 0, 0)),
             (jax.ShapeDtypeStruct((T, D), F32), (tm, D), lambda i, j, k: (i, 0)), NT, after=behind)
    return dh, dw3, dwout.reshape(N_DEV, F // N_DEV, D)


def _rows(shape):
    return lax.broadcasted_iota(jnp.int32, shape, 0)


def _down(x, s, fill, rows):
    return jnp.where(rows >= s, pltpu.roll(x, s, 0), fill)


def _up(x, s, fill, rows):
    T = x.shape[0]
    return jnp.where(rows < T - s, pltpu.roll(x, T - s, 0), fill)


def _scan_linear(a, b, rows, shift):
    T = a.shape[0]
    s = 1
    while s < T:
        b = a * shift(b, s, 0.0, rows) + b
        if 2 * s < T:
            a = a * shift(a, s, 1.0, rows)
        s *= 2
    return b


def _cumsum(c, rows, shift):
    T = c.shape[0]
    s = 1
    while s < T:
        c = c + shift(c, s, 0.0, rows)
        s *= 2
    return c


def _log1p_small(e):
    return jnp.where(e < 0.01, e * (1.0 - e * (0.5 - e * (1.0 / 3.0))), jnp.log(1.0 + e))


def _softplus(x):
    return jnp.maximum(x, 0.0) + _log1p_small(jnp.exp(-jnp.abs(x)))


def _one_minus_exp_neg(z):
    return jnp.where(z < 0.1, z * (1.0 - z * (0.5 - z * (1.0 / 6.0 - z * (1.0 / 24.0)))), 1.0 - jnp.exp(-z))


def _fgate_cum(f, b):
    T = f.shape[0]

    def body(f_ref, b_ref, o_ref):
        z = f_ref[...] + b_ref[...]
        o_ref[...] = _cumsum(-_softplus(-z), _rows(z.shape), _down)

    return pl.pallas_call(
        body, name="fgate_cumsum",
        out_shape=jax.ShapeDtypeStruct((T, LANES), F32),
        compiler_params=pltpu.CompilerParams(vmem_limit_bytes=VMEM_LIMIT_V7X),
    )(f, b)


def _fgate_cum_bwd(drow, dcol, f, b):
    T = f.shape[0]

    def body(dr_ref, dc_ref, f_ref, b_ref, df_ref, db_ref):
        z = f_ref[...] + b_ref[...]
        dlogf = _cumsum(dr_ref[...] - dc_ref[...], _rows(z.shape), _up)
        dz = dlogf * jax.nn.sigmoid(-z)
        df_ref[...] = dz.astype(BF)
        db_ref[...] = jnp.sum(dz, axis=0, keepdims=True)

    return pl.pallas_call(
        body, name="fgate_cumsum_bwd",
        out_shape=[jax.ShapeDtypeStruct((T, LANES), BF), jax.ShapeDtypeStruct((1, LANES), F32)],
        compiler_params=pltpu.CompilerParams(vmem_limit_bytes=VMEM_LIMIT_V7X),
    )(drow, dcol, f, b)


def _col(blk0):
    return lambda g: (0, blk0 + g)


def _conv_fwd(p, w, nb):
    T = p.shape[0]

    def body(b_ref, c_ref, v_ref, w_ref, o_ref):
        z = c_ref[...].astype(F32) * v_ref[...].astype(F32)
        rows = _rows(z.shape)
        conv = w_ref[2:3, :] * z + w_ref[1:2, :] * _down(z, 1, 0.0, rows) + w_ref[0:1, :] * _down(z, 2, 0.0, rows)
        o_ref[...] = b_ref[...].astype(F32) * conv

    return pl.pallas_call(
        body, name="conv_fwd", grid=(nb,),
        in_specs=[pl.BlockSpec((T, LANES), _col(0)), pl.BlockSpec((T, LANES), _col(nb)),
                  pl.BlockSpec((T, LANES), _col(2 * nb)), pl.BlockSpec((3, LANES), lambda g: (0, g))],
        out_specs=pl.BlockSpec((T, LANES), lambda g: (0, g)),
        out_shape=jax.ShapeDtypeStruct((T, nb * LANES), F32),
        compiler_params=_params("parallel"),
    )(p, p, p, w)


def _conv_bwd(dy, p, w, nb):
    T = p.shape[0]

    def body(dy_ref, b_ref, c_ref, v_ref, w_ref, db_ref, dc_ref, dv_ref, dw_ref):
        cv, vv = c_ref[...].astype(F32), v_ref[...].astype(F32)
        z = cv * vv
        rows = _rows(z.shape)
        z1, z2 = _down(z, 1, 0.0, rows), _down(z, 2, 0.0, rows)
        dyv = dy_ref[...]
        db_ref[...] = (dyv * (w_ref[2:3, :] * z + w_ref[1:2, :] * z1 + w_ref[0:1, :] * z2)).astype(BF)
        dconv = dyv * b_ref[...].astype(F32)
        dz = (w_ref[2:3, :] * dconv + w_ref[1:2, :] * _up(dconv, 1, 0.0, rows)
              + w_ref[0:1, :] * _up(dconv, 2, 0.0, rows))
        dc_ref[...] = (dz * vv).astype(BF)
        dv_ref[...] = (dz * cv).astype(BF)
        dw_ref[0:1, :] = jnp.sum(dconv * z2, axis=0, keepdims=True)
        dw_ref[1:2, :] = jnp.sum(dconv * z1, axis=0, keepdims=True)
        dw_ref[2:3, :] = jnp.sum(dconv * z, axis=0, keepdims=True)

    return pl.pallas_call(
        body, name="conv_bwd", grid=(nb,),
        in_specs=[pl.BlockSpec((T, LANES), lambda g: (0, g)), pl.BlockSpec((T, LANES), _col(0)),
                  pl.BlockSpec((T, LANES), _col(nb)), pl.BlockSpec((T, LANES), _col(2 * nb)),
                  pl.BlockSpec((3, LANES), lambda g: (0, g))],
        out_specs=[pl.BlockSpec((T, LANES), lambda g: (0, g))] * 3 + [pl.BlockSpec((3, LANES), lambda g: (0, g))],
        out_shape=[jax.ShapeDtypeStruct((T, nb * LANES), BF)] * 3 + [jax.ShapeDtypeStruct((3, nb * LANES), F32)],
        compiler_params=_params("parallel"),
    )(dy, p, p, p, w)


def _gelu(x):
    t = jnp.tanh(GELU_C * (x + GELU_K * x * x * x))
    return 0.5 * x * (1.0 + t), t


def _lru_common(x, cw_ref, cb_ref, wa_ref, ba_ref, wx_ref, bx_ref, lam_ref, rows):
    xr = (cb_ref[...] + cw_ref[3:4, :] * x + cw_ref[2:3, :] * _down(x, 1, 0.0, rows)
          + cw_ref[1:2, :] * _down(x, 2, 0.0, rows) + cw_ref[0:1, :] * _down(x, 3, 0.0, rows))
    xrb = xr.astype(BF)
    r = jax.nn.sigmoid(jnp.dot(xrb, wa_ref[...].astype(BF), preferred_element_type=F32) + ba_ref[...])
    i = jax.nn.sigmoid(jnp.dot(xrb, wx_ref[...].astype(BF), preferred_element_type=F32) + bx_ref[...])
    sp = _softplus(-lam_ref[...])
    log_a = -LRU_C * r * sp
    a = jnp.exp(log_a)
    m = jnp.sqrt(_one_minus_exp_neg(-2.0 * log_a))
    return xr, xrb, r, i, sp, a, m


def _lru_specs(T, nb, gate_blk0, x_blk0):
    vec = pl.BlockSpec((1, LANES), lambda g: (0, g))
    mat = pl.BlockSpec((None, LANES, LANES), lambda g: (g, 0, 0))
    return [pl.BlockSpec((T, LANES), _col(gate_blk0)), pl.BlockSpec((T, LANES), _col(x_blk0)),
            pl.BlockSpec((4, LANES), lambda g: (0, g)), vec, mat, vec, mat, vec, vec]


def _lru_fwd(p, cw, cb, wa, ba, wx, bx, lam, nb, gate_blk0):
    T = p.shape[0]

    def body(gate_ref, x_ref, cw_ref, cb_ref, wa_ref, ba_ref, wx_ref, bx_ref, lam_ref, y_ref, h_ref):
        x = x_ref[...].astype(F32)
        rows = _rows(x.shape)
        xr, _, _, i, _, a, m = _lru_common(x, cw_ref, cb_ref, wa_ref, ba_ref, wx_ref, bx_ref, lam_ref, rows)
        h = _scan_linear(a, m * (i * xr), rows, _down)
        h_ref[...] = h
        y_ref[...] = _gelu(gate_ref[...].astype(F32))[0] * h

    out = jax.ShapeDtypeStruct((T, nb * LANES), F32)
    return pl.pallas_call(
        body, name="lru_fwd", grid=(nb,),
        in_specs=_lru_specs(T, nb, gate_blk0, gate_blk0 + nb),
        out_specs=[pl.BlockSpec((T, LANES), lambda g: (0, g))] * 2,
        out_shape=[out, out],
        compiler_params=_params("parallel"),
    )(p, p, cw, cb, wa, ba, wx, bx, lam)


def _lru_bwd(dy, hs, p, cw, cb, wa, ba, wx, bx, lam, nb, gate_blk0):
    T = p.shape[0]

    def body(dy_ref, hs_ref, gate_ref, x_ref, cw_ref, cb_ref, wa_ref, ba_ref, wx_ref, bx_ref, lam_ref,
             dgate_ref, dx_ref, dcw_ref, dcb_ref, dwa_ref, dba_ref, dwx_ref, dbx_ref, dlam_ref):
        x = x_ref[...].astype(F32)
        rows = _rows(x.shape)
        xr, xrb, r, i, sp, a, m = _lru_common(x, cw_ref, cb_ref, wa_ref, ba_ref, wx_ref, bx_ref, lam_ref, rows)
        gate = gate_ref[...].astype(F32)
        gl, t = _gelu(gate)
        h = hs_ref[...]
        dyv = dy_ref[...]
        dgelu = 0.5 * (1.0 + t) + 0.5 * gate * (1.0 - t * t) * GELU_C * (1.0 + 3.0 * GELU_K * gate * gate)
        dgate_ref[...] = (dyv * h * dgelu).astype(BF)
        lam_adj = _scan_linear(_up(a, 1, 0.0, rows), dyv * gl, rows, _up)
        da = lam_adj * _down(h, 1, 0.0, rows)
        ix = i * xr
        dix = lam_adj * m
        dm = lam_adj * ix
        dlog_a = da * a - dm * (a * a) / jnp.maximum(m, 1e-30)
        dr = dlog_a * (-LRU_C * sp)
        dsp = jnp.sum(dlog_a * (-LRU_C * r), axis=0, keepdims=True)
        dlam_ref[...] = -dsp * jax.nn.sigmoid(-lam_ref[...])
        dpa = dr * r * (1.0 - r)
        dpx = dix * xr * i * (1.0 - i)
        dpab, dpxb = dpa.astype(BF), dpx.astype(BF)
        dxr = (dix * i
               + lax.dot_general(dpab, wa_ref[...].astype(BF), (NT, ((), ())), preferred_element_type=F32)
               + lax.dot_general(dpxb, wx_ref[...].astype(BF), (NT, ((), ())), preferred_element_type=F32))
        dwa_ref[...] = lax.dot_general(xrb, dpab, (TN, ((), ())), preferred_element_type=F32)
        dwx_ref[...] = lax.dot_general(xrb, dpxb, (TN, ((), ())), preferred_element_type=F32)
        dba_ref[...] = jnp.sum(dpa, axis=0, keepdims=True)
        dbx_ref[...] = jnp.sum(dpx, axis=0, keepdims=True)
        dcb_ref[...] = jnp.sum(dxr, axis=0, keepdims=True)
        dx_ref[...] = (cw_ref[3:4, :] * dxr + cw_ref[2:3, :] * _up(dxr, 1, 0.0, rows)
                       + cw_ref[1:2, :] * _up(dxr, 2, 0.0, rows) + cw_ref[0:1, :] * _up(dxr, 3, 0.0, rows)).astype(BF)
        for k in range(4):
            xs = x if k == 3 else _down(x, 3 - k, 0.0, rows)
            dcw_ref[k:k + 1, :] = jnp.sum(dxr * xs, axis=0, keepdims=True)

    C = nb * LANES
    seq = jax.ShapeDtypeStruct((T, C), BF)
    vec = jax.ShapeDtypeStruct((1, C), F32)
    mat = jax.ShapeDtypeStruct((nb, LANES, LANES), F32)
    vspec = pl.BlockSpec((1, LANES), lambda g: (0, g))
    mspec = pl.BlockSpec((None, LANES, LANES), lambda g: (g, 0, 0))
    sspec = pl.BlockSpec((T, LANES), lambda g: (0, g))
    return pl.pallas_call(
        body, name="lru_bwd", grid=(nb,),
        in_specs=[sspec, sspec] + _lru_specs(T, nb, gate_blk0, gate_blk0 + nb),
        out_specs=[sspec, sspec, pl.BlockSpec((4, LANES), lambda g: (0, g)), vspec, mspec, vspec, mspec, vspec, vspec],
        out_shape=[seq, seq, jax.ShapeDtypeStruct((4, C), F32), vec, mat, vec, mat, vec, vec],
        compiler_params=_params("parallel"),
    )(dy, hs, p, p, cw, cb, wa, ba, wx, bx, lam)


ATTN_BLOCK = 1024


def _causal(shape, transposed=False):
    r = lax.broadcasted_iota(jnp.int32, shape, 0)
    c = lax.broadcasted_iota(jnp.int32, shape, 1)
    return r <= c if transposed else c <= r


def _causal_pairs(n, by_query):
    if by_query:
        pairs = [(i, j) for i in range(n) for j in range(i + 1)]
    else:
        pairs = [(i, j) for j in range(n) for i in range(j, n)]
    return jnp.asarray([a for a, _ in pairs], jnp.int32), jnp.asarray([b for _, b in pairs], jnp.int32)


def _attn_fwd(p, cq, ck, nh, q_blk0):
    T = p.shape[0]
    tq = _tile(T, ATTN_BLOCK)
    nq = T // tq
    scale = LANES ** -0.5

    qi, kj = _causal_pairs(nq, by_query=True)

    def body(qi_ref, kj_ref, q_ref, k_ref, v_ref, cq_ref, ck_ref, o_ref, lse_ref, m_ref, l_ref, acc_ref):
        i, j = qi_ref[pl.program_id(1)], kj_ref[pl.program_id(1)]

        @pl.when(j == 0)
        def _():
            m_ref[...] = jnp.full(m_ref.shape, -jnp.inf, F32)
            l_ref[...] = jnp.zeros(l_ref.shape, F32)
            acc_ref[...] = jnp.zeros(acc_ref.shape, F32)

        def block(diagonal):
            s = lax.dot_general(q_ref[...], k_ref[...], (NT, ((), ())), preferred_element_type=F32) * scale
            s = s + cq_ref[...] - ck_ref[...]
            if diagonal:
                s = jnp.where(_causal(s.shape), s, -jnp.inf)
            m_new = jnp.maximum(m_ref[...], jnp.max(s, axis=-1, keepdims=True))
            alpha = jnp.exp(m_ref[...] - m_new)
            pr = jnp.exp(s - m_new)
            l_ref[...] = alpha * l_ref[...] + jnp.sum(pr, axis=-1, keepdims=True)
            acc_ref[...] = alpha * acc_ref[...] + jnp.dot(pr.astype(BF), v_ref[...], preferred_element_type=F32)
            m_ref[...] = m_new

        pl.when(j < i)(lambda: block(False))

        @pl.when(j == i)
        def _():
            block(True)
            o_ref[...] = acc_ref[...] / l_ref[...]
            lse_ref[...] = m_ref[...] + jnp.log(l_ref[...])

    def kv(off):
        return pl.BlockSpec((tq, LANES), lambda h, s, qi, kj: (kj[s], q_blk0 + off * nh + h))

    return pl.pallas_call(
        body, name="attn_fwd",
        grid_spec=pltpu.PrefetchScalarGridSpec(
            num_scalar_prefetch=2, grid=(nh, qi.shape[0]),
            in_specs=[pl.BlockSpec((tq, LANES), lambda h, s, qi, kj: (qi[s], q_blk0 + h)), kv(1), kv(2),
                      pl.BlockSpec((None, tq, 1), lambda h, s, qi, kj: (h, qi[s], 0)),
                      pl.BlockSpec((None, 1, tq), lambda h, s, qi, kj: (h, 0, kj[s]))],
            out_specs=[pl.BlockSpec((tq, LANES), lambda h, s, qi, kj: (qi[s], h)),
                       pl.BlockSpec((None, tq, 1), lambda h, s, qi, kj: (h, qi[s], 0))],
            scratch_shapes=[pltpu.VMEM((tq, 1), F32), pltpu.VMEM((tq, 1), F32), pltpu.VMEM((tq, LANES), F32)]),
        out_shape=[jax.ShapeDtypeStruct((T, nh * LANES), F32), jax.ShapeDtypeStruct((nh, T, 1), F32)],
        compiler_params=_params("parallel", "arbitrary"),
    )(qi, kj, p, p, p, cq, ck)


def _attn_bwd_q(p, cq, ck, lse, do, o, nh, q_blk0):
    T = p.shape[0]
    tq = _tile(T, ATTN_BLOCK)
    nq = T // tq
    scale = LANES ** -0.5

    qi, kj = _causal_pairs(nq, by_query=True)

    def body(qi_ref, kj_ref, q_ref, k_ref, v_ref, cq_ref, ck_ref, lse_ref, do_ref, o_ref, dq_ref, dl_ref, dr_ref,
             acc_ref):
        i, j = qi_ref[pl.program_id(1)], kj_ref[pl.program_id(1)]

        @pl.when(j == 0)
        def _():
            dl_ref[...] = jnp.sum(do_ref[...] * o_ref[...], axis=-1, keepdims=True)
            dr_ref[...] = jnp.zeros(dr_ref.shape, F32)
            acc_ref[...] = jnp.zeros(acc_ref.shape, F32)

        def block(diagonal):
            s = lax.dot_general(q_ref[...], k_ref[...], (NT, ((), ())), preferred_element_type=F32) * scale
            s = s + cq_ref[...] - ck_ref[...]
            pr = jnp.exp(s - lse_ref[...])
            if diagonal:
                pr = jnp.where(_causal(s.shape), pr, 0.0)
            dp = lax.dot_general(do_ref[...].astype(BF), v_ref[...], (NT, ((), ())), preferred_element_type=F32)
            ds = pr * (dp - dl_ref[...])
            dr_ref[...] += jnp.sum(ds, axis=-1, keepdims=True)
            acc_ref[...] += jnp.dot(ds.astype(BF), k_ref[...], preferred_element_type=F32)

        pl.when(j < i)(lambda: block(False))

        @pl.when(j == i)
        def _():
            block(True)
            dq_ref[...] = (acc_ref[...] * scale).astype(BF)

    def kv(off):
        return pl.BlockSpec((tq, LANES), lambda h, s, qi, kj: (kj[s], q_blk0 + off * nh + h))

    col = pl.BlockSpec((None, tq, 1), lambda h, s, qi, kj: (h, qi[s], 0))
    head = pl.BlockSpec((tq, LANES), lambda h, s, qi, kj: (qi[s], h))
    return pl.pallas_call(
        body, name="attn_bwd_q",
        grid_spec=pltpu.PrefetchScalarGridSpec(
            num_scalar_prefetch=2, grid=(nh, qi.shape[0]),
            in_specs=[pl.BlockSpec((tq, LANES), lambda h, s, qi, kj: (qi[s], q_blk0 + h)), kv(1), kv(2), col,
                      pl.BlockSpec((None, 1, tq), lambda h, s, qi, kj: (h, 0, kj[s])), col, head, head],
            out_specs=[head, col, col],
            scratch_shapes=[pltpu.VMEM((tq, LANES), F32)]),
        out_shape=[jax.ShapeDtypeStruct((T, nh * LANES), BF), jax.ShapeDtypeStruct((nh, T, 1), F32),
                   jax.ShapeDtypeStruct((nh, T, 1), F32)],
        compiler_params=_params("parallel", "arbitrary"),
    )(qi, kj, p, p, p, cq, ck, lse, do, o)


def _attn_bwd_kv(p, cq_row, ck_col, lse_row, delta_row, do, nh, q_blk0):
    T = p.shape[0]
    tk = _tile(T, ATTN_BLOCK)
    nk = T // tk
    scale = LANES ** -0.5

    qi, kj = _causal_pairs(nk, by_query=False)

    def body(qi_ref, kj_ref, q_ref, k_ref, v_ref, cq_ref, ck_ref, lse_ref, dl_ref, do_ref, dk_ref, dv_ref, dc_ref,
             dk_acc, dv_acc, dc_acc):
        i, j = qi_ref[pl.program_id(1)], kj_ref[pl.program_id(1)]

        @pl.when(i == j)
        def _():
            dk_acc[...] = jnp.zeros(dk_acc.shape, F32)
            dv_acc[...] = jnp.zeros(dv_acc.shape, F32)
            dc_acc[...] = jnp.zeros(dc_acc.shape, F32)

        def block(diagonal):
            st = lax.dot_general(k_ref[...], q_ref[...], (NT, ((), ())), preferred_element_type=F32) * scale
            st = st + cq_ref[...] - ck_ref[...]
            pt = jnp.exp(st - lse_ref[...])
            if diagonal:
                pt = jnp.where(_causal(st.shape, transposed=True), pt, 0.0)
            dob = do_ref[...].astype(BF)
            dv_acc[...] += jnp.dot(pt.astype(BF), dob, preferred_element_type=F32)
            dpt = lax.dot_general(v_ref[...], dob, (NT, ((), ())), preferred_element_type=F32)
            dst = pt * (dpt - dl_ref[...])
            dk_acc[...] += jnp.dot(dst.astype(BF), q_ref[...], preferred_element_type=F32)
            dc_acc[...] += jnp.sum(dst, axis=-1, keepdims=True)

        pl.when(i == j)(lambda: block(True))
        pl.when(i > j)(lambda: block(False))

        @pl.when(i == nk - 1)
        def _():
            dk_ref[...] = (dk_acc[...] * scale).astype(BF)
            dv_ref[...] = dv_acc[...].astype(BF)
            dc_ref[...] = dc_acc[...]

    def qside(blk):
        return pl.BlockSpec((tk, LANES), lambda h, s, qi, kj: (qi[s], blk + h))

    def kside(off):
        return pl.BlockSpec((tk, LANES), lambda h, s, qi, kj: (kj[s], q_blk0 + off * nh + h))

    row = pl.BlockSpec((None, 1, tk), lambda h, s, qi, kj: (h, 0, qi[s]))
    col = pl.BlockSpec((None, tk, 1), lambda h, s, qi, kj: (h, kj[s], 0))
    head = pl.BlockSpec((tk, LANES), lambda h, s, qi, kj: (kj[s], h))
    return pl.pallas_call(
        body, name="attn_bwd_kv",
        grid_spec=pltpu.PrefetchScalarGridSpec(
            num_scalar_prefetch=2, grid=(nh, qi.shape[0]),
            in_specs=[qside(q_blk0), kside(1), kside(2), row, col, row, row, qside(0)],
            out_specs=[head, head, col],
            scratch_shapes=[pltpu.VMEM((tk, LANES), F32), pltpu.VMEM((tk, LANES), F32), pltpu.VMEM((tk, 1), F32)]),
        out_shape=[jax.ShapeDtypeStruct((T, nh * LANES), BF), jax.ShapeDtypeStruct((T, nh * LANES), BF),
                   jax.ShapeDtypeStruct((nh, T, 1), F32)],
        compiler_params=_params("parallel", "arbitrary"),
    )(qi, kj, p, p, p, cq_row, ck_col, lse_row, delta_row, do)


def _mixer_dims(D):
    dc, da, dl = D // 4, D // 2, D // 4
    nh = da // LANES
    n_main = 3 * dc + 3 * da + 2 * dl
    return dc, da, dl, nh, n_main


def _pad_mix_w_in(wfull):
    D = wfull.shape[0]
    dc, da, dl, nh, n_main = _mixer_dims(D)
    a = 3 * dc + 3 * da
    return jnp.concatenate([wfull[:, :a], wfull[:, a + nh:], wfull[:, a:a + nh],
                            jnp.zeros((D, LANES - nh), wfull.dtype)], axis=1)


def _unpad_mix_w_in(wp):
    D = wp.shape[0]
    dc, da, dl, nh, n_main = _mixer_dims(D)
    a = 3 * dc + 3 * da
    return jnp.concatenate([wp[:, :a], wp[:, n_main:n_main + nh], wp[:, a:n_main]], axis=1)


def _head_cols(c, nh):
    t = jnp.transpose(c[:, :nh])
    return t[:, :, None], t[:, None, :]


def _mixer_forward(x, gain, wp, wo, sp):
    T, D = x.shape
    dc, da, dl, nh, n_main = _mixer_dims(D)
    nbc, nbl = dc // LANES, dl // LANES
    h = _rmsnorm(x, gain)
    tm = _tile(T, 512)
    tn = n_main // 4
    p = _mm("mix_in", (4, T // tm, 1),
            (h, (tm, D), lambda j, i, k: (i, 0)), (wp, (D, tn), lambda j, i, k: (0, j)),
            (jax.ShapeDtypeStruct((T, n_main), BF), (tm, tn), lambda j, i, k: (i, j)), NN)
    f = _mm("mix_in_fgate", (T // tm, 1, 1),
            (h, (tm, D), lambda i, j, k: (i, 0)), (wp, (D, LANES), lambda i, j, k: (0, n_main // LANES)),
            (jax.ShapeDtypeStruct((T, LANES), F32), (tm, LANES), lambda i, j, k: (i, 0)), NN)
    cum = _fgate_cum(f, sp["fgate_b"])
    cq, ck = _head_cols(cum, nh)
    yc = _conv_fwd(p, sp["conv_w"], nbc)
    q_blk0 = 3 * nbc
    ya, lse = _attn_fwd(p, cq, ck, nh, q_blk0)
    gate_blk0 = q_blk0 + 3 * nh
    yl, hs = _lru_fwd(p, sp["lru_conv_w"], sp["lru_conv_b"], sp["lru_w_a"], sp["lru_b_a"], sp["lru_w_x"],
                      sp["lru_b_x"], sp["lru_lambda"], nbl, gate_blk0)
    yn = _groupnorm(yc, ya, yl, sp["mix_out_norm"])
    xn = _mm("mix_out", (T // tm, 1, 1),
             (yn, (tm, D), lambda i, j, k: (i, 0)), (wo, (D, D), lambda i, j, k: (0, 0)),
             (jax.ShapeDtypeStruct((T, D), F32), (tm, D), lambda i, j, k: (i, 0)), NN,
             resid=(x, (tm, D), lambda i, j, k: (i, 0)))
    return xn, (x, h, p, f, cq, ck, yc, ya, lse, yl, hs, yn)


def _mixer_backward(dx, saved, gain, wp, wo, sp):
    x, h, p, f, cq, ck, yc, ya, lse, yl, hs, yn = saved
    T, D = x.shape
    dc, da, dl, nh, n_main = _mixer_dims(D)
    nbc, nbl = dc // LANES, dl // LANES
    q_blk0 = 3 * nbc
    gate_blk0 = q_blk0 + 3 * nh
    tm = _tile(T, 512)
    tnd, tnd2 = _tile(D, 512), _tile(D, 1024)
    dyn = _mm("mix_out_dy", (T // tm, 1, 1),
              (dx, (tm, D), lambda i, j, k: (i, 0)), (wo, (D, D), lambda i, j, k: (0, 0)),
              (jax.ShapeDtypeStruct((T, D), F32), (tm, D), lambda i, j, k: (i, 0)), NT)
    dwo = _mm("mix_out_dw", (D // tnd, D // tnd2, 1),
              (yn, (T, tnd), lambda i, j, k: (0, i)), (dx, (T, tnd2), lambda i, j, k: (0, j)),
              (jax.ShapeDtypeStruct((D, D), BF), (tnd, tnd2), lambda i, j, k: (i, j)), TN)
    dyc, dya, dyl, dgn = _groupnorm_bwd(dyn, yc, ya, yl, sp["mix_out_norm"])
    dcb, dcc, dcv, dconv_w = _conv_bwd(dyc, p, sp["conv_w"], nbc)
    dq, delta, drow = _attn_bwd_q(p, cq, ck, lse, dya, ya, nh, q_blk0)
    as_row = lambda t: t.reshape(nh, 1, T)
    dk, dv, dcol = _attn_bwd_kv(p, ck, cq, as_row(lse), as_row(delta), dya, nh, q_blk0)
    lanes = lambda t: jnp.pad(jnp.transpose(t.reshape(nh, T)), ((0, 0), (0, LANES - nh)))
    df, dfb = _fgate_cum_bwd(lanes(drow), lanes(dcol), f, sp["fgate_b"])
    dgate, dlx, dcw, dcbias, dwa, dba, dwx, dbx, dlam = _lru_bwd(
        dyl, hs, p, sp["lru_conv_w"], sp["lru_conv_b"], sp["lru_w_a"], sp["lru_b_a"], sp["lru_w_x"],
        sp["lru_b_x"], sp["lru_lambda"], nbl, gate_blk0)
    dp = jnp.concatenate([dcb, dcc, dcv, dq, dk, dv, dgate, dlx, df], axis=1)
    n_pad = n_main + LANES
    tn, tkp = n_pad // 5, n_pad // 3
    dwp = _mm("mix_in_dw", (n_pad // tn, D // tnd, 1),
              (h, (T, tnd), lambda j, i, k: (0, i)), (dp, (T, tn), lambda j, i, k: (0, j)),
              (jax.ShapeDtypeStruct((D, n_pad), BF), (tnd, tn), lambda j, i, k: (i, j)), TN)
    dh = _mm("mix_in_dh", (T // tm, 1, n_pad // tkp),
             (dp, (tm, tkp), lambda i, j, k: (i, k)), (wp, (D, tkp), lambda i, j, k: (0, k)),
             (jax.ShapeDtypeStruct((T, D), F32), (tm, D), lambda i, j, k: (i, 0)), NT)
    dxn, dgain = _rmsnorm_bwd(dh, x, gain, dx)
    small = {"norm_mix": dgain, "mix_out_norm": dgn, "conv_w": dconv_w, "fgate_b": dfb[:, :nh],
             "lru_conv_w": dcw, "lru_conv_b": dcbias, "lru_w_a": dwa, "lru_b_a": dba, "lru_w_x": dwx,
             "lru_b_x": dbx, "lru_lambda": dlam}
    return dxn, dwp, dwo, small


def _position():
    x, y, c = lax.axis_index("x"), lax.axis_index("y"), lax.axis_index("c")
    return x, y, c, [(1 - x, y), (x, 1 - y), (1 - x, 1 - y)]


def _all_gather(name, xs):
    n = len(xs)

    def body(*refs):
        x_refs, o_refs = refs[:n], refs[n:2 * n]
        send_sems, recv_sems, local_sems = refs[2 * n:]
        x, y, c, chips = _position()
        me, sibling = (x, y, c), (x, y, 1 - c)

        def copy(t, k, block, to, src=None):
            dst = o_refs[t].at[4 * block[0] + 2 * block[1] + block[2]]
            return pltpu.make_async_remote_copy(
                src_ref=dst if src is None else src, dst_ref=dst, send_sem=send_sems.at[t, k],
                recv_sem=recv_sems.at[t, k], device_id=to, device_id_type=MESH)

        mine = [pltpu.make_async_copy(x_refs[t], o_refs[t].at[4 * x + 2 * y + c], local_sems.at[t]) for t in range(n)]
        for cp in mine:
            cp.start()
        first = []
        for t in range(n):
            first.append(copy(t, 0, me, sibling, src=x_refs[t]))
            first += [copy(t, 1 + j, me, (*chip, c), src=x_refs[t]) for j, chip in enumerate(chips)]
        for cp in first:
            cp.start()
        passed = []
        for j, chip in enumerate(chips):
            for t in range(n):
                copy(t, 1 + j, (*chip, c), me).wait_recv()
                passed.append(copy(t, 4 + j, (*chip, c), sibling))
                passed[-1].start()
        for t in range(n):
            copy(t, 0, sibling, me).wait_recv()
            for j, chip in enumerate(chips):
                copy(t, 4 + j, (*chip, 1 - c), me).wait_recv()
        for cp in first + passed:
            cp.wait_send()
        for cp in mine:
            cp.wait()

    return pl.pallas_call(
        body, name=name,
        in_specs=[_any()] * n, out_specs=[_any()] * n,
        out_shape=[jax.ShapeDtypeStruct((N_DEV,) + v.shape, v.dtype) for v in xs],
        scratch_shapes=[pltpu.SemaphoreType.DMA((n, 7)), pltpu.SemaphoreType.DMA((n, 7)),
                        pltpu.SemaphoreType.DMA((n,))],
    )(*xs)


def _add_sibling(g4, recv, c_idx):
    _, _, R, C = g4.shape
    tr = _tile(R, 256)

    def body(c_ref, g_ref, r_ref, o_ref):
        o_ref[...] = (g_ref[...].astype(F32) + r_ref[...].astype(F32)).astype(o_ref.dtype)

    return pl.pallas_call(
        body, name="rs_add_sibling",
        grid_spec=pltpu.PrefetchScalarGridSpec(
            num_scalar_prefetch=1, grid=(4, R // tr),
            in_specs=[pl.BlockSpec((None, None, tr, C), lambda q, i, c_ref: (q, c_ref[0], i, 0)),
                      pl.BlockSpec((None, None, tr, C), lambda q, i, c_ref: (q, 0, i, 0))],
            out_specs=pl.BlockSpec((None, tr, C), lambda q, i, c_ref: (q, i, 0))),
        out_shape=jax.ShapeDtypeStruct((4, R, C), g4.dtype),
        compiler_params=_params("parallel", "parallel"),
    )(c_idx, g4, recv)


def _hbm_spec():
    return pl.BlockSpec(memory_space=pltpu.HBM)


def _sem_spec():
    return pl.BlockSpec(memory_space=pltpu.SEMAPHORE)


def _side_effects():
    return pltpu.CompilerParams(has_side_effects=pltpu.SideEffectType.DATAFLOW_SIDE_EFFECTING)


def _in_hbm(v):
    return pltpu.with_memory_space_constraint(v, pltpu.HBM)


def _split_start(name, srcs, lands, n_copies, copies_of, after):
    n = len(srcs)
    ns = n * n_copies

    def body(*refs):
        src_refs, land_refs = refs[:n], refs[n:2 * n]
        send_sems, recv_sems = refs[2 * n + 1:2 * n + 1 + ns], refs[2 * n + 1 + ns:2 * n + 1 + 2 * ns]
        token = refs[-1]
        for t in range(n):
            for k, (s, d, to) in enumerate(copies_of(t, src_refs[t], land_refs[t])):
                pltpu.make_async_remote_copy(src_ref=s, dst_ref=d, send_sem=send_sems[t * n_copies + k],
                                             recv_sem=recv_sems[t * n_copies + k], device_id=to,
                                             device_id_type=MESH).start()
        token[...] = jnp.zeros(token.shape, token.dtype)

    thru = [pltpu.HBM(v.shape, v.dtype) for v in list(srcs) + list(lands)]
    out = pl.pallas_call(
        body, name=name,
        in_specs=[_hbm_spec()] * (2 * n) + [_any()],
        out_specs=[_sem_spec()] * (2 * ns) + [_hbm_spec()] * (2 * n) + [pl.BlockSpec(memory_space=pltpu.VMEM)],
        out_shape=[pltpu.SemaphoreType.DMA(())] * (2 * ns) + thru + [jax.ShapeDtypeStruct((8, LANES), F32)],
        input_output_aliases={i: 2 * ns + i for i in range(2 * n)},
        compiler_params=_side_effects(),
    )(*[_in_hbm(v) for v in list(srcs) + list(lands)], after)
    return out[:ns], out[ns:2 * ns], out[2 * ns:2 * ns + n], out[2 * ns + n:2 * ns + 2 * n], out[-1]


def _split_wait(name, send_sems, recv_sems, srcs, lands, n_copies, waits_of, after):
    n = len(srcs)
    ns = n * n_copies

    def body(*refs):
        src_refs, land_refs = refs[:n], refs[n:2 * n]
        send_refs, recv_refs = refs[2 * n:2 * n + ns], refs[2 * n + ns:2 * n + 2 * ns]
        x, y, c, _ = _position()
        for t in range(n):
            for k, (s, d) in enumerate(waits_of(t, src_refs[t], land_refs[t])):
                cp = pltpu.make_async_remote_copy(src_ref=s, dst_ref=d, send_sem=send_refs[t * n_copies + k],
                                                  recv_sem=recv_refs[t * n_copies + k], device_id=(x, y, 1 - c),
                                                  device_id_type=MESH)
                cp.wait_send()
                cp.wait_recv()

    out = pl.pallas_call(
        body, name=name,
        in_specs=[_hbm_spec()] * (2 * n) + [_sem_spec()] * (2 * ns) + [_any()],
        out_specs=[_hbm_spec()] * (2 * n),
        out_shape=[pltpu.HBM(v.shape, v.dtype) for v in list(srcs) + list(lands)],
        input_output_aliases={i: i for i in range(2 * n)},
        compiler_params=_side_effects(),
    )(*srcs, *lands, *send_sems, *recv_sems, after)
    return out[:n], out[n:]


def _block_of(px, py, pc):
    return 4 * px + 2 * py + pc


def _gather_phase1_start(name, xs, after):
    lands = [lax.empty((N_DEV,) + v.shape, v.dtype) for v in xs]

    def copies_of(t, x_ref, land_ref):
        x, y, c, chips = _position()
        dst = land_ref.at[_block_of(x, y, c)]
        return [(x_ref, dst, (x, y, 1 - c))] + [(x_ref, dst, (*chip, c)) for chip in chips]

    return _split_start(name, xs, lands, 4, copies_of, after)


def _gather_phase1_wait(name, started, after):
    send_sems, recv_sems, xs, lands, _ = started

    def waits_of(t, x_ref, land_ref):
        x, y, c, chips = _position()
        return [(x_ref, land_ref.at[_block_of(x, y, 1 - c)])] + [(x_ref, land_ref.at[_block_of(*chip, c)])
                                                                  for chip in chips]

    return _split_wait(name, send_sems, recv_sems, xs, lands, 4, waits_of, after)


def _gather_phase2_start(name, lands, after):
    keep = [lax.empty((8, LANES), v.dtype) for v in lands]

    def copies_of(t, _, land_ref):
        x, y, c, chips = _position()
        return [(land_ref.at[_block_of(*chip, c)], land_ref.at[_block_of(*chip, c)], (x, y, 1 - c)) for chip in chips]

    return _split_start(name, keep, lands, 3, copies_of, after)


def _gather_phase2_wait(name, started, after):
    send_sems, recv_sems, keep, lands, _ = started

    def waits_of(t, _, land_ref):
        x, y, c, chips = _position()
        return [(land_ref.at[_block_of(*chip, c)], land_ref.at[_block_of(*chip, 1 - c)]) for chip in chips]

    return _split_wait(name, send_sems, recv_sems, keep, lands, 3, waits_of, after)[1]


def _place_block(name, land, src, block_idx):
    R, C = src.shape
    tr = _tile(R, 512)

    def body(b_ref, land_ref, s_ref, o_ref):
        o_ref[...] = s_ref[...]

    return pl.pallas_call(
        body, name=name,
        grid_spec=pltpu.PrefetchScalarGridSpec(
            num_scalar_prefetch=1, grid=(R // tr,),
            in_specs=[_any(), pl.BlockSpec((tr, C), lambda i, b: (i, 0))],
            out_specs=pl.BlockSpec((None, tr, C), lambda i, b: (b[0], i, 0))),
        out_shape=jax.ShapeDtypeStruct(land.shape, land.dtype),
        input_output_aliases={1: 0},
        compiler_params=_params("parallel"),
    )(block_idx, land, src)


def _sibling_start(name, gs, after):
    g4 = [g.reshape((4, 2) + g.shape[1:]) for g in gs]
    lands = [lax.empty((4, 1) + g.shape[2:], g.dtype) for g in g4]

    def copies_of(t, g_ref, land_ref):
        x, y, c, _ = _position()
        return [(g_ref.at[:, pl.ds(1 - c, 1)], land_ref, (x, y, 1 - c))]

    return _split_start(name, g4, lands, 1, copies_of, after)


def _sibling_wait(name, started, after):
    send_sems, recv_sems, g4, lands, _ = started

    def waits_of(t, g_ref, land_ref):
        x, y, c, _ = _position()
        return [(g_ref.at[:, pl.ds(1 - c, 1)], land_ref)]

    return _split_wait(name, send_sems, recv_sems, g4, lands, 1, waits_of, after)


def _scatter_chips_start(name, ps, after):
    lands = [lax.empty((3,) + v.shape[1:], v.dtype) for v in ps]

    def copies_of(t, p_ref, land_ref):
        x, y, c, chips = _position()
        return [(p_ref.at[2 * chip[0] + chip[1]], land_ref.at[k], (*chip, c)) for k, chip in enumerate(chips)]

    return _split_start(name, ps, lands, 3, copies_of, after)


def _scatter_chips_wait(name, started, after):
    send_sems, recv_sems, ps, lands, _ = started

    def waits_of(t, p_ref, land_ref):
        x, y, c, chips = _position()
        return [(p_ref.at[2 * chip[0] + chip[1]], land_ref.at[k]) for k, chip in enumerate(chips)]

    return _split_wait(name, send_sems, recv_sems, ps, lands, 3, waits_of, after)


def _scatter_chips_after_sibling(tag, sibling, c_idx, after):
    g4, recv = _sibling_wait("rs_sibling_wait_" + tag, sibling, after)
    ps = [_add_sibling(g, r, c_idx) for g, r in zip(g4, recv)]
    return _scatter_chips_start("rs_chips_start_" + tag, ps, ps[0])


def _sum_devices(parts):
    _, R, C = parts.shape
    tr = _tile(R, 512)

    def body(p_ref, o_ref):
        acc = p_ref[0]
        for d in range(1, N_DEV):
            acc = acc + p_ref[d]
        o_ref[...] = acc

    return pl.pallas_call(
        body, name="sum_devices", grid=(R // tr,),
        in_specs=[pl.BlockSpec((N_DEV, tr, C), lambda i: (0, i, 0))],
        out_specs=pl.BlockSpec((tr, C), lambda i: (i, 0)),
        out_shape=jax.ShapeDtypeStruct((R, C), F32),
        compiler_params=_params("parallel"),
    )(parts)


def _adam_math(w, g, m, v):
    m = ADAM_B1 * m + (1.0 - ADAM_B1) * g
    v = ADAM_B2 * v + (1.0 - ADAM_B2) * (g * g)
    m_hat = m / (1.0 - ADAM_B1 ** ADAM_STEP)
    v_hat = v / (1.0 - ADAM_B2 ** ADAM_STEP)
    return -ADAM_LR * (m_hat / (jnp.sqrt(v_hat) + ADAM_EPS) + ADAM_WD * w), m, v


def _adam_layer(name, layer, w, m, v, own, landed, chip_idx, prev, after):
    L, R, C = w.shape
    tr = _tile(R, 128)
    slab = pl.BlockSpec((None, tr, C), lambda i, q: (layer, i, 0))

    def body(q_ref, w_ref, m_ref, v_ref, own_ref, land_ref, after_ref, *rest):
        g_ref, d_ref, nm_ref, nv_ref = rest[-4:]
        g = land_ref[0].astype(F32)
        for k in range(1, 3):
            g = g + land_ref[k].astype(F32)
        g = g + own_ref[...].astype(F32)
        d, nm, nv = _adam_math(w_ref[...], g, m_ref[...], v_ref[...])
        g_ref[...] = g
        d_ref[...] = d
        nm_ref[...] = nm
        nv_ref[...] = nv

    n_prev = 0 if prev is None else 4
    out = jax.ShapeDtypeStruct((L, R, C), F32)
    return pl.pallas_call(
        body, name=name,
        grid_spec=pltpu.PrefetchScalarGridSpec(
            num_scalar_prefetch=1, grid=(R // tr,),
            in_specs=[slab, slab, slab, pl.BlockSpec((None, tr, C), lambda i, q: (q[0], i, 0)),
                      pl.BlockSpec((3, tr, C), lambda i, q: (0, i, 0)), _any()] + [_any()] * n_prev,
            out_specs=[slab] * 4),
        out_shape=[out] * 4,
        input_output_aliases={7 + k: k for k in range(n_prev)},
        compiler_params=_params("parallel"),
    )(chip_idx, w, m, v, own, landed, after, *(prev or ()))


def _adam_small(w, g, m, v):
    R, C = w.shape
    tr = _tile(R, 512)
    spec = pl.BlockSpec((tr, C), lambda i: (i, 0))

    def body(w_ref, g_ref, m_ref, v_ref, d_ref, nm_ref, nv_ref):
        d, nm, nv = _adam_math(w_ref[...], g_ref[...], m_ref[...], v_ref[...])
        d_ref[...] = d
        nm_ref[...] = nm
        nv_ref[...] = nv

    out = jax.ShapeDtypeStruct((R, C), F32)
    return pl.pallas_call(
        body, name="adam_small", grid=(R // tr,),
        in_specs=[spec] * 4, out_specs=[spec] * 3, out_shape=[out] * 3,
        compiler_params=_params("parallel"),
    )(w, g, m, v)


def _pack(arrays, row_multiple=512):
    flat = jnp.concatenate([a.reshape(-1).astype(F32) for a in arrays])
    per = row_multiple * LANES
    total = -(-flat.shape[0] // per) * per
    return jnp.pad(flat, (0, total - flat.shape[0])).reshape(total // LANES, LANES)


def _unpack(packed, shapes):
    flat = packed.reshape(-1)
    out, off = [], 0
    for s in shapes:
        n = math.prod(s)
        out.append(flat[off:off + n].reshape(s))
        off += n
    return out


BIG = ["ffn1_w_in", "ffn1_w_out", "mix_w_in", "mix_w_out", "ffn2_w_in", "ffn2_w_out"]
SHARDED_SMALL = ["conv_w", "lru_conv_w"]
WEIGHTS = ["norm_ffn1", "ffn1_w_in", "ffn1_w_out", "norm_mix", "mix_w_in", "conv_w", "fgate_b", "lru_conv_w",
           "lru_conv_b", "lru_w_a", "lru_b_a", "lru_w_x", "lru_b_x", "lru_lambda", "mix_out_norm", "mix_w_out",
           "norm_ffn2", "ffn2_w_in", "ffn2_w_out", "final_norm"]
REPLICATED = [n for n in WEIGHTS if n not in BIG and n not in SHARDED_SMALL]


def _step(x, target, w, m, v):
    L = w["norm_ffn1"].shape[0]
    T, D = x.shape[1], x.shape[2]
    dc, da, dl, nh, n_main = _mixer_dims(D)
    xi, yi, ci = lax.axis_index("x"), lax.axis_index("y"), lax.axis_index("c")
    me = 4 * xi + 2 * yi + ci
    c_idx = ci.astype(jnp.int32).reshape(1)

    taps = jnp.concatenate([w["conv_w"].reshape(-1), w["lru_conv_w"].reshape(-1)])
    taps = jnp.pad(taps, (0, (-taps.shape[0]) % (8 * LANES))).reshape(-1, LANES)
    def shards_of(l):
        return [w[n][l].astype(BF) for n in BIG] + ([taps] if l == 0 else [])

    me_idx = me.astype(jnp.int32).reshape(1)
    chip_idx = (2 * xi + yi).astype(jnp.int32).reshape(1)

    def place_own(tag, lands, own):
        return [_place_block("gather_own_" + tag, land, src, me_idx) for land, src in zip(lands, own)]

    gathered = [None] * L
    phase1 = [None] * L
    shards0 = shards_of(0)
    first_start = _gather_phase1_start("gather_p1_start_0_first", shards0[:2] + [taps], taps)
    rest_start = _gather_phase1_start("gather_p1_start_0_rest", shards0[2:len(BIG)], first_start[-1])
    own, landed = _gather_phase1_wait("gather_p1_wait_0_first", first_start, rest_start[-1])
    phase2 = _gather_phase2_start("gather_p2_start_0_first", landed, landed[0])
    landed = _gather_phase2_wait("gather_p2_wait_0_first", phase2, phase2[-1])
    first0 = place_own("0_first", landed, own)
    taps_all = first0[2].reshape(N_DEV, -1)
    n_cw = math.prod(w["conv_w"].shape)
    ch = w["conv_w"].shape[-1]
    conv_w_full = jnp.moveaxis(taps_all[:, :n_cw].reshape((N_DEV,) + w["conv_w"].shape), 0, -2).reshape(L, 3, N_DEV * ch)
    n_lw = math.prod(w["lru_conv_w"].shape)
    lru_conv_w_full = jnp.moveaxis(taps_all[:, n_cw:n_cw + n_lw].reshape((N_DEV,) + w["lru_conv_w"].shape), 0, -2
                                   ).reshape(L, 4, N_DEV * ch)

    def layer_weights(l):
        g = dict(zip(BIG, gathered[l]))
        F = g["ffn1_w_out"].shape[1] * N_DEV
        wmix = jnp.transpose(g["mix_w_in"], (1, 0, 2)).reshape(D, -1)
        return {"ffn1_w_in": g["ffn1_w_in"], "ffn1_w_out": g["ffn1_w_out"].reshape(F, D),
                "wp": _pad_mix_w_in(wmix), "wo": g["mix_w_out"].reshape(D, D),
                "ffn2_w_in": g["ffn2_w_in"], "ffn2_w_out": g["ffn2_w_out"].reshape(F, D)}

    def small_params(l):
        return {"fgate_b": jnp.pad(w["fgate_b"][l], (0, LANES - nh)).reshape(1, LANES),
                "conv_w": conv_w_full[l], "lru_conv_w": lru_conv_w_full[l],
                "lru_conv_b": w["lru_conv_b"][l].reshape(1, dl), "lru_w_a": w["lru_w_a"][l],
                "lru_b_a": w["lru_b_a"][l].reshape(1, dl), "lru_w_x": w["lru_w_x"][l],
                "lru_b_x": w["lru_b_x"][l].reshape(1, dl), "lru_lambda": w["lru_lambda"][l].reshape(1, dl),
                "mix_out_norm": w["mix_out_norm"][l].reshape(1, D)}

    gain = lambda n, l, token=None: w[n][l].reshape(1, D) + (0.0 if token is None else token[0:1, 0:1])

    xc = x[0]
    saved, lw, sps = [], [], []
    for l in range(L):
        w_in1, w_out1 = first0[:2] if l == 0 else gathered[l][:2]
        xc, s1 = _ffn_forward(xc, gain("norm_ffn1", l), w_in1, w_out1.reshape(-1, D))
        if l == 0:
            own, landed = _gather_phase1_wait("gather_p1_wait_0_rest", rest_start, xc)
            phase2 = _gather_phase2_start("gather_p2_start_0_rest", landed, landed[0])
            order_token = phase2[-1]
            if L > 1:
                phase1[1] = _gather_phase1_start("gather_p1_start_1", shards_of(1), phase2[-1])
                order_token = phase1[1][-1]
            landed = _gather_phase2_wait("gather_p2_wait_0_rest", phase2, order_token)
            gathered[0] = first0[:2] + place_own("0_rest", landed, own)
        lw.append(layer_weights(l))
        sps.append(small_params(l))
        xc, s2 = _mixer_forward(xc, gain("norm_mix", l), lw[l]["wp"], lw[l]["wo"], sps[l])
        order_token = None
        if l + 1 < L:
            own, landed = _gather_phase1_wait("gather_p1_wait_%d" % (l + 1), phase1[l + 1], xc)
            phase2 = _gather_phase2_start("gather_p2_start_%d" % (l + 1), landed, landed[0])
            order_token = phase2[-1]
            if l + 2 < L:
                phase1[l + 2] = _gather_phase1_start("gather_p1_start_%d" % (l + 2), shards_of(l + 2), phase2[-1])
                order_token = phase1[l + 2][-1]
        xc, s3 = _ffn_forward(xc, gain("norm_ffn2", l, order_token), lw[l]["ffn2_w_in"], lw[l]["ffn2_w_out"])
        if l + 1 < L:
            landed = _gather_phase2_wait("gather_p2_wait_%d" % (l + 1), phase2, xc)
            gathered[l + 1] = place_own(str(l + 1), landed, own)
        saved.append((s1, s2, s3))
    loss_part, dx, d_final = _loss_head(xc, w["final_norm"].reshape(1, D), target[0])

    small_grads = [None] * L
    parts = [None] * L
    sibling = [None] * L
    scatter = [None] * L
    scatter_early = []
    for l in reversed(range(L)):
        s1, s2, s3 = saved[l]
        dh, dw3_2, dwout_2 = _ffn_backward(dx, s3, lw[l]["ffn2_w_in"], lw[l]["ffn2_w_out"])
        dx, dg3 = _rmsnorm_bwd(dh, s3[0], gain("norm_ffn2", l), dx)
        sp_l = sps[l]
        if l + 1 < L:
            scatter[l + 1] = _scatter_chips_after_sibling(str(l + 1), sibling[l + 1], c_idx, dx)
            sp_l = dict(sp_l, mix_out_norm=sp_l["mix_out_norm"] + scatter[l + 1][-1][0:1, 0:1])
        dx, dwp, dwo, sg = _mixer_backward(dx, s2, gain("norm_mix", l), lw[l]["wp"], lw[l]["wo"], sp_l)
        dmix = jnp.transpose(_unpad_mix_w_in(dwp).reshape(D, N_DEV, -1), (1, 0, 2))
        grads = [dmix, dwo.reshape(N_DEV, D // N_DEV, D), dw3_2, dwout_2]
        between = None
        if l == 0:
            early = _sibling_start("rs_sibling_start_0_early", grads, dwp)
            grads = []

            def between(dgu):
                scatter_early.append(_scatter_chips_after_sibling("0_early", early, c_idx, dgu))
                return scatter_early[0][-1]

        dh, dw3_1, dwout_1 = _ffn_backward(dx, s1, lw[l]["ffn1_w_in"], lw[l]["ffn1_w_out"], between)
        if l + 1 < L:
            parts[l + 1] = _scatter_chips_wait("rs_chips_wait_%d" % (l + 1), scatter[l + 1], dh)
        sibling[l] = _sibling_start("rs_sibling_start_%d" % l, [dw3_1, dwout_1] + grads, dh)
        dx, dg1 = _rmsnorm_bwd(dh, s1[0], gain("norm_ffn1", l, sibling[l][-1]), dx)
        sg["norm_ffn2"], sg["norm_ffn1"] = dg3, dg1
        small_grads[l] = sg

    out_g, out_d, out_m, out_v = {}, {}, {}, {}
    prev = {n: None for n in BIG}

    def adam_big(l, after):
        own, landed = parts[l]
        for t, n in enumerate(BIG):
            prev[n] = _adam_layer("adam_%s_%d" % (n, l), l, w[n], m[n], v[n], own[t], landed[t], chip_idx, prev[n], after)

    small_names = REPLICATED + SHARDED_SMALL
    partial = []
    for n in small_names:
        if n == "final_norm":
            partial.append(d_final)
        else:
            partial.append(jnp.stack([small_grads[l][n].reshape(w[n].shape[1:]) if n not in SHARDED_SMALL
                                      else small_grads[l][n] for l in range(L)]))
    partial.append(loss_part[0, :1])
    packed = _pack(partial)
    small_gathered = _all_gather("gather_small_grads", [packed])[0]
    scatter[0] = _scatter_chips_after_sibling("0", sibling[0], c_idx, small_gathered)
    for l in reversed(range(1, L)):
        adam_big(l, scatter[0][-1])
    summed = _sum_devices(small_gathered)
    full_shapes = [w[n].shape for n in REPLICATED] + [(L, 3, N_DEV * ch), (L, 4, N_DEV * ch), (1,)]
    full = _unpack(summed, full_shapes)
    loss = full[-1][0]
    g_small = dict(zip(small_names, full[:-1]))
    for n in SHARDED_SMALL:
        g_small[n] = lax.dynamic_slice_in_dim(g_small[n], me * ch, ch, axis=2)
    shapes = [w[n].shape for n in small_names]
    d_s, m_s, v_s = _adam_small(_pack([w[n] for n in small_names]), _pack([g_small[n] for n in small_names]),
                                _pack([m[n] for n in small_names]), _pack([v[n] for n in small_names]))
    for n, d_, m_, v_ in zip(small_names, _unpack(d_s, shapes), _unpack(m_s, shapes), _unpack(v_s, shapes)):
        out_g[n], out_d[n], out_m[n], out_v[n] = g_small[n], d_, m_, v_

    behind = d_s if L == 1 else prev[BIG[-1]][0]
    own_early, landed_early = _scatter_chips_wait("rs_chips_wait_0_early", scatter_early[0], behind)
    own_late, landed_late = _scatter_chips_wait("rs_chips_wait_0", scatter[0], behind)
    parts[0] = (list(own_late) + list(own_early), list(landed_late) + list(landed_early))
    adam_big(0, d_s)
    for n in BIG:
        out_g[n], out_d[n], out_m[n], out_v[n] = prev[n]

    return (loss, dx[None], *[out_g[n] for n in WEIGHTS], *[out_d[n] for n in WEIGHTS],
            *[out_m[n] for n in WEIGHTS], *[out_v[n] for n in WEIGHTS])


def kernel(x, norm_ffn1, ffn1_w_in, ffn1_w_out, norm_mix, mix_w_in, conv_w, fgate_b, lru_conv_w, lru_conv_b, lru_w_a, lru_b_a, lru_w_x, lru_b_x, lru_lambda, mix_out_norm, mix_w_out, norm_ffn2, ffn2_w_in, ffn2_w_out, final_norm, loss_target, m_norm_ffn1, m_ffn1_w_in, m_ffn1_w_out, m_norm_mix, m_mix_w_in, m_conv_w, m_fgate_b, m_lru_conv_w, m_lru_conv_b, m_lru_w_a, m_lru_b_a, m_lru_w_x, m_lru_b_x, m_lru_lambda, m_mix_out_norm, m_mix_w_out, m_norm_ffn2, m_ffn2_w_in, m_ffn2_w_out, m_final_norm, v_norm_ffn1, v_ffn1_w_in, v_ffn1_w_out, v_norm_mix, v_mix_w_in, v_conv_w, v_fgate_b, v_lru_conv_w, v_lru_conv_b, v_lru_w_a, v_lru_b_a, v_lru_w_x, v_lru_b_x, v_lru_lambda, v_mix_out_norm, v_mix_w_out, v_norm_ffn2, v_ffn2_w_in, v_ffn2_w_out, v_final_norm):
    w = dict(norm_ffn1=norm_ffn1, ffn1_w_in=ffn1_w_in, ffn1_w_out=ffn1_w_out, norm_mix=norm_mix, mix_w_in=mix_w_in,
             conv_w=conv_w, fgate_b=fgate_b, lru_conv_w=lru_conv_w, lru_conv_b=lru_conv_b, lru_w_a=lru_w_a,
             lru_b_a=lru_b_a, lru_w_x=lru_w_x, lru_b_x=lru_b_x, lru_lambda=lru_lambda, mix_out_norm=mix_out_norm,
             mix_w_out=mix_w_out, norm_ffn2=norm_ffn2, ffn2_w_in=ffn2_w_in, ffn2_w_out=ffn2_w_out,
             final_norm=final_norm)
    m = dict(norm_ffn1=m_norm_ffn1, ffn1_w_in=m_ffn1_w_in, ffn1_w_out=m_ffn1_w_out, norm_mix=m_norm_mix,
             mix_w_in=m_mix_w_in, conv_w=m_conv_w, fgate_b=m_fgate_b, lru_conv_w=m_lru_conv_w,
             lru_conv_b=m_lru_conv_b, lru_w_a=m_lru_w_a, lru_b_a=m_lru_b_a, lru_w_x=m_lru_w_x, lru_b_x=m_lru_b_x,
             lru_lambda=m_lru_lambda, mix_out_norm=m_mix_out_norm, mix_w_out=m_mix_w_out, norm_ffn2=m_norm_ffn2,
             ffn2_w_in=m_ffn2_w_in, ffn2_w_out=m_ffn2_w_out, final_norm=m_final_norm)
    v = dict(norm_ffn1=v_norm_ffn1, ffn1_w_in=v_ffn1_w_in, ffn1_w_out=v_ffn1_w_out, norm_mix=v_norm_mix,
             mix_w_in=v_mix_w_in, conv_w=v_conv_w, fgate_b=v_fgate_b, lru_conv_w=v_lru_conv_w,
             lru_conv_b=v_lru_conv_b, lru_w_a=v_lru_w_a, lru_b_a=v_lru_b_a, lru_w_x=v_lru_w_x, lru_b_x=v_lru_b_x,
             lru_lambda=v_lru_lambda, mix_out_norm=v_mix_out_norm, mix_w_out=v_mix_w_out, norm_ffn2=v_norm_ffn2,
             ffn2_w_in=v_ffn2_w_in, ffn2_w_out=v_ffn2_w_out, final_norm=v_final_norm)
    return _step(x, loss_target, w, m, v)
```

```python
import math

import jax
import jax.numpy as jnp
from jax import lax
from jax.experimental import pallas as pl
from jax.experimental.pallas import tpu as pltpu

F32 = jnp.float32
BF = jnp.bfloat16
EPS = 1e-6
LANES = 128
VMEM_LIMIT_V7X = 56 * 1024 * 1024
MESH = pl.DeviceIdType.MESH
N_DEV = 8
LRU_C = 8.0
ADAM_LR, ADAM_B1, ADAM_B2, ADAM_EPS, ADAM_WD, ADAM_STEP = 0.001, 0.9, 0.999, 1e-08, 0.01, 10
GELU_C = math.sqrt(2.0 / math.pi)
GELU_K = 0.044715


def _params(*sem):
    return pltpu.CompilerParams(dimension_semantics=sem, vmem_limit_bytes=VMEM_LIMIT_V7X)


def _any():
    return pl.BlockSpec(memory_space=pl.ANY)


def _tile(n, target):
    if n <= target:
        return n
    t = target - target % 16
    while t >= 16:
        if n % t == 0:
            return t
        t -= 16
    return n


def _mm(name, grid, a, b, o, dims, scale=1.0, resid=None, after=None):
    nk = grid[-1]
    acc_shape = tuple(d for d in o[1] if d is not None)
    has_resid = resid is not None
    n_in = 2 + has_resid + (after is not None)

    def body(*refs):
        a_ref, b_ref = refs[0], refs[1]
        r_ref = refs[2] if has_resid else None
        o_ref = refs[n_in]

        def finish(acc):
            r = acc * scale if scale != 1.0 else acc
            if has_resid:
                r = r + r_ref[...]
            o_ref[...] = r.astype(o_ref.dtype)

        def product():
            return lax.dot_general(a_ref[...].astype(BF), b_ref[...].astype(BF), (dims, ((), ())),
                                   preferred_element_type=F32)

        if nk == 1:
            finish(product())
        else:
            acc_ref = refs[-1]
            k = pl.program_id(len(grid) - 1)

            @pl.when(k == 0)
            def _():
                acc_ref[...] = jnp.zeros(acc_ref.shape, F32)

            acc_ref[...] += product()

            @pl.when(k == nk - 1)
            def _():
                finish(acc_ref[...])

    ins = [a, b] + ([resid] if has_resid else [])
    return pl.pallas_call(
        body, name=name, grid=grid,
        in_specs=[pl.BlockSpec(blk, idx) for (_, blk, idx) in ins] + ([_any()] if after is not None else []),
        out_specs=pl.BlockSpec(o[1], o[2]),
        out_shape=o[0],
        scratch_shapes=[pltpu.VMEM(acc_shape, F32)] if nk > 1 else [],
        compiler_params=_params(*(["parallel"] * (len(grid) - 1) + ["arbitrary"])),
    )(*[x[0] for x in ins], *([after] if after is not None else []))


NN = ((1,), (0,))
NT = ((1,), (1,))
TN = ((0,), (0,))


def _rmsnorm(x, gain):
    T, D = x.shape
    tr = _tile(T, 512)

    def body(x_ref, g_ref, o_ref):
        xv = x_ref[...]
        r = lax.rsqrt(jnp.mean(xv * xv, axis=-1, keepdims=True) + EPS)
        o_ref[...] = (xv * r * g_ref[...]).astype(BF)

    return pl.pallas_call(
        body, name="rmsnorm_fwd", grid=(T // tr,),
        in_specs=[pl.BlockSpec((tr, D), lambda i: (i, 0)), pl.BlockSpec((1, D), lambda i: (0, 0))],
        out_specs=pl.BlockSpec((tr, D), lambda i: (i, 0)),
        out_shape=jax.ShapeDtypeStruct((T, D), BF),
        compiler_params=_params("parallel"),
    )(x, gain)


def _rmsnorm_bwd(dh, x, gain, dres):
    T, D = x.shape
    tr = _tile(T, 256)

    def body(dh_ref, x_ref, g_ref, dres_ref, dx_ref, dg_ref):
        i = pl.program_id(0)
        xv = x_ref[...]
        r = lax.rsqrt(jnp.mean(xv * xv, axis=-1, keepdims=True) + EPS)
        xh = xv * r
        dy = dh_ref[...].astype(F32)
        dgp = jnp.sum(dy * xh, axis=0, keepdims=True)

        @pl.when(i == 0)
        def _():
            dg_ref[...] = dgp

        @pl.when(i > 0)
        def _():
            dg_ref[...] += dgp

        dxh = dy * g_ref[...]
        dx_ref[...] = dres_ref[...] + r * (dxh - xh * jnp.mean(dxh * xh, axis=-1, keepdims=True))

    return pl.pallas_call(
        body, name="rmsnorm_bwd", grid=(T // tr,),
        in_specs=[pl.BlockSpec((tr, D), lambda i: (i, 0)), pl.BlockSpec((tr, D), lambda i: (i, 0)),
                  pl.BlockSpec((1, D), lambda i: (0, 0)), pl.BlockSpec((tr, D), lambda i: (i, 0))],
        out_specs=[pl.BlockSpec((tr, D), lambda i: (i, 0)), pl.BlockSpec((1, D), lambda i: (0, 0))],
        out_shape=[jax.ShapeDtypeStruct((T, D), F32), jax.ShapeDtypeStruct((1, D), F32)],
        compiler_params=_params("arbitrary"),
    )(dh, x, gain, dres)


def _loss_head(x, gain, target):
    T, D = x.shape
    tr = _tile(T, 256)

    def body(x_ref, g_ref, t_ref, loss_ref, dx_ref, dg_ref):
        i = pl.program_id(0)
        xv = x_ref[...]
        g = g_ref[...]
        r = lax.rsqrt(jnp.mean(xv * xv, axis=-1, keepdims=True) + EPS)
        xh = xv * r
        err = xh * g - t_ref[...]
        lp = 0.5 * jnp.sum(jnp.mean(err * err, axis=-1, keepdims=True), axis=0, keepdims=True)
        dy = err * (1.0 / D)
        dgp = jnp.sum(dy * xh, axis=0, keepdims=True)

        @pl.when(i == 0)
        def _():
            loss_ref[...] = jnp.broadcast_to(lp, loss_ref.shape)
            dg_ref[...] = dgp

        @pl.when(i > 0)
        def _():
            loss_ref[...] += jnp.broadcast_to(lp, loss_ref.shape)
            dg_ref[...] += dgp

        dxh = dy * g
        dx_ref[...] = r * (dxh - xh * jnp.mean(dxh * xh, axis=-1, keepdims=True))

    return pl.pallas_call(
        body, name="loss_head", grid=(T // tr,),
        in_specs=[pl.BlockSpec((tr, D), lambda i: (i, 0)), pl.BlockSpec((1, D), lambda i: (0, 0)),
                  pl.BlockSpec((tr, D), lambda i: (i, 0))],
        out_specs=[pl.BlockSpec((1, LANES), lambda i: (0, 0)), pl.BlockSpec((tr, D), lambda i: (i, 0)),
                   pl.BlockSpec((1, D), lambda i: (0, 0))],
        out_shape=[jax.ShapeDtypeStruct((1, LANES), F32), jax.ShapeDtypeStruct((T, D), F32),
                   jax.ShapeDtypeStruct((1, D), F32)],
        compiler_params=_params("arbitrary"),
    )(x, gain, target)


def _group_slices(D):
    dc, da = D // 4, D // 2
    return [(0, dc), (dc, dc + da), (dc + da, D)]


def _groupnorm(yc, ya, yl, gain):
    T = yc.shape[0]
    D = yc.shape[1] + ya.shape[1] + yl.shape[1]
    tr = _tile(T, 512)
    sl = _group_slices(D)

    def body(yc_ref, ya_ref, yl_ref, g_ref, o_ref):
        for y_ref, (lo, hi) in zip((yc_ref, ya_ref, yl_ref), sl):
            y = y_ref[...]
            r = lax.rsqrt(jnp.mean(y * y, axis=-1, keepdims=True) + EPS)
            o_ref[:, lo:hi] = (y * r * g_ref[:, lo:hi]).astype(BF)

    return pl.pallas_call(
        body, name="groupnorm_fwd", grid=(T // tr,),
        in_specs=[pl.BlockSpec((tr, y.shape[1]), lambda i: (i, 0)) for y in (yc, ya, yl)]
        + [pl.BlockSpec((1, D), lambda i: (0, 0))],
        out_specs=pl.BlockSpec((tr, D), lambda i: (i, 0)),
        out_shape=jax.ShapeDtypeStruct((T, D), BF),
        compiler_params=_params("parallel"),
    )(yc, ya, yl, gain)


def _groupnorm_bwd(dyn, yc, ya, yl, gain):
    T, D = dyn.shape
    tr = _tile(T, 512)
    sl = _group_slices(D)

    def body(dyn_ref, yc_ref, ya_ref, yl_ref, g_ref, dc_ref, da_ref, dl_ref, dg_ref):
        i = pl.program_id(0)
        for y_ref, d_ref, (lo, hi) in zip((yc_ref, ya_ref, yl_ref), (dc_ref, da_ref, dl_ref), sl):
            y = y_ref[...]
            r = lax.rsqrt(jnp.mean(y * y, axis=-1, keepdims=True) + EPS)
            yh = y * r
            dy = dyn_ref[:, lo:hi]
            dgp = jnp.sum(dy * yh, axis=0, keepdims=True)

            @pl.when(i == 0)
            def _():
                dg_ref[:, lo:hi] = dgp

            @pl.when(i > 0)
            def _():
                dg_ref[:, lo:hi] += dgp

            dyh = dy * g_ref[:, lo:hi]
            d_ref[...] = r * (dyh - yh * jnp.mean(dyh * yh, axis=-1, keepdims=True))

    return pl.pallas_call(
        body, name="groupnorm_bwd", grid=(T // tr,),
        in_specs=[pl.BlockSpec((tr, D), lambda i: (i, 0))]
        + [pl.BlockSpec((tr, y.shape[1]), lambda i: (i, 0)) for y in (yc, ya, yl)]
        + [pl.BlockSpec((1, D), lambda i: (0, 0))],
        out_specs=[pl.BlockSpec((tr, y.shape[1]), lambda i: (i, 0)) for y in (yc, ya, yl)]
        + [pl.BlockSpec((1, D), lambda i: (0, 0))],
        out_shape=[jax.ShapeDtypeStruct(y.shape, F32) for y in (yc, ya, yl)] + [jax.ShapeDtypeStruct((1, D), F32)],
        compiler_params=_params("arbitrary"),
    )(dyn, yc, ya, yl, gain)


def _ffn_in(h, w3):
    T, D = h.shape
    tn = w3.shape[2]
    F = 4 * tn
    tm = _tile(T, 512)

    def body(h_ref, wg_ref, wu_ref, g_ref, u_ref, a_ref):
        hv = h_ref[...]
        g = jnp.dot(hv, wg_ref[...], preferred_element_type=F32)
        u = jnp.dot(hv, wu_ref[...], preferred_element_type=F32)
        g_ref[...] = g.astype(BF)
        u_ref[...] = u.astype(BF)
        a_ref[...] = (g * jax.nn.sigmoid(g) * u).astype(BF)

    out = jax.ShapeDtypeStruct((T, F), BF)
    return pl.pallas_call(
        body, name="ffn_in_swiglu", grid=(4, T // tm),
        in_specs=[pl.BlockSpec((tm, D), lambda j, i: (i, 0)),
                  pl.BlockSpec((None, D, tn), lambda j, i: (j, 0, 0)),
                  pl.BlockSpec((None, D, tn), lambda j, i: (j + 4, 0, 0))],
        out_specs=[pl.BlockSpec((tm, tn), lambda j, i: (i, j))] * 3,
        out_shape=[out, out, out],
        compiler_params=_params("parallel", "parallel"),
    )(h, w3, w3)


def _ffn_bwd_in(dx, wout, g, u):
    T, D = dx.shape
    F = wout.shape[0]
    tn = F // 4
    tm = _tile(T, 512)

    def body(dx_ref, w_ref, g_ref, u_ref, o_ref):
        da = 0.5 * lax.dot_general(dx_ref[...].astype(BF), w_ref[...], (NT, ((), ())), preferred_element_type=F32)
        gv = g_ref[...].astype(F32)
        s = jax.nn.sigmoid(gv)
        o_ref[0] = (da * u_ref[...].astype(F32) * (s * (1.0 + gv * (1.0 - s)))).astype(BF)
        o_ref[1] = (da * gv * s).astype(BF)

    return pl.pallas_call(
        body, name="ffn_bwd_swiglu", grid=(4, T // tm),
        in_specs=[pl.BlockSpec((tm, D), lambda j, i: (i, 0)), pl.BlockSpec((tn, D), lambda j, i: (j, 0)),
                  pl.BlockSpec((tm, tn), lambda j, i: (i, j)), pl.BlockSpec((tm, tn), lambda j, i: (i, j))],
        out_specs=pl.BlockSpec((2, tm, tn), lambda j, i: (0, i, j)),
        out_shape=jax.ShapeDtypeStruct((2, T, F), BF),
        compiler_params=_params("parallel", "parallel"),
    )(dx, wout, g, u)


def _ffn_forward(x, gain, w3, wout):
    T, D = x.shape
    F = wout.shape[0]
    h = _rmsnorm(x, gain)
    g, u, a = _ffn_in(h, w3)
    tm, tk = _tile(T, 512), F // 4
    xn = _mm("ffn_out", (T // tm, 1, F // tk),
             (a, (tm, tk), lambda i, j, k: (i, k)), (wout, (tk, D), lambda i, j, k: (k, 0)),
             (jax.ShapeDtypeStruct((T, D), F32), (tm, D), lambda i, j, k: (i, 0)), NN, scale=0.5,
             resid=(x, (tm, D), lambda i, j, k: (i, 0)))
    return xn, (x, h, g, u, a)


def _ffn_backward(dx, saved, w3, wout, between=None):
    x, h, g, u, a = saved
    T, D = x.shape
    F = wout.shape[0]
    tn3 = F // 4
    dgu = _ffn_bwd_in(dx, wout, g, u)
    behind = None if between is None else between(dgu)
    tnd = _tile(D, 512)
    dwout = _mm("ffn_dwout", (F // tn3, D // tnd, 1),
                (a, (T, tn3), lambda i, j, k: (0, i)), (dx, (T, tnd), lambda i, j, k: (0, j)),
                (jax.ShapeDtypeStruct((F, D), BF), (tn3, tnd), lambda i, j, k: (i, j)), TN, scale=0.5, after=behind)
    dw3 = _mm("ffn_dwin", (8, D // tnd, 1),
              (h, (T, tnd), lambda s, i, k: (0, i)), (dgu, (None, T, tn3), lambda s, i, k: (s // 4, 0, s % 4)),
              (jax.ShapeDtypeStruct((8, D, tn3), BF), (None, tnd, tn3), lambda s, i, k: (s, i, 0)), TN, after=behind)
    tm = _tile(T, 1024)
    dh = _mm("ffn_dh", (T // tm, 1, 8),
             (dgu, (None, tm, tn3), lambda i, j, k: (k // 4, i, k % 4)), (w3, (None, D, tn3), lambda i, j, k: (k, 0, 0)),
             (jax.ShapeDtypeStruct((T, D), F32), (tm, D), lambda i, j, k: (i, 0)), NT, after=behind)
    return dh, dw3, dwout.reshape(N_DEV, F // N_DEV, D)


def _rows(shape):
    return lax.broadcasted_iota(jnp.int32, shape, 0)


def _down(x, s, fill, rows):
    return jnp.where(rows >= s, pltpu.roll(x, s, 0), fill)


def _up(x, s, fill, rows):
    T = x.shape[0]
    return jnp.where(rows < T - s, pltpu.roll(x, T - s, 0), fill)


def _scan_linear(a, b, rows, shift):
    T = a.shape[0]
    s = 1
    while s < T:
        b = a * shift(b, s, 0.0, rows) + b
        if 2 * s < T:
            a = a * shift(a, s, 1.0, rows)
        s *= 2
    return b


def _cumsum(c, rows, shift):
    T = c.shape[0]
    s = 1
    while s < T:
        c = c + shift(c, s, 0.0, rows)
        s *= 2
    return c


def _log1p_small(e):
    return jnp.where(e < 0.01, e * (1.0 - e * (0.5 - e * (1.0 / 3.0))), jnp.log(1.0 + e))


def _softplus(x):
    return jnp.maximum(x, 0.0) + _log1p_small(jnp.exp(-jnp.abs(x)))


def _one_minus_exp_neg(z):
    return jnp.where(z < 0.1, z * (1.0 - z * (0.5 - z * (1.0 / 6.0 - z * (1.0 / 24.0)))), 1.0 - jnp.exp(-z))


def _fgate_cum(f, b):
    T = f.shape[0]

    def body(f_ref, b_ref, o_ref):
        z = f_ref[...] + b_ref[...]
        o_ref[...] = _cumsum(-_softplus(-z), _rows(z.shape), _down)

    return pl.pallas_call(
        body, name="fgate_cumsum",
        out_shape=jax.ShapeDtypeStruct((T, LANES), F32),
        compiler_params=pltpu.CompilerParams(vmem_limit_bytes=VMEM_LIMIT_V7X),
    )(f, b)


def _fgate_cum_bwd(drow, dcol, f, b):
    T = f.shape[0]

    def body(dr_ref, dc_ref, f_ref, b_ref, df_ref, db_ref):
        z = f_ref[...] + b_ref[...]
        dlogf = _cumsum(dr_ref[...] - dc_ref[...], _rows(z.shape), _up)
        dz = dlogf * jax.nn.sigmoid(-z)
        df_ref[...] = dz.astype(BF)
        db_ref[...] = jnp.sum(dz, axis=0, keepdims=True)

    return pl.pallas_call(
        body, name="fgate_cumsum_bwd",
        out_shape=[jax.ShapeDtypeStruct((T, LANES), BF), jax.ShapeDtypeStruct((1, LANES), F32)],
        compiler_params=pltpu.CompilerParams(vmem_limit_bytes=VMEM_LIMIT_V7X),
    )(drow, dcol, f, b)


def _col(blk0):
    return lambda g: (0, blk0 + g)


def _conv_fwd(p, w, nb):
    T = p.shape[0]

    def body(b_ref, c_ref, v_ref, w_ref, o_ref):
        z = c_ref[...].astype(F32) * v_ref[...].astype(F32)
        rows = _rows(z.shape)
        conv = w_ref[2:3, :] * z + w_ref[1:2, :] * _down(z, 1, 0.0, rows) + w_ref[0:1, :] * _down(z, 2, 0.0, rows)
        o_ref[...] = b_ref[...].astype(F32) * conv

    return pl.pallas_call(
        body, name="conv_fwd", grid=(nb,),
        in_specs=[pl.BlockSpec((T, LANES), _col(0)), pl.BlockSpec((T, LANES), _col(nb)),
                  pl.BlockSpec((T, LANES), _col(2 * nb)), pl.BlockSpec((3, LANES), lambda g: (0, g))],
        out_specs=pl.BlockSpec((T, LANES), lambda g: (0, g)),
        out_shape=jax.ShapeDtypeStruct((T, nb * LANES), F32),
        compiler_params=_params("parallel"),
    )(p, p, p, w)


def _conv_bwd(dy, p, w, nb):
    T = p.shape[0]

    def body(dy_ref, b_ref, c_ref, v_ref, w_ref, db_ref, dc_ref, dv_ref, dw_ref):
        cv, vv = c_ref[...].astype(F32), v_ref[...].astype(F32)
        z = cv * vv
        rows = _rows(z.shape)
        z1, z2 = _down(z, 1, 0.0, rows), _down(z, 2, 0.0, rows)
        dyv = dy_ref[...]
        db_ref[...] = (dyv * (w_ref[2:3, :] * z + w_ref[1:2, :] * z1 + w_ref[0:1, :] * z2)).astype(BF)
        dconv = dyv * b_ref[...].astype(F32)
        dz = (w_ref[2:3, :] * dconv + w_ref[1:2, :] * _up(dconv, 1, 0.0, rows)
              + w_ref[0:1, :] * _up(dconv, 2, 0.0, rows))
        dc_ref[...] = (dz * vv).astype(BF)
        dv_ref[...] = (dz * cv).astype(BF)
        dw_ref[0:1, :] = jnp.sum(dconv * z2, axis=0, keepdims=True)
        dw_ref[1:2, :] = jnp.sum(dconv * z1, axis=0, keepdims=True)
        dw_ref[2:3, :] = jnp.sum(dconv * z, axis=0, keepdims=True)

    return pl.pallas_call(
        body, name="conv_bwd", grid=(nb,),
        in_specs=[pl.BlockSpec((T, LANES), lambda g: (0, g)), pl.BlockSpec((T, LANES), _col(0)),
                  pl.BlockSpec((T, LANES), _col(nb)), pl.BlockSpec((T, LANES), _col(2 * nb)),
                  pl.BlockSpec((3, LANES), lambda g: (0, g))],
        out_specs=[pl.BlockSpec((T, LANES), lambda g: (0, g))] * 3 + [pl.BlockSpec((3, LANES), lambda g: (0, g))],
        out_shape=[jax.ShapeDtypeStruct((T, nb * LANES), BF)] * 3 + [jax.ShapeDtypeStruct((3, nb * LANES), F32)],
        compiler_params=_params("parallel"),
    )(dy, p, p, p, w)


def _gelu(x):
    t = jnp.tanh(GELU_C * (x + GELU_K * x * x * x))
    return 0.5 * x * (1.0 + t), t


def _lru_common(x, cw_ref, cb_ref, wa_ref, ba_ref, wx_ref, bx_ref, lam_ref, rows):
    xr = (cb_ref[...] + cw_ref[3:4, :] * x + cw_ref[2:3, :] * _down(x, 1, 0.0, rows)
          + cw_ref[1:2, :] * _down(x, 2, 0.0, rows) + cw_ref[0:1, :] * _down(x, 3, 0.0, rows))
    xrb = xr.astype(BF)
    r = jax.nn.sigmoid(jnp.dot(xrb, wa_ref[...].astype(BF), preferred_element_type=F32) + ba_ref[...])
    i = jax.nn.sigmoid(jnp.dot(xrb, wx_ref[...].astype(BF), preferred_element_type=F32) + bx_ref[...])
    sp = _softplus(-lam_ref[...])
    log_a = -LRU_C * r * sp
    a = jnp.exp(log_a)
    m = jnp.sqrt(_one_minus_exp_neg(-2.0 * log_a))
    return xr, xrb, r, i, sp, a, m


def _lru_specs(T, nb, gate_blk0, x_blk0):
    vec = pl.BlockSpec((1, LANES), lambda g: (0, g))
    mat = pl.BlockSpec((None, LANES, LANES), lambda g: (g, 0, 0))
    return [pl.BlockSpec((T, LANES), _col(gate_blk0)), pl.BlockSpec((T, LANES), _col(x_blk0)),
            pl.BlockSpec((4, LANES), lambda g: (0, g)), vec, mat, vec, mat, vec, vec]


def _lru_fwd(p, cw, cb, wa, ba, wx, bx, lam, nb, gate_blk0):
    T = p.shape[0]

    def body(gate_ref, x_ref, cw_ref, cb_ref, wa_ref, ba_ref, wx_ref, bx_ref, lam_ref, y_ref, h_ref):
        x = x_ref[...].astype(F32)
        rows = _rows(x.shape)
        xr, _, _, i, _, a, m = _lru_common(x, cw_ref, cb_ref, wa_ref, ba_ref, wx_ref, bx_ref, lam_ref, rows)
        h = _scan_linear(a, m * (i * xr), rows, _down)
        h_ref[...] = h
        y_ref[...] = _gelu(gate_ref[...].astype(F32))[0] * h

    out = jax.ShapeDtypeStruct((T, nb * LANES), F32)
    return pl.pallas_call(
        body, name="lru_fwd", grid=(nb,),
        in_specs=_lru_specs(T, nb, gate_blk0, gate_blk0 + nb),
        out_specs=[pl.BlockSpec((T, LANES), lambda g: (0, g))] * 2,
        out_shape=[out, out],
        compiler_params=_params("parallel"),
    )(p, p, cw, cb, wa, ba, wx, bx, lam)


def _lru_bwd(dy, hs, p, cw, cb, wa, ba, wx, bx, lam, nb, gate_blk0):
    T = p.shape[0]

    def body(dy_ref, hs_ref, gate_ref, x_ref, cw_ref, cb_ref, wa_ref, ba_ref, wx_ref, bx_ref, lam_ref,
             dgate_ref, dx_ref, dcw_ref, dcb_ref, dwa_ref, dba_ref, dwx_ref, dbx_ref, dlam_ref):
        x = x_ref[...].astype(F32)
        rows = _rows(x.shape)
        xr, xrb, r, i, sp, a, m = _lru_common(x, cw_ref, cb_ref, wa_ref, ba_ref, wx_ref, bx_ref, lam_ref, rows)
        gate = gate_ref[...].astype(F32)
        gl, t = _gelu(gate)
        h = hs_ref[...]
        dyv = dy_ref[...]
        dgelu = 0.5 * (1.0 + t) + 0.5 * gate * (1.0 - t * t) * GELU_C * (1.0 + 3.0 * GELU_K * gate * gate)
        dgate_ref[...] = (dyv * h * dgelu).astype(BF)
        lam_adj = _scan_linear(_up(a, 1, 0.0, rows), dyv * gl, rows, _up)
        da = lam_adj * _down(h, 1, 0.0, rows)
        ix = i * xr
        dix = lam_adj * m
        dm = lam_adj * ix
        dlog_a = da * a - dm * (a * a) / jnp.maximum(m, 1e-30)
        dr = dlog_a * (-LRU_C * sp)
        dsp = jnp.sum(dlog_a * (-LRU_C * r), axis=0, keepdims=True)
        dlam_ref[...] = -dsp * jax.nn.sigmoid(-lam_ref[...])
        dpa = dr * r * (1.0 - r)
        dpx = dix * xr * i * (1.0 - i)
        dpab, dpxb = dpa.astype(BF), dpx.astype(BF)
        dxr = (dix * i
               + lax.dot_general(dpab, wa_ref[...].astype(BF), (NT, ((), ())), preferred_element_type=F32)
               + lax.dot_general(dpxb, wx_ref[...].astype(BF), (NT, ((), ())), preferred_element_type=F32))
        dwa_ref[...] = lax.dot_general(xrb, dpab, (TN, ((), ())), preferred_element_type=F32)
        dwx_ref[...] = lax.dot_general(xrb, dpxb, (TN, ((), ())), preferred_element_type=F32)
        dba_ref[...] = jnp.sum(dpa, axis=0, keepdims=True)
        dbx_ref[...] = jnp.sum(dpx, axis=0, keepdims=True)
        dcb_ref[...] = jnp.sum(dxr, axis=0, keepdims=True)
        dx_ref[...] = (cw_ref[3:4, :] * dxr + cw_ref[2:3, :] * _up(dxr, 1, 0.0, rows)
                       + cw_ref[1:2, :] * _up(dxr, 2, 0.0, rows) + cw_ref[0:1, :] * _up(dxr, 3, 0.0, rows)).astype(BF)
        for k in range(4):
            xs = x if k == 3 else _down(x, 3 - k, 0.0, rows)
            dcw_ref[k:k + 1, :] = jnp.sum(dxr * xs, axis=0, keepdims=True)

    C = nb * LANES
    seq = jax.ShapeDtypeStruct((T, C), BF)
    vec = jax.ShapeDtypeStruct((1, C), F32)
    mat = jax.ShapeDtypeStruct((nb, LANES, LANES), F32)
    vspec = pl.BlockSpec((1, LANES), lambda g: (0, g))
    mspec = pl.BlockSpec((None, LANES, LANES), lambda g: (g, 0, 0))
    sspec = pl.BlockSpec((T, LANES), lambda g: (0, g))
    return pl.pallas_call(
        body, name="lru_bwd", grid=(nb,),
        in_specs=[sspec, sspec] + _lru_specs(T, nb, gate_blk0, gate_blk0 + nb),
        out_specs=[sspec, sspec, pl.BlockSpec((4, LANES), lambda g: (0, g)), vspec, mspec, vspec, mspec, vspec, vspec],
        out_shape=[seq, seq, jax.ShapeDtypeStruct((4, C), F32), vec, mat, vec, mat, vec, vec],
        compiler_params=_params("parallel"),
    )(dy, hs, p, p, cw, cb, wa, ba, wx, bx, lam)


ATTN_BLOCK = 1024


def _causal(shape, transposed=False):
    r = lax.broadcasted_iota(jnp.int32, shape, 0)
    c = lax.broadcasted_iota(jnp.int32, shape, 1)
    return r <= c if transposed else c <= r


def _causal_pairs(n, by_query):
    if by_query:
        pairs = [(i, j) for i in range(n) for j in range(i + 1)]
    else:
        pairs = [(i, j) for j in range(n) for i in range(j, n)]
    return jnp.asarray([a for a, _ in pairs], jnp.int32), jnp.asarray([b for _, b in pairs], jnp.int32)


def _attn_fwd(p, cq, ck, nh, q_blk0):
    T = p.shape[0]
    tq = _tile(T, ATTN_BLOCK)
    nq = T // tq
    scale = LANES ** -0.5

    qi, kj = _causal_pairs(nq, by_query=True)

    def body(qi_ref, kj_ref, q_ref, k_ref, v_ref, cq_ref, ck_ref, o_ref, lse_ref, m_ref, l_ref, acc_ref):
        i, j = qi_ref[pl.program_id(1)], kj_ref[pl.program_id(1)]

        @pl.when(j == 0)
        def _():
            m_ref[...] = jnp.full(m_ref.shape, -jnp.inf, F32)
            l_ref[...] = jnp.zeros(l_ref.shape, F32)
            acc_ref[...] = jnp.zeros(acc_ref.shape, F32)

        def block(diagonal):
            s = lax.dot_general(q_ref[...], k_ref[...], (NT, ((), ())), preferred_element_type=F32) * scale
            s = s + cq_ref[...] - ck_ref[...]
            if diagonal:
                s = jnp.where(_causal(s.shape), s, -jnp.inf)
            m_new = jnp.maximum(m_ref[...], jnp.max(s, axis=-1, keepdims=True))
            alpha = jnp.exp(m_ref[...] - m_new)
            pr = jnp.exp(s - m_new)
            l_ref[...] = alpha * l_ref[...] + jnp.sum(pr, axis=-1, keepdims=True)
            acc_ref[...] = alpha * acc_ref[...] + jnp.dot(pr.astype(BF), v_ref[...], preferred_element_type=F32)
            m_ref[...] = m_new

        pl.when(j < i)(lambda: block(False))

        @pl.when(j == i)
        def _():
            block(True)
            o_ref[...] = acc_ref[...] / l_ref[...]
            lse_ref[...] = m_ref[...] + jnp.log(l_ref[...])

    def kv(off):
        return pl.BlockSpec((tq, LANES), lambda h, s, qi, kj: (kj[s], q_blk0 + off * nh + h))

    return pl.pallas_call(
        body, name="attn_fwd",
        grid_spec=pltpu.PrefetchScalarGridSpec(
            num_scalar_prefetch=2, grid=(nh, qi.shape[0]),
            in_specs=[pl.BlockSpec((tq, LANES), lambda h, s, qi, kj: (qi[s], q_blk0 + h)), kv(1), kv(2),
                      pl.BlockSpec((None, tq, 1), lambda h, s, qi, kj: (h, qi[s], 0)),
                      pl.BlockSpec((None, 1, tq), lambda h, s, qi, kj: (h, 0, kj[s]))],
            out_specs=[pl.BlockSpec((tq, LANES), lambda h, s, qi, kj: (qi[s], h)),
                       pl.BlockSpec((None, tq, 1), lambda h, s, qi, kj: (h, qi[s], 0))],
            scratch_shapes=[pltpu.VMEM((tq, 1), F32), pltpu.VMEM((tq, 1), F32), pltpu.VMEM((tq, LANES), F32)]),
        out_shape=[jax.ShapeDtypeStruct((T, nh * LANES), F32), jax.ShapeDtypeStruct((nh, T, 1), F32)],
        compiler_params=_params("parallel", "arbitrary"),
    )(qi, kj, p, p, p, cq, ck)


def _attn_bwd_q(p, cq, ck, lse, do, o, nh, q_blk0):
    T = p.shape[0]
    tq = _tile(T, ATTN_BLOCK)
    nq = T // tq
    scale = LANES ** -0.5

    qi, kj = _causal_pairs(nq, by_query=True)

    def body(qi_ref, kj_ref, q_ref, k_ref, v_ref, cq_ref, ck_ref, lse_ref, do_ref, o_ref, dq_ref, dl_ref, dr_ref,
             acc_ref):
        i, j = qi_ref[pl.program_id(1)], kj_ref[pl.program_id(1)]

        @pl.when(j == 0)
        def _():
            dl_ref[...] = jnp.sum(do_ref[...] * o_ref[...], axis=-1, keepdims=True)
            dr_ref[...] = jnp.zeros(dr_ref.shape, F32)
            acc_ref[...] = jnp.zeros(acc_ref.shape, F32)

        def block(diagonal):
            s = lax.dot_general(q_ref[...], k_ref[...], (NT, ((), ())), preferred_element_type=F32) * scale
            s = s + cq_ref[...] - ck_ref[...]
            pr = jnp.exp(s - lse_ref[...])
            if diagonal:
                pr = jnp.where(_causal(s.shape), pr, 0.0)
            dp = lax.dot_general(do_ref[...].astype(BF), v_ref[...], (NT, ((), ())), preferred_element_type=F32)
            ds = pr * (dp - dl_ref[...])
            dr_ref[...] += jnp.sum(ds, axis=-1, keepdims=True)
            acc_ref[...] += jnp.dot(ds.astype(BF), k_ref[...], preferred_element_type=F32)

        pl.when(j < i)(lambda: block(False))

        @pl.when(j == i)
        def _():
            block(True)
            dq_ref[...] = (acc_ref[...] * scale).astype(BF)

    def kv(off):
        return pl.BlockSpec((tq, LANES), lambda h, s, qi, kj: (kj[s], q_blk0 + off * nh + h))

    col = pl.BlockSpec((None, tq, 1), lambda h, s, qi, kj: (h, qi[s], 0))
    head = pl.BlockSpec((tq, LANES), lambda h, s, qi, kj: (qi[s], h))
    return pl.pallas_call(
        body, name="attn_bwd_q",
        grid_spec=pltpu.PrefetchScalarGridSpec(
            num_scalar_prefetch=2, grid=(nh, qi.shape[0]),
            in_specs=[pl.BlockSpec((tq, LANES), lambda h, s, qi, kj: (qi[s], q_blk0 + h)), kv(1), kv(2), col,
                      pl.BlockSpec((None, 1, tq), lambda h, s, qi, kj: (h, 0, kj[s])), col, head, head],
            out_specs=[head, col, col],
            scratch_shapes=[pltpu.VMEM((tq, LANES), F32)]),
        out_shape=[jax.ShapeDtypeStruct((T, nh * LANES), BF), jax.ShapeDtypeStruct((nh, T, 1), F32),
                   jax.ShapeDtypeStruct((nh, T, 1), F32)],
        compiler_params=_params("parallel", "arbitrary"),
    )(qi, kj, p, p, p, cq, ck, lse, do, o)


def _attn_bwd_kv(p, cq_row, ck_col, lse_row, delta_row, do, nh, q_blk0):
    T = p.shape[0]
    tk = _tile(T, ATTN_BLOCK)
    nk = T // tk
    scale = LANES ** -0.5

    qi, kj = _causal_pairs(nk, by_query=False)

    def body(qi_ref, kj_ref, q_ref, k_ref, v_ref, cq_ref, ck_ref, lse_ref, dl_ref, do_ref, dk_ref, dv_ref, dc_ref,
             dk_acc, dv_acc, dc_acc):
        i, j = qi_ref[pl.program_id(1)], kj_ref[pl.program_id(1)]

        @pl.when(i == j)
        def _():
            dk_acc[...] = jnp.zeros(dk_acc.shape, F32)
            dv_acc[...] = jnp.zeros(dv_acc.shape, F32)
            dc_acc[...] = jnp.zeros(dc_acc.shape, F32)

        def block(diagonal):
            st = lax.dot_general(k_ref[...], q_ref[...], (NT, ((), ())), preferred_element_type=F32) * scale
            st = st + cq_ref[...] - ck_ref[...]
            pt = jnp.exp(st - lse_ref[...])
            if diagonal:
                pt = jnp.where(_causal(st.shape, transposed=True), pt, 0.0)
            dob = do_ref[...].astype(BF)
            dv_acc[...] += jnp.dot(pt.astype(BF), dob, preferred_element_type=F32)
            dpt = lax.dot_general(v_ref[...], dob, (NT, ((), ())), preferred_element_type=F32)
            dst = pt * (dpt - dl_ref[...])
            dk_acc[...] += jnp.dot(dst.astype(BF), q_ref[...], preferred_element_type=F32)
            dc_acc[...] += jnp.sum(dst, axis=-1, keepdims=True)

        pl.when(i == j)(lambda: block(True))
        pl.when(i > j)(lambda: block(False))

        @pl.when(i == nk - 1)
        def _():
            dk_ref[...] = (dk_acc[...] * scale).astype(BF)
            dv_ref[...] = dv_acc[...].astype(BF)
            dc_ref[...] = dc_acc[...]

    def qside(blk):
        return pl.BlockSpec((tk, LANES), lambda h, s, qi, kj: (qi[s], blk + h))

    def kside(off):
        return pl.BlockSpec((tk, LANES), lambda h, s, qi, kj: (kj[s], q_blk0 + off * nh + h))

    row = pl.BlockSpec((None, 1, tk), lambda h, s, qi, kj: (h, 0, qi[s]))
    col = pl.BlockSpec((None, tk, 1), lambda h, s, qi, kj: (h, kj[s], 0))
    head = pl.BlockSpec((tk, LANES), lambda h, s, qi, kj: (kj[s], h))
    return pl.pallas_call(
        body, name="attn_bwd_kv",
        grid_spec=pltpu.PrefetchScalarGridSpec(
            num_scalar_prefetch=2, grid=(nh, qi.shape[0]),
            in_specs=[qside(q_blk0), kside(1), kside(2), row, col, row, row, qside(0)],
            out_specs=[head, head, col],
            scratch_shapes=[pltpu.VMEM((tk, LANES), F32), pltpu.VMEM((tk, LANES), F32), pltpu.VMEM((tk, 1), F32)]),
        out_shape=[jax.ShapeDtypeStruct((T, nh * LANES), BF), jax.ShapeDtypeStruct((T, nh * LANES), BF),
                   jax.ShapeDtypeStruct((nh, T, 1), F32)],
        compiler_params=_params("parallel", "arbitrary"),
    )(qi, kj, p, p, p, cq_row, ck_col, lse_row, delta_row, do)


def _mixer_dims(D):
    dc, da, dl = D // 4, D // 2, D // 4
    nh = da // LANES
    n_main = 3 * dc + 3 * da + 2 * dl
    return dc, da, dl, nh, n_main


def _pad_mix_w_in(wfull):
    D = wfull.shape[0]
    dc, da, dl, nh, n_main = _mixer_dims(D)
    a = 3 * dc + 3 * da
    return jnp.concatenate([wfull[:, :a], wfull[:, a + nh:], wfull[:, a:a + nh],
                            jnp.zeros((D, LANES - nh), wfull.dtype)], axis=1)


def _unpad_mix_w_in(wp):
    D = wp.shape[0]
    dc, da, dl, nh, n_main = _mixer_dims(D)
    a = 3 * dc + 3 * da
    return jnp.concatenate([wp[:, :a], wp[:, n_main:n_main + nh], wp[:, a:n_main]], axis=1)


def _head_cols(c, nh):
    t = jnp.transpose(c[:, :nh])
    return t[:, :, None], t[:, None, :]


def _mixer_forward(x, gain, wp, wo, sp):
    T, D = x.shape
    dc, da, dl, nh, n_main = _mixer_dims(D)
    nbc, nbl = dc // LANES, dl // LANES
    h = _rmsnorm(x, gain)
    tm = _tile(T, 512)
    tn = n_main // 4
    p = _mm("mix_in", (4, T // tm, 1),
            (h, (tm, D), lambda j, i, k: (i, 0)), (wp, (D, tn), lambda j, i, k: (0, j)),
            (jax.ShapeDtypeStruct((T, n_main), BF), (tm, tn), lambda j, i, k: (i, j)), NN)
    f = _mm("mix_in_fgate", (T // tm, 1, 1),
            (h, (tm, D), lambda i, j, k: (i, 0)), (wp, (D, LANES), lambda i, j, k: (0, n_main // LANES)),
            (jax.ShapeDtypeStruct((T, LANES), F32), (tm, LANES), lambda i, j, k: (i, 0)), NN)
    cum = _fgate_cum(f, sp["fgate_b"])
    cq, ck = _head_cols(cum, nh)
    yc = _conv_fwd(p, sp["conv_w"], nbc)
    q_blk0 = 3 * nbc
    ya, lse = _attn_fwd(p, cq, ck, nh, q_blk0)
    gate_blk0 = q_blk0 + 3 * nh
    yl, hs = _lru_fwd(p, sp["lru_conv_w"], sp["lru_conv_b"], sp["lru_w_a"], sp["lru_b_a"], sp["lru_w_x"],
                      sp["lru_b_x"], sp["lru_lambda"], nbl, gate_blk0)
    yn = _groupnorm(yc, ya, yl, sp["mix_out_norm"])
    xn = _mm("mix_out", (T // tm, 1, 1),
             (yn, (tm, D), lambda i, j, k: (i, 0)), (wo, (D, D), lambda i, j, k: (0, 0)),
             (jax.ShapeDtypeStruct((T, D), F32), (tm, D), lambda i, j, k: (i, 0)), NN,
             resid=(x, (tm, D), lambda i, j, k: (i, 0)))
    return xn, (x, h, p, f, cq, ck, yc, ya, lse, yl, hs, yn)


def _mixer_backward(dx, saved, gain, wp, wo, sp):
    x, h, p, f, cq, ck, yc, ya, lse, yl, hs, yn = saved
    T, D = x.shape
    dc, da, dl, nh, n_main = _mixer_dims(D)
    nbc, nbl = dc // LANES, dl // LANES
    q_blk0 = 3 * nbc
    gate_blk0 = q_blk0 + 3 * nh
    tm = _tile(T, 512)
    tnd, tnd2 = _tile(D, 512), _tile(D, 1024)
    dyn = _mm("mix_out_dy", (T // tm, 1, 1),
              (dx, (tm, D), lambda i, j, k: (i, 0)), (wo, (D, D), lambda i, j, k: (0, 0)),
              (jax.ShapeDtypeStruct((T, D), F32), (tm, D), lambda i, j, k: (i, 0)), NT)
    dwo = _mm("mix_out_dw", (D // tnd, D // tnd2, 1),
              (yn, (T, tnd), lambda i, j, k: (0, i)), (dx, (T, tnd2), lambda i, j, k: (0, j)),
              (jax.ShapeDtypeStruct((D, D), BF), (tnd, tnd2), lambda i, j, k: (i, j)), TN)
    dyc, dya, dyl, dgn = _groupnorm_bwd(dyn, yc, ya, yl, sp["mix_out_norm"])
    dcb, dcc, dcv, dconv_w = _conv_bwd(dyc, p, sp["conv_w"], nbc)
    dq, delta, drow = _attn_bwd_q(p, cq, ck, lse, dya, ya, nh, q_blk0)
    as_row = lambda t: t.reshape(nh, 1, T)
    dk, dv, dcol = _attn_bwd_kv(p, ck, cq, as_row(lse), as_row(delta), dya, nh, q_blk0)
    lanes = lambda t: jnp.pad(jnp.transpose(t.reshape(nh, T)), ((0, 0), (0, LANES - nh)))
    df, dfb = _fgate_cum_bwd(lanes(drow), lanes(dcol), f, sp["fgate_b"])
    dgate, dlx, dcw, dcbias, dwa, dba, dwx, dbx, dlam = _lru_bwd(
        dyl, hs, p, sp["lru_conv_w"], sp["lru_conv_b"], sp["lru_w_a"], sp["lru_b_a"], sp["lru_w_x"],
        sp["lru_b_x"], sp["lru_lambda"], nbl, gate_blk0)
    dp = jnp.concatenate([dcb, dcc, dcv, dq, dk, dv, dgate, dlx, df], axis=1)
    n_pad = n_main + LANES
    tn, tkp = n_pad // 5, n_pad // 3
    dwp = _mm("mix_in_dw", (n_pad // tn, D // tnd, 1),
              (h, (T, tnd), lambda j, i, k: (0, i)), (dp, (T, tn), lambda j, i, k: (0, j)),
              (jax.ShapeDtypeStruct((D, n_pad), BF), (tnd, tn), lambda j, i, k: (i, j)), TN)
    tmh = _tile(T, 1024)
    dh = _mm("mix_in_dh", (T // tmh, 1, n_pad // tkp),
             (dp, (tmh, tkp), lambda i, j, k: (i, k)), (wp, (D, tkp), lambda i, j, k: (0, k)),
             (jax.ShapeDtypeStruct((T, D), F32), (tmh, D), lambda i, j, k: (i, 0)), NT)
    dxn, dgain = _rmsnorm_bwd(dh, x, gain, dx)
    small = {"norm_mix": dgain, "mix_out_norm": dgn, "conv_w": dconv_w, "fgate_b": dfb[:, :nh],
             "lru_conv_w": dcw, "lru_conv_b": dcbias, "lru_w_a": dwa, "lru_b_a": dba, "lru_w_x": dwx,
             "lru_b_x": dbx, "lru_lambda": dlam}
    return dxn, dwp, dwo, small


def _position():
    x, y, c = lax.axis_index("x"), lax.axis_index("y"), lax.axis_index("c")
    return x, y, c, [(1 - x, y), (x, 1 - y), (1 - x, 1 - y)]


def _all_gather(name, xs):
    n = len(xs)

    def body(*refs):
        x_refs, o_refs = refs[:n], refs[n:2 * n]
        send_sems, recv_sems, local_sems = refs[2 * n:]
        x, y, c, chips = _position()
        me, sibling = (x, y, c), (x, y, 1 - c)

        def copy(t, k, block, to, src=None):
            dst = o_refs[t].at[4 * block[0] + 2 * block[1] + block[2]]
            return pltpu.make_async_remote_copy(
                src_ref=dst if src is None else src, dst_ref=dst, send_sem=send_sems.at[t, k],
                recv_sem=recv_sems.at[t, k], device_id=to, device_id_type=MESH)

        mine = [pltpu.make_async_copy(x_refs[t], o_refs[t].at[4 * x + 2 * y + c], local_sems.at[t]) for t in range(n)]
        for cp in mine:
            cp.start()
        first = []
        for t in range(n):
            first.append(copy(t, 0, me, sibling, src=x_refs[t]))
            first += [copy(t, 1 + j, me, (*chip, c), src=x_refs[t]) for j, chip in enumerate(chips)]
        for cp in first:
            cp.start()
        passed = []
        for j, chip in enumerate(chips):
            for t in range(n):
                copy(t, 1 + j, (*chip, c), me).wait_recv()
                passed.append(copy(t, 4 + j, (*chip, c), sibling))
                passed[-1].start()
        for t in range(n):
            copy(t, 0, sibling, me).wait_recv()
            for j, chip in enumerate(chips):
                copy(t, 4 + j, (*chip, 1 - c), me).wait_recv()
        for cp in first + passed:
            cp.wait_send()
        for cp in mine:
            cp.wait()

    return pl.pallas_call(
        body, name=name,
        in_specs=[_any()] * n, out_specs=[_any()] * n,
        out_shape=[jax.ShapeDtypeStruct((N_DEV,) + v.shape, v.dtype) for v in xs],
        scratch_shapes=[pltpu.SemaphoreType.DMA((n, 7)), pltpu.SemaphoreType.DMA((n, 7)),
                        pltpu.SemaphoreType.DMA((n,))],
    )(*xs)


def _add_sibling(g4, recv, c_idx):
    _, _, R, C = g4.shape
    tr = _tile(R, 512)

    def body(c_ref, g_ref, r_ref, o_ref):
        o_ref[...] = (g_ref[...].astype(F32) + r_ref[...].astype(F32)).astype(o_ref.dtype)

    return pl.pallas_call(
        body, name="rs_add_sibling",
        grid_spec=pltpu.PrefetchScalarGridSpec(
            num_scalar_prefetch=1, grid=(4, R // tr),
            in_specs=[pl.BlockSpec((None, None, tr, C), lambda q, i, c_ref: (q, c_ref[0], i, 0)),
                      pl.BlockSpec((None, None, tr, C), lambda q, i, c_ref: (q, 0, i, 0))],
            out_specs=pl.BlockSpec((None, tr, C), lambda q, i, c_ref: (q, i, 0))),
        out_shape=jax.ShapeDtypeStruct((4, R, C), g4.dtype),
        compiler_params=_params("parallel", "parallel"),
    )(c_idx, g4, recv)


def _hbm_spec():
    return pl.BlockSpec(memory_space=pltpu.HBM)


def _sem_spec():
    return pl.BlockSpec(memory_space=pltpu.SEMAPHORE)


def _side_effects():
    return pltpu.CompilerParams(has_side_effects=pltpu.SideEffectType.DATAFLOW_SIDE_EFFECTING)


def _in_hbm(v):
    return pltpu.with_memory_space_constraint(v, pltpu.HBM)


def _split_start(name, srcs, lands, n_copies, copies_of, after):
    n = len(srcs)
    ns = n * n_copies

    def body(*refs):
        src_refs, land_refs = refs[:n], refs[n:2 * n]
        send_sems, recv_sems = refs[2 * n + 1:2 * n + 1 + ns], refs[2 * n + 1 + ns:2 * n + 1 + 2 * ns]
        token = refs[-1]
        for t in range(n):
            for k, (s, d, to) in enumerate(copies_of(t, src_refs[t], land_refs[t])):
                pltpu.make_async_remote_copy(src_ref=s, dst_ref=d, send_sem=send_sems[t * n_copies + k],
                                             recv_sem=recv_sems[t * n_copies + k], device_id=to,
                                             device_id_type=MESH).start()
        token[...] = jnp.zeros(token.shape, token.dtype)

    thru = [pltpu.HBM(v.shape, v.dtype) for v in list(srcs) + list(lands)]
    out = pl.pallas_call(
        body, name=name,
        in_specs=[_hbm_spec()] * (2 * n) + [_any()],
        out_specs=[_sem_spec()] * (2 * ns) + [_hbm_spec()] * (2 * n) + [pl.BlockSpec(memory_space=pltpu.VMEM)],
        out_shape=[pltpu.SemaphoreType.DMA(())] * (2 * ns) + thru + [jax.ShapeDtypeStruct((8, LANES), F32)],
        input_output_aliases={i: 2 * ns + i for i in range(2 * n)},
        compiler_params=_side_effects(),
    )(*[_in_hbm(v) for v in list(srcs) + list(lands)], after)
    return out[:ns], out[ns:2 * ns], out[2 * ns:2 * ns + n], out[2 * ns + n:2 * ns + 2 * n], out[-1]


def _split_wait(name, send_sems, recv_sems, srcs, lands, n_copies, waits_of, after):
    n = len(srcs)
    ns = n * n_copies

    def body(*refs):
        src_refs, land_refs = refs[:n], refs[n:2 * n]
        send_refs, recv_refs = refs[2 * n:2 * n + ns], refs[2 * n + ns:2 * n + 2 * ns]
        x, y, c, _ = _position()
        for t in range(n):
            for k, (s, d) in enumerate(waits_of(t, src_refs[t], land_refs[t])):
                cp = pltpu.make_async_remote_copy(src_ref=s, dst_ref=d, send_sem=send_refs[t * n_copies + k],
                                                  recv_sem=recv_refs[t * n_copies + k], device_id=(x, y, 1 - c),
                                                  device_id_type=MESH)
                cp.wait_send()
                cp.wait_recv()

    out = pl.pallas_call(
        body, name=name,
        in_specs=[_hbm_spec()] * (2 * n) + [_sem_spec()] * (2 * ns) + [_any()],
        out_specs=[_hbm_spec()] * (2 * n),
        out_shape=[pltpu.HBM(v.shape, v.dtype) for v in list(srcs) + list(lands)],
        input_output_aliases={i: i for i in range(2 * n)},
        compiler_params=_side_effects(),
    )(*srcs, *lands, *send_sems, *recv_sems, after)
    return out[:n], out[n:]


def _block_of(px, py, pc):
    return 4 * px + 2 * py + pc


def _gather_phase1_start(name, xs, after):
    lands = [lax.empty((N_DEV,) + v.shape, v.dtype) for v in xs]

    def copies_of(t, x_ref, land_ref):
        x, y, c, chips = _position()
        dst = land_ref.at[_block_of(x, y, c)]
        return [(x_ref, dst, (x, y, 1 - c))] + [(x_ref, dst, (*chip, c)) for chip in chips]

    return _split_start(name, xs, lands, 4, copies_of, after)


def _gather_phase1_wait(name, started, after):
    send_sems, recv_sems, xs, lands, _ = started

    def waits_of(t, x_ref, land_ref):
        x, y, c, chips = _position()
        return [(x_ref, land_ref.at[_block_of(x, y, 1 - c)])] + [(x_ref, land_ref.at[_block_of(*chip, c)])
                                                                  for chip in chips]

    return _split_wait(name, send_sems, recv_sems, xs, lands, 4, waits_of, after)


def _gather_phase2_start(name, lands, after):
    keep = [lax.empty((8, LANES), v.dtype) for v in lands]

    def copies_of(t, _, land_ref):
        x, y, c, chips = _position()
        return [(land_ref.at[_block_of(*chip, c)], land_ref.at[_block_of(*chip, c)], (x, y, 1 - c)) for chip in chips]

    return _split_start(name, keep, lands, 3, copies_of, after)


def _gather_phase2_wait(name, started, after):
    send_sems, recv_sems, keep, lands, _ = started

    def waits_of(t, _, land_ref):
        x, y, c, chips = _position()
        return [(land_ref.at[_block_of(*chip, c)], land_ref.at[_block_of(*chip, 1 - c)]) for chip in chips]

    return _split_wait(name, send_sems, recv_sems, keep, lands, 3, waits_of, after)[1]


def _place_block(name, land, src, block_idx):
    R, C = src.shape
    tr = _tile(R, 512)

    def body(b_ref, land_ref, s_ref, o_ref):
        o_ref[...] = s_ref[...]

    return pl.pallas_call(
        body, name=name,
        grid_spec=pltpu.PrefetchScalarGridSpec(
            num_scalar_prefetch=1, grid=(R // tr,),
            in_specs=[_any(), pl.BlockSpec((tr, C), lambda i, b: (i, 0))],
            out_specs=pl.BlockSpec((None, tr, C), lambda i, b: (b[0], i, 0))),
        out_shape=jax.ShapeDtypeStruct(land.shape, land.dtype),
        input_output_aliases={1: 0},
        compiler_params=_params("parallel"),
    )(block_idx, land, src)


def _sibling_start(name, gs, after):
    g4 = [g.reshape((4, 2) + g.shape[1:]) for g in gs]
    lands = [lax.empty((4, 1) + g.shape[2:], g.dtype) for g in g4]

    def copies_of(t, g_ref, land_ref):
        x, y, c, _ = _position()
        return [(g_ref.at[:, pl.ds(1 - c, 1)], land_ref, (x, y, 1 - c))]

    return _split_start(name, g4, lands, 1, copies_of, after)


def _sibling_wait(name, started, after):
    send_sems, recv_sems, g4, lands, _ = started

    def waits_of(t, g_ref, land_ref):
        x, y, c, _ = _position()
        return [(g_ref.at[:, pl.ds(1 - c, 1)], land_ref)]

    return _split_wait(name, send_sems, recv_sems, g4, lands, 1, waits_of, after)


def _scatter_chips_start(name, ps, after):
    lands = [lax.empty((3,) + v.shape[1:], v.dtype) for v in ps]

    def copies_of(t, p_ref, land_ref):
        x, y, c, chips = _position()
        return [(p_ref.at[2 * chip[0] + chip[1]], land_ref.at[k], (*chip, c)) for k, chip in enumerate(chips)]

    return _split_start(name, ps, lands, 3, copies_of, after)


def _scatter_chips_wait(name, started, after):
    send_sems, recv_sems, ps, lands, _ = started

    def waits_of(t, p_ref, land_ref):
        x, y, c, chips = _position()
        return [(p_ref.at[2 * chip[0] + chip[1]], land_ref.at[k]) for k, chip in enumerate(chips)]

    return _split_wait(name, send_sems, recv_sems, ps, lands, 3, waits_of, after)


def _scatter_chips_after_sibling(tag, sibling, c_idx, after):
    g4, recv = _sibling_wait("rs_sibling_wait_" + tag, sibling, after)
    ps = [_add_sibling(g, r, c_idx) for g, r in zip(g4, recv)]
    return _scatter_chips_start("rs_chips_start_" + tag, ps, ps[0])


def _sum_devices(parts):
    _, R, C = parts.shape
    tr = _tile(R, 512)

    def body(p_ref, o_ref):
        acc = p_ref[0]
        for d in range(1, N_DEV):
            acc = acc + p_ref[d]
        o_ref[...] = acc

    return pl.pallas_call(
        body, name="sum_devices", grid=(R // tr,),
        in_specs=[pl.BlockSpec((N_DEV, tr, C), lambda i: (0, i, 0))],
        out_specs=pl.BlockSpec((tr, C), lambda i: (i, 0)),
        out_shape=jax.ShapeDtypeStruct((R, C), F32),
        compiler_params=_params("parallel"),
    )(parts)


def _adam_math(w, g, m, v):
    m = ADAM_B1 * m + (1.0 - ADAM_B1) * g
    v = ADAM_B2 * v + (1.0 - ADAM_B2) * (g * g)
    m_hat = m / (1.0 - ADAM_B1 ** ADAM_STEP)
    v_hat = v / (1.0 - ADAM_B2 ** ADAM_STEP)
    return -ADAM_LR * (m_hat / (jnp.sqrt(v_hat) + ADAM_EPS) + ADAM_WD * w), m, v


def _adam_layer(name, layer, w, m, v, own, landed, chip_idx, prev, after):
    L, R, C = w.shape
    tr = _tile(R, 256)
    slab = pl.BlockSpec((None, tr, C), lambda i, q: (layer, i, 0))

    def body(q_ref, w_ref, m_ref, v_ref, own_ref, land_ref, after_ref, *rest):
        g_ref, d_ref, nm_ref, nv_ref = rest[-4:]
        g = land_ref[0].astype(F32)
        for k in range(1, 3):
            g = g + land_ref[k].astype(F32)
        g = g + own_ref[...].astype(F32)
        d, nm, nv = _adam_math(w_ref[...], g, m_ref[...], v_ref[...])
        g_ref[...] = g
        d_ref[...] = d
        nm_ref[...] = nm
        nv_ref[...] = nv

    n_prev = 0 if prev is None else 4
    out = jax.ShapeDtypeStruct((L, R, C), F32)
    return pl.pallas_call(
        body, name=name,
        grid_spec=pltpu.PrefetchScalarGridSpec(
            num_scalar_prefetch=1, grid=(R // tr,),
            in_specs=[slab, slab, slab, pl.BlockSpec((None, tr, C), lambda i, q: (q[0], i, 0)),
                      pl.BlockSpec((3, tr, C), lambda i, q: (0, i, 0)), _any()] + [_any()] * n_prev,
            out_specs=[slab] * 4),
        out_shape=[out] * 4,
        input_output_aliases={7 + k: k for k in range(n_prev)},
        compiler_params=_params("parallel"),
    )(chip_idx, w, m, v, own, landed, after, *(prev or ()))


def _adam_small(w, g, m, v):
    R, C = w.shape
    tr = _tile(R, 512)
    spec = pl.BlockSpec((tr, C), lambda i: (i, 0))

    def body(w_ref, g_ref, m_ref, v_ref, d_ref, nm_ref, nv_ref):
        d, nm, nv = _adam_math(w_ref[...], g_ref[...], m_ref[...], v_ref[...])
        d_ref[...] = d
        nm_ref[...] = nm
        nv_ref[...] = nv

    out = jax.ShapeDtypeStruct((R, C), F32)
    return pl.pallas_call(
        body, name="adam_small", grid=(R // tr,),
        in_specs=[spec] * 4, out_specs=[spec] * 3, out_shape=[out] * 3,
        compiler_params=_params("parallel"),
    )(w, g, m, v)


def _pack(arrays, row_multiple=512):
    flat = jnp.concatenate([a.reshape(-1).astype(F32) for a in arrays])
    per = row_multiple * LANES
    total = -(-flat.shape[0] // per) * per
    return jnp.pad(flat, (0, total - flat.shape[0])).reshape(total // LANES, LANES)


def _unpack(packed, shapes):
    flat = packed.reshape(-1)
    out, off = [], 0
    for s in shapes:
        n = math.prod(s)
        out.append(flat[off:off + n].reshape(s))
        off += n
    return out


BIG = ["ffn1_w_in", "ffn1_w_out", "mix_w_in", "mix_w_out", "ffn2_w_in", "ffn2_w_out"]
SHARDED_SMALL = ["conv_w", "lru_conv_w"]
WEIGHTS = ["norm_ffn1", "ffn1_w_in", "ffn1_w_out", "norm_mix", "mix_w_in", "conv_w", "fgate_b", "lru_conv_w",
           "lru_conv_b", "lru_w_a", "lru_b_a", "lru_w_x", "lru_b_x", "lru_lambda", "mix_out_norm", "mix_w_out",
           "norm_ffn2", "ffn2_w_in", "ffn2_w_out", "final_norm"]
REPLICATED = [n for n in WEIGHTS if n not in BIG and n not in SHARDED_SMALL]


def _step(x, target, w, m, v):
    L = w["norm_ffn1"].shape[0]
    T, D = x.shape[1], x.shape[2]
    dc, da, dl, nh, n_main = _mixer_dims(D)
    xi, yi, ci = lax.axis_index("x"), lax.axis_index("y"), lax.axis_index("c")
    me = 4 * xi + 2 * yi + ci
    c_idx = ci.astype(jnp.int32).reshape(1)

    taps = jnp.concatenate([w["conv_w"].reshape(-1), w["lru_conv_w"].reshape(-1)])
    taps = jnp.pad(taps, (0, (-taps.shape[0]) % (8 * LANES))).reshape(-1, LANES)
    def shards_of(l):
        return [w[n][l].astype(BF) for n in BIG] + ([taps] if l == 0 else [])

    me_idx = me.astype(jnp.int32).reshape(1)
    chip_idx = (2 * xi + yi).astype(jnp.int32).reshape(1)

    def place_own(tag, lands, own):
        return [_place_block("gather_own_" + tag, land, src, me_idx) for land, src in zip(lands, own)]

    gathered = [None] * L
    phase1 = [None] * L
    shards0 = shards_of(0)
    first_start = _gather_phase1_start("gather_p1_start_0_first", shards0[:2] + [taps], taps)
    rest_start = _gather_phase1_start("gather_p1_start_0_rest", shards0[2:len(BIG)], first_start[-1])
    own, landed = _gather_phase1_wait("gather_p1_wait_0_first", first_start, rest_start[-1])
    phase2 = _gather_phase2_start("gather_p2_start_0_first", landed, landed[0])
    landed = _gather_phase2_wait("gather_p2_wait_0_first", phase2, phase2[-1])
    first0 = place_own("0_first", landed, own)
    taps_all = first0[2].reshape(N_DEV, -1)
    n_cw = math.prod(w["conv_w"].shape)
    ch = w["conv_w"].shape[-1]
    conv_w_full = jnp.moveaxis(taps_all[:, :n_cw].reshape((N_DEV,) + w["conv_w"].shape), 0, -2).reshape(L, 3, N_DEV * ch)
    n_lw = math.prod(w["lru_conv_w"].shape)
    lru_conv_w_full = jnp.moveaxis(taps_all[:, n_cw:n_cw + n_lw].reshape((N_DEV,) + w["lru_conv_w"].shape), 0, -2
                                   ).reshape(L, 4, N_DEV * ch)

    def layer_weights(l):
        g = dict(zip(BIG, gathered[l]))
        F = g["ffn1_w_out"].shape[1] * N_DEV
        wmix = jnp.transpose(g["mix_w_in"], (1, 0, 2)).reshape(D, -1)
        return {"ffn1_w_in": g["ffn1_w_in"], "ffn1_w_out": g["ffn1_w_out"].reshape(F, D),
                "wp": _pad_mix_w_in(wmix), "wo": g["mix_w_out"].reshape(D, D),
                "ffn2_w_in": g["ffn2_w_in"], "ffn2_w_out": g["ffn2_w_out"].reshape(F, D)}

    def small_params(l):
        return {"fgate_b": jnp.pad(w["fgate_b"][l], (0, LANES - nh)).reshape(1, LANES),
                "conv_w": conv_w_full[l], "lru_conv_w": lru_conv_w_full[l],
                "lru_conv_b": w["lru_conv_b"][l].reshape(1, dl), "lru_w_a": w["lru_w_a"][l],
                "lru_b_a": w["lru_b_a"][l].reshape(1, dl), "lru_w_x": w["lru_w_x"][l],
                "lru_b_x": w["lru_b_x"][l].reshape(1, dl), "lru_lambda": w["lru_lambda"][l].reshape(1, dl),
                "mix_out_norm": w["mix_out_norm"][l].reshape(1, D)}

    gain = lambda n, l, token=None: w[n][l].reshape(1, D) + (0.0 if token is None else token[0:1, 0:1])

    xc = x[0]
    saved, lw, sps = [], [], []
    for l in range(L):
        w_in1, w_out1 = first0[:2] if l == 0 else gathered[l][:2]
        xc, s1 = _ffn_forward(xc, gain("norm_ffn1", l), w_in1, w_out1.reshape(-1, D))
        if l == 0:
            own, landed = _gather_phase1_wait("gather_p1_wait_0_rest", rest_start, xc)
            phase2 = _gather_phase2_start("gather_p2_start_0_rest", landed, landed[0])
            order_token = phase2[-1]
            if L > 1:
                phase1[1] = _gather_phase1_start("gather_p1_start_1", shards_of(1), phase2[-1])
                order_token = phase1[1][-1]
            landed = _gather_phase2_wait("gather_p2_wait_0_rest", phase2, order_token)
            gathered[0] = first0[:2] + place_own("0_rest", landed, own)
        lw.append(layer_weights(l))
        sps.append(small_params(l))
        xc, s2 = _mixer_forward(xc, gain("norm_mix", l), lw[l]["wp"], lw[l]["wo"], sps[l])
        order_token = None
        if l + 1 < L:
            own, landed = _gather_phase1_wait("gather_p1_wait_%d" % (l + 1), phase1[l + 1], xc)
            phase2 = _gather_phase2_start("gather_p2_start_%d" % (l + 1), landed, landed[0])
            order_token = phase2[-1]
            if l + 2 < L:
                phase1[l + 2] = _gather_phase1_start("gather_p1_start_%d" % (l + 2), shards_of(l + 2), phase2[-1])
                order_token = phase1[l + 2][-1]
        xc, s3 = _ffn_forward(xc, gain("norm_ffn2", l, order_token), lw[l]["ffn2_w_in"], lw[l]["ffn2_w_out"])
        if l + 1 < L:
            landed = _gather_phase2_wait("gather_p2_wait_%d" % (l + 1), phase2, xc)
            gathered[l + 1] = place_own(str(l + 1), landed, own)
        saved.append((s1, s2, s3))
    loss_part, dx, d_final = _loss_head(xc, w["final_norm"].reshape(1, D), target[0])

    small_grads = [None] * L
    parts = [None] * L
    sibling = [None] * L
    scatter = [None] * L
    scatter_early = []
    for l in reversed(range(L)):
        s1, s2, s3 = saved[l]
        dh, dw3_2, dwout_2 = _ffn_backward(dx, s3, lw[l]["ffn2_w_in"], lw[l]["ffn2_w_out"])
        dx, dg3 = _rmsnorm_bwd(dh, s3[0], gain("norm_ffn2", l), dx)
        sp_l = sps[l]
        if l + 1 < L:
            scatter[l + 1] = _scatter_chips_after_sibling(str(l + 1), sibling[l + 1], c_idx, dx)
            sp_l = dict(sp_l, mix_out_norm=sp_l["mix_out_norm"] + scatter[l + 1][-1][0:1, 0:1])
        dx, dwp, dwo, sg = _mixer_backward(dx, s2, gain("norm_mix", l), lw[l]["wp"], lw[l]["wo"], sp_l)
        dmix = jnp.transpose(_unpad_mix_w_in(dwp).reshape(D, N_DEV, -1), (1, 0, 2))
        grads = [dmix, dwo.reshape(N_DEV, D // N_DEV, D), dw3_2, dwout_2]
        between = None
        if l == 0:
            early = _sibling_start("rs_sibling_start_0_early", grads, dwp)
            grads = []

            def between(dgu):
                scatter_early.append(_scatter_chips_after_sibling("0_early", early, c_idx, dgu))
                return scatter_early[0][-1]

        dh, dw3_1, dwout_1 = _ffn_backward(dx, s1, lw[l]["ffn1_w_in"], lw[l]["ffn1_w_out"], between)
        if l + 1 < L:
            parts[l + 1] = _scatter_chips_wait("rs_chips_wait_%d" % (l + 1), scatter[l + 1], dh)
        sibling[l] = _sibling_start("rs_sibling_start_%d" % l, [dw3_1, dwout_1] + grads, dh)
        dx, dg1 = _rmsnorm_bwd(dh, s1[0], gain("norm_ffn1", l, sibling[l][-1]), dx)
        sg["norm_ffn2"], sg["norm_ffn1"] = dg3, dg1
        small_grads[l] = sg

    out_g, out_d, out_m, out_v = {}, {}, {}, {}
    prev = {n: None for n in BIG}

    def adam_big(l, after):
        own, landed = parts[l]
        for t, n in enumerate(BIG):
            prev[n] = _adam_layer("adam_%s_%d" % (n, l), l, w[n], m[n], v[n], own[t], landed[t], chip_idx, prev[n], after)

    small_names = REPLICATED + SHARDED_SMALL
    partial = []
    for n in small_names:
        if n == "final_norm":
            partial.append(d_final)
        else:
            partial.append(jnp.stack([small_grads[l][n].reshape(w[n].shape[1:]) if n not in SHARDED_SMALL
                                      else small_grads[l][n] for l in range(L)]))
    partial.append(loss_part[0, :1])
    packed = _pack(partial)
    small_gathered = _all_gather("gather_small_grads", [packed])[0]
    scatter[0] = _scatter_chips_after_sibling("0", sibling[0], c_idx, small_gathered)
    for l in reversed(range(1, L)):
        adam_big(l, scatter[0][-1])
    summed = _sum_devices(small_gathered)
    full_shapes = [w[n].shape for n in REPLICATED] + [(L, 3, N_DEV * ch), (L, 4, N_DEV * ch), (1,)]
    full = _unpack(summed, full_shapes)
    loss = full[-1][0]
    g_small = dict(zip(small_names, full[:-1]))
    for n in SHARDED_SMALL:
        g_small[n] = lax.dynamic_slice_in_dim(g_small[n], me * ch, ch, axis=2)
    shapes = [w[n].shape for n in small_names]
    d_s, m_s, v_s = _adam_small(_pack([w[n] for n in small_names]), _pack([g_small[n] for n in small_names]),
                                _pack([m[n] for n in small_names]), _pack([v[n] for n in small_names]))
    for n, d_, m_, v_ in zip(small_names, _unpack(d_s, shapes), _unpack(m_s, shapes), _unpack(v_s, shapes)):
        out_g[n], out_d[n], out_m[n], out_v[n] = g_small[n], d_, m_, v_

    behind = d_s if L == 1 else prev[BIG[-1]][0]
    own_early, landed_early = _scatter_chips_wait("rs_chips_wait_0_early", scatter_early[0], behind)
    own_late, landed_late = _scatter_chips_wait("rs_chips_wait_0", scatter[0], behind)
    parts[0] = (list(own_late) + list(own_early), list(landed_late) + list(landed_early))
    adam_big(0, d_s)
    for n in BIG:
        out_g[n], out_d[n], out_m[n], out_v[n] = prev[n]

    return (loss, dx[None], *[out_g[n] for n in WEIGHTS], *[out_d[n] for n in WEIGHTS],
            *[out_m[n] for n in WEIGHTS], *[out_v[n] for n in WEIGHTS])


def kernel(x, norm_ffn1, ffn1_w_in, ffn1_w_out, norm_mix, mix_w_in, conv_w, fgate_b, lru_conv_w, lru_conv_b, lru_w_a, lru_b_a, lru_w_x, lru_b_x, lru_lambda, mix_out_norm, mix_w_out, norm_ffn2, ffn2_w_in, ffn2_w_out, final_norm, loss_target, m_norm_ffn1, m_ffn1_w_in, m_ffn1_w_out, m_norm_mix, m_mix_w_in, m_conv_w, m_fgate_b, m_lru_conv_w, m_lru_conv_b, m_lru_w_a, m_lru_b_a, m_lru_w_x, m_lru_b_x, m_lru_lambda, m_mix_out_norm, m_mix_w_out, m_norm_ffn2, m_ffn2_w_in, m_ffn2_w_out, m_final_norm, v_norm_ffn1, v_ffn1_w_in, v_ffn1_w_out, v_norm_mix, v_mix_w_in, v_conv_w, v_fgate_b, v_lru_conv_w, v_lru_conv_b, v_lru_w_a, v_lru_b_a, v_lru_w_x, v_lru_b_x, v_lru_lambda, v_mix_out_norm, v_mix_w_out, v_norm_ffn2, v_ffn2_w_in, v_ffn2_w_out, v_final_norm):
    w = dict(norm_ffn1=norm_ffn1, ffn1_w_in=ffn1_w_in, ffn1_w_out=ffn1_w_out, norm_mix=norm_mix, mix_w_in=mix_w_in,
             conv_w=conv_w, fgate_b=fgate_b, lru_conv_w=lru_conv_w, lru_conv_b=lru_conv_b, lru_w_a=lru_w_a,
             lru_b_a=lru_b_a, lru_w_x=lru_w_x, lru_b_x=lru_b_x, lru_lambda=lru_lambda, mix_out_norm=mix_out_norm,
             mix_w_out=mix_w_out, norm_ffn2=norm_ffn2, ffn2_w_in=ffn2_w_in, ffn2_w_out=ffn2_w_out,
             final_norm=final_norm)
    m = dict(norm_ffn1=m_norm_ffn1, ffn1_w_in=m_ffn1_w_in, ffn1_w_out=m_ffn1_w_out, norm_mix=m_norm_mix,
             mix_w_in=m_mix_w_in, conv_w=m_conv_w, fgate_b=m_fgate_b, lru_conv_w=m_lru_conv_w,
             lru_conv_b=m_lru_conv_b, lru_w_a=m_lru_w_a, lru_b_a=m_lru_b_a, lru_w_x=m_lru_w_x, lru_b_x=m_lru_b_x,
             lru_lambda=m_lru_lambda, mix_out_norm=m_mix_out_norm, mix_w_out=m_mix_w_out, norm_ffn2=m_norm_ffn2,
             ffn2_w_in=m_ffn2_w_in, ffn2_w_out=m_ffn2_w_out, final_norm=m_final_norm)
    v = dict(norm_ffn1=v_norm_ffn1, ffn1_w_in=v_ffn1_w_in, ffn1_w_out=v_ffn1_w_out, norm_mix=v_norm_mix,
             mix_w_in=v_mix_w_in, conv_w=v_conv_w, fgate_b=v_fgate_b, lru_conv_w=v_lru_conv_w,
             lru_conv_b=v_lru_conv_b, lru_w_a=v_lru_w_a, lru_b_a=v_lru_b_a, lru_w_x=v_lru_w_x, lru_b_x=v_lru_b_x,
             lru_lambda=v_lru_lambda, mix_out_norm=v_mix_out_norm, mix_w_out=v_mix_w_out, norm_ffn2=v_norm_ffn2,
             ffn2_w_in=v_ffn2_w_in, ffn2_w_out=v_ffn2_w_out, final_norm=v_final_norm)
    return _step(x, loss_target, w, m, v)
```

```python
import math

import jax
import jax.numpy as jnp
from jax import lax
from jax.experimental import pallas as pl
from jax.experimental.pallas import tpu as pltpu

F32 = jnp.float32
BF = jnp.bfloat16
EPS = 1e-6
LANES = 128
VMEM_LIMIT_V7X = 56 * 1024 * 1024
MESH = pl.DeviceIdType.MESH
N_DEV = 8
LRU_C = 8.0
ADAM_LR, ADAM_B1, ADAM_B2, ADAM_EPS, ADAM_WD, ADAM_STEP = 0.001, 0.9, 0.999, 1e-08, 0.01, 10
GELU_C = math.sqrt(2.0 / math.pi)
GELU_K = 0.044715


def _params(*sem):
    return pltpu.CompilerParams(dimension_semantics=sem, vmem_limit_bytes=VMEM_LIMIT_V7X)


def _any():
    return pl.BlockSpec(memory_space=pl.ANY)


def _tile(n, target):
    if n <= target:
        return n
    t = target - target % 16
    while t >= 16:
        if n % t == 0:
            return t
        t -= 16
    return n


def _mm(name, grid, a, b, o, dims, scale=1.0, resid=None, after=None):
    nk = grid[-1]
    acc_shape = tuple(d for d in o[1] if d is not None)
    has_resid = resid is not None
    n_in = 2 + has_resid + (after is not None)

    def body(*refs):
        a_ref, b_ref = refs[0], refs[1]
        r_ref = refs[2] if has_resid else None
        o_ref = refs[n_in]

        def finish(acc):
            r = acc * scale if scale != 1.0 else acc
            if has_resid:
                r = r + r_ref[...]
            o_ref[...] = r.astype(o_ref.dtype)

        def product():
            return lax.dot_general(a_ref[...].astype(BF), b_ref[...].astype(BF), (dims, ((), ())),
                                   preferred_element_type=F32)

        if nk == 1:
            finish(product())
        else:
            acc_ref = refs[-1]
            k = pl.program_id(len(grid) - 1)

            @pl.when(k == 0)
            def _():
                acc_ref[...] = jnp.zeros(acc_ref.shape, F32)

            acc_ref[...] += product()

            @pl.when(k == nk - 1)
            def _():
                finish(acc_ref[...])

    ins = [a, b] + ([resid] if has_resid else [])
    return pl.pallas_call(
        body, name=name, grid=grid,
        in_specs=[pl.BlockSpec(blk, idx) for (_, blk, idx) in ins] + ([_any()] if after is not None else []),
        out_specs=pl.BlockSpec(o[1], o[2]),
        out_shape=o[0],
        scratch_shapes=[pltpu.VMEM(acc_shape, F32)] if nk > 1 else [],
        compiler_params=_params(*(["parallel"] * (len(grid) - 1) + ["arbitrary"])),
    )(*[x[0] for x in ins], *([after] if after is not None else []))


NN = ((1,), (0,))
NT = ((1,), (1,))
TN = ((0,), (0,))


def _rmsnorm(x, gain):
    T, D = x.shape
    tr = _tile(T, 512)

    def body(x_ref, g_ref, o_ref):
        xv = x_ref[...]
        r = lax.rsqrt(jnp.mean(xv * xv, axis=-1, keepdims=True) + EPS)
        o_ref[...] = (xv * r * g_ref[...]).astype(BF)

    return pl.pallas_call(
        body, name="rmsnorm_fwd", grid=(T // tr,),
        in_specs=[pl.BlockSpec((tr, D), lambda i: (i, 0)), pl.BlockSpec((1, D), lambda i: (0, 0))],
        out_specs=pl.BlockSpec((tr, D), lambda i: (i, 0)),
        out_shape=jax.ShapeDtypeStruct((T, D), BF),
        compiler_params=_params("parallel"),
    )(x, gain)


def _rmsnorm_bwd(dh, x, gain, dres):
    T, D = x.shape
    tr = _tile(T, 256)

    def body(dh_ref, x_ref, g_ref, dres_ref, dx_ref, dg_ref):
        i = pl.program_id(0)
        xv = x_ref[...]
        r = lax.rsqrt(jnp.mean(xv * xv, axis=-1, keepdims=True) + EPS)
        xh = xv * r
        dy = dh_ref[...].astype(F32)
        dgp = jnp.sum(dy * xh, axis=0, keepdims=True)

        @pl.when(i == 0)
        def _():
            dg_ref[...] = dgp

        @pl.when(i > 0)
        def _():
            dg_ref[...] += dgp

        dxh = dy * g_ref[...]
        dx_ref[...] = dres_ref[...] + r * (dxh - xh * jnp.mean(dxh * xh, axis=-1, keepdims=True))

    return pl.pallas_call(
        body, name="rmsnorm_bwd", grid=(T // tr,),
        in_specs=[pl.BlockSpec((tr, D), lambda i: (i, 0)), pl.BlockSpec((tr, D), lambda i: (i, 0)),
                  pl.BlockSpec((1, D), lambda i: (0, 0)), pl.BlockSpec((tr, D), lambda i: (i, 0))],
        out_specs=[pl.BlockSpec((tr, D), lambda i: (i, 0)), pl.BlockSpec((1, D), lambda i: (0, 0))],
        out_shape=[jax.ShapeDtypeStruct((T, D), F32), jax.ShapeDtypeStruct((1, D), F32)],
        compiler_params=_params("arbitrary"),
    )(dh, x, gain, dres)


def _loss_head(x, gain, target):
    T, D = x.shape
    tr = _tile(T, 256)

    def body(x_ref, g_ref, t_ref, loss_ref, dx_ref, dg_ref):
        i = pl.program_id(0)
        xv = x_ref[...]
        g = g_ref[...]
        r = lax.rsqrt(jnp.mean(xv * xv, axis=-1, keepdims=True) + EPS)
        xh = xv * r
        err = xh * g - t_ref[...]
        lp = 0.5 * jnp.sum(jnp.mean(err * err, axis=-1, keepdims=True), axis=0, keepdims=True)
        dy = err * (1.0 / D)
        dgp = jnp.sum(dy * xh, axis=0, keepdims=True)

        @pl.when(i == 0)
        def _():
            loss_ref[...] = jnp.broadcast_to(lp, loss_ref.shape)
            dg_ref[...] = dgp

        @pl.when(i > 0)
        def _():
            loss_ref[...] += jnp.broadcast_to(lp, loss_ref.shape)
            dg_ref[...] += dgp

        dxh = dy * g
        dx_ref[...] = r * (dxh - xh * jnp.mean(dxh * xh, axis=-1, keepdims=True))

    return pl.pallas_call(
        body, name="loss_head", grid=(T // tr,),
        in_specs=[pl.BlockSpec((tr, D), lambda i: (i, 0)), pl.BlockSpec((1, D), lambda i: (0, 0)),
                  pl.BlockSpec((tr, D), lambda i: (i, 0))],
        out_specs=[pl.BlockSpec((1, LANES), lambda i: (0, 0)), pl.BlockSpec((tr, D), lambda i: (i, 0)),
                   pl.BlockSpec((1, D), lambda i: (0, 0))],
        out_shape=[jax.ShapeDtypeStruct((1, LANES), F32), jax.ShapeDtypeStruct((T, D), F32),
                   jax.ShapeDtypeStruct((1, D), F32)],
        compiler_params=_params("arbitrary"),
    )(x, gain, target)


def _group_slices(D):
    dc, da = D // 4, D // 2
    return [(0, dc), (dc, dc + da), (dc + da, D)]


def _groupnorm(yc, ya, yl, gain):
    T = yc.shape[0]
    D = yc.shape[1] + ya.shape[1] + yl.shape[1]
    tr = _tile(T, 512)
    sl = _group_slices(D)

    def body(yc_ref, ya_ref, yl_ref, g_ref, o_ref):
        for y_ref, (lo, hi) in zip((yc_ref, ya_ref, yl_ref), sl):
            y = y_ref[...]
            r = lax.rsqrt(jnp.mean(y * y, axis=-1, keepdims=True) + EPS)
            o_ref[:, lo:hi] = (y * r * g_ref[:, lo:hi]).astype(BF)

    return pl.pallas_call(
        body, name="groupnorm_fwd", grid=(T // tr,),
        in_specs=[pl.BlockSpec((tr, y.shape[1]), lambda i: (i, 0)) for y in (yc, ya, yl)]
        + [pl.BlockSpec((1, D), lambda i: (0, 0))],
        out_specs=pl.BlockSpec((tr, D), lambda i: (i, 0)),
        out_shape=jax.ShapeDtypeStruct((T, D), BF),
        compiler_params=_params("parallel"),
    )(yc, ya, yl, gain)


def _groupnorm_bwd(dyn, yc, ya, yl, gain):
    T, D = dyn.shape
    tr = _tile(T, 512)
    sl = _group_slices(D)

    def body(dyn_ref, yc_ref, ya_ref, yl_ref, g_ref, dc_ref, da_ref, dl_ref, dg_ref):
        i = pl.program_id(0)
        for y_ref, d_ref, (lo, hi) in zip((yc_ref, ya_ref, yl_ref), (dc_ref, da_ref, dl_ref), sl):
            y = y_ref[...]
            r = lax.rsqrt(jnp.mean(y * y, axis=-1, keepdims=True) + EPS)
            yh = y * r
            dy = dyn_ref[:, lo:hi]
            dgp = jnp.sum(dy * yh, axis=0, keepdims=True)

            @pl.when(i == 0)
            def _():
                dg_ref[:, lo:hi] = dgp

            @pl.when(i > 0)
            def _():
                dg_ref[:, lo:hi] += dgp

            dyh = dy * g_ref[:, lo:hi]
            d_ref[...] = r * (dyh - yh * jnp.mean(dyh * yh, axis=-1, keepdims=True))

    return pl.pallas_call(
        body, name="groupnorm_bwd", grid=(T // tr,),
        in_specs=[pl.BlockSpec((tr, D), lambda i: (i, 0))]
        + [pl.BlockSpec((tr, y.shape[1]), lambda i: (i, 0)) for y in (yc, ya, yl)]
        + [pl.BlockSpec((1, D), lambda i: (0, 0))],
        out_specs=[pl.BlockSpec((tr, y.shape[1]), lambda i: (i, 0)) for y in (yc, ya, yl)]
        + [pl.BlockSpec((1, D), lambda i: (0, 0))],
        out_shape=[jax.ShapeDtypeStruct(y.shape, F32) for y in (yc, ya, yl)] + [jax.ShapeDtypeStruct((1, D), F32)],
        compiler_params=_params("arbitrary"),
    )(dyn, yc, ya, yl, gain)


def _ffn_in(h, w3):
    T, D = h.shape
    tn = w3.shape[2]
    F = 4 * tn
    tm = _tile(T, 512)

    def body(h_ref, wg_ref, wu_ref, g_ref, u_ref, a_ref):
        hv = h_ref[...]
        g = jnp.dot(hv, wg_ref[...], preferred_element_type=F32)
        u = jnp.dot(hv, wu_ref[...], preferred_element_type=F32)
        g_ref[...] = g.astype(BF)
        u_ref[...] = u.astype(BF)
        a_ref[...] = (g * jax.nn.sigmoid(g) * u).astype(BF)

    out = jax.ShapeDtypeStruct((T, F), BF)
    return pl.pallas_call(
        body, name="ffn_in_swiglu", grid=(4, T // tm),
        in_specs=[pl.BlockSpec((tm, D), lambda j, i: (i, 0)),
                  pl.BlockSpec((None, D, tn), lambda j, i: (j, 0, 0)),
                  pl.BlockSpec((None, D, tn), lambda j, i: (j + 4, 0, 0))],
        out_specs=[pl.BlockSpec((tm, tn), lambda j, i: (i, j))] * 3,
        out_shape=[out, out, out],
        compiler_params=_params("parallel", "parallel"),
    )(h, w3, w3)


def _ffn_bwd_in(dx, wout, g, u):
    T, D = dx.shape
    F = wout.shape[0]
    tn = F // 4
    tm = _tile(T, 512)

    def body(dx_ref, w_ref, g_ref, u_ref, o_ref):
        da = 0.5 * lax.dot_general(dx_ref[...].astype(BF), w_ref[...], (NT, ((), ())), preferred_element_type=F32)
        gv = g_ref[...].astype(F32)
        s = jax.nn.sigmoid(gv)
        o_ref[0] = (da * u_ref[...].astype(F32) * (s * (1.0 + gv * (1.0 - s)))).astype(BF)
        o_ref[1] = (da * gv * s).astype(BF)

    return pl.pallas_call(
        body, name="ffn_bwd_swiglu", grid=(4, T // tm),
        in_specs=[pl.BlockSpec((tm, D), lambda j, i: (i, 0)), pl.BlockSpec((tn, D), lambda j, i: (j, 0)),
                  pl.BlockSpec((tm, tn), lambda j, i: (i, j)), pl.BlockSpec((tm, tn), lambda j, i: (i, j))],
        out_specs=pl.BlockSpec((2, tm, tn), lambda j, i: (0, i, j)),
        out_shape=jax.ShapeDtypeStruct((2, T, F), BF),
        compiler_params=_params("parallel", "parallel"),
    )(dx, wout, g, u)


def _ffn_forward(x, gain, w3, wout):
    T, D = x.shape
    F = wout.shape[0]
    h = _rmsnorm(x, gain)
    g, u, a = _ffn_in(h, w3)
    tm, tk = _tile(T, 512), F // 4
    xn = _mm("ffn_out", (T // tm, 1, F // tk),
             (a, (tm, tk), lambda i, j, k: (i, k)), (wout, (tk, D), lambda i, j, k: (k, 0)),
             (jax.ShapeDtypeStruct((T, D), F32), (tm, D), lambda i, j, k: (i, 0)), NN, scale=0.5,
             resid=(x, (tm, D), lambda i, j, k: (i, 0)))
    return xn, (x, h, g, u, a)


def _ffn_backward(dx, saved, w3, wout, between=None):
    x, h, g, u, a = saved
    T, D = x.shape
    F = wout.shape[0]
    tn3 = F // 4
    dgu = _ffn_bwd_in(dx, wout, g, u)
    behind = None if between is None else between(dgu)
    tnd = _tile(D, 512)
    dwout = _mm("ffn_dwout", (F // tn3, D // tnd, 1),
                (a, (T, tn3), lambda i, j, k: (0, i)), (dx, (T, tnd), lambda i, j, k: (0, j)),
                (jax.ShapeDtypeStruct((F, D), BF), (tn3, tnd), lambda i, j, k: (i, j)), TN, scale=0.5, after=behind)
    dw3 = _mm("ffn_dwin", (8, D // tnd, 1),
              (h, (T, tnd), lambda s, i, k: (0, i)), (dgu, (None, T, tn3), lambda s, i, k: (s // 4, 0, s % 4)),
              (jax.ShapeDtypeStruct((8, D, tn3), BF), (None, tnd, tn3), lambda s, i, k: (s, i, 0)), TN, after=behind)
    tm = _tile(T, 1024)
    dh = _mm("ffn_dh", (T // tm, 1, 8),
             (dgu, (None, tm, tn3), lambda i, j, k: (k // 4, i, k % 4)), (w3, (None, D, tn3), lambda i, j, k: (k, 0, 0)),
             (jax.ShapeDtypeStruct((T, D), F32), (tm, D), lambda i, j, k: (i, 0)), NT, after=behind)
    return dh, dw3, dwout.reshape(N_DEV, F // N_DEV, D)


def _rows(shape):
    return lax.broadcasted_iota(jnp.int32, shape, 0)


def _down(x, s, fill, rows):
    return jnp.where(rows >= s, pltpu.roll(x, s, 0), fill)


def _up(x, s, fill, rows):
    T = x.shape[0]
    return jnp.where(rows < T - s, pltpu.roll(x, T - s, 0), fill)


def _scan_linear(a, b, rows, shift):
    T = a.shape[0]
    s = 1
    while s < T:
        b = a * shift(b, s, 0.0, rows) + b
        if 2 * s < T:
            a = a * shift(a, s, 1.0, rows)
        s *= 2
    return b


def _cumsum(c, rows, shift):
    T = c.shape[0]
    s = 1
    while s < T:
        c = c + shift(c, s, 0.0, rows)
        s *= 2
    return c


def _log1p_small(e):
    return jnp.where(e < 0.01, e * (1.0 - e * (0.5 - e * (1.0 / 3.0))), jnp.log(1.0 + e))


def _softplus(x):
    return jnp.maximum(x, 0.0) + _log1p_small(jnp.exp(-jnp.abs(x)))


def _one_minus_exp_neg(z):
    return jnp.where(z < 0.1, z * (1.0 - z * (0.5 - z * (1.0 / 6.0 - z * (1.0 / 24.0)))), 1.0 - jnp.exp(-z))


def _fgate_cum(f, b):
    T = f.shape[0]

    def body(f_ref, b_ref, o_ref):
        z = f_ref[...] + b_ref[...]
        o_ref[...] = _cumsum(-_softplus(-z), _rows(z.shape), _down)

    return pl.pallas_call(
        body, name="fgate_cumsum",
        out_shape=jax.ShapeDtypeStruct((T, LANES), F32),
        compiler_params=pltpu.CompilerParams(vmem_limit_bytes=VMEM_LIMIT_V7X),
    )(f, b)


def _fgate_cum_bwd(drow, dcol, f, b):
    T = f.shape[0]

    def body(dr_ref, dc_ref, f_ref, b_ref, df_ref, db_ref):
        z = f_ref[...] + b_ref[...]
        dlogf = _cumsum(dr_ref[...] - dc_ref[...], _rows(z.shape), _up)
        dz = dlogf * jax.nn.sigmoid(-z)
        df_ref[...] = dz.astype(BF)
        db_ref[...] = jnp.sum(dz, axis=0, keepdims=True)

    return pl.pallas_call(
        body, name="fgate_cumsum_bwd",
        out_shape=[jax.ShapeDtypeStruct((T, LANES), BF), jax.ShapeDtypeStruct((1, LANES), F32)],
        compiler_params=pltpu.CompilerParams(vmem_limit_bytes=VMEM_LIMIT_V7X),
    )(drow, dcol, f, b)


def _col(blk0):
    return lambda g: (0, blk0 + g)


def _conv_fwd(p, w, nb):
    T = p.shape[0]

    def body(b_ref, c_ref, v_ref, w_ref, o_ref):
        z = c_ref[...].astype(F32) * v_ref[...].astype(F32)
        rows = _rows(z.shape)
        conv = w_ref[2:3, :] * z + w_ref[1:2, :] * _down(z, 1, 0.0, rows) + w_ref[0:1, :] * _down(z, 2, 0.0, rows)
        o_ref[...] = b_ref[...].astype(F32) * conv

    return pl.pallas_call(
        body, name="conv_fwd", grid=(nb,),
        in_specs=[pl.BlockSpec((T, LANES), _col(0)), pl.BlockSpec((T, LANES), _col(nb)),
                  pl.BlockSpec((T, LANES), _col(2 * nb)), pl.BlockSpec((3, LANES), lambda g: (0, g))],
        out_specs=pl.BlockSpec((T, LANES), lambda g: (0, g)),
        out_shape=jax.ShapeDtypeStruct((T, nb * LANES), F32),
        compiler_params=_params("parallel"),
    )(p, p, p, w)


def _conv_bwd(dy, p, w, nb):
    T = p.shape[0]

    def body(dy_ref, b_ref, c_ref, v_ref, w_ref, db_ref, dc_ref, dv_ref, dw_ref):
        cv, vv = c_ref[...].astype(F32), v_ref[...].astype(F32)
        z = cv * vv
        rows = _rows(z.shape)
        z1, z2 = _down(z, 1, 0.0, rows), _down(z, 2, 0.0, rows)
        dyv = dy_ref[...]
        db_ref[...] = (dyv * (w_ref[2:3, :] * z + w_ref[1:2, :] * z1 + w_ref[0:1, :] * z2)).astype(BF)
        dconv = dyv * b_ref[...].astype(F32)
        dz = (w_ref[2:3, :] * dconv + w_ref[1:2, :] * _up(dconv, 1, 0.0, rows)
              + w_ref[0:1, :] * _up(dconv, 2, 0.0, rows))
        dc_ref[...] = (dz * vv).astype(BF)
        dv_ref[...] = (dz * cv).astype(BF)
        dw_ref[0:1, :] = jnp.sum(dconv * z2, axis=0, keepdims=True)
        dw_ref[1:2, :] = jnp.sum(dconv * z1, axis=0, keepdims=True)
        dw_ref[2:3, :] = jnp.sum(dconv * z, axis=0, keepdims=True)

    return pl.pallas_call(
        body, name="conv_bwd", grid=(nb,),
        in_specs=[pl.BlockSpec((T, LANES), lambda g: (0, g)), pl.BlockSpec((T, LANES), _col(0)),
                  pl.BlockSpec((T, LANES), _col(nb)), pl.BlockSpec((T, LANES), _col(2 * nb)),
                  pl.BlockSpec((3, LANES), lambda g: (0, g))],
        out_specs=[pl.BlockSpec((T, LANES), lambda g: (0, g))] * 3 + [pl.BlockSpec((3, LANES), lambda g: (0, g))],
        out_shape=[jax.ShapeDtypeStruct((T, nb * LANES), BF)] * 3 + [jax.ShapeDtypeStruct((3, nb * LANES), F32)],
        compiler_params=_params("parallel"),
    )(dy, p, p, p, w)


def _gelu(x):
    t = jnp.tanh(GELU_C * (x + GELU_K * x * x * x))
    return 0.5 * x * (1.0 + t), t


def _lru_common(x, cw_ref, cb_ref, wa_ref, ba_ref, wx_ref, bx_ref, lam_ref, rows):
    xr = (cb_ref[...] + cw_ref[3:4, :] * x + cw_ref[2:3, :] * _down(x, 1, 0.0, rows)
          + cw_ref[1:2, :] * _down(x, 2, 0.0, rows) + cw_ref[0:1, :] * _down(x, 3, 0.0, rows))
    xrb = xr.astype(BF)
    r = jax.nn.sigmoid(jnp.dot(xrb, wa_ref[...].astype(BF), preferred_element_type=F32) + ba_ref[...])
    i = jax.nn.sigmoid(jnp.dot(xrb, wx_ref[...].astype(BF), preferred_element_type=F32) + bx_ref[...])
    sp = _softplus(-lam_ref[...])
    log_a = -LRU_C * r * sp
    a = jnp.exp(log_a)
    m = jnp.sqrt(_one_minus_exp_neg(-2.0 * log_a))
    return xr, xrb, r, i, sp, a, m


def _lru_specs(T, nb, gate_blk0, x_blk0):
    vec = pl.BlockSpec((1, LANES), lambda g: (0, g))
    mat = pl.BlockSpec((None, LANES, LANES), lambda g: (g, 0, 0))
    return [pl.BlockSpec((T, LANES), _col(gate_blk0)), pl.BlockSpec((T, LANES), _col(x_blk0)),
            pl.BlockSpec((4, LANES), lambda g: (0, g)), vec, mat, vec, mat, vec, vec]


def _lru_fwd(p, cw, cb, wa, ba, wx, bx, lam, nb, gate_blk0):
    T = p.shape[0]

    def body(gate_ref, x_ref, cw_ref, cb_ref, wa_ref, ba_ref, wx_ref, bx_ref, lam_ref, y_ref, h_ref):
        x = x_ref[...].astype(F32)
        rows = _rows(x.shape)
        xr, _, _, i, _, a, m = _lru_common(x, cw_ref, cb_ref, wa_ref, ba_ref, wx_ref, bx_ref, lam_ref, rows)
        h = _scan_linear(a, m * (i * xr), rows, _down)
        h_ref[...] = h
        y_ref[...] = _gelu(gate_ref[...].astype(F32))[0] * h

    out = jax.ShapeDtypeStruct((T, nb * LANES), F32)
    return pl.pallas_call(
        body, name="lru_fwd", grid=(nb,),
        in_specs=_lru_specs(T, nb, gate_blk0, gate_blk0 + nb),
        out_specs=[pl.BlockSpec((T, LANES), lambda g: (0, g))] * 2,
        out_shape=[out, out],
        compiler_params=_params("parallel"),
    )(p, p, cw, cb, wa, ba, wx, bx, lam)


def _lru_bwd(dy, hs, p, cw, cb, wa, ba, wx, bx, lam, nb, gate_blk0):
    T = p.shape[0]

    def body(dy_ref, hs_ref, gate_ref, x_ref, cw_ref, cb_ref, wa_ref, ba_ref, wx_ref, bx_ref, lam_ref,
             dgate_ref, dx_ref, dcw_ref, dcb_ref, dwa_ref, dba_ref, dwx_ref, dbx_ref, dlam_ref):
        x = x_ref[...].astype(F32)
        rows = _rows(x.shape)
        xr, xrb, r, i, sp, a, m = _lru_common(x, cw_ref, cb_ref, wa_ref, ba_ref, wx_ref, bx_ref, lam_ref, rows)
        gate = gate_ref[...].astype(F32)
        gl, t = _gelu(gate)
        h = hs_ref[...]
        dyv = dy_ref[...]
        dgelu = 0.5 * (1.0 + t) + 0.5 * gate * (1.0 - t * t) * GELU_C * (1.0 + 3.0 * GELU_K * gate * gate)
        dgate_ref[...] = (dyv * h * dgelu).astype(BF)
        lam_adj = _scan_linear(_up(a, 1, 0.0, rows), dyv * gl, rows, _up)
        da = lam_adj * _down(h, 1, 0.0, rows)
        ix = i * xr
        dix = lam_adj * m
        dm = lam_adj * ix
        dlog_a = da * a - dm * (a * a) / jnp.maximum(m, 1e-30)
        dr = dlog_a * (-LRU_C * sp)
        dsp = jnp.sum(dlog_a * (-LRU_C * r), axis=0, keepdims=True)
        dlam_ref[...] = -dsp * jax.nn.sigmoid(-lam_ref[...])
        dpa = dr * r * (1.0 - r)
        dpx = dix * xr * i * (1.0 - i)
        dpab, dpxb = dpa.astype(BF), dpx.astype(BF)
        dxr = (dix * i
               + lax.dot_general(dpab, wa_ref[...].astype(BF), (NT, ((), ())), preferred_element_type=F32)
               + lax.dot_general(dpxb, wx_ref[...].astype(BF), (NT, ((), ())), preferred_element_type=F32))
        dwa_ref[...] = lax.dot_general(xrb, dpab, (TN, ((), ())), preferred_element_type=F32)
        dwx_ref[...] = lax.dot_general(xrb, dpxb, (TN, ((), ())), preferred_element_type=F32)
        dba_ref[...] = jnp.sum(dpa, axis=0, keepdims=True)
        dbx_ref[...] = jnp.sum(dpx, axis=0, keepdims=True)
        dcb_ref[...] = jnp.sum(dxr, axis=0, keepdims=True)
        dx_ref[...] = (cw_ref[3:4, :] * dxr + cw_ref[2:3, :] * _up(dxr, 1, 0.0, rows)
                       + cw_ref[1:2, :] * _up(dxr, 2, 0.0, rows) + cw_ref[0:1, :] * _up(dxr, 3, 0.0, rows)).astype(BF)
        for k in range(4):
            xs = x if k == 3 else _down(x, 3 - k, 0.0, rows)
            dcw_ref[k:k + 1, :] = jnp.sum(dxr * xs, axis=0, keepdims=True)

    C = nb * LANES
    seq = jax.ShapeDtypeStruct((T, C), BF)
    vec = jax.ShapeDtypeStruct((1, C), F32)
    mat = jax.ShapeDtypeStruct((nb, LANES, LANES), F32)
    vspec = pl.BlockSpec((1, LANES), lambda g: (0, g))
    mspec = pl.BlockSpec((None, LANES, LANES), lambda g: (g, 0, 0))
    sspec = pl.BlockSpec((T, LANES), lambda g: (0, g))
    return pl.pallas_call(
        body, name="lru_bwd", grid=(nb,),
        in_specs=[sspec, sspec] + _lru_specs(T, nb, gate_blk0, gate_blk0 + nb),
        out_specs=[sspec, sspec, pl.BlockSpec((4, LANES), lambda g: (0, g)), vspec, mspec, vspec, mspec, vspec, vspec],
        out_shape=[seq, seq, jax.ShapeDtypeStruct((4, C), F32), vec, mat, vec, mat, vec, vec],
        compiler_params=_params("parallel"),
    )(dy, hs, p, p, cw, cb, wa, ba, wx, bx, lam)


ATTN_BLOCK = 2048


def _causal(shape, transposed=False):
    r = lax.broadcasted_iota(jnp.int32, shape, 0)
    c = lax.broadcasted_iota(jnp.int32, shape, 1)
    return r <= c if transposed else c <= r


def _causal_pairs(n, by_query):
    if by_query:
        pairs = [(i, j) for i in range(n) for j in range(i + 1)]
    else:
        pairs = [(i, j) for j in range(n) for i in range(j, n)]
    return jnp.asarray([a for a, _ in pairs], jnp.int32), jnp.asarray([b for _, b in pairs], jnp.int32)


def _attn_fwd(p, cq, ck, nh, q_blk0):
    T = p.shape[0]
    tq = _tile(T, ATTN_BLOCK)
    nq = T // tq
    scale = LANES ** -0.5

    qi, kj = _causal_pairs(nq, by_query=True)

    def body(qi_ref, kj_ref, q_ref, k_ref, v_ref, cq_ref, ck_ref, o_ref, lse_ref, m_ref, l_ref, acc_ref):
        i, j = qi_ref[pl.program_id(1)], kj_ref[pl.program_id(1)]

        @pl.when(j == 0)
        def _():
            m_ref[...] = jnp.full(m_ref.shape, -jnp.inf, F32)
            l_ref[...] = jnp.zeros(l_ref.shape, F32)
            acc_ref[...] = jnp.zeros(acc_ref.shape, F32)

        def block(diagonal):
            s = lax.dot_general(q_ref[...], k_ref[...], (NT, ((), ())), preferred_element_type=F32) * scale
            s = s + cq_ref[...] - ck_ref[...]
            if diagonal:
                s = jnp.where(_causal(s.shape), s, -jnp.inf)
            m_new = jnp.maximum(m_ref[...], jnp.max(s, axis=-1, keepdims=True))
            alpha = jnp.exp(m_ref[...] - m_new)
            pr = jnp.exp(s - m_new)
            l_ref[...] = alpha * l_ref[...] + jnp.sum(pr, axis=-1, keepdims=True)
            acc_ref[...] = alpha * acc_ref[...] + jnp.dot(pr.astype(BF), v_ref[...], preferred_element_type=F32)
            m_ref[...] = m_new

        pl.when(j < i)(lambda: block(False))

        @pl.when(j == i)
        def _():
            block(True)
            o_ref[...] = acc_ref[...] / l_ref[...]
            lse_ref[...] = m_ref[...] + jnp.log(l_ref[...])

    def kv(off):
        return pl.BlockSpec((tq, LANES), lambda h, s, qi, kj: (kj[s], q_blk0 + off * nh + h))

    return pl.pallas_call(
        body, name="attn_fwd",
        grid_spec=pltpu.PrefetchScalarGridSpec(
            num_scalar_prefetch=2, grid=(nh, qi.shape[0]),
            in_specs=[pl.BlockSpec((tq, LANES), lambda h, s, qi, kj: (qi[s], q_blk0 + h)), kv(1), kv(2),
                      pl.BlockSpec((None, tq, 1), lambda h, s, qi, kj: (h, qi[s], 0)),
                      pl.BlockSpec((None, 1, tq), lambda h, s, qi, kj: (h, 0, kj[s]))],
            out_specs=[pl.BlockSpec((tq, LANES), lambda h, s, qi, kj: (qi[s], h)),
                       pl.BlockSpec((None, tq, 1), lambda h, s, qi, kj: (h, qi[s], 0))],
            scratch_shapes=[pltpu.VMEM((tq, 1), F32), pltpu.VMEM((tq, 1), F32), pltpu.VMEM((tq, LANES), F32)]),
        out_shape=[jax.ShapeDtypeStruct((T, nh * LANES), F32), jax.ShapeDtypeStruct((nh, T, 1), F32)],
        compiler_params=_params("parallel", "arbitrary"),
    )(qi, kj, p, p, p, cq, ck)


def _attn_bwd_q(p, cq, ck, lse, do, o, nh, q_blk0):
    T = p.shape[0]
    tq = _tile(T, ATTN_BLOCK)
    nq = T // tq
    scale = LANES ** -0.5

    qi, kj = _causal_pairs(nq, by_query=True)

    def body(qi_ref, kj_ref, q_ref, k_ref, v_ref, cq_ref, ck_ref, lse_ref, do_ref, o_ref, dq_ref, dl_ref, dr_ref,
             acc_ref):
        i, j = qi_ref[pl.program_id(1)], kj_ref[pl.program_id(1)]

        @pl.when(j == 0)
        def _():
            dl_ref[...] = jnp.sum(do_ref[...] * o_ref[...], axis=-1, keepdims=True)
            dr_ref[...] = jnp.zeros(dr_ref.shape, F32)
            acc_ref[...] = jnp.zeros(acc_ref.shape, F32)

        def block(diagonal):
            s = lax.dot_general(q_ref[...], k_ref[...], (NT, ((), ())), preferred_element_type=F32) * scale
            s = s + cq_ref[...] - ck_ref[...]
            pr = jnp.exp(s - lse_ref[...])
            if diagonal:
                pr = jnp.where(_causal(s.shape), pr, 0.0)
            dp = lax.dot_general(do_ref[...].astype(BF), v_ref[...], (NT, ((), ())), preferred_element_type=F32)
            ds = pr * (dp - dl_ref[...])
            dr_ref[...] += jnp.sum(ds, axis=-1, keepdims=True)
            acc_ref[...] += jnp.dot(ds.astype(BF), k_ref[...], preferred_element_type=F32)

        pl.when(j < i)(lambda: block(False))

        @pl.when(j == i)
        def _():
            block(True)
            dq_ref[...] = (acc_ref[...] * scale).astype(BF)

    def kv(off):
        return pl.BlockSpec((tq, LANES), lambda h, s, qi, kj: (kj[s], q_blk0 + off * nh + h))

    col = pl.BlockSpec((None, tq, 1), lambda h, s, qi, kj: (h, qi[s], 0))
    head = pl.BlockSpec((tq, LANES), lambda h, s, qi, kj: (qi[s], h))
    return pl.pallas_call(
        body, name="attn_bwd_q",
        grid_spec=pltpu.PrefetchScalarGridSpec(
            num_scalar_prefetch=2, grid=(nh, qi.shape[0]),
            in_specs=[pl.BlockSpec((tq, LANES), lambda h, s, qi, kj: (qi[s], q_blk0 + h)), kv(1), kv(2), col,
                      pl.BlockSpec((None, 1, tq), lambda h, s, qi, kj: (h, 0, kj[s])), col, head, head],
            out_specs=[head, col, col],
            scratch_shapes=[pltpu.VMEM((tq, LANES), F32)]),
        out_shape=[jax.ShapeDtypeStruct((T, nh * LANES), BF), jax.ShapeDtypeStruct((nh, T, 1), F32),
                   jax.ShapeDtypeStruct((nh, T, 1), F32)],
        compiler_params=_params("parallel", "arbitrary"),
    )(qi, kj, p, p, p, cq, ck, lse, do, o)


def _attn_bwd_kv(p, cq_row, ck_col, lse_row, delta_row, do, nh, q_blk0):
    T = p.shape[0]
    tk = _tile(T, ATTN_BLOCK)
    nk = T // tk
    scale = LANES ** -0.5

    qi, kj = _causal_pairs(nk, by_query=False)

    def body(qi_ref, kj_ref, q_ref, k_ref, v_ref, cq_ref, ck_ref, lse_ref, dl_ref, do_ref, dk_ref, dv_ref, dc_ref,
             dk_acc, dv_acc, dc_acc):
        i, j = qi_ref[pl.program_id(1)], kj_ref[pl.program_id(1)]

        @pl.when(i == j)
        def _():
            dk_acc[...] = jnp.zeros(dk_acc.shape, F32)
            dv_acc[...] = jnp.zeros(dv_acc.shape, F32)
            dc_acc[...] = jnp.zeros(dc_acc.shape, F32)

        def block(diagonal):
            st = lax.dot_general(k_ref[...], q_ref[...], (NT, ((), ())), preferred_element_type=F32) * scale
            st = st + cq_ref[...] - ck_ref[...]
            pt = jnp.exp(st - lse_ref[...])
            if diagonal:
                pt = jnp.where(_causal(st.shape, transposed=True), pt, 0.0)
            dob = do_ref[...].astype(BF)
            dv_acc[...] += jnp.dot(pt.astype(BF), dob, preferred_element_type=F32)
            dpt = lax.dot_general(v_ref[...], dob, (NT, ((), ())), preferred_element_type=F32)
            dst = pt * (dpt - dl_ref[...])
            dk_acc[...] += jnp.dot(dst.astype(BF), q_ref[...], preferred_element_type=F32)
            dc_acc[...] += jnp.sum(dst, axis=-1, keepdims=True)

        pl.when(i == j)(lambda: block(True))
        pl.when(i > j)(lambda: block(False))

        @pl.when(i == nk - 1)
        def _():
            dk_ref[...] = (dk_acc[...] * scale).astype(BF)
            dv_ref[...] = dv_acc[...].astype(BF)
            dc_ref[...] = dc_acc[...]

    def qside(blk):
        return pl.BlockSpec((tk, LANES), lambda h, s, qi, kj: (qi[s], blk + h))

    def kside(off):
        return pl.BlockSpec((tk, LANES), lambda h, s, qi, kj: (kj[s], q_blk0 + off * nh + h))

    row = pl.BlockSpec((None, 1, tk), lambda h, s, qi, kj: (h, 0, qi[s]))
    col = pl.BlockSpec((None, tk, 1), lambda h, s, qi, kj: (h, kj[s], 0))
    head = pl.BlockSpec((tk, LANES), lambda h, s, qi, kj: (kj[s], h))
    return pl.pallas_call(
        body, name="attn_bwd_kv",
        grid_spec=pltpu.PrefetchScalarGridSpec(
            num_scalar_prefetch=2, grid=(nh, qi.shape[0]),
            in_specs=[qside(q_blk0), kside(1), kside(2), row, col, row, row, qside(0)],
            out_specs=[head, head, col],
            scratch_shapes=[pltpu.VMEM((tk, LANES), F32), pltpu.VMEM((tk, LANES), F32), pltpu.VMEM((tk, 1), F32)]),
        out_shape=[jax.ShapeDtypeStruct((T, nh * LANES), BF), jax.ShapeDtypeStruct((T, nh * LANES), BF),
                   jax.ShapeDtypeStruct((nh, T, 1), F32)],
        compiler_params=_params("parallel", "arbitrary"),
    )(qi, kj, p, p, p, cq_row, ck_col, lse_row, delta_row, do)


def _mixer_dims(D):
    dc, da, dl = D // 4, D // 2, D // 4
    nh = da // LANES
    n_main = 3 * dc + 3 * da + 2 * dl
    return dc, da, dl, nh, n_main


def _pad_mix_w_in(wfull):
    D = wfull.shape[0]
    dc, da, dl, nh, n_main = _mixer_dims(D)
    a = 3 * dc + 3 * da
    return jnp.concatenate([wfull[:, :a], wfull[:, a + nh:], wfull[:, a:a + nh],
                            jnp.zeros((D, LANES - nh), wfull.dtype)], axis=1)


def _unpad_mix_w_in(wp):
    D = wp.shape[0]
    dc, da, dl, nh, n_main = _mixer_dims(D)
    a = 3 * dc + 3 * da
    return jnp.concatenate([wp[:, :a], wp[:, n_main:n_main + nh], wp[:, a:n_main]], axis=1)


def _head_cols(c, nh):
    t = jnp.transpose(c[:, :nh])
    return t[:, :, None], t[:, None, :]


def _mixer_forward(x, gain, wp, wo, sp):
    T, D = x.shape
    dc, da, dl, nh, n_main = _mixer_dims(D)
    nbc, nbl = dc // LANES, dl // LANES
    h = _rmsnorm(x, gain)
    tm = _tile(T, 512)
    tn = n_main // 4
    p = _mm("mix_in", (4, T // tm, 1),
            (h, (tm, D), lambda j, i, k: (i, 0)), (wp, (D, tn), lambda j, i, k: (0, j)),
            (jax.ShapeDtypeStruct((T, n_main), BF), (tm, tn), lambda j, i, k: (i, j)), NN)
    f = _mm("mix_in_fgate", (T // tm, 1, 1),
            (h, (tm, D), lambda i, j, k: (i, 0)), (wp, (D, LANES), lambda i, j, k: (0, n_main // LANES)),
            (jax.ShapeDtypeStruct((T, LANES), F32), (tm, LANES), lambda i, j, k: (i, 0)), NN)
    cum = _fgate_cum(f, sp["fgate_b"])
    cq, ck = _head_cols(cum, nh)
    yc = _conv_fwd(p, sp["conv_w"], nbc)
    q_blk0 = 3 * nbc
    ya, lse = _attn_fwd(p, cq, ck, nh, q_blk0)
    gate_blk0 = q_blk0 + 3 * nh
    yl, hs = _lru_fwd(p, sp["lru_conv_w"], sp["lru_conv_b"], sp["lru_w_a"], sp["lru_b_a"], sp["lru_w_x"],
                      sp["lru_b_x"], sp["lru_lambda"], nbl, gate_blk0)
    yn = _groupnorm(yc, ya, yl, sp["mix_out_norm"])
    xn = _mm("mix_out", (T // tm, 1, 1),
             (yn, (tm, D), lambda i, j, k: (i, 0)), (wo, (D, D), lambda i, j, k: (0, 0)),
             (jax.ShapeDtypeStruct((T, D), F32), (tm, D), lambda i, j, k: (i, 0)), NN,
             resid=(x, (tm, D), lambda i, j, k: (i, 0)))
    return xn, (x, h, p, f, cq, ck, yc, ya, lse, yl, hs, yn)


def _mixer_backward(dx, saved, gain, wp, wo, sp):
    x, h, p, f, cq, ck, yc, ya, lse, yl, hs, yn = saved
    T, D = x.shape
    dc, da, dl, nh, n_main = _mixer_dims(D)
    nbc, nbl = dc // LANES, dl // LANES
    q_blk0 = 3 * nbc
    gate_blk0 = q_blk0 + 3 * nh
    tm = _tile(T, 512)
    tnd, tnd2 = _tile(D, 512), _tile(D, 1024)
    dyn = _mm("mix_out_dy", (T // tm, 1, 1),
              (dx, (tm, D), lambda i, j, k: (i, 0)), (wo, (D, D), lambda i, j, k: (0, 0)),
              (jax.ShapeDtypeStruct((T, D), F32), (tm, D), lambda i, j, k: (i, 0)), NT)
    dwo = _mm("mix_out_dw", (D // tnd, D // tnd2, 1),
              (yn, (T, tnd), lambda i, j, k: (0, i)), (dx, (T, tnd2), lambda i, j, k: (0, j)),
              (jax.ShapeDtypeStruct((D, D), BF), (tnd, tnd2), lambda i, j, k: (i, j)), TN)
    dyc, dya, dyl, dgn = _groupnorm_bwd(dyn, yc, ya, yl, sp["mix_out_norm"])
    dcb, dcc, dcv, dconv_w = _conv_bwd(dyc, p, sp["conv_w"], nbc)
    dq, delta, drow = _attn_bwd_q(p, cq, ck, lse, dya, ya, nh, q_blk0)
    as_row = lambda t: t.reshape(nh, 1, T)
    dk, dv, dcol = _attn_bwd_kv(p, ck, cq, as_row(lse), as_row(delta), dya, nh, q_blk0)
    lanes = lambda t: jnp.pad(jnp.transpose(t.reshape(nh, T)), ((0, 0), (0, LANES - nh)))
    df, dfb = _fgate_cum_bwd(lanes(drow), lanes(dcol), f, sp["fgate_b"])
    dgate, dlx, dcw, dcbias, dwa, dba, dwx, dbx, dlam = _lru_bwd(
        dyl, hs, p, sp["lru_conv_w"], sp["lru_conv_b"], sp["lru_w_a"], sp["lru_b_a"], sp["lru_w_x"],
        sp["lru_b_x"], sp["lru_lambda"], nbl, gate_blk0)
    dp = jnp.concatenate([dcb, dcc, dcv, dq, dk, dv, dgate, dlx, df], axis=1)
    n_pad = n_main + LANES
    tn, tkp = n_pad // 5, n_pad // 3
    dwp = _mm("mix_in_dw", (n_pad // tn, D // tnd, 1),
              (h, (T, tnd), lambda j, i, k: (0, i)), (dp, (T, tn), lambda j, i, k: (0, j)),
              (jax.ShapeDtypeStruct((D, n_pad), BF), (tnd, tn), lambda j, i, k: (i, j)), TN)
    tmh = _tile(T, 1024)
    dh = _mm("mix_in_dh", (T // tmh, 1, n_pad // tkp),
             (dp, (tmh, tkp), lambda i, j, k: (i, k)), (wp, (D, tkp), lambda i, j, k: (0, k)),
             (jax.ShapeDtypeStruct((T, D), F32), (tmh, D), lambda i, j, k: (i, 0)), NT)
    dxn, dgain = _rmsnorm_bwd(dh, x, gain, dx)
    small = {"norm_mix": dgain, "mix_out_norm": dgn, "conv_w": dconv_w, "fgate_b": dfb[:, :nh],
             "lru_conv_w": dcw, "lru_conv_b": dcbias, "lru_w_a": dwa, "lru_b_a": dba, "lru_w_x": dwx,
             "lru_b_x": dbx, "lru_lambda": dlam}
    return dxn, dwp, dwo, small


def _position():
    x, y, c = lax.axis_index("x"), lax.axis_index("y"), lax.axis_index("c")
    return x, y, c, [(1 - x, y), (x, 1 - y), (1 - x, 1 - y)]


def _all_gather(name, xs):
    n = len(xs)

    def body(*refs):
        x_refs, o_refs = refs[:n], refs[n:2 * n]
        send_sems, recv_sems, local_sems = refs[2 * n:]
        x, y, c, chips = _position()
        me, sibling = (x, y, c), (x, y, 1 - c)

        def copy(t, k, block, to, src=None):
            dst = o_refs[t].at[4 * block[0] + 2 * block[1] + block[2]]
            return pltpu.make_async_remote_copy(
                src_ref=dst if src is None else src, dst_ref=dst, send_sem=send_sems.at[t, k],
                recv_sem=recv_sems.at[t, k], device_id=to, device_id_type=MESH)

        mine = [pltpu.make_async_copy(x_refs[t], o_refs[t].at[4 * x + 2 * y + c], local_sems.at[t]) for t in range(n)]
        for cp in mine:
            cp.start()
        first = []
        for t in range(n):
            first.append(copy(t, 0, me, sibling, src=x_refs[t]))
            first += [copy(t, 1 + j, me, (*chip, c), src=x_refs[t]) for j, chip in enumerate(chips)]
        for cp in first:
            cp.start()
        passed = []
        for j, chip in enumerate(chips):
            for t in range(n):
                copy(t, 1 + j, (*chip, c), me).wait_recv()
                passed.append(copy(t, 4 + j, (*chip, c), sibling))
                passed[-1].start()
        for t in range(n):
            copy(t, 0, sibling, me).wait_recv()
            for j, chip in enumerate(chips):
                copy(t, 4 + j, (*chip, 1 - c), me).wait_recv()
        for cp in first + passed:
            cp.wait_send()
        for cp in mine:
            cp.wait()

    return pl.pallas_call(
        body, name=name,
        in_specs=[_any()] * n, out_specs=[_any()] * n,
        out_shape=[jax.ShapeDtypeStruct((N_DEV,) + v.shape, v.dtype) for v in xs],
        scratch_shapes=[pltpu.SemaphoreType.DMA((n, 7)), pltpu.SemaphoreType.DMA((n, 7)),
                        pltpu.SemaphoreType.DMA((n,))],
    )(*xs)


def _add_sibling(g4, recv, c_idx):
    _, _, R, C = g4.shape
    tr = _tile(R, 512)

    def body(c_ref, g_ref, r_ref, o_ref):
        o_ref[...] = (g_ref[...].astype(F32) + r_ref[...].astype(F32)).astype(o_ref.dtype)

    return pl.pallas_call(
        body, name="rs_add_sibling",
        grid_spec=pltpu.PrefetchScalarGridSpec(
            num_scalar_prefetch=1, grid=(4, R // tr),
            in_specs=[pl.BlockSpec((None, None, tr, C), lambda q, i, c_ref: (q, c_ref[0], i, 0)),
                      pl.BlockSpec((None, None, tr, C), lambda q, i, c_ref: (q, 0, i, 0))],
            out_specs=pl.BlockSpec((None, tr, C), lambda q, i, c_ref: (q, i, 0))),
        out_shape=jax.ShapeDtypeStruct((4, R, C), g4.dtype),
        compiler_params=_params("parallel", "parallel"),
    )(c_idx, g4, recv)


def _hbm_spec():
    return pl.BlockSpec(memory_space=pltpu.HBM)


def _sem_spec():
    return pl.BlockSpec(memory_space=pltpu.SEMAPHORE)


def _side_effects():
    return pltpu.CompilerParams(has_side_effects=pltpu.SideEffectType.DATAFLOW_SIDE_EFFECTING)


def _in_hbm(v):
    return pltpu.with_memory_space_constraint(v, pltpu.HBM)


def _split_start(name, srcs, lands, n_copies, copies_of, after):
    n = len(srcs)
    ns = n * n_copies

    def body(*refs):
        src_refs, land_refs = refs[:n], refs[n:2 * n]
        send_sems, recv_sems = refs[2 * n + 1:2 * n + 1 + ns], refs[2 * n + 1 + ns:2 * n + 1 + 2 * ns]
        token = refs[-1]
        for t in range(n):
            for k, (s, d, to) in enumerate(copies_of(t, src_refs[t], land_refs[t])):
                pltpu.make_async_remote_copy(src_ref=s, dst_ref=d, send_sem=send_sems[t * n_copies + k],
                                             recv_sem=recv_sems[t * n_copies + k], device_id=to,
                                             device_id_type=MESH).start()
        token[...] = jnp.zeros(token.shape, token.dtype)

    thru = [pltpu.HBM(v.shape, v.dtype) for v in list(srcs) + list(lands)]
    out = pl.pallas_call(
        body, name=name,
        in_specs=[_hbm_spec()] * (2 * n) + [_any()],
        out_specs=[_sem_spec()] * (2 * ns) + [_hbm_spec()] * (2 * n) + [pl.BlockSpec(memory_space=pltpu.VMEM)],
        out_shape=[pltpu.SemaphoreType.DMA(())] * (2 * ns) + thru + [jax.ShapeDtypeStruct((8, LANES), F32)],
        input_output_aliases={i: 2 * ns + i for i in range(2 * n)},
        compiler_params=_side_effects(),
    )(*[_in_hbm(v) for v in list(srcs) + list(lands)], after)
    return out[:ns], out[ns:2 * ns], out[2 * ns:2 * ns + n], out[2 * ns + n:2 * ns + 2 * n], out[-1]


def _split_wait(name, send_sems, recv_sems, srcs, lands, n_copies, waits_of, after):
    n = len(srcs)
    ns = n * n_copies

    def body(*refs):
        src_refs, land_refs = refs[:n], refs[n:2 * n]
        send_refs, recv_refs = refs[2 * n:2 * n + ns], refs[2 * n + ns:2 * n + 2 * ns]
        x, y, c, _ = _position()
        for t in range(n):
            for k, (s, d) in enumerate(waits_of(t, src_refs[t], land_refs[t])):
                cp = pltpu.make_async_remote_copy(src_ref=s, dst_ref=d, send_sem=send_refs[t * n_copies + k],
                                                  recv_sem=recv_refs[t * n_copies + k], device_id=(x, y, 1 - c),
                                                  device_id_type=MESH)
                cp.wait_send()
                cp.wait_recv()

    out = pl.pallas_call(
        body, name=name,
        in_specs=[_hbm_spec()] * (2 * n) + [_sem_spec()] * (2 * ns) + [_any()],
        out_specs=[_hbm_spec()] * (2 * n),
        out_shape=[pltpu.HBM(v.shape, v.dtype) for v in list(srcs) + list(lands)],
        input_output_aliases={i: i for i in range(2 * n)},
        compiler_params=_side_effects(),
    )(*srcs, *lands, *send_sems, *recv_sems, after)
    return out[:n], out[n:]


def _block_of(px, py, pc):
    return 4 * px + 2 * py + pc


def _gather_phase1_start(name, xs, after):
    lands = [lax.empty((N_DEV,) + v.shape, v.dtype) for v in xs]

    def copies_of(t, x_ref, land_ref):
        x, y, c, chips = _position()
        dst = land_ref.at[_block_of(x, y, c)]
        return [(x_ref, dst, (x, y, 1 - c))] + [(x_ref, dst, (*chip, c)) for chip in chips]

    return _split_start(name, xs, lands, 4, copies_of, after)


def _gather_phase1_wait(name, started, after):
    send_sems, recv_sems, xs, lands, _ = started

    def waits_of(t, x_ref, land_ref):
        x, y, c, chips = _position()
        return [(x_ref, land_ref.at[_block_of(x, y, 1 - c)])] + [(x_ref, land_ref.at[_block_of(*chip, c)])
                                                                  for chip in chips]

    return _split_wait(name, send_sems, recv_sems, xs, lands, 4, waits_of, after)


def _gather_phase2_start(name, lands, after):
    keep = [lax.empty((8, LANES), v.dtype) for v in lands]

    def copies_of(t, _, land_ref):
        x, y, c, chips = _position()
        return [(land_ref.at[_block_of(*chip, c)], land_ref.at[_block_of(*chip, c)], (x, y, 1 - c)) for chip in chips]

    return _split_start(name, keep, lands, 3, copies_of, after)


def _gather_phase2_wait(name, started, after):
    send_sems, recv_sems, keep, lands, _ = started

    def waits_of(t, _, land_ref):
        x, y, c, chips = _position()
        return [(land_ref.at[_block_of(*chip, c)], land_ref.at[_block_of(*chip, 1 - c)]) for chip in chips]

    return _split_wait(name, send_sems, recv_sems, keep, lands, 3, waits_of, after)[1]


def _place_block(name, land, src, block_idx):
    R, C = src.shape
    tr = _tile(R, 512)

    def body(b_ref, land_ref, s_ref, o_ref):
        o_ref[...] = s_ref[...]

    return pl.pallas_call(
        body, name=name,
        grid_spec=pltpu.PrefetchScalarGridSpec(
            num_scalar_prefetch=1, grid=(R // tr,),
            in_specs=[_any(), pl.BlockSpec((tr, C), lambda i, b: (i, 0))],
            out_specs=pl.BlockSpec((None, tr, C), lambda i, b: (b[0], i, 0))),
        out_shape=jax.ShapeDtypeStruct(land.shape, land.dtype),
        input_output_aliases={1: 0},
        compiler_params=_params("parallel"),
    )(block_idx, land, src)


def _sibling_start(name, gs, after):
    g4 = [g.reshape((4, 2) + g.shape[1:]) for g in gs]
    lands = [lax.empty((4, 1) + g.shape[2:], g.dtype) for g in g4]

    def copies_of(t, g_ref, land_ref):
        x, y, c, _ = _position()
        return [(g_ref.at[:, pl.ds(1 - c, 1)], land_ref, (x, y, 1 - c))]

    return _split_start(name, g4, lands, 1, copies_of, after)


def _sibling_wait(name, started, after):
    send_sems, recv_sems, g4, lands, _ = started

    def waits_of(t, g_ref, land_ref):
        x, y, c, _ = _position()
        return [(g_ref.at[:, pl.ds(1 - c, 1)], land_ref)]

    return _split_wait(name, send_sems, recv_sems, g4, lands, 1, waits_of, after)


def _scatter_chips_start(name, ps, after):
    lands = [lax.empty((3,) + v.shape[1:], v.dtype) for v in ps]

    def copies_of(t, p_ref, land_ref):
        x, y, c, chips = _position()
        return [(p_ref.at[2 * chip[0] + chip[1]], land_ref.at[k], (*chip, c)) for k, chip in enumerate(chips)]

    return _split_start(name, ps, lands, 3, copies_of, after)


def _scatter_chips_wait(name, started, after):
    send_sems, recv_sems, ps, lands, _ = started

    def waits_of(t, p_ref, land_ref):
        x, y, c, chips = _position()
        return [(p_ref.at[2 * chip[0] + chip[1]], land_ref.at[k]) for k, chip in enumerate(chips)]

    return _split_wait(name, send_sems, recv_sems, ps, lands, 3, waits_of, after)


def _scatter_chips_after_sibling(tag, sibling, c_idx, after):
    g4, recv = _sibling_wait("rs_sibling_wait_" + tag, sibling, after)
    ps = [_add_sibling(g, r, c_idx) for g, r in zip(g4, recv)]
    return _scatter_chips_start("rs_chips_start_" + tag, ps, ps[0])


def _sum_devices(parts):
    _, R, C = parts.shape
    tr = _tile(R, 512)

    def body(p_ref, o_ref):
        acc = p_ref[0]
        for d in range(1, N_DEV):
            acc = acc + p_ref[d]
        o_ref[...] = acc

    return pl.pallas_call(
        body, name="sum_devices", grid=(R // tr,),
        in_specs=[pl.BlockSpec((N_DEV, tr, C), lambda i: (0, i, 0))],
        out_specs=pl.BlockSpec((tr, C), lambda i: (i, 0)),
        out_shape=jax.ShapeDtypeStruct((R, C), F32),
        compiler_params=_params("parallel"),
    )(parts)


def _adam_math(w, g, m, v):
    m = ADAM_B1 * m + (1.0 - ADAM_B1) * g
    v = ADAM_B2 * v + (1.0 - ADAM_B2) * (g * g)
    m_hat = m / (1.0 - ADAM_B1 ** ADAM_STEP)
    v_hat = v / (1.0 - ADAM_B2 ** ADAM_STEP)
    return -ADAM_LR * (m_hat / (jnp.sqrt(v_hat) + ADAM_EPS) + ADAM_WD * w), m, v


def _adam_layer(name, layer, w, m, v, own, landed, chip_idx, prev, after):
    L, R, C = w.shape
    tr = _tile(R, 256)
    slab = pl.BlockSpec((None, tr, C), lambda i, q: (layer, i, 0))

    def body(q_ref, w_ref, m_ref, v_ref, own_ref, land_ref, after_ref, *rest):
        g_ref, d_ref, nm_ref, nv_ref = rest[-4:]
        g = land_ref[0].astype(F32)
        for k in range(1, 3):
            g = g + land_ref[k].astype(F32)
        g = g + own_ref[...].astype(F32)
        d, nm, nv = _adam_math(w_ref[...], g, m_ref[...], v_ref[...])
        g_ref[...] = g
        d_ref[...] = d
        nm_ref[...] = nm
        nv_ref[...] = nv

    n_prev = 0 if prev is None else 4
    out = jax.ShapeDtypeStruct((L, R, C), F32)
    return pl.pallas_call(
        body, name=name,
        grid_spec=pltpu.PrefetchScalarGridSpec(
            num_scalar_prefetch=1, grid=(R // tr,),
            in_specs=[slab, slab, slab, pl.BlockSpec((None, tr, C), lambda i, q: (q[0], i, 0)),
                      pl.BlockSpec((3, tr, C), lambda i, q: (0, i, 0)), _any()] + [_any()] * n_prev,
            out_specs=[slab] * 4),
        out_shape=[out] * 4,
        input_output_aliases={7 + k: k for k in range(n_prev)},
        compiler_params=_params("parallel"),
    )(chip_idx, w, m, v, own, landed, after, *(prev or ()))


def _adam_small(w, g, m, v):
    R, C = w.shape
    tr = _tile(R, 512)
    spec = pl.BlockSpec((tr, C), lambda i: (i, 0))

    def body(w_ref, g_ref, m_ref, v_ref, d_ref, nm_ref, nv_ref):
        d, nm, nv = _adam_math(w_ref[...], g_ref[...], m_ref[...], v_ref[...])
        d_ref[...] = d
        nm_ref[...] = nm
        nv_ref[...] = nv

    out = jax.ShapeDtypeStruct((R, C), F32)
    return pl.pallas_call(
        body, name="adam_small", grid=(R // tr,),
        in_specs=[spec] * 4, out_specs=[spec] * 3, out_shape=[out] * 3,
        compiler_params=_params("parallel"),
    )(w, g, m, v)


def _pack(arrays, row_multiple=512):
    flat = jnp.concatenate([a.reshape(-1).astype(F32) for a in arrays])
    per = row_multiple * LANES
    total = -(-flat.shape[0] // per) * per
    return jnp.pad(flat, (0, total - flat.shape[0])).reshape(total // LANES, LANES)


def _unpack(packed, shapes):
    flat = packed.reshape(-1)
    out, off = [], 0
    for s in shapes:
        n = math.prod(s)
        out.append(flat[off:off + n].reshape(s))
        off += n
    return out


BIG = ["ffn1_w_in", "ffn1_w_out", "mix_w_in", "mix_w_out", "ffn2_w_in", "ffn2_w_out"]
SHARDED_SMALL = ["conv_w", "lru_conv_w"]
WEIGHTS = ["norm_ffn1", "ffn1_w_in", "ffn1_w_out", "norm_mix", "mix_w_in", "conv_w", "fgate_b", "lru_conv_w",
           "lru_conv_b", "lru_w_a", "lru_b_a", "lru_w_x", "lru_b_x", "lru_lambda", "mix_out_norm", "mix_w_out",
           "norm_ffn2", "ffn2_w_in", "ffn2_w_out", "final_norm"]
REPLICATED = [n for n in WEIGHTS if n not in BIG and n not in SHARDED_SMALL]


def _step(x, target, w, m, v):
    L = w["norm_ffn1"].shape[0]
    T, D = x.shape[1], x.shape[2]
    dc, da, dl, nh, n_main = _mixer_dims(D)
    xi, yi, ci = lax.axis_index("x"), lax.axis_index("y"), lax.axis_index("c")
    me = 4 * xi + 2 * yi + ci
    c_idx = ci.astype(jnp.int32).reshape(1)

    taps = jnp.concatenate([w["conv_w"].reshape(-1), w["lru_conv_w"].reshape(-1)])
    taps = jnp.pad(taps, (0, (-taps.shape[0]) % (8 * LANES))).reshape(-1, LANES)
    def shards_of(l):
        return [w[n][l].astype(BF) for n in BIG] + ([taps] if l == 0 else [])

    me_idx = me.astype(jnp.int32).reshape(1)
    chip_idx = (2 * xi + yi).astype(jnp.int32).reshape(1)

    def place_own(tag, lands, own):
        return [_place_block("gather_own_" + tag, land, src, me_idx) for land, src in zip(lands, own)]

    gathered = [None] * L
    phase1 = [None] * L
    shards0 = shards_of(0)
    first_start = _gather_phase1_start("gather_p1_start_0_first", shards0[:2] + [taps], taps)
    rest_start = _gather_phase1_start("gather_p1_start_0_rest", shards0[2:len(BIG)], first_start[-1])
    own, landed = _gather_phase1_wait("gather_p1_wait_0_first", first_start, rest_start[-1])
    phase2 = _gather_phase2_start("gather_p2_start_0_first", landed, landed[0])
    landed = _gather_phase2_wait("gather_p2_wait_0_first", phase2, phase2[-1])
    first0 = place_own("0_first", landed, own)
    taps_all = first0[2].reshape(N_DEV, -1)
    n_cw = math.prod(w["conv_w"].shape)
    ch = w["conv_w"].shape[-1]
    conv_w_full = jnp.moveaxis(taps_all[:, :n_cw].reshape((N_DEV,) + w["conv_w"].shape), 0, -2).reshape(L, 3, N_DEV * ch)
    n_lw = math.prod(w["lru_conv_w"].shape)
    lru_conv_w_full = jnp.moveaxis(taps_all[:, n_cw:n_cw + n_lw].reshape((N_DEV,) + w["lru_conv_w"].shape), 0, -2
                                   ).reshape(L, 4, N_DEV * ch)

    def layer_weights(l):
        g = dict(zip(BIG, gathered[l]))
        F = g["ffn1_w_out"].shape[1] * N_DEV
        wmix = jnp.transpose(g["mix_w_in"], (1, 0, 2)).reshape(D, -1)
        return {"ffn1_w_in": g["ffn1_w_in"], "ffn1_w_out": g["ffn1_w_out"].reshape(F, D),
                "wp": _pad_mix_w_in(wmix), "wo": g["mix_w_out"].reshape(D, D),
                "ffn2_w_in": g["ffn2_w_in"], "ffn2_w_out": g["ffn2_w_out"].reshape(F, D)}

    def small_params(l):
        return {"fgate_b": jnp.pad(w["fgate_b"][l], (0, LANES - nh)).reshape(1, LANES),
                "conv_w": conv_w_full[l], "lru_conv_w": lru_conv_w_full[l],
                "lru_conv_b": w["lru_conv_b"][l].reshape(1, dl), "lru_w_a": w["lru_w_a"][l],
                "lru_b_a": w["lru_b_a"][l].reshape(1, dl), "lru_w_x": w["lru_w_x"][l],
                "lru_b_x": w["lru_b_x"][l].reshape(1, dl), "lru_lambda": w["lru_lambda"][l].reshape(1, dl),
                "mix_out_norm": w["mix_out_norm"][l].reshape(1, D)}

    gain = lambda n, l, token=None: w[n][l].reshape(1, D) + (0.0 if token is None else token[0:1, 0:1])

    xc = x[0]
    saved, lw, sps = [], [], []
    for l in range(L):
        w_in1, w_out1 = first0[:2] if l == 0 else gathered[l][:2]
        xc, s1 = _ffn_forward(xc, gain("norm_ffn1", l), w_in1, w_out1.reshape(-1, D))
        if l == 0:
            own, landed = _gather_phase1_wait("gather_p1_wait_0_rest", rest_start, xc)
            phase2 = _gather_phase2_start("gather_p2_start_0_rest", landed, landed[0])
            order_token = phase2[-1]
            if L > 1:
                phase1[1] = _gather_phase1_start("gather_p1_start_1", shards_of(1), phase2[-1])
                order_token = phase1[1][-1]
            landed = _gather_phase2_wait("gather_p2_wait_0_rest", phase2, order_token)
            gathered[0] = first0[:2] + place_own("0_rest", landed, own)
        lw.append(layer_weights(l))
        sps.append(small_params(l))
        xc, s2 = _mixer_forward(xc, gain("norm_mix", l), lw[l]["wp"], lw[l]["wo"], sps[l])
        order_token = None
        if l + 1 < L:
            own, landed = _gather_phase1_wait("gather_p1_wait_%d" % (l + 1), phase1[l + 1], xc)
            phase2 = _gather_phase2_start("gather_p2_start_%d" % (l + 1), landed, landed[0])
            order_token = phase2[-1]
            if l + 2 < L:
                phase1[l + 2] = _gather_phase1_start("gather_p1_start_%d" % (l + 2), shards_of(l + 2), phase2[-1])
                order_token = phase1[l + 2][-1]
        xc, s3 = _ffn_forward(xc, gain("norm_ffn2", l, order_token), lw[l]["ffn2_w_in"], lw[l]["ffn2_w_out"])
        if l + 1 < L:
            landed = _gather_phase2_wait("gather_p2_wait_%d" % (l + 1), phase2, xc)
            gathered[l + 1] = place_own(str(l + 1), landed, own)
        saved.append((s1, s2, s3))
    loss_part, dx, d_final = _loss_head(xc, w["final_norm"].reshape(1, D), target[0])

    small_grads = [None] * L
    parts = [None] * L
    sibling = [None] * L
    scatter = [None] * L
    scatter_early = []
    for l in reversed(range(L)):
        s1, s2, s3 = saved[l]
        dh, dw3_2, dwout_2 = _ffn_backward(dx, s3, lw[l]["ffn2_w_in"], lw[l]["ffn2_w_out"])
        dx, dg3 = _rmsnorm_bwd(dh, s3[0], gain("norm_ffn2", l), dx)
        sp_l = sps[l]
        if l + 1 < L:
            scatter[l + 1] = _scatter_chips_after_sibling(str(l + 1), sibling[l + 1], c_idx, dx)
            sp_l = dict(sp_l, mix_out_norm=sp_l["mix_out_norm"] + scatter[l + 1][-1][0:1, 0:1])
        dx, dwp, dwo, sg = _mixer_backward(dx, s2, gain("norm_mix", l), lw[l]["wp"], lw[l]["wo"], sp_l)
        dmix = jnp.transpose(_unpad_mix_w_in(dwp).reshape(D, N_DEV, -1), (1, 0, 2))
        grads = [dmix, dwo.reshape(N_DEV, D // N_DEV, D), dw3_2, dwout_2]
        between = None
        if l == 0:
            early = _sibling_start("rs_sibling_start_0_early", grads, dwp)
            grads = []

            def between(dgu):
                scatter_early.append(_scatter_chips_after_sibling("0_early", early, c_idx, dgu))
                return scatter_early[0][-1]

        dh, dw3_1, dwout_1 = _ffn_backward(dx, s1, lw[l]["ffn1_w_in"], lw[l]["ffn1_w_out"], between)
        if l + 1 < L:
            parts[l + 1] = _scatter_chips_wait("rs_chips_wait_%d" % (l + 1), scatter[l + 1], dh)
        sibling[l] = _sibling_start("rs_sibling_start_%d" % l, [dw3_1, dwout_1] + grads, dh)
        dx, dg1 = _rmsnorm_bwd(dh, s1[0], gain("norm_ffn1", l, sibling[l][-1]), dx)
        sg["norm_ffn2"], sg["norm_ffn1"] = dg3, dg1
        small_grads[l] = sg

    out_g, out_d, out_m, out_v = {}, {}, {}, {}
    prev = {n: None for n in BIG}

    def adam_big(l, after):
        own, landed = parts[l]
        for t, n in enumerate(BIG):
            prev[n] = _adam_layer("adam_%s_%d" % (n, l), l, w[n], m[n], v[n], own[t], landed[t], chip_idx, prev[n], after)

    small_names = REPLICATED + SHARDED_SMALL
    partial = []
    for n in small_names:
        if n == "final_norm":
            partial.append(d_final)
        else:
            partial.append(jnp.stack([small_grads[l][n].reshape(w[n].shape[1:]) if n not in SHARDED_SMALL
                                      else small_grads[l][n] for l in range(L)]))
    partial.append(loss_part[0, :1])
    packed = _pack(partial)
    small_gathered = _all_gather("gather_small_grads", [packed])[0]
    scatter[0] = _scatter_chips_after_sibling("0", sibling[0], c_idx, small_gathered)
    for l in reversed(range(1, L)):
        adam_big(l, scatter[0][-1])
    summed = _sum_devices(small_gathered)
    full_shapes = [w[n].shape for n in REPLICATED] + [(L, 3, N_DEV * ch), (L, 4, N_DEV * ch), (1,)]
    full = _unpack(summed, full_shapes)
    loss = full[-1][0]
    g_small = dict(zip(small_names, full[:-1]))
    for n in SHARDED_SMALL:
        g_small[n] = lax.dynamic_slice_in_dim(g_small[n], me * ch, ch, axis=2)
    shapes = [w[n].shape for n in small_names]
    d_s, m_s, v_s = _adam_small(_pack([w[n] for n in small_names]), _pack([g_small[n] for n in small_names]),
                                _pack([m[n] for n in small_names]), _pack([v[n] for n in small_names]))
    for n, d_, m_, v_ in zip(small_names, _unpack(d_s, shapes), _unpack(m_s, shapes), _unpack(v_s, shapes)):
        out_g[n], out_d[n], out_m[n], out_v[n] = g_small[n], d_, m_, v_

    behind = d_s if L == 1 else prev[BIG[-1]][0]
    own_early, landed_early = _scatter_chips_wait("rs_chips_wait_0_early", scatter_early[0], behind)
    own_late, landed_late = _scatter_chips_wait("rs_chips_wait_0", scatter[0], behind)
    parts[0] = (list(own_late) + list(own_early), list(landed_late) + list(landed_early))
    adam_big(0, d_s)
    for n in BIG:
        out_g[n], out_d[n], out_m[n], out_v[n] = prev[n]

    return (loss, dx[None], *[out_g[n] for n in WEIGHTS], *[out_d[n] for n in WEIGHTS],
            *[out_m[n] for n in WEIGHTS], *[out_v[n] for n in WEIGHTS])


def kernel(x, norm_ffn1, ffn1_w_in, ffn1_w_out, norm_mix, mix_w_in, conv_w, fgate_b, lru_conv_w, lru_conv_b, lru_w_a, lru_b_a, lru_w_x, lru_b_x, lru_lambda, mix_out_norm, mix_w_out, norm_ffn2, ffn2_w_in, ffn2_w_out, final_norm, loss_target, m_norm_ffn1, m_ffn1_w_in, m_ffn1_w_out, m_norm_mix, m_mix_w_in, m_conv_w, m_fgate_b, m_lru_conv_w, m_lru_conv_b, m_lru_w_a, m_lru_b_a, m_lru_w_x, m_lru_b_x, m_lru_lambda, m_mix_out_norm, m_mix_w_out, m_norm_ffn2, m_ffn2_w_in, m_ffn2_w_out, m_final_norm, v_norm_ffn1, v_ffn1_w_in, v_ffn1_w_out, v_norm_mix, v_mix_w_in, v_conv_w, v_fgate_b, v_lru_conv_w, v_lru_conv_b, v_lru_w_a, v_lru_b_a, v_lru_w_x, v_lru_b_x, v_lru_lambda, v_mix_out_norm, v_mix_w_out, v_norm_ffn2, v_ffn2_w_in, v_ffn2_w_out, v_final_norm):
    w = dict(norm_ffn1=norm_ffn1, ffn1_w_in=ffn1_w_in, ffn1_w_out=ffn1_w_out, norm_mix=norm_mix, mix_w_in=mix_w_in,
             conv_w=conv_w, fgate_b=fgate_b, lru_conv_w=lru_conv_w, lru_conv_b=lru_conv_b, lru_w_a=lru_w_a,
             lru_b_a=lru_b_a, lru_w_x=lru_w_x, lru_b_x=lru_b_x, lru_lambda=lru_lambda, mix_out_norm=mix_out_norm,
             mix_w_out=mix_w_out, norm_ffn2=norm_ffn2, ffn2_w_in=ffn2_w_in, ffn2_w_out=ffn2_w_out,
             final_norm=final_norm)
    m = dict(norm_ffn1=m_norm_ffn1, ffn1_w_in=m_ffn1_w_in, ffn1_w_out=m_ffn1_w_out, norm_mix=m_norm_mix,
             mix_w_in=m_mix_w_in, conv_w=m_conv_w, fgate_b=m_fgate_b, lru_conv_w=m_lru_conv_w,
             lru_conv_b=m_lru_conv_b, lru_w_a=m_lru_w_a, lru_b_a=m_lru_b_a, lru_w_x=m_lru_w_x, lru_b_x=m_lru_b_x,
             lru_lambda=m_lru_lambda, mix_out_norm=m_mix_out_norm, mix_w_out=m_mix_w_out, norm_ffn2=m_norm_ffn2,
             ffn2_w_in=m_ffn2_w_in, ffn2_w_out=m_ffn2_w_out, final_norm=m_final_norm)
    v = dict(norm_ffn1=v_norm_ffn1, ffn1_w_in=v_ffn1_w_in, ffn1_w_out=v_ffn1_w_out, norm_mix=v_norm_mix,
             mix_w_in=v_mix_w_in, conv_w=v_conv_w, fgate_b=v_fgate_b, lru_conv_w=v_lru_conv_w,
             lru_conv_b=v_lru_conv_b, lru_w_a=v_lru_w_a, lru_b_a=v_lru_b_a, lru_w_x=v_lru_w_x, lru_b_x=v_lru_b_x,
             lru_lambda=v_lru_lambda, mix_out_norm=v_mix_out_norm, mix_w_out=v_mix_w_out, norm_ffn2=v_norm_ffn2,
             ffn2_w_in=v_ffn2_w_in, ffn2_w_out=v_ffn2_w_out, final_norm=v_final_norm)
    return _step(x, loss_target, w, m, v)
```

```python
import math

import jax
import jax.numpy as jnp
from jax import lax
from jax.experimental import pallas as pl
from jax.experimental.pallas import tpu as pltpu

F32 = jnp.float32
BF = jnp.bfloat16
EPS = 1e-6
LANES = 128
VMEM_LIMIT_V7X = 56 * 1024 * 1024
MESH = pl.DeviceIdType.MESH
N_DEV = 8
LRU_C = 8.0
ADAM_LR, ADAM_B1, ADAM_B2, ADAM_EPS, ADAM_WD, ADAM_STEP = 0.001, 0.9, 0.999, 1e-08, 0.01, 10
GELU_C = math.sqrt(2.0 / math.pi)
GELU_K = 0.044715


def _params(*sem):
    return pltpu.CompilerParams(dimension_semantics=sem, vmem_limit_bytes=VMEM_LIMIT_V7X)


def _any():
    return pl.BlockSpec(memory_space=pl.ANY)


def _tile(n, target):
    if n <= target:
        return n
    t = target - target % 16
    while t >= 16:
        if n % t == 0:
            return t
        t -= 16
    return n


def _mm(name, grid, a, b, o, dims, scale=1.0, resid=None, after=None):
    nk = grid[-1]
    acc_shape = tuple(d for d in o[1] if d is not None)
    has_resid = resid is not None
    n_in = 2 + has_resid + (after is not None)

    def body(*refs):
        a_ref, b_ref = refs[0], refs[1]
        r_ref = refs[2] if has_resid else None
        o_ref = refs[n_in]

        def finish(acc):
            r = acc * scale if scale != 1.0 else acc
            if has_resid:
                r = r + r_ref[...]
            o_ref[...] = r.astype(o_ref.dtype)

        def product():
            return lax.dot_general(a_ref[...].astype(BF), b_ref[...].astype(BF), (dims, ((), ())),
                                   preferred_element_type=F32)

        if nk == 1:
            finish(product())
        else:
            acc_ref = refs[-1]
            k = pl.program_id(len(grid) - 1)

            @pl.when(k == 0)
            def _():
                acc_ref[...] = jnp.zeros(acc_ref.shape, F32)

            acc_ref[...] += product()

            @pl.when(k == nk - 1)
            def _():
                finish(acc_ref[...])

    ins = [a, b] + ([resid] if has_resid else [])
    return pl.pallas_call(
        body, name=name, grid=grid,
        in_specs=[pl.BlockSpec(blk, idx) for (_, blk, idx) in ins] + ([_any()] if after is not None else []),
        out_specs=pl.BlockSpec(o[1], o[2]),
        out_shape=o[0],
        scratch_shapes=[pltpu.VMEM(acc_shape, F32)] if nk > 1 else [],
        compiler_params=_params(*(["parallel"] * (len(grid) - 1) + ["arbitrary"])),
    )(*[x[0] for x in ins], *([after] if after is not None else []))


NN = ((1,), (0,))
NT = ((1,), (1,))
TN = ((0,), (0,))


def _rmsnorm(x, gain):
    T, D = x.shape
    tr = _tile(T, 512)

    def body(x_ref, g_ref, o_ref):
        xv = x_ref[...]
        r = lax.rsqrt(jnp.mean(xv * xv, axis=-1, keepdims=True) + EPS)
        o_ref[...] = (xv * r * g_ref[...]).astype(BF)

    return pl.pallas_call(
        body, name="rmsnorm_fwd", grid=(T // tr,),
        in_specs=[pl.BlockSpec((tr, D), lambda i: (i, 0)), pl.BlockSpec((1, D), lambda i: (0, 0))],
        out_specs=pl.BlockSpec((tr, D), lambda i: (i, 0)),
        out_shape=jax.ShapeDtypeStruct((T, D), BF),
        compiler_params=_params("parallel"),
    )(x, gain)


def _rmsnorm_bwd(dh, x, gain, dres):
    T, D = x.shape
    tr = _tile(T, 256)

    def body(dh_ref, x_ref, g_ref, dres_ref, dx_ref, dxb_ref, dg_ref):
        i = pl.program_id(0)
        xv = x_ref[...]
        r = lax.rsqrt(jnp.mean(xv * xv, axis=-1, keepdims=True) + EPS)
        xh = xv * r
        dy = dh_ref[...].astype(F32)
        dgp = jnp.sum(dy * xh, axis=0, keepdims=True)

        @pl.when(i == 0)
        def _():
            dg_ref[...] = dgp

        @pl.when(i > 0)
        def _():
            dg_ref[...] += dgp

        dxh = dy * g_ref[...]
        dxv = dres_ref[...] + r * (dxh - xh * jnp.mean(dxh * xh, axis=-1, keepdims=True))
        dx_ref[...] = dxv
        dxb_ref[...] = dxv.astype(BF)

    return pl.pallas_call(
        body, name="rmsnorm_bwd", grid=(T // tr,),
        in_specs=[pl.BlockSpec((tr, D), lambda i: (i, 0)), pl.BlockSpec((tr, D), lambda i: (i, 0)),
                  pl.BlockSpec((1, D), lambda i: (0, 0)), pl.BlockSpec((tr, D), lambda i: (i, 0))],
        out_specs=[pl.BlockSpec((tr, D), lambda i: (i, 0)), pl.BlockSpec((tr, D), lambda i: (i, 0)),
                   pl.BlockSpec((1, D), lambda i: (0, 0))],
        out_shape=[jax.ShapeDtypeStruct((T, D), F32), jax.ShapeDtypeStruct((T, D), BF),
                   jax.ShapeDtypeStruct((1, D), F32)],
        compiler_params=_params("arbitrary"),
    )(dh, x, gain, dres)


def _loss_head(x, gain, target):
    T, D = x.shape
    tr = _tile(T, 256)

    def body(x_ref, g_ref, t_ref, loss_ref, dx_ref, dxb_ref, dg_ref):
        i = pl.program_id(0)
        xv = x_ref[...]
        g = g_ref[...]
        r = lax.rsqrt(jnp.mean(xv * xv, axis=-1, keepdims=True) + EPS)
        xh = xv * r
        err = xh * g - t_ref[...]
        lp = 0.5 * jnp.sum(jnp.mean(err * err, axis=-1, keepdims=True), axis=0, keepdims=True)
        dy = err * (1.0 / D)
        dgp = jnp.sum(dy * xh, axis=0, keepdims=True)

        @pl.when(i == 0)
        def _():
            loss_ref[...] = jnp.broadcast_to(lp, loss_ref.shape)
            dg_ref[...] = dgp

        @pl.when(i > 0)
        def _():
            loss_ref[...] += jnp.broadcast_to(lp, loss_ref.shape)
            dg_ref[...] += dgp

        dxh = dy * g
        dxv = r * (dxh - xh * jnp.mean(dxh * xh, axis=-1, keepdims=True))
        dx_ref[...] = dxv
        dxb_ref[...] = dxv.astype(BF)

    return pl.pallas_call(
        body, name="loss_head", grid=(T // tr,),
        in_specs=[pl.BlockSpec((tr, D), lambda i: (i, 0)), pl.BlockSpec((1, D), lambda i: (0, 0)),
                  pl.BlockSpec((tr, D), lambda i: (i, 0))],
        out_specs=[pl.BlockSpec((1, LANES), lambda i: (0, 0)), pl.BlockSpec((tr, D), lambda i: (i, 0)),
                   pl.BlockSpec((tr, D), lambda i: (i, 0)), pl.BlockSpec((1, D), lambda i: (0, 0))],
        out_shape=[jax.ShapeDtypeStruct((1, LANES), F32), jax.ShapeDtypeStruct((T, D), F32),
                   jax.ShapeDtypeStruct((T, D), BF), jax.ShapeDtypeStruct((1, D), F32)],
        compiler_params=_params("arbitrary"),
    )(x, gain, target)


def _group_slices(D):
    dc, da = D // 4, D // 2
    return [(0, dc), (dc, dc + da), (dc + da, D)]


def _groupnorm(yc, ya, yl, gain):
    T = yc.shape[0]
    D = yc.shape[1] + ya.shape[1] + yl.shape[1]
    tr = _tile(T, 512)
    sl = _group_slices(D)

    def body(yc_ref, ya_ref, yl_ref, g_ref, o_ref):
        for y_ref, (lo, hi) in zip((yc_ref, ya_ref, yl_ref), sl):
            y = y_ref[...]
            r = lax.rsqrt(jnp.mean(y * y, axis=-1, keepdims=True) + EPS)
            o_ref[:, lo:hi] = (y * r * g_ref[:, lo:hi]).astype(BF)

    return pl.pallas_call(
        body, name="groupnorm_fwd", grid=(T // tr,),
        in_specs=[pl.BlockSpec((tr, y.shape[1]), lambda i: (i, 0)) for y in (yc, ya, yl)]
        + [pl.BlockSpec((1, D), lambda i: (0, 0))],
        out_specs=pl.BlockSpec((tr, D), lambda i: (i, 0)),
        out_shape=jax.ShapeDtypeStruct((T, D), BF),
        compiler_params=_params("parallel"),
    )(yc, ya, yl, gain)


def _groupnorm_bwd(dyn, yc, ya, yl, gain):
    T, D = dyn.shape
    tr = _tile(T, 512)
    sl = _group_slices(D)

    def body(dyn_ref, yc_ref, ya_ref, yl_ref, g_ref, dc_ref, da_ref, dl_ref, dg_ref):
        i = pl.program_id(0)
        for y_ref, d_ref, (lo, hi) in zip((yc_ref, ya_ref, yl_ref), (dc_ref, da_ref, dl_ref), sl):
            y = y_ref[...]
            r = lax.rsqrt(jnp.mean(y * y, axis=-1, keepdims=True) + EPS)
            yh = y * r
            dy = dyn_ref[:, lo:hi]
            dgp = jnp.sum(dy * yh, axis=0, keepdims=True)

            @pl.when(i == 0)
            def _():
                dg_ref[:, lo:hi] = dgp

            @pl.when(i > 0)
            def _():
                dg_ref[:, lo:hi] += dgp

            dyh = dy * g_ref[:, lo:hi]
            d_ref[...] = r * (dyh - yh * jnp.mean(dyh * yh, axis=-1, keepdims=True))

    return pl.pallas_call(
        body, name="groupnorm_bwd", grid=(T // tr,),
        in_specs=[pl.BlockSpec((tr, D), lambda i: (i, 0))]
        + [pl.BlockSpec((tr, y.shape[1]), lambda i: (i, 0)) for y in (yc, ya, yl)]
        + [pl.BlockSpec((1, D), lambda i: (0, 0))],
        out_specs=[pl.BlockSpec((tr, y.shape[1]), lambda i: (i, 0)) for y in (yc, ya, yl)]
        + [pl.BlockSpec((1, D), lambda i: (0, 0))],
        out_shape=[jax.ShapeDtypeStruct(y.shape, F32) for y in (yc, ya, yl)] + [jax.ShapeDtypeStruct((1, D), F32)],
        compiler_params=_params("arbitrary"),
    )(dyn, yc, ya, yl, gain)


def _ffn_in(h, w3):
    T, D = h.shape
    tn = w3.shape[2]
    F = 4 * tn
    tm = _tile(T, 512)

    def body(h_ref, wg_ref, wu_ref, g_ref, u_ref, a_ref):
        hv = h_ref[...]
        g = jnp.dot(hv, wg_ref[...], preferred_element_type=F32)
        u = jnp.dot(hv, wu_ref[...], preferred_element_type=F32)
        g_ref[...] = g.astype(BF)
        u_ref[...] = u.astype(BF)
        a_ref[...] = (g * jax.nn.sigmoid(g) * u).astype(BF)

    out = jax.ShapeDtypeStruct((T, F), BF)
    return pl.pallas_call(
        body, name="ffn_in_swiglu", grid=(4, T // tm),
        in_specs=[pl.BlockSpec((tm, D), lambda j, i: (i, 0)),
                  pl.BlockSpec((None, D, tn), lambda j, i: (j, 0, 0)),
                  pl.BlockSpec((None, D, tn), lambda j, i: (j + 4, 0, 0))],
        out_specs=[pl.BlockSpec((tm, tn), lambda j, i: (i, j))] * 3,
        out_shape=[out, out, out],
        compiler_params=_params("parallel", "parallel"),
    )(h, w3, w3)


def _ffn_bwd_in(dx, wout, g, u):
    T, D = dx.shape
    F = wout.shape[0]
    tn = F // 4
    tm = _tile(T, 512)

    def body(dx_ref, w_ref, g_ref, u_ref, o_ref):
        da = 0.5 * lax.dot_general(dx_ref[...].astype(BF), w_ref[...], (NT, ((), ())), preferred_element_type=F32)
        gv = g_ref[...].astype(F32)
        s = jax.nn.sigmoid(gv)
        o_ref[0] = (da * u_ref[...].astype(F32) * (s * (1.0 + gv * (1.0 - s)))).astype(BF)
        o_ref[1] = (da * gv * s).astype(BF)

    return pl.pallas_call(
        body, name="ffn_bwd_swiglu", grid=(4, T // tm),
        in_specs=[pl.BlockSpec((tm, D), lambda j, i: (i, 0)), pl.BlockSpec((tn, D), lambda j, i: (j, 0)),
                  pl.BlockSpec((tm, tn), lambda j, i: (i, j)), pl.BlockSpec((tm, tn), lambda j, i: (i, j))],
        out_specs=pl.BlockSpec((2, tm, tn), lambda j, i: (0, i, j)),
        out_shape=jax.ShapeDtypeStruct((2, T, F), BF),
        compiler_params=_params("parallel", "parallel"),
    )(dx, wout, g, u)


def _ffn_forward(x, gain, w3, wout):
    T, D = x.shape
    F = wout.shape[0]
    h = _rmsnorm(x, gain)
    g, u, a = _ffn_in(h, w3)
    tm, tk = _tile(T, 512), F // 4
    xn = _mm("ffn_out", (T // tm, 1, F // tk),
             (a, (tm, tk), lambda i, j, k: (i, k)), (wout, (tk, D), lambda i, j, k: (k, 0)),
             (jax.ShapeDtypeStruct((T, D), F32), (tm, D), lambda i, j, k: (i, 0)), NN, scale=0.5,
             resid=(x, (tm, D), lambda i, j, k: (i, 0)))
    return xn, (x, h, g, u, a)


def _ffn_backward(dxb, saved, w3, wout, between=None):
    x, h, g, u, a = saved
    T, D = x.shape
    F = wout.shape[0]
    tn3 = F // 4
    dgu = _ffn_bwd_in(dxb, wout, g, u)
    behind = None if between is None else between(dgu)
    tnd = _tile(D, 512)
    dwout = _mm("ffn_dwout", (F // tn3, D // tnd, 1),
                (a, (T, tn3), lambda i, j, k: (0, i)), (dxb, (T, tnd), lambda i, j, k: (0, j)),
                (jax.ShapeDtypeStruct((F, D), BF), (tn3, tnd), lambda i, j, k: (i, j)), TN, scale=0.5, after=behind)
    dw3 = _mm("ffn_dwin", (8, D // tnd, 1),
              (h, (T, tnd), lambda s, i, k: (0, i)), (dgu, (None, T, tn3), lambda s, i, k: (s // 4, 0, s % 4)),
              (jax.ShapeDtypeStruct((8, D, tn3), BF), (None, tnd, tn3), lambda s, i, k: (s, i, 0)), TN, after=behind)
    tm = _tile(T, 1024)
    dh = _mm("ffn_dh", (T // tm, 1, 8),
             (dgu, (None, tm, tn3), lambda i, j, k: (k // 4, i, k % 4)), (w3, (None, D, tn3), lambda i, j, k: (k, 0, 0)),
             (jax.ShapeDtypeStruct((T, D), F32), (tm, D), lambda i, j, k: (i, 0)), NT, after=behind)
    return dh, dw3, dwout.reshape(N_DEV, F // N_DEV, D)


def _rows(shape):
    return lax.broadcasted_iota(jnp.int32, shape, 0)


def _down(x, s, fill, rows):
    return jnp.where(rows >= s, pltpu.roll(x, s, 0), fill)


def _up(x, s, fill, rows):
    T = x.shape[0]
    return jnp.where(rows < T - s, pltpu.roll(x, T - s, 0), fill)


def _scan_linear(a, b, rows, shift):
    T = a.shape[0]
    s = 1
    while s < T:
        b = a * shift(b, s, 0.0, rows) + b
        if 2 * s < T:
            a = a * shift(a, s, 1.0, rows)
        s *= 2
    return b


def _cumsum(c, rows, shift):
    T = c.shape[0]
    s = 1
    while s < T:
        c = c + shift(c, s, 0.0, rows)
        s *= 2
    return c


def _log1p_small(e):
    return jnp.where(e < 0.01, e * (1.0 - e * (0.5 - e * (1.0 / 3.0))), jnp.log(1.0 + e))


def _softplus(x):
    return jnp.maximum(x, 0.0) + _log1p_small(jnp.exp(-jnp.abs(x)))


def _one_minus_exp_neg(z):
    return jnp.where(z < 0.1, z * (1.0 - z * (0.5 - z * (1.0 / 6.0 - z * (1.0 / 24.0)))), 1.0 - jnp.exp(-z))


def _fgate_cum(f, b):
    T = f.shape[0]

    def body(f_ref, b_ref, o_ref):
        z = f_ref[...] + b_ref[...]
        o_ref[...] = _cumsum(-_softplus(-z), _rows(z.shape), _down)

    return pl.pallas_call(
        body, name="fgate_cumsum",
        out_shape=jax.ShapeDtypeStruct((T, LANES), F32),
        compiler_params=pltpu.CompilerParams(vmem_limit_bytes=VMEM_LIMIT_V7X),
    )(f, b)


def _fgate_cum_bwd(drow, dcol, f, b):
    T = f.shape[0]

    def body(dr_ref, dc_ref, f_ref, b_ref, df_ref, db_ref):
        z = f_ref[...] + b_ref[...]
        dlogf = _cumsum(dr_ref[...] - dc_ref[...], _rows(z.shape), _up)
        dz = dlogf * jax.nn.sigmoid(-z)
        df_ref[...] = dz.astype(BF)
        db_ref[...] = jnp.sum(dz, axis=0, keepdims=True)

    return pl.pallas_call(
        body, name="fgate_cumsum_bwd",
        out_shape=[jax.ShapeDtypeStruct((T, LANES), BF), jax.ShapeDtypeStruct((1, LANES), F32)],
        compiler_params=pltpu.CompilerParams(vmem_limit_bytes=VMEM_LIMIT_V7X),
    )(drow, dcol, f, b)


def _col(blk0):
    return lambda g: (0, blk0 + g)


def _conv_fwd(p, w, nb):
    T = p.shape[0]

    def body(b_ref, c_ref, v_ref, w_ref, o_ref):
        z = c_ref[...].astype(F32) * v_ref[...].astype(F32)
        rows = _rows(z.shape)
        conv = w_ref[2:3, :] * z + w_ref[1:2, :] * _down(z, 1, 0.0, rows) + w_ref[0:1, :] * _down(z, 2, 0.0, rows)
        o_ref[...] = b_ref[...].astype(F32) * conv

    return pl.pallas_call(
        body, name="conv_fwd", grid=(nb,),
        in_specs=[pl.BlockSpec((T, LANES), _col(0)), pl.BlockSpec((T, LANES), _col(nb)),
                  pl.BlockSpec((T, LANES), _col(2 * nb)), pl.BlockSpec((3, LANES), lambda g: (0, g))],
        out_specs=pl.BlockSpec((T, LANES), lambda g: (0, g)),
        out_shape=jax.ShapeDtypeStruct((T, nb * LANES), F32),
        compiler_params=_params("parallel"),
    )(p, p, p, w)


def _conv_bwd(dy, p, w, nb):
    T = p.shape[0]

    def body(dy_ref, b_ref, c_ref, v_ref, w_ref, db_ref, dc_ref, dv_ref, dw_ref):
        cv, vv = c_ref[...].astype(F32), v_ref[...].astype(F32)
        z = cv * vv
        rows = _rows(z.shape)
        z1, z2 = _down(z, 1, 0.0, rows), _down(z, 2, 0.0, rows)
        dyv = dy_ref[...]
        db_ref[...] = (dyv * (w_ref[2:3, :] * z + w_ref[1:2, :] * z1 + w_ref[0:1, :] * z2)).astype(BF)
        dconv = dyv * b_ref[...].astype(F32)
        dz = (w_ref[2:3, :] * dconv + w_ref[1:2, :] * _up(dconv, 1, 0.0, rows)
              + w_ref[0:1, :] * _up(dconv, 2, 0.0, rows))
        dc_ref[...] = (dz * vv).astype(BF)
        dv_ref[...] = (dz * cv).astype(BF)
        dw_ref[0:1, :] = jnp.sum(dconv * z2, axis=0, keepdims=True)
        dw_ref[1:2, :] = jnp.sum(dconv * z1, axis=0, keepdims=True)
        dw_ref[2:3, :] = jnp.sum(dconv * z, axis=0, keepdims=True)

    return pl.pallas_call(
        body, name="conv_bwd", grid=(nb,),
        in_specs=[pl.BlockSpec((T, LANES), lambda g: (0, g)), pl.BlockSpec((T, LANES), _col(0)),
                  pl.BlockSpec((T, LANES), _col(nb)), pl.BlockSpec((T, LANES), _col(2 * nb)),
                  pl.BlockSpec((3, LANES), lambda g: (0, g))],
        out_specs=[pl.BlockSpec((T, LANES), lambda g: (0, g))] * 3 + [pl.BlockSpec((3, LANES), lambda g: (0, g))],
        out_shape=[jax.ShapeDtypeStruct((T, nb * LANES), BF)] * 3 + [jax.ShapeDtypeStruct((3, nb * LANES), F32)],
        compiler_params=_params("parallel"),
    )(dy, p, p, p, w)


def _gelu(x):
    t = jnp.tanh(GELU_C * (x + GELU_K * x * x * x))
    return 0.5 * x * (1.0 + t), t


def _lru_common(x, cw_ref, cb_ref, wa_ref, ba_ref, wx_ref, bx_ref, lam_ref, rows):
    xr = (cb_ref[...] + cw_ref[3:4, :] * x + cw_ref[2:3, :] * _down(x, 1, 0.0, rows)
          + cw_ref[1:2, :] * _down(x, 2, 0.0, rows) + cw_ref[0:1, :] * _down(x, 3, 0.0, rows))
    xrb = xr.astype(BF)
    r = jax.nn.sigmoid(jnp.dot(xrb, wa_ref[...].astype(BF), preferred_element_type=F32) + ba_ref[...])
    i = jax.nn.sigmoid(jnp.dot(xrb, wx_ref[...].astype(BF), preferred_element_type=F32) + bx_ref[...])
    sp = _softplus(-lam_ref[...])
    log_a = -LRU_C * r * sp
    a = jnp.exp(log_a)
    m = jnp.sqrt(_one_minus_exp_neg(-2.0 * log_a))
    return xr, xrb, r, i, sp, a, m


def _lru_specs(T, nb, gate_blk0, x_blk0):
    vec = pl.BlockSpec((1, LANES), lambda g: (0, g))
    mat = pl.BlockSpec((None, LANES, LANES), lambda g: (g, 0, 0))
    return [pl.BlockSpec((T, LANES), _col(gate_blk0)), pl.BlockSpec((T, LANES), _col(x_blk0)),
            pl.BlockSpec((4, LANES), lambda g: (0, g)), vec, mat, vec, mat, vec, vec]


def _lru_fwd(p, cw, cb, wa, ba, wx, bx, lam, nb, gate_blk0):
    T = p.shape[0]

    def body(gate_ref, x_ref, cw_ref, cb_ref, wa_ref, ba_ref, wx_ref, bx_ref, lam_ref, y_ref, h_ref):
        x = x_ref[...].astype(F32)
        rows = _rows(x.shape)
        xr, _, _, i, _, a, m = _lru_common(x, cw_ref, cb_ref, wa_ref, ba_ref, wx_ref, bx_ref, lam_ref, rows)
        h = _scan_linear(a, m * (i * xr), rows, _down)
        h_ref[...] = h
        y_ref[...] = _gelu(gate_ref[...].astype(F32))[0] * h

    out = jax.ShapeDtypeStruct((T, nb * LANES), F32)
    return pl.pallas_call(
        body, name="lru_fwd", grid=(nb,),
        in_specs=_lru_specs(T, nb, gate_blk0, gate_blk0 + nb),
        out_specs=[pl.BlockSpec((T, LANES), lambda g: (0, g))] * 2,
        out_shape=[out, out],
        compiler_params=_params("parallel"),
    )(p, p, cw, cb, wa, ba, wx, bx, lam)


def _lru_bwd(dy, hs, p, cw, cb, wa, ba, wx, bx, lam, nb, gate_blk0):
    T = p.shape[0]

    def body(dy_ref, hs_ref, gate_ref, x_ref, cw_ref, cb_ref, wa_ref, ba_ref, wx_ref, bx_ref, lam_ref,
             dgate_ref, dx_ref, dcw_ref, dcb_ref, dwa_ref, dba_ref, dwx_ref, dbx_ref, dlam_ref):
        x = x_ref[...].astype(F32)
        rows = _rows(x.shape)
        xr, xrb, r, i, sp, a, m = _lru_common(x, cw_ref, cb_ref, wa_ref, ba_ref, wx_ref, bx_ref, lam_ref, rows)
        gate = gate_ref[...].astype(F32)
        gl, t = _gelu(gate)
        h = hs_ref[...]
        dyv = dy_ref[...]
        dgelu = 0.5 * (1.0 + t) + 0.5 * gate * (1.0 - t * t) * GELU_C * (1.0 + 3.0 * GELU_K * gate * gate)
        dgate_ref[...] = (dyv * h * dgelu).astype(BF)
        lam_adj = _scan_linear(_up(a, 1, 0.0, rows), dyv * gl, rows, _up)
        da = lam_adj * _down(h, 1, 0.0, rows)
        ix = i * xr
        dix = lam_adj * m
        dm = lam_adj * ix
        dlog_a = da * a - dm * (a * a) / jnp.maximum(m, 1e-30)
        dr = dlog_a * (-LRU_C * sp)
        dsp = jnp.sum(dlog_a * (-LRU_C * r), axis=0, keepdims=True)
        dlam_ref[...] = -dsp * jax.nn.sigmoid(-lam_ref[...])
        dpa = dr * r * (1.0 - r)
        dpx = dix * xr * i * (1.0 - i)
        dpab, dpxb = dpa.astype(BF), dpx.astype(BF)
        dxr = (dix * i
               + lax.dot_general(dpab, wa_ref[...].astype(BF), (NT, ((), ())), preferred_element_type=F32)
               + lax.dot_general(dpxb, wx_ref[...].astype(BF), (NT, ((), ())), preferred_element_type=F32))
        dwa_ref[...] = lax.dot_general(xrb, dpab, (TN, ((), ())), preferred_element_type=F32)
        dwx_ref[...] = lax.dot_general(xrb, dpxb, (TN, ((), ())), preferred_element_type=F32)
        dba_ref[...] = jnp.sum(dpa, axis=0, keepdims=True)
        dbx_ref[...] = jnp.sum(dpx, axis=0, keepdims=True)
        dcb_ref[...] = jnp.sum(dxr, axis=0, keepdims=True)
        dx_ref[...] = (cw_ref[3:4, :] * dxr + cw_ref[2:3, :] * _up(dxr, 1, 0.0, rows)
                       + cw_ref[1:2, :] * _up(dxr, 2, 0.0, rows) + cw_ref[0:1, :] * _up(dxr, 3, 0.0, rows)).astype(BF)
        for k in range(4):
            xs = x if k == 3 else _down(x, 3 - k, 0.0, rows)
            dcw_ref[k:k + 1, :] = jnp.sum(dxr * xs, axis=0, keepdims=True)

    C = nb * LANES
    seq = jax.ShapeDtypeStruct((T, C), BF)
    vec = jax.ShapeDtypeStruct((1, C), F32)
    mat = jax.ShapeDtypeStruct((nb, LANES, LANES), F32)
    vspec = pl.BlockSpec((1, LANES), lambda g: (0, g))
    mspec = pl.BlockSpec((None, LANES, LANES), lambda g: (g, 0, 0))
    sspec = pl.BlockSpec((T, LANES), lambda g: (0, g))
    return pl.pallas_call(
        body, name="lru_bwd", grid=(nb,),
        in_specs=[sspec, sspec] + _lru_specs(T, nb, gate_blk0, gate_blk0 + nb),
        out_specs=[sspec, sspec, pl.BlockSpec((4, LANES), lambda g: (0, g)), vspec, mspec, vspec, mspec, vspec, vspec],
        out_shape=[seq, seq, jax.ShapeDtypeStruct((4, C), F32), vec, mat, vec, mat, vec, vec],
        compiler_params=_params("parallel"),
    )(dy, hs, p, p, cw, cb, wa, ba, wx, bx, lam)


ATTN_BLOCK = 2048


def _causal(shape, transposed=False):
    r = lax.broadcasted_iota(jnp.int32, shape, 0)
    c = lax.broadcasted_iota(jnp.int32, shape, 1)
    return r <= c if transposed else c <= r


def _causal_pairs(n, by_query):
    if by_query:
        pairs = [(i, j) for i in range(n) for j in range(i + 1)]
    else:
        pairs = [(i, j) for j in range(n) for i in range(j, n)]
    return jnp.asarray([a for a, _ in pairs], jnp.int32), jnp.asarray([b for _, b in pairs], jnp.int32)


def _attn_fwd(p, cq, ck, nh, q_blk0):
    T = p.shape[0]
    tq = _tile(T, ATTN_BLOCK)
    nq = T // tq
    scale = LANES ** -0.5

    qi, kj = _causal_pairs(nq, by_query=True)

    def body(qi_ref, kj_ref, q_ref, k_ref, v_ref, cq_ref, ck_ref, o_ref, lse_ref, m_ref, l_ref, acc_ref):
        i, j = qi_ref[pl.program_id(1)], kj_ref[pl.program_id(1)]

        @pl.when(j == 0)
        def _():
            m_ref[...] = jnp.full(m_ref.shape, -jnp.inf, F32)
            l_ref[...] = jnp.zeros(l_ref.shape, F32)
            acc_ref[...] = jnp.zeros(acc_ref.shape, F32)

        def block(diagonal):
            s = lax.dot_general(q_ref[...], k_ref[...], (NT, ((), ())), preferred_element_type=F32) * scale
            s = s + cq_ref[...] - ck_ref[...]
            if diagonal:
                s = jnp.where(_causal(s.shape), s, -jnp.inf)
            m_new = jnp.maximum(m_ref[...], jnp.max(s, axis=-1, keepdims=True))
            alpha = jnp.exp(m_ref[...] - m_new)
            pr = jnp.exp(s - m_new)
            l_ref[...] = alpha * l_ref[...] + jnp.sum(pr, axis=-1, keepdims=True)
            acc_ref[...] = alpha * acc_ref[...] + jnp.dot(pr.astype(BF), v_ref[...], preferred_element_type=F32)
            m_ref[...] = m_new

        pl.when(j < i)(lambda: block(False))

        @pl.when(j == i)
        def _():
            block(True)
            o_ref[...] = acc_ref[...] / l_ref[...]
            lse_ref[...] = m_ref[...] + jnp.log(l_ref[...])

    def kv(off):
        return pl.BlockSpec((tq, LANES), lambda h, s, qi, kj: (kj[s], q_blk0 + off * nh + h))

    return pl.pallas_call(
        body, name="attn_fwd",
        grid_spec=pltpu.PrefetchScalarGridSpec(
            num_scalar_prefetch=2, grid=(nh, qi.shape[0]),
            in_specs=[pl.BlockSpec((tq, LANES), lambda h, s, qi, kj: (qi[s], q_blk0 + h)), kv(1), kv(2),
                      pl.BlockSpec((None, tq, 1), lambda h, s, qi, kj: (h, qi[s], 0)),
                      pl.BlockSpec((None, 1, tq), lambda h, s, qi, kj: (h, 0, kj[s]))],
            out_specs=[pl.BlockSpec((tq, LANES), lambda h, s, qi, kj: (qi[s], h)),
                       pl.BlockSpec((None, tq, 1), lambda h, s, qi, kj: (h, qi[s], 0))],
            scratch_shapes=[pltpu.VMEM((tq, 1), F32), pltpu.VMEM((tq, 1), F32), pltpu.VMEM((tq, LANES), F32)]),
        out_shape=[jax.ShapeDtypeStruct((T, nh * LANES), F32), jax.ShapeDtypeStruct((nh, T, 1), F32)],
        compiler_params=_params("parallel", "arbitrary"),
    )(qi, kj, p, p, p, cq, ck)


def _attn_bwd_q(p, cq, ck, lse, do, o, nh, q_blk0):
    T = p.shape[0]
    tq = _tile(T, ATTN_BLOCK)
    nq = T // tq
    scale = LANES ** -0.5

    qi, kj = _causal_pairs(nq, by_query=True)

    def body(qi_ref, kj_ref, q_ref, k_ref, v_ref, cq_ref, ck_ref, lse_ref, do_ref, o_ref, dq_ref, dl_ref, dr_ref,
             acc_ref):
        i, j = qi_ref[pl.program_id(1)], kj_ref[pl.program_id(1)]

        @pl.when(j == 0)
        def _():
            dl_ref[...] = jnp.sum(do_ref[...] * o_ref[...], axis=-1, keepdims=True)
            dr_ref[...] = jnp.zeros(dr_ref.shape, F32)
            acc_ref[...] = jnp.zeros(acc_ref.shape, F32)

        def block(diagonal):
            s = lax.dot_general(q_ref[...], k_ref[...], (NT, ((), ())), preferred_element_type=F32) * scale
            s = s + cq_ref[...] - ck_ref[...]
            pr = jnp.exp(s - lse_ref[...])
            if diagonal:
                pr = jnp.where(_causal(s.shape), pr, 0.0)
            dp = lax.dot_general(do_ref[...].astype(BF), v_ref[...], (NT, ((), ())), preferred_element_type=F32)
            ds = pr * (dp - dl_ref[...])
            dr_ref[...] += jnp.sum(ds, axis=-1, keepdims=True)
            acc_ref[...] += jnp.dot(ds.astype(BF), k_ref[...], preferred_element_type=F32)

        pl.when(j < i)(lambda: block(False))

        @pl.when(j == i)
        def _():
            block(True)
            dq_ref[...] = (acc_ref[...] * scale).astype(BF)

    def kv(off):
        return pl.BlockSpec((tq, LANES), lambda h, s, qi, kj: (kj[s], q_blk0 + off * nh + h))

    col = pl.BlockSpec((None, tq, 1), lambda h, s, qi, kj: (h, qi[s], 0))
    head = pl.BlockSpec((tq, LANES), lambda h, s, qi, kj: (qi[s], h))
    return pl.pallas_call(
        body, name="attn_bwd_q",
        grid_spec=pltpu.PrefetchScalarGridSpec(
            num_scalar_prefetch=2, grid=(nh, qi.shape[0]),
            in_specs=[pl.BlockSpec((tq, LANES), lambda h, s, qi, kj: (qi[s], q_blk0 + h)), kv(1), kv(2), col,
                      pl.BlockSpec((None, 1, tq), lambda h, s, qi, kj: (h, 0, kj[s])), col, head, head],
            out_specs=[head, col, col],
            scratch_shapes=[pltpu.VMEM((tq, LANES), F32)]),
        out_shape=[jax.ShapeDtypeStruct((T, nh * LANES), BF), jax.ShapeDtypeStruct((nh, T, 1), F32),
                   jax.ShapeDtypeStruct((nh, T, 1), F32)],
        compiler_params=_params("parallel", "arbitrary"),
    )(qi, kj, p, p, p, cq, ck, lse, do, o)


def _attn_bwd_kv(p, cq_row, ck_col, lse_row, delta_row, do, nh, q_blk0):
    T = p.shape[0]
    tk = _tile(T, ATTN_BLOCK)
    nk = T // tk
    scale = LANES ** -0.5

    qi, kj = _causal_pairs(nk, by_query=False)

    def body(qi_ref, kj_ref, q_ref, k_ref, v_ref, cq_ref, ck_ref, lse_ref, dl_ref, do_ref, dk_ref, dv_ref, dc_ref,
             dk_acc, dv_acc, dc_acc):
        i, j = qi_ref[pl.program_id(1)], kj_ref[pl.program_id(1)]

        @pl.when(i == j)
        def _():
            dk_acc[...] = jnp.zeros(dk_acc.shape, F32)
            dv_acc[...] = jnp.zeros(dv_acc.shape, F32)
            dc_acc[...] = jnp.zeros(dc_acc.shape, F32)

        def block(diagonal):
            st = lax.dot_general(k_ref[...], q_ref[...], (NT, ((), ())), preferred_element_type=F32) * scale
            st = st + cq_ref[...] - ck_ref[...]
            pt = jnp.exp(st - lse_ref[...])
            if diagonal:
                pt = jnp.where(_causal(st.shape, transposed=True), pt, 0.0)
            dob = do_ref[...].astype(BF)
            dv_acc[...] += jnp.dot(pt.astype(BF), dob, preferred_element_type=F32)
            dpt = lax.dot_general(v_ref[...], dob, (NT, ((), ())), preferred_element_type=F32)
            dst = pt * (dpt - dl_ref[...])
            dk_acc[...] += jnp.dot(dst.astype(BF), q_ref[...], preferred_element_type=F32)
            dc_acc[...] += jnp.sum(dst, axis=-1, keepdims=True)

        pl.when(i == j)(lambda: block(True))
        pl.when(i > j)(lambda: block(False))

        @pl.when(i == nk - 1)
        def _():
            dk_ref[...] = (dk_acc[...] * scale).astype(BF)
            dv_ref[...] = dv_acc[...].astype(BF)
            dc_ref[...] = dc_acc[...]

    def qside(blk):
        return pl.BlockSpec((tk, LANES), lambda h, s, qi, kj: (qi[s], blk + h))

    def kside(off):
        return pl.BlockSpec((tk, LANES), lambda h, s, qi, kj: (kj[s], q_blk0 + off * nh + h))

    row = pl.BlockSpec((None, 1, tk), lambda h, s, qi, kj: (h, 0, qi[s]))
    col = pl.BlockSpec((None, tk, 1), lambda h, s, qi, kj: (h, kj[s], 0))
    head = pl.BlockSpec((tk, LANES), lambda h, s, qi, kj: (kj[s], h))
    return pl.pallas_call(
        body, name="attn_bwd_kv",
        grid_spec=pltpu.PrefetchScalarGridSpec(
            num_scalar_prefetch=2, grid=(nh, qi.shape[0]),
            in_specs=[qside(q_blk0), kside(1), kside(2), row, col, row, row, qside(0)],
            out_specs=[head, head, col],
            scratch_shapes=[pltpu.VMEM((tk, LANES), F32), pltpu.VMEM((tk, LANES), F32), pltpu.VMEM((tk, 1), F32)]),
        out_shape=[jax.ShapeDtypeStruct((T, nh * LANES), BF), jax.ShapeDtypeStruct((T, nh * LANES), BF),
                   jax.ShapeDtypeStruct((nh, T, 1), F32)],
        compiler_params=_params("parallel", "arbitrary"),
    )(qi, kj, p, p, p, cq_row, ck_col, lse_row, delta_row, do)


def _mixer_dims(D):
    dc, da, dl = D // 4, D // 2, D // 4
    nh = da // LANES
    n_main = 3 * dc + 3 * da + 2 * dl
    return dc, da, dl, nh, n_main


def _pad_mix_w_in(wfull):
    D = wfull.shape[0]
    dc, da, dl, nh, n_main = _mixer_dims(D)
    a = 3 * dc + 3 * da
    return jnp.concatenate([wfull[:, :a], wfull[:, a + nh:], wfull[:, a:a + nh],
                            jnp.zeros((D, LANES - nh), wfull.dtype)], axis=1)


def _unpad_mix_w_in(wp):
    D = wp.shape[0]
    dc, da, dl, nh, n_main = _mixer_dims(D)
    a = 3 * dc + 3 * da
    return jnp.concatenate([wp[:, :a], wp[:, n_main:n_main + nh], wp[:, a:n_main]], axis=1)


def _head_cols(c, nh):
    t = jnp.transpose(c[:, :nh])
    return t[:, :, None], t[:, None, :]


def _mixer_forward(x, gain, wp, wo, sp):
    T, D = x.shape
    dc, da, dl, nh, n_main = _mixer_dims(D)
    nbc, nbl = dc // LANES, dl // LANES
    h = _rmsnorm(x, gain)
    tm = _tile(T, 512)
    tn = n_main // 4
    p = _mm("mix_in", (4, T // tm, 1),
            (h, (tm, D), lambda j, i, k: (i, 0)), (wp, (D, tn), lambda j, i, k: (0, j)),
            (jax.ShapeDtypeStruct((T, n_main), BF), (tm, tn), lambda j, i, k: (i, j)), NN)
    f = _mm("mix_in_fgate", (T // tm, 1, 1),
            (h, (tm, D), lambda i, j, k: (i, 0)), (wp, (D, LANES), lambda i, j, k: (0, n_main // LANES)),
            (jax.ShapeDtypeStruct((T, LANES), F32), (tm, LANES), lambda i, j, k: (i, 0)), NN)
    cum = _fgate_cum(f, sp["fgate_b"])
    cq, ck = _head_cols(cum, nh)
    yc = _conv_fwd(p, sp["conv_w"], nbc)
    q_blk0 = 3 * nbc
    ya, lse = _attn_fwd(p, cq, ck, nh, q_blk0)
    gate_blk0 = q_blk0 + 3 * nh
    yl, hs = _lru_fwd(p, sp["lru_conv_w"], sp["lru_conv_b"], sp["lru_w_a"], sp["lru_b_a"], sp["lru_w_x"],
                      sp["lru_b_x"], sp["lru_lambda"], nbl, gate_blk0)
    yn = _groupnorm(yc, ya, yl, sp["mix_out_norm"])
    xn = _mm("mix_out", (T // tm, 1, 1),
             (yn, (tm, D), lambda i, j, k: (i, 0)), (wo, (D, D), lambda i, j, k: (0, 0)),
             (jax.ShapeDtypeStruct((T, D), F32), (tm, D), lambda i, j, k: (i, 0)), NN,
             resid=(x, (tm, D), lambda i, j, k: (i, 0)))
    return xn, (x, h, p, f, cq, ck, yc, ya, lse, yl, hs, yn)


def _mixer_backward(dx, dxb, saved, gain, wp, wo, sp):
    x, h, p, f, cq, ck, yc, ya, lse, yl, hs, yn = saved
    T, D = x.shape
    dc, da, dl, nh, n_main = _mixer_dims(D)
    nbc, nbl = dc // LANES, dl // LANES
    q_blk0 = 3 * nbc
    gate_blk0 = q_blk0 + 3 * nh
    tm = _tile(T, 512)
    tnd, tnd2 = _tile(D, 512), _tile(D, 1024)
    dyn = _mm("mix_out_dy", (T // tm, 1, 1),
              (dxb, (tm, D), lambda i, j, k: (i, 0)), (wo, (D, D), lambda i, j, k: (0, 0)),
              (jax.ShapeDtypeStruct((T, D), F32), (tm, D), lambda i, j, k: (i, 0)), NT)
    dwo = _mm("mix_out_dw", (D // tnd, D // tnd2, 1),
              (yn, (T, tnd), lambda i, j, k: (0, i)), (dxb, (T, tnd2), lambda i, j, k: (0, j)),
              (jax.ShapeDtypeStruct((D, D), BF), (tnd, tnd2), lambda i, j, k: (i, j)), TN)
    dyc, dya, dyl, dgn = _groupnorm_bwd(dyn, yc, ya, yl, sp["mix_out_norm"])
    dcb, dcc, dcv, dconv_w = _conv_bwd(dyc, p, sp["conv_w"], nbc)
    dq, delta, drow = _attn_bwd_q(p, cq, ck, lse, dya, ya, nh, q_blk0)
    as_row = lambda t: t.reshape(nh, 1, T)
    dk, dv, dcol = _attn_bwd_kv(p, ck, cq, as_row(lse), as_row(delta), dya, nh, q_blk0)
    lanes = lambda t: jnp.pad(jnp.transpose(t.reshape(nh, T)), ((0, 0), (0, LANES - nh)))
    df, dfb = _fgate_cum_bwd(lanes(drow), lanes(dcol), f, sp["fgate_b"])
    dgate, dlx, dcw, dcbias, dwa, dba, dwx, dbx, dlam = _lru_bwd(
        dyl, hs, p, sp["lru_conv_w"], sp["lru_conv_b"], sp["lru_w_a"], sp["lru_b_a"], sp["lru_w_x"],
        sp["lru_b_x"], sp["lru_lambda"], nbl, gate_blk0)
    dp = jnp.concatenate([dcb, dcc, dcv, dq, dk, dv, dgate, dlx, df], axis=1)
    n_pad = n_main + LANES
    tn, tkp = n_pad // 5, n_pad // 3
    dwp = _mm("mix_in_dw", (n_pad // tn, D // tnd, 1),
              (h, (T, tnd), lambda j, i, k: (0, i)), (dp, (T, tn), lambda j, i, k: (0, j)),
              (jax.ShapeDtypeStruct((D, n_pad), BF), (tnd, tn), lambda j, i, k: (i, j)), TN)
    tmh = _tile(T, 1024)
    dh = _mm("mix_in_dh", (T // tmh, 1, n_pad // tkp),
             (dp, (tmh, tkp), lambda i, j, k: (i, k)), (wp, (D, tkp), lambda i, j, k: (0, k)),
             (jax.ShapeDtypeStruct((T, D), F32), (tmh, D), lambda i, j, k: (i, 0)), NT)
    dxn, dxnb, dgain = _rmsnorm_bwd(dh, x, gain, dx)
    small = {"norm_mix": dgain, "mix_out_norm": dgn, "conv_w": dconv_w, "fgate_b": dfb[:, :nh],
             "lru_conv_w": dcw, "lru_conv_b": dcbias, "lru_w_a": dwa, "lru_b_a": dba, "lru_w_x": dwx,
             "lru_b_x": dbx, "lru_lambda": dlam}
    return dxn, dxnb, dwp, dwo, small


def _position():
    x, y, c = lax.axis_index("x"), lax.axis_index("y"), lax.axis_index("c")
    return x, y, c, [(1 - x, y), (x, 1 - y), (1 - x, 1 - y)]


def _all_gather(name, xs):
    n = len(xs)

    def body(*refs):
        x_refs, o_refs = refs[:n], refs[n:2 * n]
        send_sems, recv_sems, local_sems = refs[2 * n:]
        x, y, c, chips = _position()
        me, sibling = (x, y, c), (x, y, 1 - c)

        def copy(t, k, block, to, src=None):
            dst = o_refs[t].at[4 * block[0] + 2 * block[1] + block[2]]
            return pltpu.make_async_remote_copy(
                src_ref=dst if src is None else src, dst_ref=dst, send_sem=send_sems.at[t, k],
                recv_sem=recv_sems.at[t, k], device_id=to, device_id_type=MESH)

        mine = [pltpu.make_async_copy(x_refs[t], o_refs[t].at[4 * x + 2 * y + c], local_sems.at[t]) for t in range(n)]
        for cp in mine:
            cp.start()
        first = []
        for t in range(n):
            first.append(copy(t, 0, me, sibling, src=x_refs[t]))
            first += [copy(t, 1 + j, me, (*chip, c), src=x_refs[t]) for j, chip in enumerate(chips)]
        for cp in first:
            cp.start()
        passed = []
        for j, chip in enumerate(chips):
            for t in range(n):
                copy(t, 1 + j, (*chip, c), me).wait_recv()
                passed.append(copy(t, 4 + j, (*chip, c), sibling))
                passed[-1].start()
        for t in range(n):
            copy(t, 0, sibling, me).wait_recv()
            for j, chip in enumerate(chips):
                copy(t, 4 + j, (*chip, 1 - c), me).wait_recv()
        for cp in first + passed:
            cp.wait_send()
        for cp in mine:
            cp.wait()

    return pl.pallas_call(
        body, name=name,
        in_specs=[_any()] * n, out_specs=[_any()] * n,
        out_shape=[jax.ShapeDtypeStruct((N_DEV,) + v.shape, v.dtype) for v in xs],
        scratch_shapes=[pltpu.SemaphoreType.DMA((n, 7)), pltpu.SemaphoreType.DMA((n, 7)),
                        pltpu.SemaphoreType.DMA((n,))],
    )(*xs)


def _add_sibling(g4, recv, c_idx):
    _, _, R, C = g4.shape
    tr = _tile(R, 512)

    def body(c_ref, g_ref, r_ref, o_ref):
        o_ref[...] = (g_ref[...].astype(F32) + r_ref[...].astype(F32)).astype(o_ref.dtype)

    return pl.pallas_call(
        body, name="rs_add_sibling",
        grid_spec=pltpu.PrefetchScalarGridSpec(
            num_scalar_prefetch=1, grid=(4, R // tr),
            in_specs=[pl.BlockSpec((None, None, tr, C), lambda q, i, c_ref: (q, c_ref[0], i, 0)),
                      pl.BlockSpec((None, None, tr, C), lambda q, i, c_ref: (q, 0, i, 0))],
            out_specs=pl.BlockSpec((None, tr, C), lambda q, i, c_ref: (q, i, 0))),
        out_shape=jax.ShapeDtypeStruct((4, R, C), g4.dtype),
        compiler_params=_params("parallel", "parallel"),
    )(c_idx, g4, recv)


def _hbm_spec():
    return pl.BlockSpec(memory_space=pltpu.HBM)


def _sem_spec():
    return pl.BlockSpec(memory_space=pltpu.SEMAPHORE)


def _side_effects():
    return pltpu.CompilerParams(has_side_effects=pltpu.SideEffectType.DATAFLOW_SIDE_EFFECTING)


def _in_hbm(v):
    return pltpu.with_memory_space_constraint(v, pltpu.HBM)


def _split_start(name, srcs, lands, n_copies, copies_of, after):
    n = len(srcs)
    ns = n * n_copies

    def body(*refs):
        src_refs, land_refs = refs[:n], refs[n:2 * n]
        send_sems, recv_sems = refs[2 * n + 1:2 * n + 1 + ns], refs[2 * n + 1 + ns:2 * n + 1 + 2 * ns]
        token = refs[-1]
        for t in range(n):
            for k, (s, d, to) in enumerate(copies_of(t, src_refs[t], land_refs[t])):
                pltpu.make_async_remote_copy(src_ref=s, dst_ref=d, send_sem=send_sems[t * n_copies + k],
                                             recv_sem=recv_sems[t * n_copies + k], device_id=to,
                                             device_id_type=MESH).start()
        token[...] = jnp.zeros(token.shape, token.dtype)

    thru = [pltpu.HBM(v.shape, v.dtype) for v in list(srcs) + list(lands)]
    out = pl.pallas_call(
        body, name=name,
        in_specs=[_hbm_spec()] * (2 * n) + [_any()],
        out_specs=[_sem_spec()] * (2 * ns) + [_hbm_spec()] * (2 * n) + [pl.BlockSpec(memory_space=pltpu.VMEM)],
        out_shape=[pltpu.SemaphoreType.DMA(())] * (2 * ns) + thru + [jax.ShapeDtypeStruct((8, LANES), F32)],
        input_output_aliases={i: 2 * ns + i for i in range(2 * n)},
        compiler_params=_side_effects(),
    )(*[_in_hbm(v) for v in list(srcs) + list(lands)], after)
    return out[:ns], out[ns:2 * ns], out[2 * ns:2 * ns + n], out[2 * ns + n:2 * ns + 2 * n], out[-1]


def _split_wait(name, send_sems, recv_sems, srcs, lands, n_copies, waits_of, after):
    n = len(srcs)
    ns = n * n_copies

    def body(*refs):
        src_refs, land_refs = refs[:n], refs[n:2 * n]
        send_refs, recv_refs = refs[2 * n:2 * n + ns], refs[2 * n + ns:2 * n + 2 * ns]
        x, y, c, _ = _position()
        for t in range(n):
            for k, (s, d) in enumerate(waits_of(t, src_refs[t], land_refs[t])):
                cp = pltpu.make_async_remote_copy(src_ref=s, dst_ref=d, send_sem=send_refs[t * n_copies + k],
                                                  recv_sem=recv_refs[t * n_copies + k], device_id=(x, y, 1 - c),
                                                  device_id_type=MESH)
                cp.wait_send()
                cp.wait_recv()

    out = pl.pallas_call(
        body, name=name,
        in_specs=[_hbm_spec()] * (2 * n) + [_sem_spec()] * (2 * ns) + [_any()],
        out_specs=[_hbm_spec()] * (2 * n),
        out_shape=[pltpu.HBM(v.shape, v.dtype) for v in list(srcs) + list(lands)],
        input_output_aliases={i: i for i in range(2 * n)},
        compiler_params=_side_effects(),
    )(*srcs, *lands, *send_sems, *recv_sems, after)
    return out[:n], out[n:]


def _block_of(px, py, pc):
    return 4 * px + 2 * py + pc


def _gather_phase1_start(name, xs, after):
    lands = [lax.empty((N_DEV,) + v.shape, v.dtype) for v in xs]

    def copies_of(t, x_ref, land_ref):
        x, y, c, chips = _position()
        dst = land_ref.at[_block_of(x, y, c)]
        return [(x_ref, dst, (x, y, 1 - c))] + [(x_ref, dst, (*chip, c)) for chip in chips]

    return _split_start(name, xs, lands, 4, copies_of, after)


def _gather_phase1_wait(name, started, after):
    send_sems, recv_sems, xs, lands, _ = started

    def waits_of(t, x_ref, land_ref):
        x, y, c, chips = _position()
        return [(x_ref, land_ref.at[_block_of(x, y, 1 - c)])] + [(x_ref, land_ref.at[_block_of(*chip, c)])
                                                                  for chip in chips]

    return _split_wait(name, send_sems, recv_sems, xs, lands, 4, waits_of, after)


def _gather_phase2_start(name, lands, after):
    keep = [lax.empty((8, LANES), v.dtype) for v in lands]

    def copies_of(t, _, land_ref):
        x, y, c, chips = _position()
        return [(land_ref.at[_block_of(*chip, c)], land_ref.at[_block_of(*chip, c)], (x, y, 1 - c)) for chip in chips]

    return _split_start(name, keep, lands, 3, copies_of, after)


def _gather_phase2_wait(name, started, after):
    send_sems, recv_sems, keep, lands, _ = started

    def waits_of(t, _, land_ref):
        x, y, c, chips = _position()
        return [(land_ref.at[_block_of(*chip, c)], land_ref.at[_block_of(*chip, 1 - c)]) for chip in chips]

    return _split_wait(name, send_sems, recv_sems, keep, lands, 3, waits_of, after)[1]


def _place_block(name, land, src, block_idx):
    R, C = src.shape
    tr = _tile(R, 512)

    def body(b_ref, land_ref, s_ref, o_ref):
        o_ref[...] = s_ref[...]

    return pl.pallas_call(
        body, name=name,
        grid_spec=pltpu.PrefetchScalarGridSpec(
            num_scalar_prefetch=1, grid=(R // tr,),
            in_specs=[_any(), pl.BlockSpec((tr, C), lambda i, b: (i, 0))],
            out_specs=pl.BlockSpec((None, tr, C), lambda i, b: (b[0], i, 0))),
        out_shape=jax.ShapeDtypeStruct(land.shape, land.dtype),
        input_output_aliases={1: 0},
        compiler_params=_params("parallel"),
    )(block_idx, land, src)


def _sibling_start(name, gs, after):
    g4 = [g.reshape((4, 2) + g.shape[1:]) for g in gs]
    lands = [lax.empty((4, 1) + g.shape[2:], g.dtype) for g in g4]

    def copies_of(t, g_ref, land_ref):
        x, y, c, _ = _position()
        return [(g_ref.at[:, pl.ds(1 - c, 1)], land_ref, (x, y, 1 - c))]

    return _split_start(name, g4, lands, 1, copies_of, after)


def _sibling_wait(name, started, after):
    send_sems, recv_sems, g4, lands, _ = started

    def waits_of(t, g_ref, land_ref):
        x, y, c, _ = _position()
        return [(g_ref.at[:, pl.ds(1 - c, 1)], land_ref)]

    return _split_wait(name, send_sems, recv_sems, g4, lands, 1, waits_of, after)


def _scatter_chips_start(name, ps, after):
    lands = [lax.empty((3,) + v.shape[1:], v.dtype) for v in ps]

    def copies_of(t, p_ref, land_ref):
        x, y, c, chips = _position()
        return [(p_ref.at[2 * chip[0] + chip[1]], land_ref.at[k], (*chip, c)) for k, chip in enumerate(chips)]

    return _split_start(name, ps, lands, 3, copies_of, after)


def _scatter_chips_wait(name, started, after):
    send_sems, recv_sems, ps, lands, _ = started

    def waits_of(t, p_ref, land_ref):
        x, y, c, chips = _position()
        return [(p_ref.at[2 * chip[0] + chip[1]], land_ref.at[k]) for k, chip in enumerate(chips)]

    return _split_wait(name, send_sems, recv_sems, ps, lands, 3, waits_of, after)


def _scatter_chips_after_sibling(tag, sibling, c_idx, after):
    g4, recv = _sibling_wait("rs_sibling_wait_" + tag, sibling, after)
    ps = [_add_sibling(g, r, c_idx) for g, r in zip(g4, recv)]
    return _scatter_chips_start("rs_chips_start_" + tag, ps, ps[0])


def _sum_devices(parts):
    _, R, C = parts.shape
    tr = _tile(R, 512)

    def body(p_ref, o_ref):
        acc = p_ref[0]
        for d in range(1, N_DEV):
            acc = acc + p_ref[d]
        o_ref[...] = acc

    return pl.pallas_call(
        body, name="sum_devices", grid=(R // tr,),
        in_specs=[pl.BlockSpec((N_DEV, tr, C), lambda i: (0, i, 0))],
        out_specs=pl.BlockSpec((tr, C), lambda i: (i, 0)),
        out_shape=jax.ShapeDtypeStruct((R, C), F32),
        compiler_params=_params("parallel"),
    )(parts)


def _adam_math(w, g, m, v):
    m = ADAM_B1 * m + (1.0 - ADAM_B1) * g
    v = ADAM_B2 * v + (1.0 - ADAM_B2) * (g * g)
    m_hat = m / (1.0 - ADAM_B1 ** ADAM_STEP)
    v_hat = v / (1.0 - ADAM_B2 ** ADAM_STEP)
    return -ADAM_LR * (m_hat / (jnp.sqrt(v_hat) + ADAM_EPS) + ADAM_WD * w), m, v


def _adam_layer(name, layer, w, m, v, own, landed, chip_idx, prev, after):
    L, R, C = w.shape
    tr = _tile(R, 256)
    slab = pl.BlockSpec((None, tr, C), lambda i, q: (layer, i, 0))

    def body(q_ref, w_ref, m_ref, v_ref, own_ref, land_ref, after_ref, *rest):
        g_ref, d_ref, nm_ref, nv_ref = rest[-4:]
        g = land_ref[0].astype(F32)
        for k in range(1, 3):
            g = g + land_ref[k].astype(F32)
        g = g + own_ref[...].astype(F32)
        d, nm, nv = _adam_math(w_ref[...], g, m_ref[...], v_ref[...])
        g_ref[...] = g
        d_ref[...] = d
        nm_ref[...] = nm
        nv_ref[...] = nv

    n_prev = 0 if prev is None else 4
    out = jax.ShapeDtypeStruct((L, R, C), F32)
    return pl.pallas_call(
        body, name=name,
        grid_spec=pltpu.PrefetchScalarGridSpec(
            num_scalar_prefetch=1, grid=(R // tr,),
            in_specs=[slab, slab, slab, pl.BlockSpec((None, tr, C), lambda i, q: (q[0], i, 0)),
                      pl.BlockSpec((3, tr, C), lambda i, q: (0, i, 0)), _any()] + [_any()] * n_prev,
            out_specs=[slab] * 4),
        out_shape=[out] * 4,
        input_output_aliases={7 + k: k for k in range(n_prev)},
        compiler_params=_params("parallel"),
    )(chip_idx, w, m, v, own, landed, after, *(prev or ()))


def _adam_small(w, g, m, v):
    R, C = w.shape
    tr = _tile(R, 512)
    spec = pl.BlockSpec((tr, C), lambda i: (i, 0))

    def body(w_ref, g_ref, m_ref, v_ref, d_ref, nm_ref, nv_ref):
        d, nm, nv = _adam_math(w_ref[...], g_ref[...], m_ref[...], v_ref[...])
        d_ref[...] = d
        nm_ref[...] = nm
        nv_ref[...] = nv

    out = jax.ShapeDtypeStruct((R, C), F32)
    return pl.pallas_call(
        body, name="adam_small", grid=(R // tr,),
        in_specs=[spec] * 4, out_specs=[spec] * 3, out_shape=[out] * 3,
        compiler_params=_params("parallel"),
    )(w, g, m, v)


def _pack(arrays, row_multiple=512):
    flat = jnp.concatenate([a.reshape(-1).astype(F32) for a in arrays])
    per = row_multiple * LANES
    total = -(-flat.shape[0] // per) * per
    return jnp.pad(flat, (0, total - flat.shape[0])).reshape(total // LANES, LANES)


def _unpack(packed, shapes):
    flat = packed.reshape(-1)
    out, off = [], 0
    for s in shapes:
        n = math.prod(s)
        out.append(flat[off:off + n].reshape(s))
        off += n
    return out


BIG = ["ffn1_w_in", "ffn1_w_out", "mix_w_in", "mix_w_out", "ffn2_w_in", "ffn2_w_out"]
SHARDED_SMALL = ["conv_w", "lru_conv_w"]
WEIGHTS = ["norm_ffn1", "ffn1_w_in", "ffn1_w_out", "norm_mix", "mix_w_in", "conv_w", "fgate_b", "lru_conv_w",
           "lru_conv_b", "lru_w_a", "lru_b_a", "lru_w_x", "lru_b_x", "lru_lambda", "mix_out_norm", "mix_w_out",
           "norm_ffn2", "ffn2_w_in", "ffn2_w_out", "final_norm"]
REPLICATED = [n for n in WEIGHTS if n not in BIG and n not in SHARDED_SMALL]


def _step(x, target, w, m, v):
    L = w["norm_ffn1"].shape[0]
    T, D = x.shape[1], x.shape[2]
    dc, da, dl, nh, n_main = _mixer_dims(D)
    xi, yi, ci = lax.axis_index("x"), lax.axis_index("y"), lax.axis_index("c")
    me = 4 * xi + 2 * yi + ci
    c_idx = ci.astype(jnp.int32).reshape(1)

    taps = jnp.concatenate([w["conv_w"].reshape(-1), w["lru_conv_w"].reshape(-1)])
    taps = jnp.pad(taps, (0, (-taps.shape[0]) % (8 * LANES))).reshape(-1, LANES)
    def shards_of(l):
        return [w[n][l].astype(BF) for n in BIG] + ([taps] if l == 0 else [])

    me_idx = me.astype(jnp.int32).reshape(1)
    chip_idx = (2 * xi + yi).astype(jnp.int32).reshape(1)

    def place_own(tag, lands, own):
        return [_place_block("gather_own_" + tag, land, src, me_idx) for land, src in zip(lands, own)]

    gathered = [None] * L
    phase1 = [None] * L
    shards0 = shards_of(0)
    first_start = _gather_phase1_start("gather_p1_start_0_first", shards0[:2] + [taps], taps)
    rest_start = _gather_phase1_start("gather_p1_start_0_rest", shards0[2:len(BIG)], first_start[-1])
    own, landed = _gather_phase1_wait("gather_p1_wait_0_first", first_start, rest_start[-1])
    phase2 = _gather_phase2_start("gather_p2_start_0_first", landed, landed[0])
    landed = _gather_phase2_wait("gather_p2_wait_0_first", phase2, phase2[-1])
    first0 = place_own("0_first", landed, own)
    taps_all = first0[2].reshape(N_DEV, -1)
    n_cw = math.prod(w["conv_w"].shape)
    ch = w["conv_w"].shape[-1]
    conv_w_full = jnp.moveaxis(taps_all[:, :n_cw].reshape((N_DEV,) + w["conv_w"].shape), 0, -2).reshape(L, 3, N_DEV * ch)
    n_lw = math.prod(w["lru_conv_w"].shape)
    lru_conv_w_full = jnp.moveaxis(taps_all[:, n_cw:n_cw + n_lw].reshape((N_DEV,) + w["lru_conv_w"].shape), 0, -2
                                   ).reshape(L, 4, N_DEV * ch)

    def layer_weights(l):
        g = dict(zip(BIG, gathered[l]))
        F = g["ffn1_w_out"].shape[1] * N_DEV
        wmix = jnp.transpose(g["mix_w_in"], (1, 0, 2)).reshape(D, -1)
        return {"ffn1_w_in": g["ffn1_w_in"], "ffn1_w_out": g["ffn1_w_out"].reshape(F, D),
                "wp": _pad_mix_w_in(wmix), "wo": g["mix_w_out"].reshape(D, D),
                "ffn2_w_in": g["ffn2_w_in"], "ffn2_w_out": g["ffn2_w_out"].reshape(F, D)}

    def small_params(l):
        return {"fgate_b": jnp.pad(w["fgate_b"][l], (0, LANES - nh)).reshape(1, LANES),
                "conv_w": conv_w_full[l], "lru_conv_w": lru_conv_w_full[l],
                "lru_conv_b": w["lru_conv_b"][l].reshape(1, dl), "lru_w_a": w["lru_w_a"][l],
                "lru_b_a": w["lru_b_a"][l].reshape(1, dl), "lru_w_x": w["lru_w_x"][l],
                "lru_b_x": w["lru_b_x"][l].reshape(1, dl), "lru_lambda": w["lru_lambda"][l].reshape(1, dl),
                "mix_out_norm": w["mix_out_norm"][l].reshape(1, D)}

    gain = lambda n, l, token=None: w[n][l].reshape(1, D) + (0.0 if token is None else token[0:1, 0:1])

    xc = x[0]
    saved, lw, sps = [], [], []
    for l in range(L):
        w_in1, w_out1 = first0[:2] if l == 0 else gathered[l][:2]
        xc, s1 = _ffn_forward(xc, gain("norm_ffn1", l), w_in1, w_out1.reshape(-1, D))
        if l == 0:
            own, landed = _gather_phase1_wait("gather_p1_wait_0_rest", rest_start, xc)
            phase2 = _gather_phase2_start("gather_p2_start_0_rest", landed, landed[0])
            order_token = phase2[-1]
            if L > 1:
                phase1[1] = _gather_phase1_start("gather_p1_start_1", shards_of(1), phase2[-1])
                order_token = phase1[1][-1]
            landed = _gather_phase2_wait("gather_p2_wait_0_rest", phase2, order_token)
            gathered[0] = first0[:2] + place_own("0_rest", landed, own)
        lw.append(layer_weights(l))
        sps.append(small_params(l))
        xc, s2 = _mixer_forward(xc, gain("norm_mix", l), lw[l]["wp"], lw[l]["wo"], sps[l])
        order_token = None
        if l + 1 < L:
            own, landed = _gather_phase1_wait("gather_p1_wait_%d" % (l + 1), phase1[l + 1], xc)
            phase2 = _gather_phase2_start("gather_p2_start_%d" % (l + 1), landed, landed[0])
            order_token = phase2[-1]
            if l + 2 < L:
                phase1[l + 2] = _gather_phase1_start("gather_p1_start_%d" % (l + 2), shards_of(l + 2), phase2[-1])
                order_token = phase1[l + 2][-1]
        xc, s3 = _ffn_forward(xc, gain("norm_ffn2", l, order_token), lw[l]["ffn2_w_in"], lw[l]["ffn2_w_out"])
        if l + 1 < L:
            landed = _gather_phase2_wait("gather_p2_wait_%d" % (l + 1), phase2, xc)
            gathered[l + 1] = place_own(str(l + 1), landed, own)
        saved.append((s1, s2, s3))
    loss_part, dx, dxb, d_final = _loss_head(xc, w["final_norm"].reshape(1, D), target[0])

    small_grads = [None] * L
    parts = [None] * L
    sibling = [None] * L
    scatter = [None] * L
    scatter_early = []
    for l in reversed(range(L)):
        s1, s2, s3 = saved[l]
        dh, dw3_2, dwout_2 = _ffn_backward(dxb, s3, lw[l]["ffn2_w_in"], lw[l]["ffn2_w_out"])
        dx, dxb, dg3 = _rmsnorm_bwd(dh, s3[0], gain("norm_ffn2", l), dx)
        sp_l = sps[l]
        if l + 1 < L:
            scatter[l + 1] = _scatter_chips_after_sibling(str(l + 1), sibling[l + 1], c_idx, dx)
            sp_l = dict(sp_l, mix_out_norm=sp_l["mix_out_norm"] + scatter[l + 1][-1][0:1, 0:1])
        dx, dxb, dwp, dwo, sg = _mixer_backward(dx, dxb, s2, gain("norm_mix", l), lw[l]["wp"], lw[l]["wo"], sp_l)
        dmix = jnp.transpose(_unpad_mix_w_in(dwp).reshape(D, N_DEV, -1), (1, 0, 2))
        grads = [dmix, dwo.reshape(N_DEV, D // N_DEV, D), dw3_2, dwout_2]
        between = None
        if l == 0:
            early = _sibling_start("rs_sibling_start_0_early", grads, dwp)
            grads = []

            def between(dgu):
                scatter_early.append(_scatter_chips_after_sibling("0_early", early, c_idx, dgu))
                return scatter_early[0][-1]

        dh, dw3_1, dwout_1 = _ffn_backward(dxb, s1, lw[l]["ffn1_w_in"], lw[l]["ffn1_w_out"], between)
        if l + 1 < L:
            parts[l + 1] = _scatter_chips_wait("rs_chips_wait_%d" % (l + 1), scatter[l + 1], dh)
        sibling[l] = _sibling_start("rs_sibling_start_%d" % l, [dw3_1, dwout_1] + grads, dh)
        dx, dxb, dg1 = _rmsnorm_bwd(dh, s1[0], gain("norm_ffn1", l, sibling[l][-1]), dx)
        sg["norm_ffn2"], sg["norm_ffn1"] = dg3, dg1
        small_grads[l] = sg

    out_g, out_d, out_m, out_v = {}, {}, {}, {}
    prev = {n: None for n in BIG}

    def adam_big(l, after):
        own, landed = parts[l]
        for t, n in enumerate(BIG):
            prev[n] = _adam_layer("adam_%s_%d" % (n, l), l, w[n], m[n], v[n], own[t], landed[t], chip_idx, prev[n], after)

    small_names = REPLICATED + SHARDED_SMALL
    partial = []
    for n in small_names:
        if n == "final_norm":
            partial.append(d_final)
        else:
            partial.append(jnp.stack([small_grads[l][n].reshape(w[n].shape[1:]) if n not in SHARDED_SMALL
                                      else small_grads[l][n] for l in range(L)]))
    partial.append(loss_part[0, :1])
    packed = _pack(partial)
    small_gathered = _all_gather("gather_small_grads", [packed])[0]
    scatter[0] = _scatter_chips_after_sibling("0", sibling[0], c_idx, small_gathered)
    for l in reversed(range(1, L)):
        adam_big(l, scatter[0][-1])
    summed = _sum_devices(small_gathered)
    full_shapes = [w[n].shape for n in REPLICATED] + [(L, 3, N_DEV * ch), (L, 4, N_DEV * ch), (1,)]
    full = _unpack(summed, full_shapes)
    loss = full[-1][0]
    g_small = dict(zip(small_names, full[:-1]))
    for n in SHARDED_SMALL:
        g_small[n] = lax.dynamic_slice_in_dim(g_small[n], me * ch, ch, axis=2)
    shapes = [w[n].shape for n in small_names]
    d_s, m_s, v_s = _adam_small(_pack([w[n] for n in small_names]), _pack([g_small[n] for n in small_names]),
                                _pack([m[n] for n in small_names]), _pack([v[n] for n in small_names]))
    for n, d_, m_, v_ in zip(small_names, _unpack(d_s, shapes), _unpack(m_s, shapes), _unpack(v_s, shapes)):
        out_g[n], out_d[n], out_m[n], out_v[n] = g_small[n], d_, m_, v_

    behind = d_s if L == 1 else prev[BIG[-1]][0]
    own_early, landed_early = _scatter_chips_wait("rs_chips_wait_0_early", scatter_early[0], behind)
    own_late, landed_late = _scatter_chips_wait("rs_chips_wait_0", scatter[0], behind)
    parts[0] = (list(own_late) + list(own_early), list(landed_late) + list(landed_early))
    adam_big(0, d_s)
    for n in BIG:
        out_g[n], out_d[n], out_m[n], out_v[n] = prev[n]

    return (loss, dx[None], *[out_g[n] for n in WEIGHTS], *[out_d[n] for n in WEIGHTS],
            *[out_m[n] for n in WEIGHTS], *[out_v[n] for n in WEIGHTS])


def kernel(x, norm_ffn1, ffn1_w_in, ffn1_w_out, norm_mix, mix_w_in, conv_w, fgate_b, lru_conv_w, lru_conv_b, lru_w_a, lru_b_a, lru_w_x, lru_b_x, lru_lambda, mix_out_norm, mix_w_out, norm_ffn2, ffn2_w_in, ffn2_w_out, final_norm, loss_target, m_norm_ffn1, m_ffn1_w_in, m_ffn1_w_out, m_norm_mix, m_mix_w_in, m_conv_w, m_fgate_b, m_lru_conv_w, m_lru_conv_b, m_lru_w_a, m_lru_b_a, m_lru_w_x, m_lru_b_x, m_lru_lambda, m_mix_out_norm, m_mix_w_out, m_norm_ffn2, m_ffn2_w_in, m_ffn2_w_out, m_final_norm, v_norm_ffn1, v_ffn1_w_in, v_ffn1_w_out, v_norm_mix, v_mix_w_in, v_conv_w, v_fgate_b, v_lru_conv_w, v_lru_conv_b, v_lru_w_a, v_lru_b_a, v_lru_w_x, v_lru_b_x, v_lru_lambda, v_mix_out_norm, v_mix_w_out, v_norm_ffn2, v_ffn2_w_in, v_ffn2_w_out, v_final_norm):
    w = dict(norm_ffn1=norm_ffn1, ffn1_w_in=ffn1_w_in, ffn1_w_out=ffn1_w_out, norm_mix=norm_mix, mix_w_in=mix_w_in,
             conv_w=conv_w, fgate_b=fgate_b, lru_conv_w=lru_conv_w, lru_conv_b=lru_conv_b, lru_w_a=lru_w_a,
             lru_b_a=lru_b_a, lru_w_x=lru_w_x, lru_b_x=lru_b_x, lru_lambda=lru_lambda, mix_out_norm=mix_out_norm,
             mix_w_out=mix_w_out, norm_ffn2=norm_ffn2, ffn2_w_in=ffn2_w_in, ffn2_w_out=ffn2_w_out,
             final_norm=final_norm)
    m = dict(norm_ffn1=m_norm_ffn1, ffn1_w_in=m_ffn1_w_in, ffn1_w_out=m_ffn1_w_out, norm_mix=m_norm_mix,
             mix_w_in=m_mix_w_in, conv_w=m_conv_w, fgate_b=m_fgate_b, lru_conv_w=m_lru_conv_w,
             lru_conv_b=m_lru_conv_b, lru_w_a=m_lru_w_a, lru_b_a=m_lru_b_a, lru_w_x=m_lru_w_x, lru_b_x=m_lru_b_x,
             lru_lambda=m_lru_lambda, mix_out_norm=m_mix_out_norm, mix_w_out=m_mix_w_out, norm_ffn2=m_norm_ffn2,
             ffn2_w_in=m_ffn2_w_in, ffn2_w_out=m_ffn2_w_out, final_norm=m_final_norm)
    v = dict(norm_ffn1=v_norm_ffn1, ffn1_w_in=v_ffn1_w_in, ffn1_w_out=v_ffn1_w_out, norm_mix=v_norm_mix,
             mix_w_in=v_mix_w_in, conv_w=v_conv_w, fgate_b=v_fgate_b, lru_conv_w=v_lru_conv_w,
             lru_conv_b=v_lru_conv_b, lru_w_a=v_lru_w_a, lru_b_a=v_lru_b_a, lru_w_x=v_lru_w_x, lru_b_x=v_lru_b_x,
             lru_lambda=v_lru_lambda, mix_out_norm=v_mix_out_norm, mix_w_out=v_mix_w_out, norm_ffn2=v_norm_ffn2,
             ffn2_w_in=v_ffn2_w_in, ffn2_w_out=v_ffn2_w_out, final_norm=v_final_norm)
    return _step(x, loss_target, w, m, v)
```
